```python
import jax, jax.numpy as jnp
from jax import lax
import numpy as np


D_MODEL = 1024
BATCH = 16
SEQ = 2048
DEPTH = 1

N_META = 16
GLA_WIDTH = D_MODEL // 2
GLA_HEADS = 4
GLA_DV = GLA_WIDTH // GLA_HEADS
GLA_DK = GLA_DV // 2
GLA_KEY = GLA_HEADS * GLA_DK
GLA_GATE_RANK = 16
GLA_TAU = 16.0
GLA_CHUNK = 64
RWKV_WIDTH = D_MODEL - GLA_WIDTH
RWKV_HEAD = 64
RWKV_HEADS = RWKV_WIDTH // RWKV_HEAD
RWKV_W_RANK = 64
RWKV_A_RANK = 64
RWKV_G_RANK = 128
MIX_WIDTH = GLA_WIDTH + RWKV_WIDTH
GLA_SIZES = (GLA_KEY, GLA_KEY, GLA_WIDTH, GLA_WIDTH, GLA_GATE_RANK)
RWKV_SIZES = (RWKV_WIDTH, RWKV_WIDTH, RWKV_WIDTH, RWKV_W_RANK, RWKV_A_RANK, RWKV_G_RANK)
GLA_IN = sum(GLA_SIZES)
RWKV_IN = sum(RWKV_SIZES)
IN_WIDTH = GLA_IN + RWKV_IN
N_EXPERTS = 32
TOP_K = 4
D_FF = D_MODEL
SWIGLU_ALPHA = 1.702
SWIGLU_LIMIT = 7.0
MOE_BLOCK = 256
DEEPNORM_ALPHA = (2.0 * DEPTH) ** 0.25
DEEPNORM_BETA = (8.0 * DEPTH) ** -0.25
LN_EPS = 1e-5
RWKV_LN_EPS = 64e-5
RMS_EPS = 1e-6

kernel_name = 'hybrid_gla_rwkv7_moe_deepnorm'


def split_cols(t, sizes):
    idx = np.cumsum(sizes)[:-1].tolist()
    return jnp.split(t, idx, axis=-1)


def layer_norm(x, g, b, eps=LN_EPS):
    xf = x.astype(jnp.float32)
    mu = jnp.mean(xf, axis=-1, keepdims=True)
    var = jnp.mean(jnp.square(xf - mu), axis=-1, keepdims=True)
    y = (xf - mu) * lax.rsqrt(var + eps)
    return (y * g.astype(jnp.float32) + b.astype(jnp.float32)).astype(x.dtype)


def gla_mixer(q, k, v, g, gk_low, gk_w2, gk_b, norm_g):
    Bn, L, _ = q.shape
    pad = (-N_META) % GLA_CHUNK
    Lp = L + pad
    nc = Lp // GLA_CHUNK
    f32 = jnp.float32
    lg = jax.nn.log_sigmoid(gk_low.astype(f32) @ gk_w2.astype(f32) + gk_b.astype(f32)) / GLA_TAU

    def chunks(t, d):
        t = t.astype(f32).reshape(Bn, L, GLA_HEADS, d)
        t = jnp.pad(t, ((0, 0), (pad, 0), (0, 0), (0, 0)))
        return t.reshape(Bn, nc, GLA_CHUNK, GLA_HEADS, d).transpose(1, 0, 3, 2, 4)

    qc = chunks(q, GLA_DK) * (GLA_DK ** -0.5)
    kc = chunks(k, GLA_DK)
    vc = chunks(v, GLA_DV)
    lgc = chunks(lg, GLA_DK)
    causal = jnp.tril(jnp.ones((GLA_CHUNK, GLA_CHUNK), dtype=bool))

    def step(S, inp):
        q_c, k_c, v_c, lg_c = inp
        bc = jnp.cumsum(lg_c, axis=2)
        diff = bc[:, :, :, None, :] - bc[:, :, None, :, :]
        diff = jnp.where(causal[:, :, None], diff, -jnp.inf)
        A = jnp.sum(q_c[:, :, :, None, :] * k_c[:, :, None, :, :] * jnp.exp(diff), axis=-1)
        o = jnp.einsum('bhts,bhse->bhte', A, v_c) + jnp.einsum('bhtd,bhde->bhte', q_c * jnp.exp(bc), S)
        b_last = bc[:, :, -1:, :]
        S = jnp.exp(b_last[:, :, 0, :])[..., None] * S + jnp.einsum('bhsd,bhse->bhde', k_c * jnp.exp(b_last - bc), v_c)
        return S, o

    S0 = jnp.zeros((Bn, GLA_HEADS, GLA_DK, GLA_DV), f32)
    _, o = lax.scan(step, S0, (qc, kc, vc, lgc))
    o = o.transpose(1, 0, 3, 2, 4).reshape(Bn, Lp, GLA_HEADS, GLA_DV)[:, pad:]
    o = o * lax.rsqrt(jnp.mean(o * o, axis=-1, keepdims=True) + RMS_EPS) * norm_g.astype(f32)
    return o.reshape(Bn, L, GLA_WIDTH) * jax.nn.silu(g.astype(f32))


def rwkv7_mixer(p, mu, w0, w2, a0, a2, g2, k_k, k_a, r_k, ln_g, ln_b):
    Bn, L, _ = p.shape
    f32 = jnp.float32
    p = p.astype(f32)
    prev = jnp.pad(p, ((0, 0), (1, 0), (0, 0)))[:, :L]
    p = p + (prev - p) * mu.astype(f32)
    r, k, v, w_low, a_low, g_low = split_cols(p, RWKV_SIZES)
    w = -jax.nn.softplus(-(w0.astype(f32) + jnp.tanh(w_low) @ w2.astype(f32))) - 0.5
    decay = jnp.exp(-jnp.exp(w))
    a = jax.nn.sigmoid(a0.astype(f32) + a_low @ a2.astype(f32))
    g = jax.nn.sigmoid(g_low) @ g2.astype(f32)

    def heads(t):
        return t.reshape(Bn, L, RWKV_HEADS, RWKV_HEAD)

    kk = heads(k * k_k.astype(f32))
    kk = kk / jnp.maximum(jnp.sqrt(jnp.sum(kk * kk, axis=-1, keepdims=True)), 1e-12)
    k = k * (1.0 + (a - 1.0) * k_a.astype(f32))
    r_h, k_h, v_h, w_h, a_h = heads(r), heads(k), heads(v), heads(decay), heads(a)

    def step(S, inp):
        r_t, w_t, k_t, v_t, kk_t, a_t = inp
        sa = jnp.einsum('bhij,bhj->bhi', S, -kk_t)
        S = S * w_t[:, :, None, :] + sa[..., None] * (kk_t * a_t)[:, :, None, :] + v_t[..., None] * k_t[:, :, None, :]
        return S, jnp.einsum('bhij,bhj->bhi', S, r_t)

    S0 = jnp.zeros((Bn, RWKV_HEADS, RWKV_HEAD, RWKV_HEAD), f32)
    xs = (jnp.moveaxis(r_h, 1, 0), jnp.moveaxis(w_h, 1, 0), jnp.moveaxis(k_h, 1, 0),
          jnp.moveaxis(v_h, 1, 0), jnp.moveaxis(kk, 1, 0), jnp.moveaxis(a_h, 1, 0))
    _, y = lax.scan(step, S0, xs)
    y = jnp.moveaxis(y, 0, 1)
    mean = jnp.mean(y, axis=-1, keepdims=True)
    var = jnp.mean(jnp.square(y - mean), axis=-1, keepdims=True)
    y = (y - mean) * lax.rsqrt(var + RWKV_LN_EPS)
    y = y * ln_g.astype(f32).reshape(RWKV_HEADS, RWKV_HEAD) + ln_b.astype(f32).reshape(RWKV_HEADS, RWKV_HEAD)
    y = y + jnp.sum(r_h * k_h * r_k.astype(f32), axis=-1, keepdims=True) * v_h
    return y.reshape(Bn, L, RWKV_WIDTH) * g


def moe_ffn(h, router_w, router_b, w_up, b_up, w_down, b_down):
    Bn, L, D = h.shape
    T = Bn * L
    TK = T * TOP_K
    xt = h.reshape(T, D)
    logits = xt.astype(jnp.float32) @ router_w.astype(jnp.float32) + router_b.astype(jnp.float32)
    top_logits, top_idx = lax.top_k(logits, TOP_K)
    gates = jax.nn.softmax(top_logits, axis=-1).astype(h.dtype)
    flat_e = top_idx.reshape(-1).astype(jnp.int32)
    flat_tok = jnp.arange(TK, dtype=jnp.int32) // TOP_K
    order = jnp.argsort(flat_e, stable=True)
    sorted_e = flat_e[order]
    counts = jnp.bincount(flat_e, length=N_EXPERTS).astype(jnp.int32)
    padded = (counts + MOE_BLOCK - 1) // MOE_BLOCK * MOE_BLOCK
    ends_p = jnp.cumsum(padded)
    starts_p = ends_p - padded
    starts = jnp.cumsum(counts) - counts
    rank = jnp.arange(TK, dtype=jnp.int32) - starts[sorted_e]
    dest = starts_p[sorted_e] + rank
    nb = -(-TK // MOE_BLOCK) + N_EXPERTS
    slot_tok = jnp.full((nb * MOE_BLOCK,), T, jnp.int32).at[dest].set(flat_tok[order])
    slot_gate = jnp.zeros((nb * MOE_BLOCK,), h.dtype).at[dest].set(gates.reshape(-1)[order])
    block_start = jnp.arange(nb, dtype=jnp.int32) * MOE_BLOCK
    block_e = jnp.minimum(jnp.searchsorted(ends_p, block_start, side='right'), N_EXPERTS - 1)
    x_pad = jnp.concatenate([xt, jnp.zeros((1, D), xt.dtype)], axis=0)

    def expert_block(args):
        tok, e = args
        xb = x_pad[tok]
        u = xb @ w_up[e] + b_up[e]
        x_glu = jnp.minimum(u[:, 0::2], SWIGLU_LIMIT)
        x_lin = jnp.clip(u[:, 1::2], -SWIGLU_LIMIT, SWIGLU_LIMIT)
        act = x_glu * jax.nn.sigmoid(SWIGLU_ALPHA * x_glu) * (x_lin + 1.0)
        return act @ w_down[e] + b_down[e]

    y = lax.map(expert_block, (slot_tok.reshape(nb, MOE_BLOCK), block_e))
    y = y.reshape(-1, D) * slot_gate[:, None]
    out = jnp.zeros((T + 1, D), h.dtype).at[slot_tok].add(y.astype(h.dtype))[:T]
    return out.reshape(Bn, L, D)


def setup_inputs(seed: int = 0) -> dict:
    key = jax.random.key(seed)
    ks = iter(jax.random.split(key, 40))
    f32 = jnp.float32

    def nrm(shape, scale):
        return scale * jax.random.normal(next(ks), shape, f32)

    x = jax.random.normal(next(ks), (BATCH, SEQ, D_MODEL), f32)
    meta = nrm((N_META, D_MODEL), 1.0)
    ln_in_g = 1.0 + nrm((D_MODEL,), 0.02)
    ln_in_b = nrm((D_MODEL,), 0.02)
    w_in = nrm((DEPTH, D_MODEL, IN_WIDTH), D_MODEL ** -0.5)
    gla_gk_w2 = nrm((DEPTH, GLA_GATE_RANK, GLA_KEY), GLA_GATE_RANK ** -0.5)
    gla_gk_b = nrm((DEPTH, GLA_KEY), 0.1)
    gla_norm_g = 1.0 + nrm((DEPTH, GLA_DV), 0.02)
    rwkv_mu = jax.random.uniform(next(ks), (DEPTH, RWKV_IN), f32)
    n = jnp.arange(RWKV_WIDTH, dtype=f32) / (RWKV_WIDTH - 1)
    rwkv_w0 = (-6.5 + 5.0 * n ** 0.85)[None, :] + nrm((DEPTH, RWKV_WIDTH), 0.05)
    rwkv_w2 = nrm((DEPTH, RWKV_W_RANK, RWKV_WIDTH), 0.5 * RWKV_W_RANK ** -0.5)
    rwkv_a0 = nrm((DEPTH, RWKV_WIDTH), 0.1)
    rwkv_a2 = nrm((DEPTH, RWKV_A_RANK, RWKV_WIDTH), 0.5 * RWKV_A_RANK ** -0.5)
    rwkv_g2 = nrm((DEPTH, RWKV_G_RANK, RWKV_WIDTH), RWKV_G_RANK ** -0.5)
    rwkv_k_k = 0.85 + nrm((DEPTH, RWKV_WIDTH), 0.02)
    rwkv_k_a = 1.0 + nrm((DEPTH, RWKV_WIDTH), 0.02)
    rwkv_r_k = nrm((DEPTH, RWKV_HEADS, RWKV_HEAD), 0.1)
    rwkv_ln_g = 1.0 + nrm((DEPTH, RWKV_WIDTH), 0.02)
    rwkv_ln_b = nrm((DEPTH, RWKV_WIDTH), 0.02)
    w_out = nrm((DEPTH, MIX_WIDTH, D_MODEL), DEEPNORM_BETA * MIX_WIDTH ** -0.5)
    ln1_g = 1.0 + nrm((DEPTH, D_MODEL), 0.02)
    ln1_b = nrm((DEPTH, D_MODEL), 0.02)
    router_w = nrm((DEPTH, D_MODEL, N_EXPERTS), D_MODEL ** -0.5)
    router_b = nrm((DEPTH, N_EXPERTS), 0.01)
    exp_w_up = nrm((DEPTH, N_EXPERTS, D_MODEL, 2 * D_FF), D_MODEL ** -0.5)
    exp_b_up = nrm((DEPTH, N_EXPERTS, 2 * D_FF), 0.01)
    exp_w_down = nrm((DEPTH, N_EXPERTS, D_FF, D_MODEL), DEEPNORM_BETA * D_FF ** -0.5)
    exp_b_down = nrm((DEPTH, N_EXPERTS, D_MODEL), 0.01)
    ln2_g = 1.0 + nrm((DEPTH, D_MODEL), 0.02)
    ln2_b = nrm((DEPTH, D_MODEL), 0.02)
    return {'x': x, 'meta': meta, 'ln_in_g': ln_in_g, 'ln_in_b': ln_in_b, 'w_in': w_in,
            'gla_gk_w2': gla_gk_w2, 'gla_gk_b': gla_gk_b, 'gla_norm_g': gla_norm_g,
            'rwkv_mu': rwkv_mu, 'rwkv_w0': rwkv_w0, 'rwkv_w2': rwkv_w2, 'rwkv_a0': rwkv_a0,
            'rwkv_a2': rwkv_a2, 'rwkv_g2': rwkv_g2, 'rwkv_k_k': rwkv_k_k, 'rwkv_k_a': rwkv_k_a,
            'rwkv_r_k': rwkv_r_k, 'rwkv_ln_g': rwkv_ln_g, 'rwkv_ln_b': rwkv_ln_b, 'w_out': w_out,
            'ln1_g': ln1_g, 'ln1_b': ln1_b, 'router_w': router_w, 'router_b': router_b,
            'exp_w_up': exp_w_up, 'exp_b_up': exp_b_up, 'exp_w_down': exp_w_down, 'exp_b_down': exp_b_down,
            'ln2_g': ln2_g, 'ln2_b': ln2_b}


def reference(x, meta, ln_in_g, ln_in_b, w_in, gla_gk_w2, gla_gk_b, gla_norm_g,
              rwkv_mu, rwkv_w0, rwkv_w2, rwkv_a0, rwkv_a2, rwkv_g2, rwkv_k_k, rwkv_k_a,
              rwkv_r_k, rwkv_ln_g, rwkv_ln_b, w_out, ln1_g, ln1_b, router_w, router_b,
              exp_w_up, exp_b_up, exp_w_down, exp_b_down, ln2_g, ln2_b):
    Bn = x.shape[0]
    meta_b = jnp.broadcast_to(meta[None].astype(x.dtype), (Bn, N_META, D_MODEL))
    h = jnp.concatenate([meta_b, x], axis=1)
    h = layer_norm(h, ln_in_g, ln_in_b)
    for l in range(DEPTH):
        proj = h @ w_in[l]
        q, k, v, g, gk_low = split_cols(proj[..., :GLA_IN], GLA_SIZES)
        y_gla = gla_mixer(q, k, v, g, gk_low, gla_gk_w2[l], gla_gk_b[l], gla_norm_g[l])
        y_rwkv = rwkv7_mixer(proj[..., GLA_IN:], rwkv_mu[l], rwkv_w0[l], rwkv_w2[l], rwkv_a0[l],
                             rwkv_a2[l], rwkv_g2[l], rwkv_k_k[l], rwkv_k_a[l], rwkv_r_k[l],
                             rwkv_ln_g[l], rwkv_ln_b[l])
        mix = jnp.concatenate([y_gla, y_rwkv], axis=-1).astype(h.dtype) @ w_out[l]
        h = layer_norm(DEEPNORM_ALPHA * h + mix, ln1_g[l], ln1_b[l])
        ffn = moe_ffn(h, router_w[l], router_b[l], exp_w_up[l], exp_b_up[l], exp_w_down[l], exp_b_down[l])
        h = layer_norm(DEEPNORM_ALPHA * h + ffn, ln2_g[l], ln2_b[l])
    return h[:, N_META:]
```

```python
import functools

import jax
import jax.numpy as jnp
from jax import lax
from jax.experimental import pallas as pl
from jax.experimental.pallas import tpu as pltpu

F32 = jnp.float32
BF16 = jnp.bfloat16
I32 = jnp.int32
HIGHEST = lax.Precision.HIGHEST

D_MODEL = 1024
N_META = 16
CHUNK = 64
N_FRONT = (-N_META) % CHUNK
GLA_HEADS = 4
GLA_DK = 64
GLA_DV = 128
GLA_KEY = GLA_HEADS * GLA_DK
GLA_WIDTH = GLA_HEADS * GLA_DV
GLA_GATE_RANK = 16
GLA_TAU = 16.0
GLA_COLS = 2 * GLA_KEY + 2 * GLA_WIDTH + 128
RWKV_WIDTH = 512
RWKV_HEAD = 64
RWKV_HEADS = RWKV_WIDTH // RWKV_HEAD
RWKV_W_RANK = 64
RWKV_A_RANK = 64
RWKV_G_RANK = 128
RWKV_COLS = 3 * RWKV_WIDTH + RWKV_W_RANK + RWKV_A_RANK + RWKV_G_RANK
N_EXPERTS = 32
TOP_K = 4
D_FF = D_MODEL
SWIGLU_ALPHA = 1.702
SWIGLU_LIMIT = 7.0
MOE_BLOCK = 256
DEPTH = 1
DEEPNORM_ALPHA = (2.0 * DEPTH) ** 0.25
LN_EPS = 1e-5
RWKV_LN_EPS = 64e-5
RMS_EPS = 1e-6

ROUTER_TILE = 384
VMEM_LIMIT = 56 * 1024 * 1024


def _mm(a, b):
    return jnp.dot(a.astype(BF16), b.astype(BF16), preferred_element_type=F32)


def _mm_nt(a, b):
    return lax.dot_general(a.astype(BF16), b.astype(BF16), (((1,), (1,)), ((), ())),
                           preferred_element_type=F32)


def _mm_tn(a, b):
    return lax.dot_general(a.astype(BF16), b.astype(BF16), (((0,), (0,)), ((), ())),
                           preferred_element_type=F32)


def _mm_f32(a, b):
    return jnp.dot(a, b, preferred_element_type=F32, precision=HIGHEST)


def _layer_norm(x, g, b):
    mu = jnp.mean(x, axis=-1, keepdims=True)
    xc = x - mu
    var = jnp.mean(xc * xc, axis=-1, keepdims=True)
    return xc * lax.rsqrt(var + LN_EPS) * g + b


def _sigmoid(x):
    return 1.0 / (1.0 + jnp.exp(-x))


def _log_sigmoid(x):
    return jnp.minimum(x, 0.0) - jnp.log(1.0 + jnp.exp(-jnp.abs(x)))


def _tri_masks(n):
    r = lax.broadcasted_iota(I32, (n, n), 0)
    c = lax.broadcasted_iota(I32, (n, n), 1)
    return r >= c, r > c, r == c


def _ln_inproj_kernel(x_ref, g_ref, b_ref, w_ref, h_ref, pg_ref, pr_ref, *, tiles_per_seq):
    i = pl.program_id(0)
    y = _layer_norm(x_ref[...], g_ref[...], b_ref[...])
    row = lax.broadcasted_iota(I32, (y.shape[0], 1), 0)
    is_front = jnp.logical_and(i % tiles_per_seq == 0, row < N_FRONT)
    y = jnp.where(is_front, 0.0, y)
    h_ref[...] = y
    p = _mm(y, w_ref[...])
    pg_ref[...] = p[:, :GLA_COLS]
    pr_ref[...] = p[:, GLA_COLS:]


def _ln_inproj(hcat, g, b, w, lp):
    tp = hcat.shape[0]
    tiles_per_seq = 1
    for cand in range(1, lp // 8 + 1):
        if lp % cand == 0 and (lp // cand) % 8 == 0 and lp // cand >= N_FRONT and lp // cand <= 384:
            tiles_per_seq = cand
            break
    tm = lp // tiles_per_seq
    ncols = GLA_COLS + RWKV_COLS
    return pl.pallas_call(
        functools.partial(_ln_inproj_kernel, tiles_per_seq=tiles_per_seq),
        name="ln_inproj",
        grid=(tp // tm,),
        in_specs=[
            pl.BlockSpec((tm, D_MODEL), lambda i: (i, 0)),
            pl.BlockSpec((1, D_MODEL), lambda i: (0, 0)),
            pl.BlockSpec((1, D_MODEL), lambda i: (0, 0)),
            pl.BlockSpec((D_MODEL, ncols), lambda i: (0, 0)),
        ],
        out_specs=[
            pl.BlockSpec((tm, D_MODEL), lambda i: (i, 0)),
            pl.BlockSpec((tm, GLA_COLS), lambda i: (i, 0)),
            pl.BlockSpec((tm, RWKV_COLS), lambda i: (i, 0)),
        ],
        out_shape=[
            jax.ShapeDtypeStruct((tp, D_MODEL), F32),
            jax.ShapeDtypeStruct((tp, GLA_COLS), F32),
            jax.ShapeDtypeStruct((tp, RWKV_COLS), F32),
        ],
        compiler_params=pltpu.CompilerParams(dimension_semantics=("arbitrary",),
                                             vmem_limit_bytes=VMEM_LIMIT),
    )(hcat, g, b, w)


def _gla_kernel(pg_ref, w2_ref, gkb_ref, ng_ref, o_ref, st_ref):
    c = pl.program_id(1)

    @pl.when(c == 0)
    def _():
        st_ref[...] = jnp.zeros_like(st_ref)

    p = pg_ref[0]
    q = p[:, 0:GLA_KEY] * (GLA_DK ** -0.5)
    k = p[:, GLA_KEY:2 * GLA_KEY]
    v = p[:, 2 * GLA_KEY:2 * GLA_KEY + GLA_WIDTH]
    g = p[:, 2 * GLA_KEY + GLA_WIDTH:2 * GLA_KEY + 2 * GLA_WIDTH]
    gl = p[:, 2 * GLA_KEY + 2 * GLA_WIDTH:]
    lg = _log_sigmoid(_mm_f32(gl, w2_ref[...]) + gkb_ref[...]) * (1.0 / GLA_TAU)
    row = lax.broadcasted_iota(I32, (CHUNK, 1), 0)
    lg = jnp.where(jnp.logical_and(c == 0, row < N_FRONT), 0.0, lg)
    incl, _, _ = _tri_masks(CHUNK)
    bc = _mm_f32(incl.astype(F32), lg)
    b_last = bc[CHUNK - 1:CHUNK, :]
    qe = q * jnp.exp(bc)
    ke = k * jnp.exp(-bc)
    kl = k * jnp.exp(b_last - bc)
    e_last = jnp.exp(b_last)
    outs = []
    for h in range(GLA_HEADS):
        ks = slice(h * GLA_DK, (h + 1) * GLA_DK)
        vs = slice(h * GLA_DV, (h + 1) * GLA_DV)
        a = jnp.where(incl, _mm_nt(qe[:, ks], ke[:, ks]), 0.0)
        st = st_ref[h]
        o = _mm(a, v[:, vs]) + _mm_nt(qe[:, ks], st)
        st_ref[h] = st * e_last[:, ks] + _mm_tn(v[:, vs], kl[:, ks])
        o = o * lax.rsqrt(jnp.mean(o * o, axis=-1, keepdims=True) + RMS_EPS) * ng_ref[...]
        gh = g[:, vs]
        outs.append(o * (gh * _sigmoid(gh)))
    o_ref[0] = jnp.concatenate(outs, axis=1)


def _gla(pg, w2p, gkb, ng):
    bn, lp, _ = pg.shape
    return pl.pallas_call(
        _gla_kernel,
        name="gla_mixer",
        grid=(bn, lp // CHUNK),
        in_specs=[
            pl.BlockSpec((1, CHUNK, GLA_COLS), lambda b, c: (b, c, 0)),
            pl.BlockSpec((128, GLA_KEY), lambda b, c: (0, 0)),
            pl.BlockSpec((1, GLA_KEY), lambda b, c: (0, 0)),
            pl.BlockSpec((1, GLA_DV), lambda b, c: (0, 0)),
        ],
        out_specs=pl.BlockSpec((1, CHUNK, GLA_WIDTH), lambda b, c: (b, c, 0)),
        out_shape=jax.ShapeDtypeStruct((bn, lp, GLA_WIDTH), F32),
        scratch_shapes=[pltpu.VMEM((GLA_HEADS, GLA_DV, GLA_DK), F32)],
        compiler_params=pltpu.CompilerParams(dimension_semantics=("arbitrary", "arbitrary")),
    )(pg, w2p, gkb, ng)


def _rwkv_kernel(pr_ref, pv_ref, mu_ref, w0_ref, w2_ref, a0_ref, a2_ref, g2_ref, kk_ref, ka_ref,
                 rk_ref, lng_ref, lnb_ref, seg_ref, o_ref, s_ref):
    c = pl.program_id(1)

    @pl.when(c == 0)
    def _():
        s_ref[...] = jnp.zeros_like(s_ref)

    p = pr_ref[0]
    prev_row = jnp.where(c > 0, pv_ref[0][7:8, :], 0.0)
    row = lax.broadcasted_iota(I32, (CHUNK, 1), 0)
    prev = jnp.where(row == 0, prev_row, pltpu.roll(p, 1, 0))
    p = p + (prev - p) * mu_ref[...]
    W = RWKV_WIDTH
    r = p[:, 0:W]
    k = p[:, W:2 * W]
    v = p[:, 2 * W:3 * W]
    w_low = p[:, 3 * W:3 * W + RWKV_W_RANK]
    a_low = p[:, 3 * W + RWKV_W_RANK:3 * W + RWKV_W_RANK + RWKV_A_RANK]
    g_low = p[:, 3 * W + RWKV_W_RANK + RWKV_A_RANK:]
    wx = w0_ref[...] + _mm(jnp.tanh(w_low), w2_ref[...])
    w = _log_sigmoid(wx) - 0.5
    logd = -jnp.exp(w)
    a = _sigmoid(a0_ref[...] + _mm(a_low, a2_ref[...]))
    g = _mm(_sigmoid(g_low), g2_ref[...])
    seg = seg_ref[...]
    kk = k * kk_ref[...]
    kk = kk / jnp.maximum(jnp.sqrt(_mm_f32(kk * kk, seg)), 1e-12)
    k = k * (1.0 + (a - 1.0) * ka_ref[...])
    bonus = _mm_f32(r * k * rk_ref[...], seg) * v

    incl, strict, diag = _tri_masks(CHUNK)
    c_in = _mm_f32(incl.astype(F32), logd)
    c_ex = c_in - logd
    c_last = c_in[CHUNK - 1:CHUNK, :]
    e_neg = jnp.exp(-c_in)
    e_end = jnp.exp(c_last - c_in)
    at = -kk * jnp.exp(c_ex)
    rt = r * jnp.exp(c_in)
    kt = k * e_neg
    bt = kk * a * e_neg
    kh = k * e_end
    bh = kk * a * e_end
    w_end = jnp.exp(c_last)
    eye = diag.astype(F32)
    outs = []
    for h in range(RWKV_HEADS):
        hs = slice(h * RWKV_HEAD, (h + 1) * RWKV_HEAD)
        vh = v[:, hs]
        gm = _mm_nt(jnp.concatenate([at[:, hs], rt[:, hs]], axis=0),
                    jnp.concatenate([bt[:, hs], kt[:, hs]], axis=0))
        a_ab = jnp.where(strict, gm[:CHUNK, :CHUNK], 0.0)
        a_ak = jnp.where(strict, gm[:CHUNK, CHUNK:], 0.0)
        a_rb = jnp.where(incl, gm[CHUNK:, :CHUNK], 0.0)
        a_rk = jnp.where(incl, gm[CHUNK:, CHUNK:], 0.0)
        tinv = eye + a_ab
        pw = a_ab
        for _ in range(5):
            pw = _mm_f32(pw, pw)
            tinv = tinv + _mm_f32(tinv, pw)
        pq = _mm_f32(tinv, jnp.concatenate([at[:, hs], _mm(a_ak, vh)], axis=1))
        ry = _mm(a_rb, pq) + jnp.concatenate([rt[:, hs], _mm(a_rk, vh)], axis=1)
        m = _mm_tn(bh[:, hs], pq[:, :RWKV_HEAD]) + eye * w_end[:, hs]
        nt = _mm_tn(vh, kh[:, hs]) + _mm_tn(pq[:, RWKV_HEAD:], bh[:, hs])
        s = s_ref[h]
        y = _mm_nt(ry[:, :RWKV_HEAD], s) + ry[:, RWKV_HEAD:]
        s_ref[h] = _mm_nt(s, m) + nt
        mean = jnp.mean(y, axis=-1, keepdims=True)
        yc = y - mean
        var = jnp.mean(yc * yc, axis=-1, keepdims=True)
        outs.append(yc * lax.rsqrt(var + RWKV_LN_EPS))
    y = jnp.concatenate(outs, axis=1) * lng_ref[...] + lnb_ref[...]
    o_ref[0] = (y + bonus) * g


def _rwkv(pr, mu, w0, w2, a0, a2, g2, k_k, k_a, r_k, ln_g, ln_b, seg):
    bn, lp, _ = pr.shape
    vec = lambda n: pl.BlockSpec((1, n), lambda b, c: (0, 0))
    mat = lambda m, n: pl.BlockSpec((m, n), lambda b, c: (0, 0))
    return pl.pallas_call(
        _rwkv_kernel,
        name="rwkv_mixer",
        grid=(bn, lp // CHUNK),
        in_specs=[
            pl.BlockSpec((1, CHUNK, RWKV_COLS), lambda b, c: (b, c, 0)),
            pl.BlockSpec((1, 8, RWKV_COLS), lambda b, c: (b, jnp.maximum(c * (CHUNK // 8) - 1, 0), 0)),
            vec(RWKV_COLS), vec(RWKV_WIDTH), mat(RWKV_W_RANK, RWKV_WIDTH), vec(RWKV_WIDTH),
            mat(RWKV_A_RANK, RWKV_WIDTH), mat(RWKV_G_RANK, RWKV_WIDTH), vec(RWKV_WIDTH), vec(RWKV_WIDTH),
            vec(RWKV_WIDTH), vec(RWKV_WIDTH), vec(RWKV_WIDTH), mat(RWKV_WIDTH, RWKV_WIDTH),
        ],
        out_specs=pl.BlockSpec((1, CHUNK, RWKV_WIDTH), lambda b, c: (b, c, 0)),
        out_shape=jax.ShapeDtypeStruct((bn, lp, RWKV_WIDTH), F32),
        scratch_shapes=[pltpu.VMEM((RWKV_HEADS, RWKV_HEAD, RWKV_HEAD), F32)],
        compiler_params=pltpu.CompilerParams(dimension_semantics=("arbitrary", "arbitrary")),
    )(pr, pr, mu, w0, w2, a0, a2, g2, k_k, k_a, r_k, ln_g, ln_b, seg)


def _outproj_router_kernel(yg_ref, yr_ref, h_ref, wo_ref, g_ref, b_ref, rwt_ref, rb_ref, tri_ref,
                           h1_ref, idx_ref, gate_ref, rank_ref, cnt_ref, base_ref):
    i = pl.program_id(0)

    @pl.when(i == 0)
    def _():
        base_ref[...] = jnp.zeros_like(base_ref)

    wo = wo_ref[...]
    mix = _mm(yg_ref[...], wo[:GLA_WIDTH]) + _mm(yr_ref[...], wo[GLA_WIDTH:])
    h1 = _layer_norm(DEEPNORM_ALPHA * h_ref[...] + mix, g_ref[...], b_ref[...])
    h1_ref[...] = h1
    work = lax.dot_general(rwt_ref[...], h1, (((1,), (1,)), ((), ())), preferred_element_type=F32,
                           precision=HIGHEST) + rb_ref[...][:, 0:1]
    tm = work.shape[1]
    e_iota = lax.broadcasted_iota(I32, (N_EXPERTS, tm), 0)
    base = base_ref[...][:, 0:1]
    prior = jnp.zeros((N_EXPERTS, 1), F32)
    tri = tri_ref[...]
    vals = []
    for kk in range(TOP_K):
        m = jnp.max(work, axis=0, keepdims=True)
        sel = jnp.min(jnp.where(work == m, e_iota, N_EXPERTS), axis=0, keepdims=True)
        onehot = e_iota == sel
        work = jnp.where(onehot, -jnp.inf, work)
        oh = onehot.astype(F32)
        cnt = jnp.dot(oh.astype(BF16), tri, preferred_element_type=F32)
        rank = jnp.sum(oh * (base + prior + cnt - 1.0), axis=0, keepdims=True)
        prior = prior + cnt[:, tm - 1:tm]
        vals.append(m)
        idx_ref[kk:kk + 1, :] = sel
        rank_ref[kk:kk + 1, :] = rank.astype(I32)
    es = [jnp.exp(vv - vals[0]) for vv in vals]
    den = es[0] + es[1] + es[2] + es[3]
    for kk in range(TOP_K):
        gate_ref[kk:kk + 1, :] = es[kk] / den
    new_base = base + prior
    base_ref[...] = jnp.broadcast_to(new_base, base_ref.shape)
    cnt_ref[...] = jnp.broadcast_to(new_base, cnt_ref.shape)


def _outproj_router(yg, yr, h, wo, g, b, rwt, rb, tri):
    tp = h.shape[0]
    tm = ROUTER_TILE
    const = lambda m, n: pl.BlockSpec((m, n), lambda i: (0, 0))
    return pl.pallas_call(
        _outproj_router_kernel,
        name="outproj_router",
        grid=(tp // tm,),
        in_specs=[
            pl.BlockSpec((tm, GLA_WIDTH), lambda i: (i, 0)),
            pl.BlockSpec((tm, RWKV_WIDTH), lambda i: (i, 0)),
            pl.BlockSpec((tm, D_MODEL), lambda i: (i, 0)),
            const(D_MODEL, D_MODEL), const(1, D_MODEL), const(1, D_MODEL),
            const(N_EXPERTS, D_MODEL), const(N_EXPERTS, 128), const(tm, tm),
        ],
        out_specs=[
            pl.BlockSpec((tm, D_MODEL), lambda i: (i, 0)),
            pl.BlockSpec((TOP_K, tm), lambda i: (0, i)),
            pl.BlockSpec((TOP_K, tm), lambda i: (0, i)),
            pl.BlockSpec((TOP_K, tm), lambda i: (0, i)),
            pl.BlockSpec((N_EXPERTS, 128), lambda i: (0, 0)),
        ],
        out_shape=[
            jax.ShapeDtypeStruct((tp, D_MODEL), F32),
            jax.ShapeDtypeStruct((TOP_K, tp), I32),
            jax.ShapeDtypeStruct((TOP_K, tp), F32),
            jax.ShapeDtypeStruct((TOP_K, tp), I32),
            jax.ShapeDtypeStruct((N_EXPERTS, 128), F32),
        ],
        scratch_shapes=[pltpu.VMEM((N_EXPERTS, 128), F32)],
        compiler_params=pltpu.CompilerParams(dimension_semantics=("arbitrary",),
                                             vmem_limit_bytes=VMEM_LIMIT),
    )(yg, yr, h, wo, g, b, rwt, rb, tri)


def _zero_blocks_kernel(last_ref, o_ref):
    o_ref[...] = jnp.zeros_like(o_ref)


def _zero_blocks(last_block, n_slots):
    return pl.pallas_call(
        _zero_blocks_kernel,
        name="moe_zero_blocks",
        grid_spec=pltpu.PrefetchScalarGridSpec(
            num_scalar_prefetch=1,
            grid=(N_EXPERTS,),
            in_specs=[],
            out_specs=pl.BlockSpec((MOE_BLOCK, D_MODEL), lambda e, last: (last[e], 0)),
        ),
        out_shape=jax.ShapeDtypeStruct((n_slots, D_MODEL), F32),
        compiler_params=pltpu.CompilerParams(dimension_semantics=("arbitrary",)),
    )(last_block)


def _dispatch_kernel(pos_ref, x_ref, xs_in_ref, xs_ref, sem):
    del xs_in_ref
    tm = x_ref.shape[0]

    def row_copy(r, kk):
        return pltpu.make_async_copy(x_ref.at[pl.ds(r, 1)], xs_ref.at[pl.ds(pos_ref[kk, r], 1)], sem)

    def start(r, carry):
        for kk in range(TOP_K):
            row_copy(r, kk).start()
        return carry

    def wait(r, carry):
        for kk in range(TOP_K):
            row_copy(r, kk).wait()
        return carry

    lax.fori_loop(0, tm, start, 0)
    lax.fori_loop(0, tm, wait, 0)


def _dispatch(pos, x, xs):
    tp = x.shape[0]
    tm = ROUTER_TILE
    return pl.pallas_call(
        _dispatch_kernel,
        name="moe_dispatch",
        grid=(tp // tm,),
        in_specs=[
            pl.BlockSpec((TOP_K, tm), lambda i: (0, i), memory_space=pltpu.SMEM),
            pl.BlockSpec((tm, D_MODEL), lambda i: (i, 0)),
            pl.BlockSpec(memory_space=pl.ANY),
        ],
        out_specs=pl.BlockSpec(memory_space=pl.ANY),
        out_shape=jax.ShapeDtypeStruct(xs.shape, xs.dtype),
        scratch_shapes=[pltpu.SemaphoreType.DMA(())],
        input_output_aliases={2: 0},
        compiler_params=pltpu.CompilerParams(dimension_semantics=("arbitrary",)),
    )(pos, x, xs)


def _moe_kernel(be_ref, nu_ref, xs_ref, wg_ref, wl_ref, wd_ref, bg_ref, bl_ref, bd_ref, ys_ref):
    @pl.when(pl.program_id(0) < nu_ref[0])
    def _():
        x = xs_ref[...].astype(BF16)
        x_glu = jnp.dot(x, wg_ref[0], preferred_element_type=F32) + bg_ref[0]
        x_lin = jnp.dot(x, wl_ref[0], preferred_element_type=F32) + bl_ref[0]
        x_glu = jnp.minimum(x_glu, SWIGLU_LIMIT)
        x_lin = jnp.clip(x_lin, -SWIGLU_LIMIT, SWIGLU_LIMIT)
        act = x_glu * _sigmoid(SWIGLU_ALPHA * x_glu) * (x_lin + 1.0)
        ys_ref[...] = jnp.dot(act.astype(BF16), wd_ref[0], preferred_element_type=F32) + bd_ref[0]


def _moe(block_e, n_used, xs, wg, wl, wd, bg, bl, bd):
    n_slots = xs.shape[0]
    nb = n_slots // MOE_BLOCK
    blk = lambda i, be, nu: (jnp.minimum(i, nu[0] - 1), 0)
    wspec = pl.BlockSpec((1, D_MODEL, D_FF), lambda i, be, nu: (be[i], 0, 0))
    bspec = pl.BlockSpec((1, 1, D_FF), lambda i, be, nu: (be[i], 0, 0))
    return pl.pallas_call(
        _moe_kernel,
        name="moe_experts",
        grid_spec=pltpu.PrefetchScalarGridSpec(
            num_scalar_prefetch=2,
            grid=(nb,),
            in_specs=[pl.BlockSpec((MOE_BLOCK, D_MODEL), blk), wspec, wspec, wspec, bspec, bspec, bspec],
            out_specs=pl.BlockSpec((MOE_BLOCK, D_MODEL), blk),
        ),
        out_shape=jax.ShapeDtypeStruct((n_slots, D_MODEL), F32),
        compiler_params=pltpu.CompilerParams(dimension_semantics=("arbitrary",),
                                             vmem_limit_bytes=VMEM_LIMIT),
    )(block_e, n_used, xs, wg, wl, wd, bg, bl, bd)


def _combine_kernel(pos_ref, gt_ref, h1_ref, g_ref, b_ref, ys_ref, o_ref, buf, sem):
    def row_copy(r, kk):
        return pltpu.make_async_copy(ys_ref.at[pl.ds(pos_ref[0, 0, kk, r], 1)], buf.at[kk, pl.ds(r, 1)], sem)

    def start(r, carry):
        for kk in range(TOP_K):
            row_copy(r, kk).start()
        return carry

    def wait(r, carry):
        for kk in range(TOP_K):
            row_copy(r, kk).wait()
        return carry

    lax.fori_loop(0, CHUNK, start, 0)
    lax.fori_loop(0, CHUNK, wait, 0)
    gt = gt_ref[...]
    ffn = buf[0] * gt[:, 0:1]
    for kk in range(1, TOP_K):
        ffn = ffn + buf[kk] * gt[:, kk:kk + 1]
    o_ref[0] = _layer_norm(DEEPNORM_ALPHA * h1_ref[0] + ffn, g_ref[...], b_ref[...])


def _combine(pos4, gates_t, h1, g, b, ys, bn, lp):
    seq = lp - CHUNK
    nc = lp // CHUNK
    return pl.pallas_call(
        _combine_kernel,
        name="moe_combine",
        grid=(bn, seq // CHUNK),
        in_specs=[
            pl.BlockSpec((1, 1, TOP_K, CHUNK), lambda bb, j: (bb, j + 1, 0, 0), memory_space=pltpu.SMEM),
            pl.BlockSpec((CHUNK, TOP_K), lambda bb, j: (bb * nc + j + 1, 0)),
            pl.BlockSpec((1, CHUNK, D_MODEL), lambda bb, j: (bb, j + 1, 0)),
            pl.BlockSpec((1, D_MODEL), lambda bb, j: (0, 0)),
            pl.BlockSpec((1, D_MODEL), lambda bb, j: (0, 0)),
            pl.BlockSpec(memory_space=pl.ANY),
        ],
        out_specs=pl.BlockSpec((1, CHUNK, D_MODEL), lambda bb, j: (bb, j, 0)),
        out_shape=jax.ShapeDtypeStruct((bn, seq, D_MODEL), F32),
        scratch_shapes=[pltpu.VMEM((TOP_K, CHUNK, D_MODEL), F32), pltpu.SemaphoreType.DMA(())],
        compiler_params=pltpu.CompilerParams(dimension_semantics=("arbitrary", "arbitrary")),
    )(pos4, gates_t, h1.reshape(bn, lp, D_MODEL), g, b, ys)


def kernel(x, meta, ln_in_g, ln_in_b, w_in, gla_gk_w2, gla_gk_b, gla_norm_g, rwkv_mu, rwkv_w0, rwkv_w2, rwkv_a0, rwkv_a2, rwkv_g2, rwkv_k_k, rwkv_k_a, rwkv_r_k, rwkv_ln_g, rwkv_ln_b, w_out, ln1_g, ln1_b, router_w, router_b, exp_w_up, exp_b_up, exp_w_down, exp_b_down, ln2_g, ln2_b):
    bn, seq, _ = x.shape
    assert seq % CHUNK == 0
    lp = seq + CHUNK
    tp = bn * lp
    assert tp % ROUTER_TILE == 0
    row = lambda t: t.reshape(1, -1).astype(F32)

    hcat = jnp.concatenate([jnp.zeros((bn, N_FRONT, D_MODEL), F32),
                            jnp.broadcast_to(meta[None].astype(F32), (bn, N_META, D_MODEL)), x], axis=1)
    hcat = hcat.reshape(tp, D_MODEL)
    gla_in = 2 * GLA_KEY + 2 * GLA_WIDTH + GLA_GATE_RANK
    w = w_in[0]
    w_cols = jnp.concatenate([w[:, :gla_in], jnp.zeros((D_MODEL, 128 - GLA_GATE_RANK), F32), w[:, gla_in:]],
                             axis=1).astype(BF16)
    h, pg, pr = _ln_inproj(hcat, row(ln_in_g), row(ln_in_b), w_cols, lp)

    w2p = jnp.concatenate([gla_gk_w2[0], jnp.zeros((128 - GLA_GATE_RANK, GLA_KEY), F32)], axis=0)
    y_gla = _gla(pg.reshape(bn, lp, GLA_COLS), w2p, row(gla_gk_b[0]), row(gla_norm_g[0]))

    head_id = jnp.arange(RWKV_WIDTH, dtype=I32) // RWKV_HEAD
    seg = (head_id[:, None] == head_id[None, :]).astype(F32)
    y_rwkv = _rwkv(pr.reshape(bn, lp, RWKV_COLS), row(rwkv_mu[0]), row(rwkv_w0[0]), rwkv_w2[0],
                   row(rwkv_a0[0]), rwkv_a2[0], rwkv_g2[0], row(rwkv_k_k[0]), row(rwkv_k_a[0]),
                   row(rwkv_r_k[0]), row(rwkv_ln_g[0]), row(rwkv_ln_b[0]), seg)

    tri = jnp.triu(jnp.ones((ROUTER_TILE, ROUTER_TILE), F32)).astype(BF16)
    rb = jnp.broadcast_to(router_b[0].reshape(N_EXPERTS, 1), (N_EXPERTS, 128))
    h1, idx, gates, rank, cnt = _outproj_router(
        y_gla.reshape(tp, GLA_WIDTH), y_rwkv.reshape(tp, RWKV_WIDTH), h, w_out[0].astype(BF16),
        row(ln1_g[0]), row(ln1_b[0]), router_w[0].T, rb, tri)

    counts = cnt[:, 0].astype(I32)
    padded = (counts + MOE_BLOCK - 1) // MOE_BLOCK * MOE_BLOCK
    ends_p = jnp.cumsum(padded)
    starts_p = ends_p - padded
    pos = jnp.take(starts_p, idx) + rank
    nb = tp * TOP_K // MOE_BLOCK + N_EXPERTS
    n_slots = nb * MOE_BLOCK
    block_start = jnp.arange(nb, dtype=I32) * MOE_BLOCK
    block_e = jnp.minimum(jnp.searchsorted(ends_p, block_start, side='right'), N_EXPERTS - 1).astype(I32)
    n_used = (ends_p[-1:] // MOE_BLOCK).astype(I32)
    last_block = jnp.maximum(ends_p // MOE_BLOCK - 1, 0).astype(I32)

    xs = _dispatch(pos, h1, _zero_blocks(last_block, n_slots))
    w_up = exp_w_up[0]
    wg = w_up[:, :, 0::2].astype(BF16)
    wl = w_up[:, :, 1::2].astype(BF16)
    bg = exp_b_up[0][:, None, 0::2]
    bl = exp_b_up[0][:, None, 1::2]
    ys = _moe(block_e, n_used, xs, wg, wl, exp_w_down[0].astype(BF16), bg, bl, exp_b_down[0][:, None, :])

    pos4 = pos.reshape(TOP_K, bn, lp // CHUNK, CHUNK).transpose(1, 2, 0, 3)
    return _combine(pos4, gates.T, h1, row(ln2_g[0]), row(ln2_b[0]), ys, bn, lp)
```

```python
import functools

import jax
import jax.numpy as jnp
from jax import lax
from jax.experimental import pallas as pl
from jax.experimental.pallas import tpu as pltpu

F32 = jnp.float32
BF16 = jnp.bfloat16
I32 = jnp.int32
HIGHEST = lax.Precision.HIGHEST

D_MODEL = 1024
N_META = 16
CHUNK = 64
N_FRONT = (-N_META) % CHUNK
GLA_HEADS = 4
GLA_DK = 64
GLA_DV = 128
GLA_KEY = GLA_HEADS * GLA_DK
GLA_WIDTH = GLA_HEADS * GLA_DV
GLA_GATE_RANK = 16
GLA_TAU = 16.0
GLA_COLS = 2 * GLA_KEY + 2 * GLA_WIDTH + 128
RWKV_WIDTH = 512
RWKV_HEAD = 64
RWKV_HEADS = RWKV_WIDTH // RWKV_HEAD
RWKV_W_RANK = 64
RWKV_A_RANK = 64
RWKV_G_RANK = 128
RWKV_COLS = 3 * RWKV_WIDTH + RWKV_W_RANK + RWKV_A_RANK + RWKV_G_RANK
N_EXPERTS = 32
TOP_K = 4
D_FF = D_MODEL
SWIGLU_ALPHA = 1.702
SWIGLU_LIMIT = 7.0
MOE_BLOCK = 256
DEPTH = 1
DEEPNORM_ALPHA = (2.0 * DEPTH) ** 0.25
LN_EPS = 1e-5
RWKV_LN_EPS = 64e-5
RMS_EPS = 1e-6

ROUTER_TILE = 384
VMEM_LIMIT = 56 * 1024 * 1024


def _mm(a, b):
    return jnp.dot(a.astype(BF16), b.astype(BF16), preferred_element_type=F32)


def _mm_nt(a, b):
    return lax.dot_general(a.astype(BF16), b.astype(BF16), (((1,), (1,)), ((), ())),
                           preferred_element_type=F32)


def _mm_tn(a, b):
    return lax.dot_general(a.astype(BF16), b.astype(BF16), (((0,), (0,)), ((), ())),
                           preferred_element_type=F32)


def _mm_f32(a, b):
    return jnp.dot(a, b, preferred_element_type=F32, precision=HIGHEST)


def _layer_norm(x, g, b):
    mu = jnp.mean(x, axis=-1, keepdims=True)
    xc = x - mu
    var = jnp.mean(xc * xc, axis=-1, keepdims=True)
    return xc * lax.rsqrt(var + LN_EPS) * g + b


def _sigmoid(x):
    return 1.0 / (1.0 + jnp.exp(-x))


def _log_sigmoid(x):
    return jnp.minimum(x, 0.0) - jnp.log(1.0 + jnp.exp(-jnp.abs(x)))


def _tri_masks(n):
    r = lax.broadcasted_iota(I32, (n, n), 0)
    c = lax.broadcasted_iota(I32, (n, n), 1)
    return r >= c, r > c, r == c


def _ln_inproj_kernel(x_ref, g_ref, b_ref, w_ref, h_ref, pg_ref, pr_ref, *, tiles_per_seq):
    i = pl.program_id(0)
    y = _layer_norm(x_ref[...], g_ref[...], b_ref[...])
    row = lax.broadcasted_iota(I32, (y.shape[0], 1), 0)
    is_front = jnp.logical_and(i % tiles_per_seq == 0, row < N_FRONT)
    y = jnp.where(is_front, 0.0, y)
    h_ref[...] = y
    p = _mm(y, w_ref[...])
    pg_ref[...] = p[:, :GLA_COLS]
    pr_ref[...] = p[:, GLA_COLS:]


def _ln_inproj(hcat, g, b, w, lp):
    tp = hcat.shape[0]
    tiles_per_seq = 1
    for cand in range(1, lp // 8 + 1):
        if lp % cand == 0 and (lp // cand) % 8 == 0 and lp // cand >= N_FRONT and lp // cand <= 384:
            tiles_per_seq = cand
            break
    tm = lp // tiles_per_seq
    ncols = GLA_COLS + RWKV_COLS
    return pl.pallas_call(
        functools.partial(_ln_inproj_kernel, tiles_per_seq=tiles_per_seq),
        name="ln_inproj",
        grid=(tp // tm,),
        in_specs=[
            pl.BlockSpec((tm, D_MODEL), lambda i: (i, 0)),
            pl.BlockSpec((1, D_MODEL), lambda i: (0, 0)),
            pl.BlockSpec((1, D_MODEL), lambda i: (0, 0)),
            pl.BlockSpec((D_MODEL, ncols), lambda i: (0, 0)),
        ],
        out_specs=[
            pl.BlockSpec((tm, D_MODEL), lambda i: (i, 0)),
            pl.BlockSpec((tm, GLA_COLS), lambda i: (i, 0)),
            pl.BlockSpec((tm, RWKV_COLS), lambda i: (i, 0)),
        ],
        out_shape=[
            jax.ShapeDtypeStruct((tp, D_MODEL), F32),
            jax.ShapeDtypeStruct((tp, GLA_COLS), F32),
            jax.ShapeDtypeStruct((tp, RWKV_COLS), F32),
        ],
        compiler_params=pltpu.CompilerParams(dimension_semantics=("arbitrary",),
                                             vmem_limit_bytes=VMEM_LIMIT),
    )(hcat, g, b, w)


def _gla_kernel(pg_ref, w2_ref, gkb_ref, ng_ref, o_ref, st_ref):
    c = pl.program_id(1)

    @pl.when(c == 0)
    def _():
        st_ref[...] = jnp.zeros_like(st_ref)

    p = pg_ref[0]
    q = p[:, 0:GLA_KEY] * (GLA_DK ** -0.5)
    k = p[:, GLA_KEY:2 * GLA_KEY]
    v = p[:, 2 * GLA_KEY:2 * GLA_KEY + GLA_WIDTH]
    g = p[:, 2 * GLA_KEY + GLA_WIDTH:2 * GLA_KEY + 2 * GLA_WIDTH]
    gl = p[:, 2 * GLA_KEY + 2 * GLA_WIDTH:]
    lg = _log_sigmoid(_mm_f32(gl, w2_ref[...]) + gkb_ref[...]) * (1.0 / GLA_TAU)
    row = lax.broadcasted_iota(I32, (CHUNK, 1), 0)
    lg = jnp.where(jnp.logical_and(c == 0, row < N_FRONT), 0.0, lg)
    incl, _, _ = _tri_masks(CHUNK)
    bc = _mm_f32(incl.astype(F32), lg)
    b_last = bc[CHUNK - 1:CHUNK, :]
    qe = q * jnp.exp(bc)
    ke = k * jnp.exp(-bc)
    kl = k * jnp.exp(b_last - bc)
    e_last = jnp.exp(b_last)
    outs = []
    for h in range(GLA_HEADS):
        ks = slice(h * GLA_DK, (h + 1) * GLA_DK)
        vs = slice(h * GLA_DV, (h + 1) * GLA_DV)
        a = jnp.where(incl, _mm_nt(qe[:, ks], ke[:, ks]), 0.0)
        st = st_ref[h]
        o = _mm(a, v[:, vs]) + _mm_nt(qe[:, ks], st)
        st_ref[h] = st * e_last[:, ks] + _mm_tn(v[:, vs], kl[:, ks])
        o = o * lax.rsqrt(jnp.mean(o * o, axis=-1, keepdims=True) + RMS_EPS) * ng_ref[...]
        gh = g[:, vs]
        outs.append(o * (gh * _sigmoid(gh)))
    o_ref[0] = jnp.concatenate(outs, axis=1)


def _gla(pg, w2p, gkb, ng):
    bn, lp, _ = pg.shape
    return pl.pallas_call(
        _gla_kernel,
        name="gla_mixer",
        grid=(bn, lp // CHUNK),
        in_specs=[
            pl.BlockSpec((1, CHUNK, GLA_COLS), lambda b, c: (b, c, 0)),
            pl.BlockSpec((128, GLA_KEY), lambda b, c: (0, 0)),
            pl.BlockSpec((1, GLA_KEY), lambda b, c: (0, 0)),
            pl.BlockSpec((1, GLA_DV), lambda b, c: (0, 0)),
        ],
        out_specs=pl.BlockSpec((1, CHUNK, GLA_WIDTH), lambda b, c: (b, c, 0)),
        out_shape=jax.ShapeDtypeStruct((bn, lp, GLA_WIDTH), F32),
        scratch_shapes=[pltpu.VMEM((GLA_HEADS, GLA_DV, GLA_DK), F32)],
        compiler_params=pltpu.CompilerParams(dimension_semantics=("arbitrary", "arbitrary")),
    )(pg, w2p, gkb, ng)


RWKV_GROUP = 11
def _mm_split3(ones_bf16, x):
    hi = x.astype(BF16)
    r1 = x - hi.astype(F32)
    mid = r1.astype(BF16)
    lo = (r1 - mid.astype(F32)).astype(BF16)
    dot = lambda t: jnp.dot(ones_bf16, t, preferred_element_type=F32)
    return dot(hi) + dot(mid) + dot(lo)


def _mm_split3_rhs_ones(x, ones_bf16):
    hi = x.astype(BF16)
    r1 = x - hi.astype(F32)
    mid = r1.astype(BF16)
    lo = (r1 - mid.astype(F32)).astype(BF16)
    dot = lambda t: jnp.dot(t, ones_bf16, preferred_element_type=F32)
    return dot(hi) + dot(mid) + dot(lo)


def _rwkv_kernel(pr_ref, pv_ref, mu_ref, w0_ref, w2_ref, a0_ref, a2_ref, g2_ref, kk_ref, ka_ref,
                 rk_ref, lng_ref, lnb_ref, seg_ref, o_ref, s_ref, hd_ref, wend_ref, rm_ref, yn_ref,
                 y_ref, bonus_ref, gate_ref):
    c = pl.program_id(1)
    rows_n = pr_ref.shape[1]
    ng = rows_n // CHUNK

    @pl.when(c == 0)
    def _():
        s_ref[...] = jnp.zeros_like(s_ref)

    p = pr_ref[0]
    prev_row = jnp.where(c > 0, pv_ref[0][7:8, :], 0.0)
    row = lax.broadcasted_iota(I32, (rows_n, 1), 0)
    prev = jnp.where(row == 0, prev_row, pltpu.roll(p, 1, 0))
    p = p + (prev - p) * mu_ref[...]
    W = RWKV_WIDTH
    r = p[:, 0:W]
    k = p[:, W:2 * W]
    v = p[:, 2 * W:3 * W]
    w_low = p[:, 3 * W:3 * W + RWKV_W_RANK]
    a_low = p[:, 3 * W + RWKV_W_RANK:3 * W + RWKV_W_RANK + RWKV_A_RANK]
    g_low = p[:, 3 * W + RWKV_W_RANK + RWKV_A_RANK:]
    wx = w0_ref[...] + _mm(jnp.tanh(w_low), w2_ref[...])
    w = _log_sigmoid(wx) - 0.5
    logd = -jnp.exp(w)
    a = _sigmoid(a0_ref[...] + _mm(a_low, a2_ref[...]))
    g = _mm(_sigmoid(g_low), g2_ref[...])
    seg = seg_ref[...]
    kk = k * kk_ref[...]
    kk = kk / jnp.maximum(jnp.sqrt(_mm_split3_rhs_ones(kk * kk, seg)), 1e-12)
    k = k * (1.0 + (a - 1.0) * ka_ref[...])
    bonus = _mm_split3_rhs_ones(r * k * rk_ref[...], seg) * v

    bonus_ref[...] = bonus
    gate_ref[...] = g

    incl, strict, diag = _tri_masks(CHUNK)
    tril = incl.astype(BF16)
    c_in = jnp.concatenate([_mm_split3(tril, logd[i * CHUNK:(i + 1) * CHUNK]) for i in range(ng)], axis=0)
    g3 = lambda t: t.reshape(ng, CHUNK, W)
    logd, c_in, r, k, v, kk, a = g3(logd), g3(c_in), g3(r), g3(k), g3(v), g3(kk), g3(a)
    c_last = c_in[:, CHUNK - 1:CHUNK, :]
    e_neg = jnp.exp(-c_in)
    e_end = jnp.exp(c_last - c_in)
    kka = kk * a
    per_head = (-kk * jnp.exp(c_in - logd), r * jnp.exp(c_in), kka * e_neg, k * e_neg, kka * e_end,
                k * e_end, v)
    for i, t in enumerate(per_head):
        t = t.astype(BF16)
        for h in range(RWKV_HEADS):
            hd_ref[i, h] = t[:, :, h * RWKV_HEAD:(h + 1) * RWKV_HEAD]
    w_end = jnp.exp(c_last)
    for h in range(RWKV_HEADS):
        wend_ref[h] = w_end[:, :, h * RWKV_HEAD:(h + 1) * RWKV_HEAD]

    eye = diag.astype(F32)
    bmm = lambda x, y: jnp.einsum('gts,gsd->gtd', x.astype(BF16), y.astype(BF16), preferred_element_type=F32)
    bmm_nt = lambda x, y: jnp.einsum('gtd,gsd->gts', x.astype(BF16), y.astype(BF16), preferred_element_type=F32)
    bmm_tn = lambda x, y: jnp.einsum('gtk,gtd->gkd', x.astype(BF16), y.astype(BF16), preferred_element_type=F32)

    def head_body(h, carry):
        at, rt, bt, kt, bh, kh, vh = [hd_ref[i, h] for i in range(7)]
        gm = bmm_nt(jnp.concatenate([at, rt], axis=1), jnp.concatenate([bt, kt], axis=1))
        a_ab = jnp.where(strict, gm[:, :CHUNK, :CHUNK], 0.0)
        a_ak = jnp.where(strict, gm[:, :CHUNK, CHUNK:], 0.0)
        a_rb = jnp.where(incl, gm[:, CHUNK:, :CHUNK], 0.0)
        a_rk = jnp.where(incl, gm[:, CHUNK:, CHUNK:], 0.0)
        akv = bmm(jnp.concatenate([a_ak, a_rk], axis=1), vh)
        xk = eye + a_ab
        pk = bmm(a_ab, a_ab)
        for _ in range(4):
            both = bmm(jnp.concatenate([xk, pk], axis=1), pk)
            xk = xk + both[:, :CHUNK]
            pk = both[:, CHUNK:]
        xk = xk + bmm(xk, pk)
        pq = bmm(xk, jnp.concatenate([at.astype(F32), akv[:, :CHUNK]], axis=2))
        ry = bmm(a_rb, pq) + jnp.concatenate([rt.astype(F32), akv[:, CHUNK:]], axis=2)
        mn = bmm_tn(bh, pq) + jnp.concatenate([eye * wend_ref[h], bmm_tn(kh, vh)], axis=2)
        rm_ref[:, h] = jnp.concatenate([ry[:, :, :RWKV_HEAD], mn[:, :, :RWKV_HEAD]], axis=1).astype(BF16)
        yn_ref[:, h] = jnp.concatenate([ry[:, :, RWKV_HEAD:], mn[:, :, RWKV_HEAD:]], axis=1)
        return carry

    lax.fori_loop(0, RWKV_HEADS, head_body, 0)

    def chunk_body(ci, carry):
        res = jnp.einsum('hmk,hkv->hmv', rm_ref[ci], s_ref[...].astype(BF16),
                         preferred_element_type=F32) + yn_ref[ci]
        y_ref[ci] = res[:, :CHUNK]
        s_ref[...] = res[:, CHUNK:]
        return carry

    lax.fori_loop(0, ng, chunk_body, 0)

    for ci in range(ng):
        y = y_ref[ci]
        mean = jnp.mean(y, axis=-1, keepdims=True)
        yc = y - mean
        var = jnp.mean(yc * yc, axis=-1, keepdims=True)
        yn = yc * lax.rsqrt(var + RWKV_LN_EPS)
        yn = jnp.concatenate([yn[h] for h in range(RWKV_HEADS)], axis=1)
        rows = slice(ci * CHUNK, (ci + 1) * CHUNK)
        o_ref[0, rows, :] = (yn * lng_ref[...] + lnb_ref[...] + bonus_ref[rows, :]) * gate_ref[rows, :]


def _rwkv(pr, mu, w0, w2, a0, a2, g2, k_k, k_a, r_k, ln_g, ln_b, seg):
    bn, lp, _ = pr.shape
    nc = lp // CHUNK
    ng = max(d for d in range(1, RWKV_GROUP + 1) if nc % d == 0)
    rows = ng * CHUNK
    vec = lambda n: pl.BlockSpec((1, n), lambda b, c: (0, 0))
    mat = lambda m, n: pl.BlockSpec((m, n), lambda b, c: (0, 0))
    hh, hd = RWKV_HEADS, RWKV_HEAD
    return pl.pallas_call(
        _rwkv_kernel,
        name="rwkv_mixer",
        grid=(bn, nc // ng),
        in_specs=[
            pl.BlockSpec((1, rows, RWKV_COLS), lambda b, c: (b, c, 0)),
            pl.BlockSpec((1, 8, RWKV_COLS), lambda b, c: (b, jnp.maximum(c * (rows // 8) - 1, 0), 0)),
            vec(RWKV_COLS), vec(RWKV_WIDTH), mat(RWKV_W_RANK, RWKV_WIDTH), vec(RWKV_WIDTH),
            mat(RWKV_A_RANK, RWKV_WIDTH), mat(RWKV_G_RANK, RWKV_WIDTH), vec(RWKV_WIDTH), vec(RWKV_WIDTH),
            vec(RWKV_WIDTH), vec(RWKV_WIDTH), vec(RWKV_WIDTH), mat(RWKV_WIDTH, RWKV_WIDTH),
        ],
        out_specs=pl.BlockSpec((1, rows, RWKV_WIDTH), lambda b, c: (b, c, 0)),
        out_shape=jax.ShapeDtypeStruct((bn, lp, RWKV_WIDTH), F32),
        scratch_shapes=[
            pltpu.VMEM((hh, hd, hd), F32),
            pltpu.VMEM((7, hh, ng, CHUNK, hd), BF16),
            pltpu.VMEM((hh, ng, 1, hd), F32),
            pltpu.VMEM((ng, hh, 2 * CHUNK, hd), BF16),
            pltpu.VMEM((ng, hh, 2 * CHUNK, hd), F32),
            pltpu.VMEM((ng, hh, CHUNK, hd), F32),
            pltpu.VMEM((rows, RWKV_WIDTH), F32),
            pltpu.VMEM((rows, RWKV_WIDTH), F32),
        ],
        compiler_params=pltpu.CompilerParams(dimension_semantics=("arbitrary", "arbitrary"),
                                             vmem_limit_bytes=VMEM_LIMIT),
    )(pr, pr, mu, w0, w2, a0, a2, g2, k_k, k_a, r_k, ln_g, ln_b, seg.astype(BF16))


def _outproj_router_kernel(yg_ref, yr_ref, h_ref, wo_ref, g_ref, b_ref, rwt_ref, rb_ref, tri_ref,
                           h1_ref, idx_ref, gate_ref, rank_ref, cnt_ref, base_ref):
    i = pl.program_id(0)

    @pl.when(i == 0)
    def _():
        base_ref[...] = jnp.zeros_like(base_ref)

    wo = wo_ref[...]
    mix = _mm(yg_ref[...], wo[:GLA_WIDTH]) + _mm(yr_ref[...], wo[GLA_WIDTH:])
    h1 = _layer_norm(DEEPNORM_ALPHA * h_ref[...] + mix, g_ref[...], b_ref[...])
    h1_ref[...] = h1
    work = lax.dot_general(rwt_ref[...], h1, (((1,), (1,)), ((), ())), preferred_element_type=F32,
                           precision=HIGHEST) + rb_ref[...][:, 0:1]
    tm = work.shape[1]
    e_iota = lax.broadcasted_iota(I32, (N_EXPERTS, tm), 0)
    base = base_ref[...][:, 0:1]
    prior = jnp.zeros((N_EXPERTS, 1), F32)
    tri = tri_ref[...]
    vals = []
    for kk in range(TOP_K):
        m = jnp.max(work, axis=0, keepdims=True)
        sel = jnp.min(jnp.where(work == m, e_iota, N_EXPERTS), axis=0, keepdims=True)
        onehot = e_iota == sel
        work = jnp.where(onehot, -jnp.inf, work)
        oh = onehot.astype(F32)
        cnt = jnp.dot(oh.astype(BF16), tri, preferred_element_type=F32)
        rank = jnp.sum(oh * (base + prior + cnt - 1.0), axis=0, keepdims=True)
        prior = prior + cnt[:, tm - 1:tm]
        vals.append(m)
        idx_ref[kk:kk + 1, :] = sel
        rank_ref[kk:kk + 1, :] = rank.astype(I32)
    es = [jnp.exp(vv - vals[0]) for vv in vals]
    den = es[0] + es[1] + es[2] + es[3]
    for kk in range(TOP_K):
        gate_ref[kk:kk + 1, :] = es[kk] / den
    new_base = base + prior
    base_ref[...] = jnp.broadcast_to(new_base, base_ref.shape)
    cnt_ref[...] = jnp.broadcast_to(new_base, cnt_ref.shape)


def _outproj_router(yg, yr, h, wo, g, b, rwt, rb, tri):
    tp = h.shape[0]
    tm = ROUTER_TILE
    const = lambda m, n: pl.BlockSpec((m, n), lambda i: (0, 0))
    return pl.pallas_call(
        _outproj_router_kernel,
        name="outproj_router",
        grid=(tp // tm,),
        in_specs=[
            pl.BlockSpec((tm, GLA_WIDTH), lambda i: (i, 0)),
            pl.BlockSpec((tm, RWKV_WIDTH), lambda i: (i, 0)),
            pl.BlockSpec((tm, D_MODEL), lambda i: (i, 0)),
            const(D_MODEL, D_MODEL), const(1, D_MODEL), const(1, D_MODEL),
            const(N_EXPERTS, D_MODEL), const(N_EXPERTS, 128), const(tm, tm),
        ],
        out_specs=[
            pl.BlockSpec((tm, D_MODEL), lambda i: (i, 0)),
            pl.BlockSpec((TOP_K, tm), lambda i: (0, i)),
            pl.BlockSpec((TOP_K, tm), lambda i: (0, i)),
            pl.BlockSpec((TOP_K, tm), lambda i: (0, i)),
            pl.BlockSpec((N_EXPERTS, 128), lambda i: (0, 0)),
        ],
        out_shape=[
            jax.ShapeDtypeStruct((tp, D_MODEL), F32),
            jax.ShapeDtypeStruct((TOP_K, tp), I32),
            jax.ShapeDtypeStruct((TOP_K, tp), F32),
            jax.ShapeDtypeStruct((TOP_K, tp), I32),
            jax.ShapeDtypeStruct((N_EXPERTS, 128), F32),
        ],
        scratch_shapes=[pltpu.VMEM((N_EXPERTS, 128), F32)],
        compiler_params=pltpu.CompilerParams(dimension_semantics=("arbitrary",),
                                             vmem_limit_bytes=VMEM_LIMIT),
    )(yg, yr, h, wo, g, b, rwt, rb, tri)


def _zero_blocks_kernel(last_ref, o_ref):
    o_ref[...] = jnp.zeros_like(o_ref)


def _zero_blocks(last_block, n_slots):
    return pl.pallas_call(
        _zero_blocks_kernel,
        name="moe_zero_blocks",
        grid_spec=pltpu.PrefetchScalarGridSpec(
            num_scalar_prefetch=1,
            grid=(N_EXPERTS,),
            in_specs=[],
            out_specs=pl.BlockSpec((MOE_BLOCK, D_MODEL), lambda e, last: (last[e], 0)),
        ),
        out_shape=jax.ShapeDtypeStruct((n_slots, D_MODEL), F32),
        compiler_params=pltpu.CompilerParams(dimension_semantics=("arbitrary",)),
    )(last_block)


def _dispatch_kernel(pos_ref, x_ref, xs_in_ref, xs_ref, sem):
    del xs_in_ref
    tm = x_ref.shape[0]

    def row_copy(r, kk):
        return pltpu.make_async_copy(x_ref.at[pl.ds(r, 1)], xs_ref.at[pl.ds(pos_ref[kk, r], 1)], sem)

    def start(r, carry):
        for kk in range(TOP_K):
            row_copy(r, kk).start()
        return carry

    def wait(r, carry):
        for kk in range(TOP_K):
            row_copy(r, kk).wait()
        return carry

    lax.fori_loop(0, tm, start, 0)
    lax.fori_loop(0, tm, wait, 0)


def _dispatch(pos, x, xs):
    tp = x.shape[0]
    tm = ROUTER_TILE
    return pl.pallas_call(
        _dispatch_kernel,
        name="moe_dispatch",
        grid=(tp // tm,),
        in_specs=[
            pl.BlockSpec((TOP_K, tm), lambda i: (0, i), memory_space=pltpu.SMEM),
            pl.BlockSpec((tm, D_MODEL), lambda i: (i, 0)),
            pl.BlockSpec(memory_space=pl.ANY),
        ],
        out_specs=pl.BlockSpec(memory_space=pl.ANY),
        out_shape=jax.ShapeDtypeStruct(xs.shape, xs.dtype),
        scratch_shapes=[pltpu.SemaphoreType.DMA(())],
        input_output_aliases={2: 0},
        compiler_params=pltpu.CompilerParams(dimension_semantics=("arbitrary",)),
    )(pos, x, xs)


SPLIT_TILE = 256


def _split_up_kernel(w_ref, perm_ref, wg_ref, wl_ref):
    half = SPLIT_TILE // 2
    for t in range(w_ref.shape[2] // SPLIT_TILE):
        d = jnp.dot(w_ref[0, :, t * SPLIT_TILE:(t + 1) * SPLIT_TILE].astype(BF16), perm_ref[...],
                    preferred_element_type=F32).astype(BF16)
        wg_ref[0, :, t * half:(t + 1) * half] = d[:, :half]
        wl_ref[0, :, t * half:(t + 1) * half] = d[:, half:]


def _split_up_weights(w_up):
    ne, dm, two_ff = w_up.shape
    cols = 1024
    src = jnp.arange(SPLIT_TILE, dtype=I32)[:, None]
    dst = jnp.arange(SPLIT_TILE, dtype=I32)[None, :]
    half = SPLIT_TILE // 2
    perm = (src == jnp.where(dst < half, 2 * dst, 2 * (dst - half) + 1)).astype(BF16)
    return pl.pallas_call(
        _split_up_kernel,
        name="moe_split_up_weights",
        grid=(ne, two_ff // cols),
        in_specs=[
            pl.BlockSpec((1, dm, cols), lambda e, j: (e, 0, j)),
            pl.BlockSpec((SPLIT_TILE, SPLIT_TILE), lambda e, j: (0, 0)),
        ],
        out_specs=[
            pl.BlockSpec((1, dm, cols // 2), lambda e, j: (e, 0, j)),
            pl.BlockSpec((1, dm, cols // 2), lambda e, j: (e, 0, j)),
        ],
        out_shape=[
            jax.ShapeDtypeStruct((ne, dm, two_ff // 2), BF16),
            jax.ShapeDtypeStruct((ne, dm, two_ff // 2), BF16),
        ],
        compiler_params=pltpu.CompilerParams(dimension_semantics=("arbitrary", "arbitrary")),
    )(w_up, perm)


def _moe_kernel(be_ref, nu_ref, xs_ref, wg_ref, wl_ref, wd_ref, bg_ref, bl_ref, bd_ref, ys_ref):
    @pl.when(pl.program_id(0) < nu_ref[0])
    def _():
        x = xs_ref[...].astype(BF16)
        x_glu = jnp.dot(x, wg_ref[0], preferred_element_type=F32) + bg_ref[0]
        x_lin = jnp.dot(x, wl_ref[0], preferred_element_type=F32) + bl_ref[0]
        x_glu = jnp.minimum(x_glu, SWIGLU_LIMIT)
        x_lin = jnp.clip(x_lin, -SWIGLU_LIMIT, SWIGLU_LIMIT)
        act = x_glu * _sigmoid(SWIGLU_ALPHA * x_glu) * (x_lin + 1.0)
        ys_ref[...] = jnp.dot(act.astype(BF16), wd_ref[0], preferred_element_type=F32) + bd_ref[0]


def _moe(block_e, n_used, xs, wg, wl, wd, bg, bl, bd):
    n_slots = xs.shape[0]
    nb = n_slots // MOE_BLOCK
    blk = lambda i, be, nu: (jnp.minimum(i, nu[0] - 1), 0)
    wspec = pl.BlockSpec((1, D_MODEL, D_FF), lambda i, be, nu: (be[i], 0, 0))
    bspec = pl.BlockSpec((1, 1, D_FF), lambda i, be, nu: (be[i], 0, 0))
    return pl.pallas_call(
        _moe_kernel,
        name="moe_experts",
        grid_spec=pltpu.PrefetchScalarGridSpec(
            num_scalar_prefetch=2,
            grid=(nb,),
            in_specs=[pl.BlockSpec((MOE_BLOCK, D_MODEL), blk), wspec, wspec, wspec, bspec, bspec, bspec],
            out_specs=pl.BlockSpec((MOE_BLOCK, D_MODEL), blk),
        ),
        out_shape=jax.ShapeDtypeStruct((n_slots, D_MODEL), F32),
        compiler_params=pltpu.CompilerParams(dimension_semantics=("arbitrary",),
                                             vmem_limit_bytes=VMEM_LIMIT),
    )(block_e, n_used, xs, wg, wl, wd, bg, bl, bd)


def _combine_kernel(pos_ref, gt_ref, h1_ref, g_ref, b_ref, ys_ref, o_ref, buf, sem):
    def row_copy(r, kk):
        return pltpu.make_async_copy(ys_ref.at[pl.ds(pos_ref[0, 0, kk, r], 1)], buf.at[kk, pl.ds(r, 1)], sem)

    def start(r, carry):
        for kk in range(TOP_K):
            row_copy(r, kk).start()
        return carry

    def wait(r, carry):
        for kk in range(TOP_K):
            row_copy(r, kk).wait()
        return carry

    lax.fori_loop(0, CHUNK, start, 0)
    lax.fori_loop(0, CHUNK, wait, 0)
    gt = gt_ref[...]
    ffn = buf[0] * gt[:, 0:1]
    for kk in range(1, TOP_K):
        ffn = ffn + buf[kk] * gt[:, kk:kk + 1]
    o_ref[0] = _layer_norm(DEEPNORM_ALPHA * h1_ref[0] + ffn, g_ref[...], b_ref[...])


def _combine(pos4, gates_t, h1, g, b, ys, bn, lp):
    seq = lp - CHUNK
    nc = lp // CHUNK
    return pl.pallas_call(
        _combine_kernel,
        name="moe_combine",
        grid=(bn, seq // CHUNK),
        in_specs=[
            pl.BlockSpec((1, 1, TOP_K, CHUNK), lambda bb, j: (bb, j + 1, 0, 0), memory_space=pltpu.SMEM),
            pl.BlockSpec((CHUNK, TOP_K), lambda bb, j: (bb * nc + j + 1, 0)),
            pl.BlockSpec((1, CHUNK, D_MODEL), lambda bb, j: (bb, j + 1, 0)),
            pl.BlockSpec((1, D_MODEL), lambda bb, j: (0, 0)),
            pl.BlockSpec((1, D_MODEL), lambda bb, j: (0, 0)),
            pl.BlockSpec(memory_space=pl.ANY),
        ],
        out_specs=pl.BlockSpec((1, CHUNK, D_MODEL), lambda bb, j: (bb, j, 0)),
        out_shape=jax.ShapeDtypeStruct((bn, seq, D_MODEL), F32),
        scratch_shapes=[pltpu.VMEM((TOP_K, CHUNK, D_MODEL), F32), pltpu.SemaphoreType.DMA(())],
        compiler_params=pltpu.CompilerParams(dimension_semantics=("arbitrary", "arbitrary")),
    )(pos4, gates_t, h1.reshape(bn, lp, D_MODEL), g, b, ys)


def kernel(x, meta, ln_in_g, ln_in_b, w_in, gla_gk_w2, gla_gk_b, gla_norm_g, rwkv_mu, rwkv_w0, rwkv_w2, rwkv_a0, rwkv_a2, rwkv_g2, rwkv_k_k, rwkv_k_a, rwkv_r_k, rwkv_ln_g, rwkv_ln_b, w_out, ln1_g, ln1_b, router_w, router_b, exp_w_up, exp_b_up, exp_w_down, exp_b_down, ln2_g, ln2_b):
    bn, seq, _ = x.shape
    assert seq % CHUNK == 0
    lp = seq + CHUNK
    tp = bn * lp
    assert tp % ROUTER_TILE == 0
    row = lambda t: t.reshape(1, -1).astype(F32)

    hcat = jnp.concatenate([jnp.zeros((bn, N_FRONT, D_MODEL), F32),
                            jnp.broadcast_to(meta[None].astype(F32), (bn, N_META, D_MODEL)), x], axis=1)
    hcat = hcat.reshape(tp, D_MODEL)
    gla_in = 2 * GLA_KEY + 2 * GLA_WIDTH + GLA_GATE_RANK
    w = w_in[0]
    w_cols = jnp.concatenate([w[:, :gla_in], jnp.zeros((D_MODEL, 128 - GLA_GATE_RANK), F32), w[:, gla_in:]],
                             axis=1).astype(BF16)
    h, pg, pr = _ln_inproj(hcat, row(ln_in_g), row(ln_in_b), w_cols, lp)

    w2p = jnp.concatenate([gla_gk_w2[0], jnp.zeros((128 - GLA_GATE_RANK, GLA_KEY), F32)], axis=0)
    y_gla = _gla(pg.reshape(bn, lp, GLA_COLS), w2p, row(gla_gk_b[0]), row(gla_norm_g[0]))

    head_id = jnp.arange(RWKV_WIDTH, dtype=I32) // RWKV_HEAD
    seg = (head_id[:, None] == head_id[None, :]).astype(F32)
    y_rwkv = _rwkv(pr.reshape(bn, lp, RWKV_COLS), row(rwkv_mu[0]), row(rwkv_w0[0]), rwkv_w2[0],
                   row(rwkv_a0[0]), rwkv_a2[0], rwkv_g2[0], row(rwkv_k_k[0]), row(rwkv_k_a[0]),
                   row(rwkv_r_k[0]), row(rwkv_ln_g[0]), row(rwkv_ln_b[0]), seg)

    tri = jnp.triu(jnp.ones((ROUTER_TILE, ROUTER_TILE), F32)).astype(BF16)
    rb = jnp.broadcast_to(router_b[0].reshape(N_EXPERTS, 1), (N_EXPERTS, 128))
    h1, idx, gates, rank, cnt = _outproj_router(
        y_gla.reshape(tp, GLA_WIDTH), y_rwkv.reshape(tp, RWKV_WIDTH), h, w_out[0].astype(BF16),
        row(ln1_g[0]), row(ln1_b[0]), router_w[0].T, rb, tri)

    counts = cnt[:, 0].astype(I32)
    padded = (counts + MOE_BLOCK - 1) // MOE_BLOCK * MOE_BLOCK
    ends_p = jnp.cumsum(padded)
    starts_p = ends_p - padded
    e_ids = jnp.arange(N_EXPERTS, dtype=I32)
    start_of = jnp.sum(jnp.where(idx[None] == e_ids[:, None, None], starts_p[:, None, None], 0), axis=0)
    pos = start_of + rank
    nb = tp * TOP_K // MOE_BLOCK + N_EXPERTS
    n_slots = nb * MOE_BLOCK
    block_start = jnp.arange(nb, dtype=I32) * MOE_BLOCK
    block_e = jnp.minimum(jnp.sum((block_start[:, None] >= ends_p[None, :]).astype(I32), axis=1), N_EXPERTS - 1)
    n_used = (ends_p[-1:] // MOE_BLOCK).astype(I32)
    last_block = jnp.maximum(ends_p // MOE_BLOCK - 1, 0).astype(I32)

    xs = _dispatch(pos, h1, _zero_blocks(last_block, n_slots))
    wg, wl = _split_up_weights(exp_w_up[0])
    bg = exp_b_up[0][:, None, 0::2]
    bl = exp_b_up[0][:, None, 1::2]
    ys = _moe(block_e, n_used, xs, wg, wl, exp_w_down[0].astype(BF16), bg, bl, exp_b_down[0][:, None, :])

    pos4 = pos.reshape(TOP_K, bn, lp // CHUNK, CHUNK).transpose(1, 2, 0, 3)
    return _combine(pos4, gates.T, h1, row(ln2_g[0]), row(ln2_b[0]), ys, bn, lp)
```

```python
import functools

import jax
import jax.numpy as jnp
from jax import lax
from jax.experimental import pallas as pl
from jax.experimental.pallas import tpu as pltpu

F32 = jnp.float32
BF16 = jnp.bfloat16
I32 = jnp.int32
HIGHEST = lax.Precision.HIGHEST

D_MODEL = 1024
N_META = 16
CHUNK = 64
N_FRONT = (-N_META) % CHUNK
GLA_HEADS = 4
GLA_DK = 64
GLA_DV = 128
GLA_KEY = GLA_HEADS * GLA_DK
GLA_WIDTH = GLA_HEADS * GLA_DV
GLA_GATE_RANK = 16
GLA_TAU = 16.0
GLA_COLS = 2 * GLA_KEY + 2 * GLA_WIDTH + 128
RWKV_WIDTH = 512
RWKV_HEAD = 64
RWKV_HEADS = RWKV_WIDTH // RWKV_HEAD
RWKV_W_RANK = 64
RWKV_A_RANK = 64
RWKV_G_RANK = 128
RWKV_COLS = 3 * RWKV_WIDTH + RWKV_W_RANK + RWKV_A_RANK + RWKV_G_RANK
N_EXPERTS = 32
TOP_K = 4
D_FF = D_MODEL
SWIGLU_ALPHA = 1.702
SWIGLU_LIMIT = 7.0
MOE_BLOCK = 256
DEPTH = 1
DEEPNORM_ALPHA = (2.0 * DEPTH) ** 0.25
LN_EPS = 1e-5
RWKV_LN_EPS = 64e-5
RMS_EPS = 1e-6

ROUTER_TILE = 384
DMA_UNROLL = 4
VMEM_LIMIT = 56 * 1024 * 1024


def _mm(a, b):
    return jnp.dot(a.astype(BF16), b.astype(BF16), preferred_element_type=F32)


def _mm_nt(a, b):
    return lax.dot_general(a.astype(BF16), b.astype(BF16), (((1,), (1,)), ((), ())),
                           preferred_element_type=F32)


def _mm_tn(a, b):
    return lax.dot_general(a.astype(BF16), b.astype(BF16), (((0,), (0,)), ((), ())),
                           preferred_element_type=F32)


def _mm_f32(a, b):
    return jnp.dot(a, b, preferred_element_type=F32, precision=HIGHEST)


def _layer_norm(x, g, b):
    mu = jnp.mean(x, axis=-1, keepdims=True)
    xc = x - mu
    var = jnp.mean(xc * xc, axis=-1, keepdims=True)
    return xc * lax.rsqrt(var + LN_EPS) * g + b


def _sigmoid(x):
    return 1.0 / (1.0 + jnp.exp(-x))


def _log_sigmoid(x):
    return jnp.minimum(x, 0.0) - jnp.log(1.0 + jnp.exp(-jnp.abs(x)))


ROW_TILE = (8, 128)


def _tiled_rows(n):
    return (n * ROW_TILE[0], ROW_TILE[1])


def _row_tile(ref, i):
    return ref.at[pl.ds(pl.multiple_of(i * ROW_TILE[0], ROW_TILE[0]), ROW_TILE[0])]


def _store_row_tiles(ref, x, row0=0):
    n = x.shape[0]
    for j in range(ROW_TILE[0]):
        ref[pl.ds(row0 * ROW_TILE[0] + j, n, stride=ROW_TILE[0]), :] = x[:, j * ROW_TILE[1]:(j + 1) * ROW_TILE[1]]


def _load_row_tiles(ref, row0=0, n=None):
    n = ref.shape[0] // ROW_TILE[0] if n is None else n
    return jnp.concatenate([ref[pl.ds(row0 * ROW_TILE[0] + j, n, stride=ROW_TILE[0]), :]
                            for j in range(ROW_TILE[0])], axis=1)


def _tri_masks(n):
    r = lax.broadcasted_iota(I32, (n, n), 0)
    c = lax.broadcasted_iota(I32, (n, n), 1)
    return r >= c, r > c, r == c


def _ln_inproj_kernel(x_ref, g_ref, b_ref, w_ref, h_ref, pg_ref, pr_ref, *, tiles_per_seq):
    i = pl.program_id(0)
    y = _layer_norm(x_ref[...], g_ref[...], b_ref[...])
    row = lax.broadcasted_iota(I32, (y.shape[0], 1), 0)
    is_front = jnp.logical_and(i % tiles_per_seq == 0, row < N_FRONT)
    y = jnp.where(is_front, 0.0, y)
    h_ref[...] = y
    p = _mm(y, w_ref[...])
    pg_ref[...] = p[:, :GLA_COLS]
    pr_ref[...] = p[:, GLA_COLS:]


def _ln_inproj(hcat, g, b, w, lp):
    tp = hcat.shape[0]
    tiles_per_seq = 1
    for cand in range(1, lp // 8 + 1):
        if lp % cand == 0 and (lp // cand) % 8 == 0 and lp // cand >= N_FRONT and lp // cand <= 384:
            tiles_per_seq = cand
            break
    tm = lp // tiles_per_seq
    ncols = GLA_COLS + RWKV_COLS
    return pl.pallas_call(
        functools.partial(_ln_inproj_kernel, tiles_per_seq=tiles_per_seq),
        name="ln_inproj",
        grid=(tp // tm,),
        in_specs=[
            pl.BlockSpec((tm, D_MODEL), lambda i: (i, 0)),
            pl.BlockSpec((1, D_MODEL), lambda i: (0, 0)),
            pl.BlockSpec((1, D_MODEL), lambda i: (0, 0)),
            pl.BlockSpec((D_MODEL, ncols), lambda i: (0, 0)),
        ],
        out_specs=[
            pl.BlockSpec((tm, D_MODEL), lambda i: (i, 0)),
            pl.BlockSpec((tm, GLA_COLS), lambda i: (i, 0)),
            pl.BlockSpec((tm, RWKV_COLS), lambda i: (i, 0)),
        ],
        out_shape=[
            jax.ShapeDtypeStruct((tp, D_MODEL), F32),
            jax.ShapeDtypeStruct((tp, GLA_COLS), F32),
            jax.ShapeDtypeStruct((tp, RWKV_COLS), F32),
        ],
        compiler_params=pltpu.CompilerParams(dimension_semantics=("arbitrary",),
                                             vmem_limit_bytes=VMEM_LIMIT),
    )(hcat, g, b, w)


GLA_GROUP = 11


def _gla_kernel(pg_ref, w2_ref, gkb_ref, ng_ref, o_ref, st_ref, qe_ref, oi_ref, kvt_ref, el_ref):
    c = pl.program_id(1)

    @pl.when(c == 0)
    def _():
        st_ref[...] = jnp.zeros_like(st_ref)

    rows_n = pg_ref.shape[1]
    ng = rows_n // CHUNK
    g_off = 2 * GLA_KEY + GLA_WIDTH
    p = pg_ref[0]
    gl = p[:, g_off + GLA_WIDTH:]
    lg = _log_sigmoid(_mm_f32(gl, w2_ref[...]) + gkb_ref[...]) * (1.0 / GLA_TAU)
    row = lax.broadcasted_iota(I32, (rows_n, 1), 0)
    lg = jnp.where(jnp.logical_and(c == 0, row < N_FRONT), 0.0, lg)
    incl, _, _ = _tri_masks(CHUNK)
    tril = incl.astype(BF16)
    bc = jnp.concatenate([_mm_split3(tril, lg[i * CHUNK:(i + 1) * CHUNK]) for i in range(ng)], axis=0)
    g3 = lambda t: t.reshape(ng, CHUNK, t.shape[-1])
    bc = g3(bc)
    b_last = bc[:, CHUNK - 1:CHUNK, :]
    k = g3(p[:, GLA_KEY:2 * GLA_KEY])
    qe = g3(p[:, 0:GLA_KEY]) * (GLA_DK ** -0.5) * jnp.exp(bc)
    ke = k * jnp.exp(-bc)
    kl = k * jnp.exp(b_last - bc)
    e_last = jnp.exp(b_last)
    v = g3(p[:, 2 * GLA_KEY:g_off]).astype(BF16)
    for h in range(GLA_HEADS):
        ks = slice(h * GLA_DK, (h + 1) * GLA_DK)
        vs = slice(h * GLA_DV, (h + 1) * GLA_DV)
        qh = qe[:, :, ks].astype(BF16)
        a = jnp.einsum('gtd,gsd->gts', qh, ke[:, :, ks].astype(BF16), preferred_element_type=F32)
        a = jnp.where(incl, a, 0.0).astype(BF16)
        oi_ref[:, h] = jnp.einsum('gts,gsv->gtv', a, v[:, :, vs], preferred_element_type=F32)
        kvt_ref[:, h] = jnp.einsum('gtv,gtd->gvd', v[:, :, vs], kl[:, :, ks].astype(BF16),
                                   preferred_element_type=F32)
        qe_ref[:, h] = qh
        el_ref[:, h] = e_last[:, :, ks]

    def chunk_body(ci, carry):
        st = st_ref[...]
        oi_ref[ci] = oi_ref[ci] + jnp.einsum('htd,hvd->htv', qe_ref[ci], st.astype(BF16),
                                             preferred_element_type=F32)
        st_ref[...] = st * el_ref[ci] + kvt_ref[ci]
        return carry

    lax.fori_loop(0, ng, chunk_body, 0)

    for ci in range(ng):
        o = oi_ref[ci]
        o = o * lax.rsqrt(jnp.mean(o * o, axis=-1, keepdims=True) + RMS_EPS) * ng_ref[...]
        o = jnp.concatenate([o[h] for h in range(GLA_HEADS)], axis=1)
        rows = slice(ci * CHUNK, (ci + 1) * CHUNK)
        gate = pg_ref[0, rows, g_off:g_off + GLA_WIDTH]
        o_ref[0, rows, :] = o * (gate * _sigmoid(gate))


def _gla(pg, w2p, gkb, ng_w):
    bn, lp, _ = pg.shape
    nc = lp // CHUNK
    ng = max(d for d in range(1, GLA_GROUP + 1) if nc % d == 0)
    rows = ng * CHUNK
    hh = GLA_HEADS
    return pl.pallas_call(
        _gla_kernel,
        name="gla_mixer",
        grid=(bn, nc // ng),
        in_specs=[
            pl.BlockSpec((1, rows, GLA_COLS), lambda b, c: (b, c, 0)),
            pl.BlockSpec((128, GLA_KEY), lambda b, c: (0, 0)),
            pl.BlockSpec((1, GLA_KEY), lambda b, c: (0, 0)),
            pl.BlockSpec((1, GLA_DV), lambda b, c: (0, 0)),
        ],
        out_specs=pl.BlockSpec((1, rows, GLA_WIDTH), lambda b, c: (b, c, 0)),
        out_shape=jax.ShapeDtypeStruct((bn, lp, GLA_WIDTH), F32),
        scratch_shapes=[
            pltpu.VMEM((hh, GLA_DV, GLA_DK), F32),
            pltpu.VMEM((ng, hh, CHUNK, GLA_DK), BF16),
            pltpu.VMEM((ng, hh, CHUNK, GLA_DV), F32),
            pltpu.VMEM((ng, hh, GLA_DV, GLA_DK), F32),
            pltpu.VMEM((ng, hh, 1, GLA_DK), F32),
        ],
        compiler_params=pltpu.CompilerParams(dimension_semantics=("arbitrary", "arbitrary"),
                                             vmem_limit_bytes=VMEM_LIMIT),
    )(pg, w2p, gkb, ng_w)


RWKV_GROUP = 11
def _mm_split3(ones_bf16, x):
    hi = x.astype(BF16)
    r1 = x - hi.astype(F32)
    mid = r1.astype(BF16)
    lo = (r1 - mid.astype(F32)).astype(BF16)
    dot = lambda t: jnp.dot(ones_bf16, t, preferred_element_type=F32)
    return dot(hi) + dot(mid) + dot(lo)


def _mm_split3_rhs_ones(x, ones_bf16):
    hi = x.astype(BF16)
    r1 = x - hi.astype(F32)
    mid = r1.astype(BF16)
    lo = (r1 - mid.astype(F32)).astype(BF16)
    dot = lambda t: jnp.dot(t, ones_bf16, preferred_element_type=F32)
    return dot(hi) + dot(mid) + dot(lo)


def _rwkv_kernel(pr_ref, pv_ref, mu_ref, w0_ref, w2_ref, a0_ref, a2_ref, g2_ref, kk_ref, ka_ref,
                 rk_ref, lng_ref, lnb_ref, seg_ref, o_ref, s_ref, hd_ref, wend_ref, rm_ref, yn_ref,
                 y_ref, bonus_ref, gate_ref):
    c = pl.program_id(1)
    rows_n = pr_ref.shape[1]
    ng = rows_n // CHUNK

    @pl.when(c == 0)
    def _():
        s_ref[...] = jnp.zeros_like(s_ref)

    p = pr_ref[0]
    prev_row = jnp.where(c > 0, pv_ref[0][7:8, :], 0.0)
    row = lax.broadcasted_iota(I32, (rows_n, 1), 0)
    prev = jnp.where(row == 0, prev_row, pltpu.roll(p, 1, 0))
    p = p + (prev - p) * mu_ref[...]
    W = RWKV_WIDTH
    r = p[:, 0:W]
    k = p[:, W:2 * W]
    v = p[:, 2 * W:3 * W]
    w_low = p[:, 3 * W:3 * W + RWKV_W_RANK]
    a_low = p[:, 3 * W + RWKV_W_RANK:3 * W + RWKV_W_RANK + RWKV_A_RANK]
    g_low = p[:, 3 * W + RWKV_W_RANK + RWKV_A_RANK:]
    wx = w0_ref[...] + _mm(jnp.tanh(w_low), w2_ref[...])
    w = _log_sigmoid(wx) - 0.5
    logd = -jnp.exp(w)
    a = _sigmoid(a0_ref[...] + _mm(a_low, a2_ref[...]))
    g = _mm(_sigmoid(g_low), g2_ref[...])
    seg = seg_ref[...]
    kk = k * kk_ref[...]
    kk = kk / jnp.maximum(jnp.sqrt(_mm_split3_rhs_ones(kk * kk, seg)), 1e-12)
    k = k * (1.0 + (a - 1.0) * ka_ref[...])
    bonus = _mm_split3_rhs_ones(r * k * rk_ref[...], seg) * v

    bonus_ref[...] = bonus
    gate_ref[...] = g

    incl, strict, diag = _tri_masks(CHUNK)
    tril = incl.astype(BF16)
    c_in = jnp.concatenate([_mm_split3(tril, logd[i * CHUNK:(i + 1) * CHUNK]) for i in range(ng)], axis=0)
    g3 = lambda t: t.reshape(ng, CHUNK, W)
    logd, c_in, r, k, v, kk, a = g3(logd), g3(c_in), g3(r), g3(k), g3(v), g3(kk), g3(a)
    c_last = c_in[:, CHUNK - 1:CHUNK, :]
    e_neg = jnp.exp(-c_in)
    e_end = jnp.exp(c_last - c_in)
    kka = kk * a
    per_head = (-kk * jnp.exp(c_in - logd), r * jnp.exp(c_in), kka * e_neg, k * e_neg, kka * e_end,
                k * e_end, v)
    for i, t in enumerate(per_head):
        t = t.astype(BF16)
        for h in range(RWKV_HEADS):
            hd_ref[i, h] = t[:, :, h * RWKV_HEAD:(h + 1) * RWKV_HEAD]
    w_end = jnp.exp(c_last)
    for h in range(RWKV_HEADS):
        wend_ref[h] = w_end[:, :, h * RWKV_HEAD:(h + 1) * RWKV_HEAD]

    eye = diag.astype(F32)
    bmm = lambda x, y: jnp.einsum('gts,gsd->gtd', x.astype(BF16), y.astype(BF16), preferred_element_type=F32)
    bmm_nt = lambda x, y: jnp.einsum('gtd,gsd->gts', x.astype(BF16), y.astype(BF16), preferred_element_type=F32)
    bmm_tn = lambda x, y: jnp.einsum('gtk,gtd->gkd', x.astype(BF16), y.astype(BF16), preferred_element_type=F32)

    def head_body(h, carry):
        at, rt, bt, kt, bh, kh, vh = [hd_ref[i, h] for i in range(7)]
        gm = bmm_nt(jnp.concatenate([at, rt], axis=1), jnp.concatenate([bt, kt], axis=1))
        a_ab = jnp.where(strict, gm[:, :CHUNK, :CHUNK], 0.0)
        a_ak = jnp.where(strict, gm[:, :CHUNK, CHUNK:], 0.0)
        a_rb = jnp.where(incl, gm[:, CHUNK:, :CHUNK], 0.0)
        a_rk = jnp.where(incl, gm[:, CHUNK:, CHUNK:], 0.0)
        akv = bmm(jnp.concatenate([a_ak, a_rk], axis=1), vh)
        xk = eye + a_ab
        pk = bmm(a_ab, a_ab)
        for _ in range(4):
            both = bmm(jnp.concatenate([xk, pk], axis=1), pk)
            xk = xk + both[:, :CHUNK]
            pk = both[:, CHUNK:]
        xk = xk + bmm(xk, pk)
        pq = bmm(xk, jnp.concatenate([at.astype(F32), akv[:, :CHUNK]], axis=2))
        ry = bmm(a_rb, pq) + jnp.concatenate([rt.astype(F32), akv[:, CHUNK:]], axis=2)
        mn = bmm_tn(bh, pq) + jnp.concatenate([eye * wend_ref[h], bmm_tn(kh, vh)], axis=2)
        rm_ref[:, h] = jnp.concatenate([ry[:, :, :RWKV_HEAD], mn[:, :, :RWKV_HEAD]], axis=1).astype(BF16)
        yn_ref[:, h] = jnp.concatenate([ry[:, :, RWKV_HEAD:], mn[:, :, RWKV_HEAD:]], axis=1)
        return carry

    lax.fori_loop(0, RWKV_HEADS, head_body, 0)

    def chunk_body(ci, carry):
        res = jnp.einsum('hmk,hkv->hmv', rm_ref[ci], s_ref[...].astype(BF16),
                         preferred_element_type=F32) + yn_ref[ci]
        y_ref[ci] = res[:, :CHUNK]
        s_ref[...] = res[:, CHUNK:]
        return carry

    lax.fori_loop(0, ng, chunk_body, 0)

    for ci in range(ng):
        y = y_ref[ci]
        mean = jnp.mean(y, axis=-1, keepdims=True)
        yc = y - mean
        var = jnp.mean(yc * yc, axis=-1, keepdims=True)
        yn = yc * lax.rsqrt(var + RWKV_LN_EPS)
        yn = jnp.concatenate([yn[h] for h in range(RWKV_HEADS)], axis=1)
        rows = slice(ci * CHUNK, (ci + 1) * CHUNK)
        o_ref[0, rows, :] = (yn * lng_ref[...] + lnb_ref[...] + bonus_ref[rows, :]) * gate_ref[rows, :]


def _rwkv(pr, mu, w0, w2, a0, a2, g2, k_k, k_a, r_k, ln_g, ln_b, seg):
    bn, lp, _ = pr.shape
    nc = lp // CHUNK
    ng = max(d for d in range(1, RWKV_GROUP + 1) if nc % d == 0)
    rows = ng * CHUNK
    vec = lambda n: pl.BlockSpec((1, n), lambda b, c: (0, 0))
    mat = lambda m, n: pl.BlockSpec((m, n), lambda b, c: (0, 0))
    hh, hd = RWKV_HEADS, RWKV_HEAD
    return pl.pallas_call(
        _rwkv_kernel,
        name="rwkv_mixer",
        grid=(bn, nc // ng),
        in_specs=[
            pl.BlockSpec((1, rows, RWKV_COLS), lambda b, c: (b, c, 0)),
            pl.BlockSpec((1, 8, RWKV_COLS), lambda b, c: (b, jnp.maximum(c * (rows // 8) - 1, 0), 0)),
            vec(RWKV_COLS), vec(RWKV_WIDTH), mat(RWKV_W_RANK, RWKV_WIDTH), vec(RWKV_WIDTH),
            mat(RWKV_A_RANK, RWKV_WIDTH), mat(RWKV_G_RANK, RWKV_WIDTH), vec(RWKV_WIDTH), vec(RWKV_WIDTH),
            vec(RWKV_WIDTH), vec(RWKV_WIDTH), vec(RWKV_WIDTH), mat(RWKV_WIDTH, RWKV_WIDTH),
        ],
        out_specs=pl.BlockSpec((1, rows, RWKV_WIDTH), lambda b, c: (b, c, 0)),
        out_shape=jax.ShapeDtypeStruct((bn, lp, RWKV_WIDTH), F32),
        scratch_shapes=[
            pltpu.VMEM((hh, hd, hd), F32),
            pltpu.VMEM((7, hh, ng, CHUNK, hd), BF16),
            pltpu.VMEM((hh, ng, 1, hd), F32),
            pltpu.VMEM((ng, hh, 2 * CHUNK, hd), BF16),
            pltpu.VMEM((ng, hh, 2 * CHUNK, hd), F32),
            pltpu.VMEM((ng, hh, CHUNK, hd), F32),
            pltpu.VMEM((rows, RWKV_WIDTH), F32),
            pltpu.VMEM((rows, RWKV_WIDTH), F32),
        ],
        compiler_params=pltpu.CompilerParams(dimension_semantics=("arbitrary", "arbitrary"),
                                             vmem_limit_bytes=VMEM_LIMIT),
    )(pr, pr, mu, w0, w2, a0, a2, g2, k_k, k_a, r_k, ln_g, ln_b, seg.astype(BF16))


def _outproj_router_kernel(yg_ref, yr_ref, h_ref, wo_ref, g_ref, b_ref, rwt_ref, rb_ref, tri_ref,
                           h1_ref, idx_ref, gate_ref, rank_ref, cnt_ref, base_ref):
    i = pl.program_id(0)

    @pl.when(i == 0)
    def _():
        base_ref[...] = jnp.zeros_like(base_ref)

    wo = wo_ref[...]
    mix = _mm(yg_ref[...], wo[:GLA_WIDTH]) + _mm(yr_ref[...], wo[GLA_WIDTH:])
    h1 = _layer_norm(DEEPNORM_ALPHA * h_ref[...] + mix, g_ref[...], b_ref[...])
    _store_row_tiles(h1_ref, h1)
    work = lax.dot_general(rwt_ref[...], h1, (((1,), (1,)), ((), ())), preferred_element_type=F32,
                           precision=HIGHEST) + rb_ref[...][:, 0:1]
    tm = work.shape[1]
    e_iota = lax.broadcasted_iota(I32, (N_EXPERTS, tm), 0)
    base = base_ref[...][:, 0:1]
    prior = jnp.zeros((N_EXPERTS, 1), F32)
    tri = tri_ref[...]
    vals = []
    for kk in range(TOP_K):
        m = jnp.max(work, axis=0, keepdims=True)
        sel = jnp.min(jnp.where(work == m, e_iota, N_EXPERTS), axis=0, keepdims=True)
        onehot = e_iota == sel
        work = jnp.where(onehot, -jnp.inf, work)
        oh = onehot.astype(F32)
        cnt = jnp.dot(oh.astype(BF16), tri, preferred_element_type=F32)
        rank = jnp.sum(oh * (base + prior + cnt - 1.0), axis=0, keepdims=True)
        prior = prior + cnt[:, tm - 1:tm]
        vals.append(m)
        idx_ref[kk:kk + 1, :] = sel
        rank_ref[kk:kk + 1, :] = rank.astype(I32)
    es = [jnp.exp(vv - vals[0]) for vv in vals]
    den = es[0] + es[1] + es[2] + es[3]
    for kk in range(TOP_K):
        gate_ref[kk:kk + 1, :] = es[kk] / den
    new_base = base + prior
    base_ref[...] = jnp.broadcast_to(new_base, base_ref.shape)
    cnt_ref[...] = jnp.broadcast_to(new_base, cnt_ref.shape)


def _outproj_router(yg, yr, h, wo, g, b, rwt, rb, tri):
    tp = h.shape[0]
    tm = ROUTER_TILE
    const = lambda m, n: pl.BlockSpec((m, n), lambda i: (0, 0))
    return pl.pallas_call(
        _outproj_router_kernel,
        name="outproj_router",
        grid=(tp // tm,),
        in_specs=[
            pl.BlockSpec((tm, GLA_WIDTH), lambda i: (i, 0)),
            pl.BlockSpec((tm, RWKV_WIDTH), lambda i: (i, 0)),
            pl.BlockSpec((tm, D_MODEL), lambda i: (i, 0)),
            const(D_MODEL, D_MODEL), const(1, D_MODEL), const(1, D_MODEL),
            const(N_EXPERTS, D_MODEL), const(N_EXPERTS, 128), const(tm, tm),
        ],
        out_specs=[
            pl.BlockSpec(_tiled_rows(tm), lambda i: (i, 0)),
            pl.BlockSpec((TOP_K, tm), lambda i: (0, i)),
            pl.BlockSpec((TOP_K, tm), lambda i: (0, i)),
            pl.BlockSpec((TOP_K, tm), lambda i: (0, i)),
            pl.BlockSpec((N_EXPERTS, 128), lambda i: (0, 0)),
        ],
        out_shape=[
            jax.ShapeDtypeStruct(_tiled_rows(tp), F32),
            jax.ShapeDtypeStruct((TOP_K, tp), I32),
            jax.ShapeDtypeStruct((TOP_K, tp), F32),
            jax.ShapeDtypeStruct((TOP_K, tp), I32),
            jax.ShapeDtypeStruct((N_EXPERTS, 128), F32),
        ],
        scratch_shapes=[pltpu.VMEM((N_EXPERTS, 128), F32)],
        compiler_params=pltpu.CompilerParams(dimension_semantics=("arbitrary",),
                                             vmem_limit_bytes=VMEM_LIMIT),
    )(yg, yr, h, wo, g, b, rwt, rb, tri)


def _zero_blocks_kernel(last_ref, o_ref):
    o_ref[...] = jnp.zeros_like(o_ref)


def _zero_blocks(last_block, n_slots):
    return pl.pallas_call(
        _zero_blocks_kernel,
        name="moe_zero_blocks",
        grid_spec=pltpu.PrefetchScalarGridSpec(
            num_scalar_prefetch=1,
            grid=(N_EXPERTS,),
            in_specs=[],
            out_specs=pl.BlockSpec(_tiled_rows(MOE_BLOCK), lambda e, last: (last[e], 0)),
        ),
        out_shape=jax.ShapeDtypeStruct(_tiled_rows(n_slots), F32),
        compiler_params=pltpu.CompilerParams(dimension_semantics=("arbitrary",)),
    )(last_block)


def _dispatch_kernel(pos_ref, x_ref, xs_in_ref, xs_ref, sem):
    del xs_in_ref
    tm = pos_ref.shape[1]

    def start(r, carry):
        for kk in range(TOP_K):
            pltpu.make_async_copy(_row_tile(x_ref, r), _row_tile(xs_ref, pos_ref[kk, r]),
                                  sem).start(priority=kk % 2)
        return carry

    lax.fori_loop(0, tm, start, 0, unroll=DMA_UNROLL)
    for kk in range(TOP_K):
        pltpu.make_async_copy(x_ref, xs_ref.at[pl.ds(0, x_ref.shape[0])], sem).wait()


def _dispatch(pos, x, xs):
    tp = pos.shape[1]
    tm = ROUTER_TILE
    return pl.pallas_call(
        _dispatch_kernel,
        name="moe_dispatch",
        grid=(tp // tm,),
        in_specs=[
            pl.BlockSpec((TOP_K, tm), lambda i: (0, i), memory_space=pltpu.SMEM),
            pl.BlockSpec(_tiled_rows(tm), lambda i: (i, 0)),
            pl.BlockSpec(memory_space=pl.ANY),
        ],
        out_specs=pl.BlockSpec(memory_space=pl.ANY),
        out_shape=jax.ShapeDtypeStruct(xs.shape, xs.dtype),
        scratch_shapes=[pltpu.SemaphoreType.DMA(())],
        input_output_aliases={2: 0},
        compiler_params=pltpu.CompilerParams(dimension_semantics=("arbitrary",)),
    )(pos, x, xs)


SPLIT_TILE = 256


def _split_up_kernel(w_ref, perm_ref, wg_ref, wl_ref):
    half = SPLIT_TILE // 2
    for t in range(w_ref.shape[2] // SPLIT_TILE):
        d = jnp.dot(w_ref[0, :, t * SPLIT_TILE:(t + 1) * SPLIT_TILE].astype(BF16), perm_ref[...],
                    preferred_element_type=F32).astype(BF16)
        wg_ref[0, :, t * half:(t + 1) * half] = d[:, :half]
        wl_ref[0, :, t * half:(t + 1) * half] = d[:, half:]


def _split_up_weights(w_up):
    ne, dm, two_ff = w_up.shape
    cols = 1024
    src = jnp.arange(SPLIT_TILE, dtype=I32)[:, None]
    dst = jnp.arange(SPLIT_TILE, dtype=I32)[None, :]
    half = SPLIT_TILE // 2
    perm = (src == jnp.where(dst < half, 2 * dst, 2 * (dst - half) + 1)).astype(BF16)
    return pl.pallas_call(
        _split_up_kernel,
        name="moe_split_up_weights",
        grid=(ne, two_ff // cols),
        in_specs=[
            pl.BlockSpec((1, dm, cols), lambda e, j: (e, 0, j)),
            pl.BlockSpec((SPLIT_TILE, SPLIT_TILE), lambda e, j: (0, 0)),
        ],
        out_specs=[
            pl.BlockSpec((1, dm, cols // 2), lambda e, j: (e, 0, j)),
            pl.BlockSpec((1, dm, cols // 2), lambda e, j: (e, 0, j)),
        ],
        out_shape=[
            jax.ShapeDtypeStruct((ne, dm, two_ff // 2), BF16),
            jax.ShapeDtypeStruct((ne, dm, two_ff // 2), BF16),
        ],
        compiler_params=pltpu.CompilerParams(dimension_semantics=("arbitrary", "arbitrary")),
    )(w_up, perm)


def _moe_kernel(be_ref, nu_ref, xs_ref, wg_ref, wl_ref, wd_ref, bg_ref, bl_ref, bd_ref, ys_ref):
    @pl.when(pl.program_id(0) < nu_ref[0])
    def _():
        x = _load_row_tiles(xs_ref).astype(BF16)
        x_glu = jnp.dot(x, wg_ref[0], preferred_element_type=F32) + bg_ref[0]
        x_lin = jnp.dot(x, wl_ref[0], preferred_element_type=F32) + bl_ref[0]
        x_glu = jnp.minimum(x_glu, SWIGLU_LIMIT)
        x_lin = jnp.clip(x_lin, -SWIGLU_LIMIT, SWIGLU_LIMIT)
        act = x_glu * _sigmoid(SWIGLU_ALPHA * x_glu) * (x_lin + 1.0)
        _store_row_tiles(ys_ref, jnp.dot(act.astype(BF16), wd_ref[0], preferred_element_type=F32) + bd_ref[0])


def _moe(block_e, n_used, xs, wg, wl, wd, bg, bl, bd):
    n_slots = xs.shape[0] // ROW_TILE[0]
    nb = n_slots // MOE_BLOCK
    blk = lambda i, be, nu: (jnp.maximum(jnp.minimum(i, nu[0] - 1), 0), 0)
    wspec = pl.BlockSpec((1, D_MODEL, D_FF), lambda i, be, nu: (be[i], 0, 0))
    bspec = pl.BlockSpec((1, 1, D_FF), lambda i, be, nu: (be[i], 0, 0))
    return pl.pallas_call(
        _moe_kernel,
        name="moe_experts",
        grid_spec=pltpu.PrefetchScalarGridSpec(
            num_scalar_prefetch=2,
            grid=(nb,),
            in_specs=[pl.BlockSpec(_tiled_rows(MOE_BLOCK), blk), wspec, wspec, wspec, bspec, bspec, bspec],
            out_specs=pl.BlockSpec(_tiled_rows(MOE_BLOCK), blk),
        ),
        out_shape=jax.ShapeDtypeStruct(_tiled_rows(n_slots), F32),
        compiler_params=pltpu.CompilerParams(dimension_semantics=("arbitrary",),
                                             vmem_limit_bytes=VMEM_LIMIT),
    )(block_e, n_used, xs, wg, wl, wd, bg, bl, bd)


def _combine_kernel(pos_ref, posn_ref, gt_ref, h1_ref, g_ref, b_ref, ys_ref, o_ref, buf, sem):
    s = pl.program_id(0)
    slot = s % 2
    rows = TOP_K * CHUNK

    def issue(p_ref, to_slot):
        def body(r, carry):
            for kk in range(TOP_K):
                pltpu.make_async_copy(_row_tile(ys_ref, p_ref[0, kk, r]),
                                      _row_tile(buf, to_slot * rows + kk * CHUNK + r),
                                      sem.at[to_slot]).start(priority=kk % 2)
            return carry

        lax.fori_loop(0, CHUNK, body, 0, unroll=DMA_UNROLL)

    @pl.when(s == 0)
    def _():
        issue(pos_ref, 0)

    @pl.when(s + 1 < pl.num_programs(0))
    def _():
        issue(posn_ref, 1 - slot)

    base = pl.multiple_of(slot * rows, rows)
    span = rows * ROW_TILE[0]
    pltpu.make_async_copy(ys_ref.at[pl.ds(0, span)], buf.at[pl.ds(pl.multiple_of(base * ROW_TILE[0], span), span)],
                          sem.at[slot]).wait()
    gt = gt_ref[...]
    ffn = _load_row_tiles(buf, base, CHUNK) * gt[:, 0:1]
    for kk in range(1, TOP_K):
        ffn = ffn + _load_row_tiles(buf, pl.multiple_of(base + kk * CHUNK, CHUNK), CHUNK) * gt[:, kk:kk + 1]
    o_ref[0] = _layer_norm(DEEPNORM_ALPHA * _load_row_tiles(h1_ref) + ffn, g_ref[...], b_ref[...])


def _combine(pos3, gates_t, h1, g, b, ys, bn, lp):
    seq = lp - CHUNK
    nc = lp // CHUNK
    ncs = seq // CHUNK
    n_steps = bn * ncs
    chunk_of = lambda s: (s // ncs) * nc + s % ncs + 1
    nxt = lambda s: jnp.minimum(s + 1, n_steps - 1)
    return pl.pallas_call(
        _combine_kernel,
        name="moe_combine",
        grid=(n_steps,),
        in_specs=[
            pl.BlockSpec((1, TOP_K, CHUNK), lambda s: (chunk_of(s), 0, 0), memory_space=pltpu.SMEM),
            pl.BlockSpec((1, TOP_K, CHUNK), lambda s: (chunk_of(nxt(s)), 0, 0), memory_space=pltpu.SMEM),
            pl.BlockSpec((CHUNK, TOP_K), lambda s: (chunk_of(s), 0)),
            pl.BlockSpec(_tiled_rows(CHUNK), lambda s: (chunk_of(s), 0)),
            pl.BlockSpec((1, D_MODEL), lambda s: (0, 0)),
            pl.BlockSpec((1, D_MODEL), lambda s: (0, 0)),
            pl.BlockSpec(memory_space=pl.ANY),
        ],
        out_specs=pl.BlockSpec((1, CHUNK, D_MODEL), lambda s: (s // ncs, s % ncs, 0)),
        out_shape=jax.ShapeDtypeStruct((bn, seq, D_MODEL), F32),
        scratch_shapes=[pltpu.VMEM(_tiled_rows(2 * TOP_K * CHUNK), F32), pltpu.SemaphoreType.DMA((2,))],
        compiler_params=pltpu.CompilerParams(dimension_semantics=("arbitrary",)),
    )(pos3, pos3, gates_t, h1, g, b, ys)


def kernel(x, meta, ln_in_g, ln_in_b, w_in, gla_gk_w2, gla_gk_b, gla_norm_g, rwkv_mu, rwkv_w0, rwkv_w2, rwkv_a0, rwkv_a2, rwkv_g2, rwkv_k_k, rwkv_k_a, rwkv_r_k, rwkv_ln_g, rwkv_ln_b, w_out, ln1_g, ln1_b, router_w, router_b, exp_w_up, exp_b_up, exp_w_down, exp_b_down, ln2_g, ln2_b):
    bn, seq, _ = x.shape
    assert seq % CHUNK == 0
    lp = seq + CHUNK
    tp = bn * lp
    assert tp % ROUTER_TILE == 0
    row = lambda t: t.reshape(1, -1).astype(F32)

    hcat = jnp.concatenate([jnp.zeros((bn, N_FRONT, D_MODEL), F32),
                            jnp.broadcast_to(meta[None].astype(F32), (bn, N_META, D_MODEL)), x], axis=1)
    hcat = hcat.reshape(tp, D_MODEL)
    gla_in = 2 * GLA_KEY + 2 * GLA_WIDTH + GLA_GATE_RANK
    w = w_in[0]
    w_cols = jnp.concatenate([w[:, :gla_in], jnp.zeros((D_MODEL, 128 - GLA_GATE_RANK), F32), w[:, gla_in:]],
                             axis=1).astype(BF16)
    h, pg, pr = _ln_inproj(hcat, row(ln_in_g), row(ln_in_b), w_cols, lp)

    w2p = jnp.concatenate([gla_gk_w2[0], jnp.zeros((128 - GLA_GATE_RANK, GLA_KEY), F32)], axis=0)
    y_gla = _gla(pg.reshape(bn, lp, GLA_COLS), w2p, row(gla_gk_b[0]), row(gla_norm_g[0]))

    head_id = jnp.arange(RWKV_WIDTH, dtype=I32) // RWKV_HEAD
    seg = (head_id[:, None] == head_id[None, :]).astype(F32)
    y_rwkv = _rwkv(pr.reshape(bn, lp, RWKV_COLS), row(rwkv_mu[0]), row(rwkv_w0[0]), rwkv_w2[0],
                   row(rwkv_a0[0]), rwkv_a2[0], rwkv_g2[0], row(rwkv_k_k[0]), row(rwkv_k_a[0]),
                   row(rwkv_r_k[0]), row(rwkv_ln_g[0]), row(rwkv_ln_b[0]), seg)

    tri = jnp.triu(jnp.ones((ROUTER_TILE, ROUTER_TILE), F32)).astype(BF16)
    rb = jnp.broadcast_to(router_b[0].reshape(N_EXPERTS, 1), (N_EXPERTS, 128))
    h1, idx, gates, rank, cnt = _outproj_router(
        y_gla.reshape(tp, GLA_WIDTH), y_rwkv.reshape(tp, RWKV_WIDTH), h, w_out[0].astype(BF16),
        row(ln1_g[0]), row(ln1_b[0]), router_w[0].T, rb, tri)

    counts = cnt[:, 0].astype(I32)
    padded = (counts + MOE_BLOCK - 1) // MOE_BLOCK * MOE_BLOCK
    ends_p = jnp.cumsum(padded)
    starts_p = ends_p - padded
    e_ids = jnp.arange(N_EXPERTS, dtype=I32)
    start_of = jnp.sum(jnp.where(idx[None] == e_ids[:, None, None], starts_p[:, None, None], 0), axis=0)
    pos = start_of + rank
    nb = tp * TOP_K // MOE_BLOCK + N_EXPERTS
    n_slots = nb * MOE_BLOCK
    block_start = jnp.arange(nb, dtype=I32) * MOE_BLOCK
    block_e = jnp.minimum(jnp.sum((block_start[:, None] >= ends_p[None, :]).astype(I32), axis=1), N_EXPERTS - 1)
    n_used = (ends_p[-1:] // MOE_BLOCK).astype(I32)
    last_block = jnp.maximum(ends_p // MOE_BLOCK - 1, 0).astype(I32)

    xs = _dispatch(pos, h1, _zero_blocks(last_block, n_slots))
    wg, wl = _split_up_weights(exp_w_up[0])
    bg = exp_b_up[0][:, None, 0::2]
    bl = exp_b_up[0][:, None, 1::2]
    ys = _moe(block_e, n_used, xs, wg, wl, exp_w_down[0].astype(BF16), bg, bl, exp_b_down[0][:, None, :])

    pos3 = pos.reshape(TOP_K, tp // CHUNK, CHUNK).transpose(1, 0, 2)
    return _combine(pos3, gates.T, h1, row(ln2_g[0]), row(ln2_b[0]), ys, bn, lp)
```

```python
import functools

import jax
import jax.numpy as jnp
from jax import lax
from jax.experimental import pallas as pl
from jax.experimental.pallas import tpu as pltpu

F32 = jnp.float32
BF16 = jnp.bfloat16
I32 = jnp.int32
HIGHEST = lax.Precision.HIGHEST

D_MODEL = 1024
N_META = 16
CHUNK = 64
N_FRONT = (-N_META) % CHUNK
GLA_HEADS = 4
GLA_DK = 64
GLA_DV = 128
GLA_KEY = GLA_HEADS * GLA_DK
GLA_WIDTH = GLA_HEADS * GLA_DV
GLA_GATE_RANK = 16
GLA_TAU = 16.0
GLA_COLS = 2 * GLA_KEY + 2 * GLA_WIDTH + 128
RWKV_WIDTH = 512
RWKV_HEAD = 64
RWKV_HEADS = RWKV_WIDTH // RWKV_HEAD
RWKV_W_RANK = 64
RWKV_A_RANK = 64
RWKV_G_RANK = 128
RWKV_COLS = 3 * RWKV_WIDTH + RWKV_W_RANK + RWKV_A_RANK + RWKV_G_RANK
N_EXPERTS = 32
TOP_K = 4
D_FF = D_MODEL
SWIGLU_ALPHA = 1.702
SWIGLU_LIMIT = 7.0
MOE_BLOCK = 512
DEPTH = 1
DEEPNORM_ALPHA = (2.0 * DEPTH) ** 0.25
LN_EPS = 1e-5
RWKV_LN_EPS = 64e-5
RMS_EPS = 1e-6

ROUTER_TILE = 768
DMA_UNROLL = 4
VMEM_LIMIT = 56 * 1024 * 1024


def _mm(a, b):
    return jnp.dot(a.astype(BF16), b.astype(BF16), preferred_element_type=F32)


def _mm_nt(a, b):
    return lax.dot_general(a.astype(BF16), b.astype(BF16), (((1,), (1,)), ((), ())),
                           preferred_element_type=F32)


def _mm_tn(a, b):
    return lax.dot_general(a.astype(BF16), b.astype(BF16), (((0,), (0,)), ((), ())),
                           preferred_element_type=F32)


def _mm_f32(a, b):
    return jnp.dot(a, b, preferred_element_type=F32, precision=HIGHEST)


def _layer_norm(x, g, b):
    mu = jnp.mean(x, axis=-1, keepdims=True)
    xc = x - mu
    var = jnp.mean(xc * xc, axis=-1, keepdims=True)
    return xc * lax.rsqrt(var + LN_EPS) * g + b


def _sigmoid(x):
    return 1.0 / (1.0 + jnp.exp(-x))


def _log_sigmoid(x):
    return jnp.minimum(x, 0.0) - jnp.log(1.0 + jnp.exp(-jnp.abs(x)))


ROW_TILE = (8, 128)


def _tiled_rows(n):
    return (n * ROW_TILE[0], ROW_TILE[1])


def _row_tile(ref, i):
    return ref.at[pl.ds(pl.multiple_of(i * ROW_TILE[0], ROW_TILE[0]), ROW_TILE[0])]


def _store_row_tiles(ref, x, row0=0):
    n = x.shape[0]
    for j in range(ROW_TILE[0]):
        ref[pl.ds(row0 * ROW_TILE[0] + j, n, stride=ROW_TILE[0]), :] = x[:, j * ROW_TILE[1]:(j + 1) * ROW_TILE[1]]


def _load_row_tiles(ref, row0=0, n=None):
    n = ref.shape[0] // ROW_TILE[0] if n is None else n
    return jnp.concatenate([ref[pl.ds(row0 * ROW_TILE[0] + j, n, stride=ROW_TILE[0]), :]
                            for j in range(ROW_TILE[0])], axis=1)


def _tri_masks(n):
    r = lax.broadcasted_iota(I32, (n, n), 0)
    c = lax.broadcasted_iota(I32, (n, n), 1)
    return r >= c, r > c, r == c


def _ln_inproj_kernel(x_ref, g_ref, b_ref, w_ref, h_ref, pg_ref, pr_ref, *, tiles_per_seq):
    i = pl.program_id(0)
    y = _layer_norm(x_ref[...], g_ref[...], b_ref[...])
    row = lax.broadcasted_iota(I32, (y.shape[0], 1), 0)
    is_front = jnp.logical_and(i % tiles_per_seq == 0, row < N_FRONT)
    y = jnp.where(is_front, 0.0, y)
    h_ref[...] = y
    p = _mm(y, w_ref[...])
    pg_ref[...] = p[:, :GLA_COLS]
    pr_ref[...] = p[:, GLA_COLS:]


def _ln_inproj(hcat, g, b, w, lp):
    tp = hcat.shape[0]
    tiles_per_seq = 1
    for cand in range(1, lp // 8 + 1):
        if lp % cand == 0 and (lp // cand) % 8 == 0 and lp // cand >= N_FRONT and lp // cand <= 384:
            tiles_per_seq = cand
            break
    tm = lp // tiles_per_seq
    ncols = GLA_COLS + RWKV_COLS
    return pl.pallas_call(
        functools.partial(_ln_inproj_kernel, tiles_per_seq=tiles_per_seq),
        name="ln_inproj",
        grid=(tp // tm,),
        in_specs=[
            pl.BlockSpec((tm, D_MODEL), lambda i: (i, 0)),
            pl.BlockSpec((1, D_MODEL), lambda i: (0, 0)),
            pl.BlockSpec((1, D_MODEL), lambda i: (0, 0)),
            pl.BlockSpec((D_MODEL, ncols), lambda i: (0, 0)),
        ],
        out_specs=[
            pl.BlockSpec((tm, D_MODEL), lambda i: (i, 0)),
            pl.BlockSpec((tm, GLA_COLS), lambda i: (i, 0)),
            pl.BlockSpec((tm, RWKV_COLS), lambda i: (i, 0)),
        ],
        out_shape=[
            jax.ShapeDtypeStruct((tp, D_MODEL), F32),
            jax.ShapeDtypeStruct((tp, GLA_COLS), F32),
            jax.ShapeDtypeStruct((tp, RWKV_COLS), F32),
        ],
        compiler_params=pltpu.CompilerParams(dimension_semantics=("arbitrary",),
                                             vmem_limit_bytes=VMEM_LIMIT),
    )(hcat, g, b, w)


GLA_GROUP = 11


def _gla_kernel(pg_ref, w2_ref, gkb_ref, ng_ref, o_ref, st_ref, qe_ref, oi_ref, kvt_ref, el_ref):
    c = pl.program_id(1)

    @pl.when(c == 0)
    def _():
        st_ref[...] = jnp.zeros_like(st_ref)

    rows_n = pg_ref.shape[1]
    ng = rows_n // CHUNK
    g_off = 2 * GLA_KEY + GLA_WIDTH
    p = pg_ref[0]
    gl = p[:, g_off + GLA_WIDTH:]
    lg = _log_sigmoid(_mm_f32(gl, w2_ref[...]) + gkb_ref[...]) * (1.0 / GLA_TAU)
    row = lax.broadcasted_iota(I32, (rows_n, 1), 0)
    lg = jnp.where(jnp.logical_and(c == 0, row < N_FRONT), 0.0, lg)
    incl, _, _ = _tri_masks(CHUNK)
    tril = incl.astype(BF16)
    bc = jnp.concatenate([_mm_split3(tril, lg[i * CHUNK:(i + 1) * CHUNK]) for i in range(ng)], axis=0)
    g3 = lambda t: t.reshape(ng, CHUNK, t.shape[-1])
    bc = g3(bc)
    b_last = bc[:, CHUNK - 1:CHUNK, :]
    k = g3(p[:, GLA_KEY:2 * GLA_KEY])
    qe = g3(p[:, 0:GLA_KEY]) * (GLA_DK ** -0.5) * jnp.exp(bc)
    ke = k * jnp.exp(-bc)
    kl = k * jnp.exp(b_last - bc)
    e_last = jnp.exp(b_last)
    v = g3(p[:, 2 * GLA_KEY:g_off]).astype(BF16)
    for h in range(GLA_HEADS):
        ks = slice(h * GLA_DK, (h + 1) * GLA_DK)
        vs = slice(h * GLA_DV, (h + 1) * GLA_DV)
        qh = qe[:, :, ks].astype(BF16)
        a = jnp.einsum('gtd,gsd->gts', qh, ke[:, :, ks].astype(BF16), preferred_element_type=F32)
        a = jnp.where(incl, a, 0.0).astype(BF16)
        oi_ref[:, h] = jnp.einsum('gts,gsv->gtv', a, v[:, :, vs], preferred_element_type=F32)
        kvt_ref[:, h] = jnp.einsum('gtv,gtd->gvd', v[:, :, vs], kl[:, :, ks].astype(BF16),
                                   preferred_element_type=F32)
        qe_ref[:, h] = qh
        el_ref[:, h] = e_last[:, :, ks]

    def chunk_body(ci, carry):
        st = st_ref[...]
        oi_ref[ci] = oi_ref[ci] + jnp.einsum('htd,hvd->htv', qe_ref[ci], st.astype(BF16),
                                             preferred_element_type=F32)
        st_ref[...] = st * el_ref[ci] + kvt_ref[ci]
        return carry

    lax.fori_loop(0, ng, chunk_body, 0)

    for ci in range(ng):
        o = oi_ref[ci]
        o = o * lax.rsqrt(jnp.mean(o * o, axis=-1, keepdims=True) + RMS_EPS) * ng_ref[...]
        o = jnp.concatenate([o[h] for h in range(GLA_HEADS)], axis=1)
        rows = slice(ci * CHUNK, (ci + 1) * CHUNK)
        gate = pg_ref[0, rows, g_off:g_off + GLA_WIDTH]
        o_ref[0, rows, :] = o * (gate * _sigmoid(gate))


def _gla(pg, w2p, gkb, ng_w):
    bn, lp, _ = pg.shape
    nc = lp // CHUNK
    ng = max(d for d in range(1, GLA_GROUP + 1) if nc % d == 0)
    rows = ng * CHUNK
    hh = GLA_HEADS
    return pl.pallas_call(
        _gla_kernel,
        name="gla_mixer",
        grid=(bn, nc // ng),
        in_specs=[
            pl.BlockSpec((1, rows, GLA_COLS), lambda b, c: (b, c, 0)),
            pl.BlockSpec((128, GLA_KEY), lambda b, c: (0, 0)),
            pl.BlockSpec((1, GLA_KEY), lambda b, c: (0, 0)),
            pl.BlockSpec((1, GLA_DV), lambda b, c: (0, 0)),
        ],
        out_specs=pl.BlockSpec((1, rows, GLA_WIDTH), lambda b, c: (b, c, 0)),
        out_shape=jax.ShapeDtypeStruct((bn, lp, GLA_WIDTH), F32),
        scratch_shapes=[
            pltpu.VMEM((hh, GLA_DV, GLA_DK), F32),
            pltpu.VMEM((ng, hh, CHUNK, GLA_DK), BF16),
            pltpu.VMEM((ng, hh, CHUNK, GLA_DV), F32),
            pltpu.VMEM((ng, hh, GLA_DV, GLA_DK), F32),
            pltpu.VMEM((ng, hh, 1, GLA_DK), F32),
        ],
        compiler_params=pltpu.CompilerParams(dimension_semantics=("arbitrary", "arbitrary"),
                                             vmem_limit_bytes=VMEM_LIMIT),
    )(pg, w2p, gkb, ng_w)


RWKV_GROUP = 11
RWKV_HEADS_PER_TRIP = 2
def _mm_split3(ones_bf16, x):
    hi = x.astype(BF16)
    r1 = x - hi.astype(F32)
    mid = r1.astype(BF16)
    lo = (r1 - mid.astype(F32)).astype(BF16)
    return jnp.dot(jnp.concatenate([ones_bf16] * 3, axis=1), jnp.concatenate([hi, mid, lo], axis=0),
                   preferred_element_type=F32)


def _mm_split2_rhs_ones(x, ones_bf16):
    hi = x.astype(BF16)
    mid = (x - hi.astype(F32)).astype(BF16)
    dot = lambda t: jnp.dot(t, ones_bf16, preferred_element_type=F32)
    return dot(hi) + dot(mid)


def _rwkv_kernel(pr_ref, pv_ref, mu_ref, w0_ref, w2_ref, a0_ref, a2_ref, g2_ref, kk_ref, ka_ref,
                 rk_ref, lng_ref, lnb_ref, seg_ref, o_ref, s_ref, hd_ref, wend_ref, rm_ref, yn_ref,
                 y_ref, bonus_ref, gate_ref):
    c = pl.program_id(1)
    rows_n = pr_ref.shape[1]
    ng = rows_n // CHUNK

    @pl.when(c == 0)
    def _():
        s_ref[...] = jnp.zeros_like(s_ref)

    p = pr_ref[0]
    prev_row = jnp.where(c > 0, pv_ref[0][7:8, :], 0.0)
    row = lax.broadcasted_iota(I32, (rows_n, 1), 0)
    prev = jnp.where(row == 0, prev_row, pltpu.roll(p, 1, 0))
    p = p + (prev - p) * mu_ref[...]
    W = RWKV_WIDTH
    r = p[:, 0:W]
    k = p[:, W:2 * W]
    v = p[:, 2 * W:3 * W]
    w_low = p[:, 3 * W:3 * W + RWKV_W_RANK]
    a_low = p[:, 3 * W + RWKV_W_RANK:3 * W + RWKV_W_RANK + RWKV_A_RANK]
    g_low = p[:, 3 * W + RWKV_W_RANK + RWKV_A_RANK:]
    wx = w0_ref[...] + _mm(jnp.tanh(w_low), w2_ref[...])
    w = _log_sigmoid(wx) - 0.5
    logd = -jnp.exp(w)
    a = _sigmoid(a0_ref[...] + _mm(a_low, a2_ref[...]))
    g = _mm(_sigmoid(g_low), g2_ref[...])
    seg = seg_ref[...]
    kk = k * kk_ref[...]
    kk = kk * lax.rsqrt(jnp.maximum(_mm_split2_rhs_ones(kk * kk, seg), 1e-24))
    k = k * (1.0 + (a - 1.0) * ka_ref[...])
    bonus = _mm_split2_rhs_ones(r * k * rk_ref[...], seg) * v

    bonus_ref[...] = bonus
    gate_ref[...] = g

    incl, strict, diag = _tri_masks(CHUNK)
    tril = incl.astype(BF16)
    c_in = jnp.concatenate([_mm_split3(tril, logd[i * CHUNK:(i + 1) * CHUNK]) for i in range(ng)], axis=0)
    g3 = lambda t: t.reshape(ng, CHUNK, W)
    logd, c_in, r, k, v, kk, a = g3(logd), g3(c_in), g3(r), g3(k), g3(v), g3(kk), g3(a)
    c_last = c_in[:, CHUNK - 1:CHUNK, :]
    e_neg = jnp.exp(-c_in)
    e_end = jnp.exp(c_last - c_in)
    kka = kk * a
    per_head = (-kk * jnp.exp(c_in - logd), r * jnp.exp(c_in), kka * e_neg, k * e_neg, kka * e_end,
                k * e_end, v)
    for i, t in enumerate(per_head):
        t = t.astype(BF16)
        for h in range(RWKV_HEADS):
            hd_ref[i, h] = t[:, :, h * RWKV_HEAD:(h + 1) * RWKV_HEAD]
    w_end = jnp.exp(c_last)
    for h in range(RWKV_HEADS):
        wend_ref[h] = w_end[:, :, h * RWKV_HEAD:(h + 1) * RWKV_HEAD]

    eye = diag.astype(F32)
    bmm = lambda x, y: jnp.einsum('gts,gsd->gtd', x.astype(BF16), y.astype(BF16), preferred_element_type=F32)
    bmm_nt = lambda x, y: jnp.einsum('gtd,gsd->gts', x.astype(BF16), y.astype(BF16), preferred_element_type=F32)
    bmm_tn = lambda x, y: jnp.einsum('gtk,gtd->gkd', x.astype(BF16), y.astype(BF16), preferred_element_type=F32)

    def solve_head(operands, w_end_h):
        at, rt, bt, kt, bh, kh, vh = operands
        gm = bmm_nt(jnp.concatenate([at, rt], axis=1), jnp.concatenate([bt, kt], axis=1))
        a_ab = jnp.where(strict, gm[:, :CHUNK, :CHUNK], 0.0)
        a_ak = jnp.where(strict, gm[:, :CHUNK, CHUNK:], 0.0)
        a_rb = jnp.where(incl, gm[:, CHUNK:, :CHUNK], 0.0)
        a_rk = jnp.where(incl, gm[:, CHUNK:, CHUNK:], 0.0)
        akv = bmm(jnp.concatenate([a_ak, a_rk], axis=1), vh)
        xk = eye + a_ab
        pk = bmm(a_ab, a_ab)
        for _ in range(4):
            both = bmm(jnp.concatenate([xk, pk], axis=1), pk)
            xk = xk + both[:, :CHUNK]
            pk = both[:, CHUNK:]
        xk = xk + bmm(xk, pk)
        pq = bmm(xk, jnp.concatenate([at.astype(F32), akv[:, :CHUNK]], axis=2))
        ry = bmm(a_rb, pq) + jnp.concatenate([rt.astype(F32), akv[:, CHUNK:]], axis=2)
        mn = bmm_tn(bh, pq) + jnp.concatenate([eye * w_end_h, bmm_tn(kh, vh)], axis=2)
        rm = jnp.concatenate([ry[:, :, :RWKV_HEAD], mn[:, :, :RWKV_HEAD]], axis=1).astype(BF16)
        yn = jnp.concatenate([ry[:, :, RWKV_HEAD:], mn[:, :, RWKV_HEAD:]], axis=1)
        return rm, yn

    def head_group_body(hg, carry):
        heads = [RWKV_HEADS_PER_TRIP * hg + j for j in range(RWKV_HEADS_PER_TRIP)]
        operands = [jnp.concatenate([hd_ref[i, h] for h in heads], axis=0) for i in range(7)]
        rm, yn = solve_head(operands, jnp.concatenate([wend_ref[h] for h in heads], axis=0))
        for j, h in enumerate(heads):
            rm_ref[:, h] = rm[j * ng:(j + 1) * ng]
            yn_ref[:, h] = yn[j * ng:(j + 1) * ng]
        return carry

    lax.fori_loop(0, RWKV_HEADS // RWKV_HEADS_PER_TRIP, head_group_body, 0)

    def chunk_body(ci, carry):
        res = jnp.einsum('hmk,hkv->hmv', rm_ref[ci], s_ref[...].astype(BF16),
                         preferred_element_type=F32) + yn_ref[ci]
        y_ref[ci] = res[:, :CHUNK]
        s_ref[...] = res[:, CHUNK:]
        return carry

    lax.fori_loop(0, ng, chunk_body, 0)

    for ci in range(ng):
        y = y_ref[ci]
        mean = jnp.mean(y, axis=-1, keepdims=True)
        yc = y - mean
        var = jnp.mean(yc * yc, axis=-1, keepdims=True)
        yn = yc * lax.rsqrt(var + RWKV_LN_EPS)
        yn = jnp.concatenate([yn[h] for h in range(RWKV_HEADS)], axis=1)
        rows = slice(ci * CHUNK, (ci + 1) * CHUNK)
        o_ref[0, rows, :] = (yn * lng_ref[...] + lnb_ref[...] + bonus_ref[rows, :]) * gate_ref[rows, :]


def _rwkv(pr, mu, w0, w2, a0, a2, g2, k_k, k_a, r_k, ln_g, ln_b, seg):
    bn, lp, _ = pr.shape
    nc = lp // CHUNK
    ng = max(d for d in range(1, RWKV_GROUP + 1) if nc % d == 0)
    rows = ng * CHUNK
    vec = lambda n: pl.BlockSpec((1, n), lambda b, c: (0, 0))
    mat = lambda m, n: pl.BlockSpec((m, n), lambda b, c: (0, 0))
    hh, hd = RWKV_HEADS, RWKV_HEAD
    return pl.pallas_call(
        _rwkv_kernel,
        name="rwkv_mixer",
        grid=(bn, nc // ng),
        in_specs=[
            pl.BlockSpec((1, rows, RWKV_COLS), lambda b, c: (b, c, 0)),
            pl.BlockSpec((1, 8, RWKV_COLS), lambda b, c: (b, jnp.maximum(c * (rows // 8) - 1, 0), 0)),
            vec(RWKV_COLS), vec(RWKV_WIDTH), mat(RWKV_W_RANK, RWKV_WIDTH), vec(RWKV_WIDTH),
            mat(RWKV_A_RANK, RWKV_WIDTH), mat(RWKV_G_RANK, RWKV_WIDTH), vec(RWKV_WIDTH), vec(RWKV_WIDTH),
            vec(RWKV_WIDTH), vec(RWKV_WIDTH), vec(RWKV_WIDTH), mat(RWKV_WIDTH, RWKV_WIDTH),
        ],
        out_specs=pl.BlockSpec((1, rows, RWKV_WIDTH), lambda b, c: (b, c, 0)),
        out_shape=jax.ShapeDtypeStruct((bn, lp, RWKV_WIDTH), F32),
        scratch_shapes=[
            pltpu.VMEM((hh, hd, hd), F32),
            pltpu.VMEM((7, hh, ng, CHUNK, hd), BF16),
            pltpu.VMEM((hh, ng, 1, hd), F32),
            pltpu.VMEM((ng, hh, 2 * CHUNK, hd), BF16),
            pltpu.VMEM((ng, hh, 2 * CHUNK, hd), F32),
            pltpu.VMEM((ng, hh, CHUNK, hd), F32),
            pltpu.VMEM((rows, RWKV_WIDTH), F32),
            pltpu.VMEM((rows, RWKV_WIDTH), F32),
        ],
        compiler_params=pltpu.CompilerParams(dimension_semantics=("arbitrary", "arbitrary"),
                                             vmem_limit_bytes=VMEM_LIMIT),
    )(pr, pr, mu, w0, w2, a0, a2, g2, k_k, k_a, r_k, ln_g, ln_b, seg.astype(BF16))


def _outproj_router_kernel(yg_ref, yr_ref, h_ref, wo_ref, g_ref, b_ref, rwt_ref, rb_ref, tri_ref,
                           h1_ref, idx_ref, gate_ref, rank_ref, cnt_ref, base_ref):
    i = pl.program_id(0)

    @pl.when(i == 0)
    def _():
        base_ref[...] = jnp.zeros_like(base_ref)

    wo = wo_ref[...]
    mix = _mm(yg_ref[...], wo[:GLA_WIDTH]) + _mm(yr_ref[...], wo[GLA_WIDTH:])
    h1 = _layer_norm(DEEPNORM_ALPHA * h_ref[...] + mix, g_ref[...], b_ref[...])
    _store_row_tiles(h1_ref, h1)
    work = lax.dot_general(rwt_ref[...], h1, (((1,), (1,)), ((), ())), preferred_element_type=F32,
                           precision=HIGHEST) + rb_ref[...][:, 0:1]
    tm = work.shape[1]
    e_iota = lax.broadcasted_iota(I32, (N_EXPERTS, tm), 0)
    base = base_ref[...][:, 0:1]
    prior = jnp.zeros((N_EXPERTS, 1), F32)
    tri = tri_ref[...]
    vals = []
    for kk in range(TOP_K):
        m = jnp.max(work, axis=0, keepdims=True)
        sel = jnp.min(jnp.where(work == m, e_iota, N_EXPERTS), axis=0, keepdims=True)
        onehot = e_iota == sel
        work = jnp.where(onehot, -jnp.inf, work)
        oh = onehot.astype(F32)
        cnt = jnp.dot(oh.astype(BF16), tri, preferred_element_type=F32)
        rank = jnp.sum(oh * (base + prior + cnt - 1.0), axis=0, keepdims=True)
        prior = prior + cnt[:, tm - 1:tm]
        vals.append(m)
        idx_ref[kk:kk + 1, :] = sel
        rank_ref[kk:kk + 1, :] = rank.astype(I32)
    es = [jnp.exp(vv - vals[0]) for vv in vals]
    den = es[0] + es[1] + es[2] + es[3]
    for kk in range(TOP_K):
        gate_ref[kk:kk + 1, :] = es[kk] / den
    new_base = base + prior
    base_ref[...] = jnp.broadcast_to(new_base, base_ref.shape)
    cnt_ref[...] = jnp.broadcast_to(new_base, cnt_ref.shape)


def _outproj_router(yg, yr, h, wo, g, b, rwt, rb, tri):
    tp = h.shape[0]
    tm = ROUTER_TILE
    const = lambda m, n: pl.BlockSpec((m, n), lambda i: (0, 0))
    return pl.pallas_call(
        _outproj_router_kernel,
        name="outproj_router",
        grid=(tp // tm,),
        in_specs=[
            pl.BlockSpec((tm, GLA_WIDTH), lambda i: (i, 0)),
            pl.BlockSpec((tm, RWKV_WIDTH), lambda i: (i, 0)),
            pl.BlockSpec((tm, D_MODEL), lambda i: (i, 0)),
            const(D_MODEL, D_MODEL), const(1, D_MODEL), const(1, D_MODEL),
            const(N_EXPERTS, D_MODEL), const(N_EXPERTS, 128), const(tm, tm),
        ],
        out_specs=[
            pl.BlockSpec(_tiled_rows(tm), lambda i: (i, 0)),
            pl.BlockSpec((TOP_K, tm), lambda i: (0, i)),
            pl.BlockSpec((TOP_K, tm), lambda i: (0, i)),
            pl.BlockSpec((TOP_K, tm), lambda i: (0, i)),
            pl.BlockSpec((N_EXPERTS, 128), lambda i: (0, 0)),
        ],
        out_shape=[
            jax.ShapeDtypeStruct(_tiled_rows(tp), F32),
            jax.ShapeDtypeStruct((TOP_K, tp), I32),
            jax.ShapeDtypeStruct((TOP_K, tp), F32),
            jax.ShapeDtypeStruct((TOP_K, tp), I32),
            jax.ShapeDtypeStruct((N_EXPERTS, 128), F32),
        ],
        scratch_shapes=[pltpu.VMEM((N_EXPERTS, 128), F32)],
        compiler_params=pltpu.CompilerParams(dimension_semantics=("arbitrary",),
                                             vmem_limit_bytes=VMEM_LIMIT),
    )(yg, yr, h, wo, g, b, rwt, rb, tri)


def _zero_blocks_kernel(last_ref, o_ref):
    o_ref[...] = jnp.zeros_like(o_ref)


def _zero_blocks(last_block, n_slots):
    return pl.pallas_call(
        _zero_blocks_kernel,
        name="moe_zero_blocks",
        grid_spec=pltpu.PrefetchScalarGridSpec(
            num_scalar_prefetch=1,
            grid=(N_EXPERTS,),
            in_specs=[],
            out_specs=pl.BlockSpec(_tiled_rows(MOE_BLOCK), lambda e, last: (last[e], 0)),
        ),
        out_shape=jax.ShapeDtypeStruct(_tiled_rows(n_slots), F32),
        compiler_params=pltpu.CompilerParams(dimension_semantics=("arbitrary",)),
    )(last_block)


def _dispatch_kernel(pos_ref, x_ref, xs_in_ref, xs_ref, sem):
    del xs_in_ref
    tm = pos_ref.shape[1]

    def start(r, carry):
        for kk in range(TOP_K):
            pltpu.make_async_copy(_row_tile(x_ref, r), _row_tile(xs_ref, pos_ref[kk, r]),
                                  sem).start(priority=kk % 2)
        return carry

    lax.fori_loop(0, tm, start, 0, unroll=DMA_UNROLL)
    for kk in range(TOP_K):
        pltpu.make_async_copy(x_ref, xs_ref.at[pl.ds(0, x_ref.shape[0])], sem).wait()


def _dispatch(pos, x, xs):
    tp = pos.shape[1]
    tm = ROUTER_TILE
    return pl.pallas_call(
        _dispatch_kernel,
        name="moe_dispatch",
        grid=(tp // tm,),
        in_specs=[
            pl.BlockSpec((TOP_K, tm), lambda i: (0, i), memory_space=pltpu.SMEM),
            pl.BlockSpec(_tiled_rows(tm), lambda i: (i, 0)),
            pl.BlockSpec(memory_space=pl.ANY),
        ],
        out_specs=pl.BlockSpec(memory_space=pl.ANY),
        out_shape=jax.ShapeDtypeStruct(xs.shape, xs.dtype),
        scratch_shapes=[pltpu.SemaphoreType.DMA(())],
        input_output_aliases={2: 0},
        compiler_params=pltpu.CompilerParams(dimension_semantics=("arbitrary",)),
    )(pos, x, xs)


SPLIT_TILE = 256


def _split_up_kernel(w_ref, perm_ref, wg_ref, wl_ref):
    half = SPLIT_TILE // 2
    for t in range(w_ref.shape[2] // SPLIT_TILE):
        d = jnp.dot(w_ref[0, :, t * SPLIT_TILE:(t + 1) * SPLIT_TILE].astype(BF16), perm_ref[...],
                    preferred_element_type=F32).astype(BF16)
        wg_ref[0, :, t * half:(t + 1) * half] = d[:, :half]
        wl_ref[0, :, t * half:(t + 1) * half] = d[:, half:]


def _split_up_weights(w_up):
    ne, dm, two_ff = w_up.shape
    cols = 1024
    src = jnp.arange(SPLIT_TILE, dtype=I32)[:, None]
    dst = jnp.arange(SPLIT_TILE, dtype=I32)[None, :]
    half = SPLIT_TILE // 2
    perm = (src == jnp.where(dst < half, 2 * dst, 2 * (dst - half) + 1)).astype(BF16)
    return pl.pallas_call(
        _split_up_kernel,
        name="moe_split_up_weights",
        grid=(ne, two_ff // cols),
        in_specs=[
            pl.BlockSpec((1, dm, cols), lambda e, j: (e, 0, j)),
            pl.BlockSpec((SPLIT_TILE, SPLIT_TILE), lambda e, j: (0, 0)),
        ],
        out_specs=[
            pl.BlockSpec((1, dm, cols // 2), lambda e, j: (e, 0, j)),
            pl.BlockSpec((1, dm, cols // 2), lambda e, j: (e, 0, j)),
        ],
        out_shape=[
            jax.ShapeDtypeStruct((ne, dm, two_ff // 2), BF16),
            jax.ShapeDtypeStruct((ne, dm, two_ff // 2), BF16),
        ],
        compiler_params=pltpu.CompilerParams(dimension_semantics=("arbitrary", "arbitrary")),
    )(w_up, perm)


def _moe_kernel(be_ref, nu_ref, xs_ref, wg_ref, wl_ref, wd_ref, bg_ref, bl_ref, bd_ref, ys_ref):
    @pl.when(pl.program_id(0) < nu_ref[0])
    def _():
        x = _load_row_tiles(xs_ref).astype(BF16)
        x_glu = jnp.dot(x, wg_ref[0], preferred_element_type=F32) + bg_ref[0]
        x_lin = jnp.dot(x, wl_ref[0], preferred_element_type=F32) + bl_ref[0]
        x_glu = jnp.minimum(x_glu, SWIGLU_LIMIT)
        x_lin = jnp.clip(x_lin, -SWIGLU_LIMIT, SWIGLU_LIMIT)
        act = x_glu * _sigmoid(SWIGLU_ALPHA * x_glu) * (x_lin + 1.0)
        _store_row_tiles(ys_ref, jnp.dot(act.astype(BF16), wd_ref[0], preferred_element_type=F32) + bd_ref[0])


def _moe(block_e, n_used, xs, wg, wl, wd, bg, bl, bd):
    n_slots = xs.shape[0] // ROW_TILE[0]
    nb = n_slots // MOE_BLOCK
    blk = lambda i, be, nu: (jnp.maximum(jnp.minimum(i, nu[0] - 1), 0), 0)
    wspec = pl.BlockSpec((1, D_MODEL, D_FF), lambda i, be, nu: (be[i], 0, 0))
    bspec = pl.BlockSpec((1, 1, D_FF), lambda i, be, nu: (be[i], 0, 0))
    return pl.pallas_call(
        _moe_kernel,
        name="moe_experts",
        grid_spec=pltpu.PrefetchScalarGridSpec(
            num_scalar_prefetch=2,
            grid=(nb,),
            in_specs=[pl.BlockSpec(_tiled_rows(MOE_BLOCK), blk), wspec, wspec, wspec, bspec, bspec, bspec],
            out_specs=pl.BlockSpec(_tiled_rows(MOE_BLOCK), blk),
        ),
        out_shape=jax.ShapeDtypeStruct(_tiled_rows(n_slots), F32),
        compiler_params=pltpu.CompilerParams(dimension_semantics=("arbitrary",),
                                             vmem_limit_bytes=VMEM_LIMIT),
    )(block_e, n_used, xs, wg, wl, wd, bg, bl, bd)


def _combine_kernel(pos_ref, posn_ref, gt_ref, h1_ref, g_ref, b_ref, ys_ref, o_ref, buf, sem):
    s = pl.program_id(0)
    slot = s % 2
    rows = TOP_K * CHUNK

    def issue(p_ref, to_slot):
        def body(r, carry):
            for kk in range(TOP_K):
                pltpu.make_async_copy(_row_tile(ys_ref, p_ref[0, kk, r]),
                                      _row_tile(buf, to_slot * rows + kk * CHUNK + r),
                                      sem.at[to_slot]).start(priority=kk % 2)
            return carry

        lax.fori_loop(0, CHUNK, body, 0, unroll=DMA_UNROLL)

    @pl.when(s == 0)
    def _():
        issue(pos_ref, 0)

    @pl.when(s + 1 < pl.num_programs(0))
    def _():
        issue(posn_ref, 1 - slot)

    base = pl.multiple_of(slot * rows, rows)
    span = rows * ROW_TILE[0]
    pltpu.make_async_copy(ys_ref.at[pl.ds(0, span)], buf.at[pl.ds(pl.multiple_of(base * ROW_TILE[0], span), span)],
                          sem.at[slot]).wait()
    gt = gt_ref[...]
    ffn = _load_row_tiles(buf, base, CHUNK) * gt[:, 0:1]
    for kk in range(1, TOP_K):
        ffn = ffn + _load_row_tiles(buf, pl.multiple_of(base + kk * CHUNK, CHUNK), CHUNK) * gt[:, kk:kk + 1]
    o_ref[0] = _layer_norm(DEEPNORM_ALPHA * _load_row_tiles(h1_ref) + ffn, g_ref[...], b_ref[...])


def _combine(pos3, gates_t, h1, g, b, ys, bn, lp):
    seq = lp - CHUNK
    nc = lp // CHUNK
    ncs = seq // CHUNK
    n_steps = bn * ncs
    chunk_of = lambda s: (s // ncs) * nc + s % ncs + 1
    nxt = lambda s: jnp.minimum(s + 1, n_steps - 1)
    return pl.pallas_call(
        _combine_kernel,
        name="moe_combine",
        grid=(n_steps,),
        in_specs=[
            pl.BlockSpec((1, TOP_K, CHUNK), lambda s: (chunk_of(s), 0, 0), memory_space=pltpu.SMEM),
            pl.BlockSpec((1, TOP_K, CHUNK), lambda s: (chunk_of(nxt(s)), 0, 0), memory_space=pltpu.SMEM),
            pl.BlockSpec((CHUNK, TOP_K), lambda s: (chunk_of(s), 0)),
            pl.BlockSpec(_tiled_rows(CHUNK), lambda s: (chunk_of(s), 0)),
            pl.BlockSpec((1, D_MODEL), lambda s: (0, 0)),
            pl.BlockSpec((1, D_MODEL), lambda s: (0, 0)),
            pl.BlockSpec(memory_space=pl.ANY),
        ],
        out_specs=pl.BlockSpec((1, CHUNK, D_MODEL), lambda s: (s // ncs, s % ncs, 0)),
        out_shape=jax.ShapeDtypeStruct((bn, seq, D_MODEL), F32),
        scratch_shapes=[pltpu.VMEM(_tiled_rows(2 * TOP_K * CHUNK), F32), pltpu.SemaphoreType.DMA((2,))],
        compiler_params=pltpu.CompilerParams(dimension_semantics=("arbitrary",)),
    )(pos3, pos3, gates_t, h1, g, b, ys)


def kernel(x, meta, ln_in_g, ln_in_b, w_in, gla_gk_w2, gla_gk_b, gla_norm_g, rwkv_mu, rwkv_w0, rwkv_w2, rwkv_a0, rwkv_a2, rwkv_g2, rwkv_k_k, rwkv_k_a, rwkv_r_k, rwkv_ln_g, rwkv_ln_b, w_out, ln1_g, ln1_b, router_w, router_b, exp_w_up, exp_b_up, exp_w_down, exp_b_down, ln2_g, ln2_b):
    bn, seq, _ = x.shape
    assert seq % CHUNK == 0
    lp = seq + CHUNK
    tp = bn * lp
    assert tp % ROUTER_TILE == 0
    row = lambda t: t.reshape(1, -1).astype(F32)

    hcat = jnp.concatenate([jnp.zeros((bn, N_FRONT, D_MODEL), F32),
                            jnp.broadcast_to(meta[None].astype(F32), (bn, N_META, D_MODEL)), x], axis=1)
    hcat = hcat.reshape(tp, D_MODEL)
    gla_in = 2 * GLA_KEY + 2 * GLA_WIDTH + GLA_GATE_RANK
    w = w_in[0]
    w_cols = jnp.concatenate([w[:, :gla_in], jnp.zeros((D_MODEL, 128 - GLA_GATE_RANK), F32), w[:, gla_in:]],
                             axis=1).astype(BF16)
    h, pg, pr = _ln_inproj(hcat, row(ln_in_g), row(ln_in_b), w_cols, lp)

    w2p = jnp.concatenate([gla_gk_w2[0], jnp.zeros((128 - GLA_GATE_RANK, GLA_KEY), F32)], axis=0)
    y_gla = _gla(pg.reshape(bn, lp, GLA_COLS), w2p, row(gla_gk_b[0]), row(gla_norm_g[0]))

    head_id = jnp.arange(RWKV_WIDTH, dtype=I32) // RWKV_HEAD
    seg = (head_id[:, None] == head_id[None, :]).astype(F32)
    y_rwkv = _rwkv(pr.reshape(bn, lp, RWKV_COLS), row(rwkv_mu[0]), row(rwkv_w0[0]), rwkv_w2[0],
                   row(rwkv_a0[0]), rwkv_a2[0], rwkv_g2[0], row(rwkv_k_k[0]), row(rwkv_k_a[0]),
                   row(rwkv_r_k[0]), row(rwkv_ln_g[0]), row(rwkv_ln_b[0]), seg)

    tri = jnp.triu(jnp.ones((ROUTER_TILE, ROUTER_TILE), F32)).astype(BF16)
    rb = jnp.broadcast_to(router_b[0].reshape(N_EXPERTS, 1), (N_EXPERTS, 128))
    h1, idx, gates, rank, cnt = _outproj_router(
        y_gla.reshape(tp, GLA_WIDTH), y_rwkv.reshape(tp, RWKV_WIDTH), h, w_out[0].astype(BF16),
        row(ln1_g[0]), row(ln1_b[0]), router_w[0].T, rb, tri)

    counts = cnt[:, 0].astype(I32)
    padded = (counts + MOE_BLOCK - 1) // MOE_BLOCK * MOE_BLOCK
    ends_p = jnp.cumsum(padded)
    starts_p = ends_p - padded
    e_ids = jnp.arange(N_EXPERTS, dtype=I32)
    start_of = jnp.sum(jnp.where(idx[None] == e_ids[:, None, None], starts_p[:, None, None], 0), axis=0)
    pos = start_of + rank
    nb = tp * TOP_K // MOE_BLOCK + N_EXPERTS
    n_slots = nb * MOE_BLOCK
    block_start = jnp.arange(nb, dtype=I32) * MOE_BLOCK
    block_e = jnp.minimum(jnp.sum((block_start[:, None] >= ends_p[None, :]).astype(I32), axis=1), N_EXPERTS - 1)
    n_used = (ends_p[-1:] // MOE_BLOCK).astype(I32)
    last_block = jnp.maximum(ends_p // MOE_BLOCK - 1, 0).astype(I32)

    xs = _dispatch(pos, h1, _zero_blocks(last_block, n_slots))
    wg, wl = _split_up_weights(exp_w_up[0])
    bg = exp_b_up[0][:, None, 0::2]
    bl = exp_b_up[0][:, None, 1::2]
    ys = _moe(block_e, n_used, xs, wg, wl, exp_w_down[0].astype(BF16), bg, bl, exp_b_down[0][:, None, :])

    pos3 = pos.reshape(TOP_K, tp // CHUNK, CHUNK).transpose(1, 0, 2)
    return _combine(pos3, gates.T, h1, row(ln2_g[0]), row(ln2_b[0]), ys, bn, lp)
```

```python
import functools

import jax
import jax.numpy as jnp
from jax import lax
from jax.experimental import pallas as pl
from jax.experimental.pallas import tpu as pltpu

F32 = jnp.float32
BF16 = jnp.bfloat16
I32 = jnp.int32
HIGHEST = lax.Precision.HIGHEST

D_MODEL = 1024
N_META = 16
CHUNK = 64
N_FRONT = (-N_META) % CHUNK
GLA_HEADS = 4
GLA_DK = 64
GLA_DV = 128
GLA_KEY = GLA_HEADS * GLA_DK
GLA_WIDTH = GLA_HEADS * GLA_DV
GLA_GATE_RANK = 16
GLA_TAU = 16.0
GLA_COLS = 2 * GLA_KEY + 2 * GLA_WIDTH + 128
RWKV_WIDTH = 512
RWKV_HEAD = 64
RWKV_HEADS = RWKV_WIDTH // RWKV_HEAD
RWKV_W_RANK = 64
RWKV_A_RANK = 64
RWKV_G_RANK = 128
RWKV_COLS = 3 * RWKV_WIDTH + RWKV_W_RANK + RWKV_A_RANK + RWKV_G_RANK
N_EXPERTS = 32
TOP_K = 4
D_FF = D_MODEL
SWIGLU_ALPHA = 1.702
SWIGLU_LIMIT = 7.0
MOE_BLOCK = 512
DEPTH = 1
DEEPNORM_ALPHA = (2.0 * DEPTH) ** 0.25
LN_EPS = 1e-5
RWKV_LN_EPS = 64e-5
RMS_EPS = 1e-6

ROUTER_TILE = 768
DMA_UNROLL = 4
VMEM_LIMIT = 56 * 1024 * 1024


def _mm(a, b):
    return jnp.dot(a.astype(BF16), b.astype(BF16), preferred_element_type=F32)


def _mm_nt(a, b):
    return lax.dot_general(a.astype(BF16), b.astype(BF16), (((1,), (1,)), ((), ())),
                           preferred_element_type=F32)


def _mm_tn(a, b):
    return lax.dot_general(a.astype(BF16), b.astype(BF16), (((0,), (0,)), ((), ())),
                           preferred_element_type=F32)


def _mm_f32(a, b):
    return jnp.dot(a, b, preferred_element_type=F32, precision=HIGHEST)


def _layer_norm(x, g, b):
    mu = jnp.mean(x, axis=-1, keepdims=True)
    xc = x - mu
    var = jnp.mean(xc * xc, axis=-1, keepdims=True)
    return xc * lax.rsqrt(var + LN_EPS) * g + b


def _sigmoid(x):
    return 1.0 / (1.0 + jnp.exp(-x))


def _log_sigmoid(x):
    return jnp.minimum(x, 0.0) - jnp.log(1.0 + jnp.exp(-jnp.abs(x)))


ROW_TILE = (8, 128)


def _tiled_rows(n):
    return (n * ROW_TILE[0], ROW_TILE[1])


def _row_tile(ref, i):
    return ref.at[pl.ds(pl.multiple_of(i * ROW_TILE[0], ROW_TILE[0]), ROW_TILE[0])]


def _store_row_tiles(ref, x, row0=0):
    n = x.shape[0]
    for j in range(ROW_TILE[0]):
        ref[pl.ds(row0 * ROW_TILE[0] + j, n, stride=ROW_TILE[0]), :] = x[:, j * ROW_TILE[1]:(j + 1) * ROW_TILE[1]]


def _load_row_tiles(ref, row0=0, n=None):
    n = ref.shape[0] // ROW_TILE[0] if n is None else n
    return jnp.concatenate([ref[pl.ds(row0 * ROW_TILE[0] + j, n, stride=ROW_TILE[0]), :]
                            for j in range(ROW_TILE[0])], axis=1)


def _tri_masks(n):
    r = lax.broadcasted_iota(I32, (n, n), 0)
    c = lax.broadcasted_iota(I32, (n, n), 1)
    return r >= c, r > c, r == c


def _ln_inproj_kernel(x_ref, g_ref, b_ref, w_ref, h_ref, pg_ref, pr_ref, *, tiles_per_seq):
    i = pl.program_id(0)
    y = _layer_norm(x_ref[...], g_ref[...], b_ref[...])
    row = lax.broadcasted_iota(I32, (y.shape[0], 1), 0)
    is_front = jnp.logical_and(i % tiles_per_seq == 0, row < N_FRONT)
    y = jnp.where(is_front, 0.0, y)
    h_ref[...] = y
    p = _mm(y, w_ref[...])
    pg_ref[...] = p[:, :GLA_COLS]
    pr_ref[...] = p[:, GLA_COLS:]


def _ln_inproj(hcat, g, b, w, lp):
    tp = hcat.shape[0]
    tiles_per_seq = 1
    for cand in range(1, lp // 8 + 1):
        if lp % cand == 0 and (lp // cand) % 8 == 0 and lp // cand >= N_FRONT and lp // cand <= 384:
            tiles_per_seq = cand
            break
    tm = lp // tiles_per_seq
    ncols = GLA_COLS + RWKV_COLS
    return pl.pallas_call(
        functools.partial(_ln_inproj_kernel, tiles_per_seq=tiles_per_seq),
        name="ln_inproj",
        grid=(tp // tm,),
        in_specs=[
            pl.BlockSpec((tm, D_MODEL), lambda i: (i, 0)),
            pl.BlockSpec((1, D_MODEL), lambda i: (0, 0)),
            pl.BlockSpec((1, D_MODEL), lambda i: (0, 0)),
            pl.BlockSpec((D_MODEL, ncols), lambda i: (0, 0)),
        ],
        out_specs=[
            pl.BlockSpec((tm, D_MODEL), lambda i: (i, 0)),
            pl.BlockSpec((tm, GLA_COLS), lambda i: (i, 0)),
            pl.BlockSpec((tm, RWKV_COLS), lambda i: (i, 0)),
        ],
        out_shape=[
            jax.ShapeDtypeStruct((tp, D_MODEL), F32),
            jax.ShapeDtypeStruct((tp, GLA_COLS), F32),
            jax.ShapeDtypeStruct((tp, RWKV_COLS), F32),
        ],
        compiler_params=pltpu.CompilerParams(dimension_semantics=("arbitrary",),
                                             vmem_limit_bytes=VMEM_LIMIT),
    )(hcat, g, b, w)


GLA_GROUP = 11


def _gla_kernel(pg_ref, w2_ref, gkb_ref, ng_ref, o_ref, st_ref, qe_ref, oi_ref, kvt_ref, el_ref):
    c = pl.program_id(1)

    @pl.when(c == 0)
    def _():
        st_ref[...] = jnp.zeros_like(st_ref)

    rows_n = pg_ref.shape[1]
    ng = rows_n // CHUNK
    g_off = 2 * GLA_KEY + GLA_WIDTH
    p = pg_ref[0]
    gl = p[:, g_off + GLA_WIDTH:]
    lg = _log_sigmoid(_mm_f32(gl, w2_ref[...]) + gkb_ref[...]) * (1.0 / GLA_TAU)
    row = lax.broadcasted_iota(I32, (rows_n, 1), 0)
    lg = jnp.where(jnp.logical_and(c == 0, row < N_FRONT), 0.0, lg)
    incl, _, _ = _tri_masks(CHUNK)
    tril = incl.astype(BF16)
    bc = jnp.concatenate([_mm_split3(tril, lg[i * CHUNK:(i + 1) * CHUNK]) for i in range(ng)], axis=0)
    g3 = lambda t: t.reshape(ng, CHUNK, t.shape[-1])
    bc = g3(bc)
    b_last = bc[:, CHUNK - 1:CHUNK, :]
    k = g3(p[:, GLA_KEY:2 * GLA_KEY])
    qe = g3(p[:, 0:GLA_KEY]) * (GLA_DK ** -0.5) * jnp.exp(bc)
    ke = k * jnp.exp(-bc)
    kl = k * jnp.exp(b_last - bc)
    e_last = jnp.exp(b_last)
    v = g3(p[:, 2 * GLA_KEY:g_off]).astype(BF16)
    for h in range(GLA_HEADS):
        ks = slice(h * GLA_DK, (h + 1) * GLA_DK)
        vs = slice(h * GLA_DV, (h + 1) * GLA_DV)
        qh = qe[:, :, ks].astype(BF16)
        a = jnp.einsum('gtd,gsd->gts', qh, ke[:, :, ks].astype(BF16), preferred_element_type=F32)
        a = jnp.where(incl, a, 0.0).astype(BF16)
        oi_ref[:, h] = jnp.einsum('gts,gsv->gtv', a, v[:, :, vs], preferred_element_type=F32)
        kvt_ref[:, h] = jnp.einsum('gtv,gtd->gvd', v[:, :, vs], kl[:, :, ks].astype(BF16),
                                   preferred_element_type=F32)
        qe_ref[:, h] = qh
        el_ref[:, h] = e_last[:, :, ks]

    def chunk_body(ci, carry):
        st = st_ref[...]
        oi_ref[ci] = oi_ref[ci] + jnp.einsum('htd,hvd->htv', qe_ref[ci], st.astype(BF16),
                                             preferred_element_type=F32)
        st_ref[...] = st * el_ref[ci] + kvt_ref[ci]
        return carry

    lax.fori_loop(0, ng, chunk_body, 0)

    for ci in range(ng):
        o = oi_ref[ci]
        o = o * lax.rsqrt(jnp.mean(o * o, axis=-1, keepdims=True) + RMS_EPS) * ng_ref[...]
        o = jnp.concatenate([o[h] for h in range(GLA_HEADS)], axis=1)
        rows = slice(ci * CHUNK, (ci + 1) * CHUNK)
        gate = pg_ref[0, rows, g_off:g_off + GLA_WIDTH]
        o_ref[0, rows, :] = o * (gate * _sigmoid(gate))


def _gla(pg, w2p, gkb, ng_w):
    bn, lp, _ = pg.shape
    nc = lp // CHUNK
    ng = max(d for d in range(1, GLA_GROUP + 1) if nc % d == 0)
    rows = ng * CHUNK
    hh = GLA_HEADS
    return pl.pallas_call(
        _gla_kernel,
        name="gla_mixer",
        grid=(bn, nc // ng),
        in_specs=[
            pl.BlockSpec((1, rows, GLA_COLS), lambda b, c: (b, c, 0)),
            pl.BlockSpec((128, GLA_KEY), lambda b, c: (0, 0)),
            pl.BlockSpec((1, GLA_KEY), lambda b, c: (0, 0)),
            pl.BlockSpec((1, GLA_DV), lambda b, c: (0, 0)),
        ],
        out_specs=pl.BlockSpec((1, rows, GLA_WIDTH), lambda b, c: (b, c, 0)),
        out_shape=jax.ShapeDtypeStruct((bn, lp, GLA_WIDTH), F32),
        scratch_shapes=[
            pltpu.VMEM((hh, GLA_DV, GLA_DK), F32),
            pltpu.VMEM((ng, hh, CHUNK, GLA_DK), BF16),
            pltpu.VMEM((ng, hh, CHUNK, GLA_DV), F32),
            pltpu.VMEM((ng, hh, GLA_DV, GLA_DK), F32),
            pltpu.VMEM((ng, hh, 1, GLA_DK), F32),
        ],
        compiler_params=pltpu.CompilerParams(dimension_semantics=("arbitrary", "arbitrary"),
                                             vmem_limit_bytes=VMEM_LIMIT),
    )(pg, w2p, gkb, ng_w)


RWKV_GROUP = 11
RWKV_SLAB = 256
def _mm_split3(ones_bf16, x):
    hi = x.astype(BF16)
    r1 = x - hi.astype(F32)
    mid = r1.astype(BF16)
    lo = (r1 - mid.astype(F32)).astype(BF16)
    return jnp.dot(jnp.concatenate([ones_bf16] * 3, axis=1), jnp.concatenate([hi, mid, lo], axis=0),
                   preferred_element_type=F32)


def _mm_split2_rhs_ones(x, ones_bf16):
    hi = x.astype(BF16)
    mid = (x - hi.astype(F32)).astype(BF16)
    dot = lambda t: jnp.dot(t, ones_bf16, preferred_element_type=F32)
    return dot(hi) + dot(mid)


def _rwkv_kernel(pr_ref, pv_ref, mu_ref, w0_ref, w2_ref, a0_ref, a2_ref, g2_ref, kk_ref, ka_ref,
                 rk_ref, lng_ref, lnb_ref, seg_ref, o_ref, s_ref, ops_ref, wend_ref, rm_ref, yn_ref,
                 y_ref, bonus_ref, gate_ref):
    c = pl.program_id(1)
    rows_n = pr_ref.shape[1]
    ng = rows_n // CHUNK

    @pl.when(c == 0)
    def _():
        s_ref[...] = jnp.zeros_like(s_ref)

    p = pr_ref[0]
    prev_row = jnp.where(c > 0, pv_ref[0][7:8, :], 0.0)
    row = lax.broadcasted_iota(I32, (rows_n, 1), 0)
    prev = jnp.where(row == 0, prev_row, pltpu.roll(p, 1, 0))
    p = p + (prev - p) * mu_ref[...]
    W = RWKV_WIDTH
    r = p[:, 0:W]
    k = p[:, W:2 * W]
    v = p[:, 2 * W:3 * W]
    w_low = p[:, 3 * W:3 * W + RWKV_W_RANK]
    a_low = p[:, 3 * W + RWKV_W_RANK:3 * W + RWKV_W_RANK + RWKV_A_RANK]
    g_low = p[:, 3 * W + RWKV_W_RANK + RWKV_A_RANK:]
    wx = w0_ref[...] + _mm(jnp.tanh(w_low), w2_ref[...])
    w = _log_sigmoid(wx) - 0.5
    logd = -jnp.exp(w)
    a = _sigmoid(a0_ref[...] + _mm(a_low, a2_ref[...]))
    g = _mm(_sigmoid(g_low), g2_ref[...])
    seg = seg_ref[...]
    kk = k * kk_ref[...]
    kk = kk * lax.rsqrt(jnp.maximum(_mm_split2_rhs_ones(kk * kk, seg), 1e-24))
    k = k * (1.0 + (a - 1.0) * ka_ref[...])
    bonus = _mm_split2_rhs_ones(r * k * rk_ref[...], seg) * v

    bonus_ref[...] = bonus
    gate_ref[...] = g

    incl, strict, diag = _tri_masks(CHUNK)
    tril = incl.astype(BF16)
    c_in = jnp.concatenate([_mm_split3(tril, logd[i * CHUNK:(i + 1) * CHUNK]) for i in range(ng)], axis=0)
    g3 = lambda t: t.reshape(ng, CHUNK, W)
    logd, c_in, r, k, v, kk, a = g3(logd), g3(c_in), g3(r), g3(k), g3(v), g3(kk), g3(a)
    c_last = c_in[:, CHUNK - 1:CHUNK, :]
    e_neg = jnp.exp(-c_in)
    e_end = jnp.exp(c_last - c_in)
    kka = kk * a
    per_head = (-kk * jnp.exp(c_in - logd), r * jnp.exp(c_in), kka * e_neg, k * e_neg, kka * e_end,
                k * e_end, v)
    GW = RWKV_SLAB
    n_slab = W // GW
    for i, t in enumerate(per_head):
        t = t.astype(BF16)
        for sl in range(n_slab):
            ops_ref[i, sl] = t[:, :, sl * GW:(sl + 1) * GW]
    w_end = jnp.exp(c_last)
    for sl in range(n_slab):
        wend_ref[sl] = w_end[:, :, sl * GW:(sl + 1) * GW]

    hpl = GW // RWKV_HEAD
    lane_head = lax.broadcasted_iota(I32, (1, 1, GW), 2) // RWKV_HEAD
    head_masks = [lane_head == h for h in range(hpl)]
    rr = lax.broadcasted_iota(I32, (CHUNK, GW), 0)
    cc = lax.broadcasted_iota(I32, (CHUNK, GW), 1) % RWKV_HEAD
    incl4, strict4, eye4 = rr >= cc, rr > cc, (rr == cc).astype(F32)

    def block_diag(x):
        x = x.astype(BF16)
        return jnp.concatenate([jnp.where(m, x, jnp.zeros_like(x)) for m in head_masks], axis=1)

    def head_blocks(full):
        out = jnp.where(head_masks[0], full[:, :RWKV_HEAD], 0.0)
        for h in range(1, hpl):
            out = out + jnp.where(head_masks[h], full[:, h * RWKV_HEAD:(h + 1) * RWKV_HEAD], 0.0)
        return out

    bmm = lambda x, y: jnp.einsum('gts,gsd->gtd', x.astype(BF16), y.astype(BF16), preferred_element_type=F32)
    bmm_nt = lambda x, y: jnp.einsum('gtd,gsd->gts', x.astype(BF16), y.astype(BF16), preferred_element_type=F32)
    bmm_tn = lambda x, y: jnp.einsum('gtk,gtd->gkd', x.astype(BF16), y.astype(BF16), preferred_element_type=F32)

    def slab_body(sl, carry):
        at, rt, bt, kt, bh, kh, vv = [ops_ref[i, sl] for i in range(7)]
        gm = bmm_nt(jnp.concatenate([at, rt], axis=1),
                    jnp.concatenate([block_diag(bt), block_diag(kt)], axis=1))
        a_ab = jnp.where(strict4, gm[:, :CHUNK, :GW], 0.0)
        a_ak = jnp.where(strict4, gm[:, :CHUNK, GW:], 0.0)
        a_rb = jnp.where(incl4, gm[:, CHUNK:, :GW], 0.0)
        a_rk = jnp.where(incl4, gm[:, CHUNK:, GW:], 0.0)
        akv = bmm(jnp.concatenate([a_ak, a_rk], axis=1), block_diag(vv))
        xk = eye4 + a_ab
        pk = bmm(a_ab, block_diag(a_ab))
        for _ in range(4):
            both = bmm(jnp.concatenate([xk, pk], axis=1), block_diag(pk))
            xk = xk + both[:, :CHUNK]
            pk = both[:, CHUNK:]
        xk = xk + bmm(xk, block_diag(pk))
        pq = bmm(xk, jnp.concatenate([block_diag(at), block_diag(akv[:, :CHUNK])], axis=2))
        p_bd, q_bd = block_diag(pq[:, :, :GW]), block_diag(pq[:, :, GW:])
        ry = bmm(a_rb, jnp.concatenate([p_bd, q_bd], axis=2)) + jnp.concatenate(
            [rt.astype(F32), akv[:, CHUNK:]], axis=2)
        m_sbs = head_blocks(bmm_tn(bh, pq[:, :, :GW])) + eye4 * wend_ref[sl]
        n_sbs = head_blocks(bmm_tn(jnp.concatenate([bh, kh], axis=1),
                                   jnp.concatenate([pq[:, :, GW:].astype(BF16), vv], axis=1)))
        rm_ref[sl] = jnp.concatenate([ry[:, :, :GW], m_sbs], axis=1).astype(BF16)
        yn_ref[sl] = jnp.concatenate([ry[:, :, GW:], n_sbs], axis=1)
        return carry

    lax.fori_loop(0, n_slab, slab_body, 0)

    def chunk_body(ci, carry):
        for sl in range(n_slab):
            lanes = slice(sl * GW, (sl + 1) * GW)
            st = s_ref[:, lanes]
            st_bd = jnp.concatenate([jnp.where(m[0], st, 0.0) for m in head_masks], axis=0).astype(BF16)
            res = jnp.dot(rm_ref[sl, ci], st_bd, preferred_element_type=F32) + yn_ref[sl, ci]
            y_ref[ci, :, lanes] = res[:CHUNK]
            s_ref[:, lanes] = res[CHUNK:]
        return carry

    lax.fori_loop(0, ng, chunk_body, 0)

    y = y_ref[...].reshape(rows_n, W)
    mean = _mm_split2_rhs_ones(y, seg) * (1.0 / RWKV_HEAD)
    yc = y - mean
    var = _mm_split2_rhs_ones(yc * yc, seg) * (1.0 / RWKV_HEAD)
    yn = yc * lax.rsqrt(var + RWKV_LN_EPS)
    o_ref[0] = (yn * lng_ref[...] + lnb_ref[...] + bonus_ref[...]) * gate_ref[...]


def _rwkv(pr, mu, w0, w2, a0, a2, g2, k_k, k_a, r_k, ln_g, ln_b, seg):
    bn, lp, _ = pr.shape
    nc = lp // CHUNK
    ng = max(d for d in range(1, RWKV_GROUP + 1) if nc % d == 0)
    rows = ng * CHUNK
    vec = lambda n: pl.BlockSpec((1, n), lambda b, c: (0, 0))
    mat = lambda m, n: pl.BlockSpec((m, n), lambda b, c: (0, 0))
    hd = RWKV_HEAD
    n_slab = RWKV_WIDTH // RWKV_SLAB
    return pl.pallas_call(
        _rwkv_kernel,
        name="rwkv_mixer",
        grid=(bn, nc // ng),
        in_specs=[
            pl.BlockSpec((1, rows, RWKV_COLS), lambda b, c: (b, c, 0)),
            pl.BlockSpec((1, 8, RWKV_COLS), lambda b, c: (b, jnp.maximum(c * (rows // 8) - 1, 0), 0)),
            vec(RWKV_COLS), vec(RWKV_WIDTH), mat(RWKV_W_RANK, RWKV_WIDTH), vec(RWKV_WIDTH),
            mat(RWKV_A_RANK, RWKV_WIDTH), mat(RWKV_G_RANK, RWKV_WIDTH), vec(RWKV_WIDTH), vec(RWKV_WIDTH),
            vec(RWKV_WIDTH), vec(RWKV_WIDTH), vec(RWKV_WIDTH), mat(RWKV_WIDTH, RWKV_WIDTH),
        ],
        out_specs=pl.BlockSpec((1, rows, RWKV_WIDTH), lambda b, c: (b, c, 0)),
        out_shape=jax.ShapeDtypeStruct((bn, lp, RWKV_WIDTH), F32),
        scratch_shapes=[
            pltpu.VMEM((hd, RWKV_WIDTH), F32),
            pltpu.VMEM((7, n_slab, ng, CHUNK, RWKV_SLAB), BF16),
            pltpu.VMEM((n_slab, ng, 1, RWKV_SLAB), F32),
            pltpu.VMEM((n_slab, ng, 2 * CHUNK, RWKV_SLAB), BF16),
            pltpu.VMEM((n_slab, ng, 2 * CHUNK, RWKV_SLAB), F32),
            pltpu.VMEM((ng, CHUNK, RWKV_WIDTH), F32),
            pltpu.VMEM((rows, RWKV_WIDTH), F32),
            pltpu.VMEM((rows, RWKV_WIDTH), F32),
        ],
        compiler_params=pltpu.CompilerParams(dimension_semantics=("arbitrary", "arbitrary"),
                                             vmem_limit_bytes=VMEM_LIMIT),
    )(pr, pr, mu, w0, w2, a0, a2, g2, k_k, k_a, r_k, ln_g, ln_b, seg.astype(BF16))


def _outproj_router_kernel(yg_ref, yr_ref, h_ref, wo_ref, g_ref, b_ref, rwt_ref, rb_ref, tri_ref,
                           h1_ref, idx_ref, gate_ref, rank_ref, cnt_ref, base_ref):
    i = pl.program_id(0)

    @pl.when(i == 0)
    def _():
        base_ref[...] = jnp.zeros_like(base_ref)

    wo = wo_ref[...]
    mix = _mm(yg_ref[...], wo[:GLA_WIDTH]) + _mm(yr_ref[...], wo[GLA_WIDTH:])
    h1 = _layer_norm(DEEPNORM_ALPHA * h_ref[...] + mix, g_ref[...], b_ref[...])
    _store_row_tiles(h1_ref, h1)
    work = lax.dot_general(rwt_ref[...], h1, (((1,), (1,)), ((), ())), preferred_element_type=F32,
                           precision=HIGHEST) + rb_ref[...][:, 0:1]
    tm = work.shape[1]
    e_iota = lax.broadcasted_iota(I32, (N_EXPERTS, tm), 0)
    base = base_ref[...][:, 0:1]
    prior = jnp.zeros((N_EXPERTS, 1), F32)
    tri = tri_ref[...]
    vals = []
    for kk in range(TOP_K):
        m = jnp.max(work, axis=0, keepdims=True)
        sel = jnp.min(jnp.where(work == m, e_iota, N_EXPERTS), axis=0, keepdims=True)
        onehot = e_iota == sel
        work = jnp.where(onehot, -jnp.inf, work)
        oh = onehot.astype(F32)
        cnt = jnp.dot(oh.astype(BF16), tri, preferred_element_type=F32)
        rank = jnp.sum(oh * (base + prior + cnt - 1.0), axis=0, keepdims=True)
        prior = prior + cnt[:, tm - 1:tm]
        vals.append(m)
        idx_ref[kk:kk + 1, :] = sel
        rank_ref[kk:kk + 1, :] = rank.astype(I32)
    es = [jnp.exp(vv - vals[0]) for vv in vals]
    den = es[0] + es[1] + es[2] + es[3]
    for kk in range(TOP_K):
        gate_ref[kk:kk + 1, :] = es[kk] / den
    new_base = base + prior
    base_ref[...] = jnp.broadcast_to(new_base, base_ref.shape)
    cnt_ref[...] = jnp.broadcast_to(new_base, cnt_ref.shape)


def _outproj_router(yg, yr, h, wo, g, b, rwt, rb, tri):
    tp = h.shape[0]
    tm = ROUTER_TILE
    const = lambda m, n: pl.BlockSpec((m, n), lambda i: (0, 0))
    return pl.pallas_call(
        _outproj_router_kernel,
        name="outproj_router",
        grid=(tp // tm,),
        in_specs=[
            pl.BlockSpec((tm, GLA_WIDTH), lambda i: (i, 0)),
            pl.BlockSpec((tm, RWKV_WIDTH), lambda i: (i, 0)),
            pl.BlockSpec((tm, D_MODEL), lambda i: (i, 0)),
            const(D_MODEL, D_MODEL), const(1, D_MODEL), const(1, D_MODEL),
            const(N_EXPERTS, D_MODEL), const(N_EXPERTS, 128), const(tm, tm),
        ],
        out_specs=[
            pl.BlockSpec(_tiled_rows(tm), lambda i: (i, 0)),
            pl.BlockSpec((TOP_K, tm), lambda i: (0, i)),
            pl.BlockSpec((TOP_K, tm), lambda i: (0, i)),
            pl.BlockSpec((TOP_K, tm), lambda i: (0, i)),
            pl.BlockSpec((N_EXPERTS, 128), lambda i: (0, 0)),
        ],
        out_shape=[
            jax.ShapeDtypeStruct(_tiled_rows(tp), F32),
            jax.ShapeDtypeStruct((TOP_K, tp), I32),
            jax.ShapeDtypeStruct((TOP_K, tp), F32),
            jax.ShapeDtypeStruct((TOP_K, tp), I32),
            jax.ShapeDtypeStruct((N_EXPERTS, 128), F32),
        ],
        scratch_shapes=[pltpu.VMEM((N_EXPERTS, 128), F32)],
        compiler_params=pltpu.CompilerParams(dimension_semantics=("arbitrary",),
                                             vmem_limit_bytes=VMEM_LIMIT),
    )(yg, yr, h, wo, g, b, rwt, rb, tri)


def _zero_blocks_kernel(last_ref, o_ref):
    o_ref[...] = jnp.zeros_like(o_ref)


def _zero_blocks(last_block, n_slots):
    return pl.pallas_call(
        _zero_blocks_kernel,
        name="moe_zero_blocks",
        grid_spec=pltpu.PrefetchScalarGridSpec(
            num_scalar_prefetch=1,
            grid=(N_EXPERTS,),
            in_specs=[],
            out_specs=pl.BlockSpec(_tiled_rows(MOE_BLOCK), lambda e, last: (last[e], 0)),
        ),
        out_shape=jax.ShapeDtypeStruct(_tiled_rows(n_slots), F32),
        compiler_params=pltpu.CompilerParams(dimension_semantics=("arbitrary",)),
    )(last_block)


def _dispatch_kernel(pos_ref, x_ref, xs_in_ref, xs_ref, sem):
    del xs_in_ref
    tm = pos_ref.shape[1]

    def start(r, carry):
        for kk in range(TOP_K):
            pltpu.make_async_copy(_row_tile(x_ref, r), _row_tile(xs_ref, pos_ref[kk, r]),
                                  sem).start(priority=kk % 2)
        return carry

    lax.fori_loop(0, tm, start, 0, unroll=DMA_UNROLL)
    for kk in range(TOP_K):
        pltpu.make_async_copy(x_ref, xs_ref.at[pl.ds(0, x_ref.shape[0])], sem).wait()


def _dispatch(pos, x, xs):
    tp = pos.shape[1]
    tm = ROUTER_TILE
    return pl.pallas_call(
        _dispatch_kernel,
        name="moe_dispatch",
        grid=(tp // tm,),
        in_specs=[
            pl.BlockSpec((TOP_K, tm), lambda i: (0, i), memory_space=pltpu.SMEM),
            pl.BlockSpec(_tiled_rows(tm), lambda i: (i, 0)),
            pl.BlockSpec(memory_space=pl.ANY),
        ],
        out_specs=pl.BlockSpec(memory_space=pl.ANY),
        out_shape=jax.ShapeDtypeStruct(xs.shape, xs.dtype),
        scratch_shapes=[pltpu.SemaphoreType.DMA(())],
        input_output_aliases={2: 0},
        compiler_params=pltpu.CompilerParams(dimension_semantics=("arbitrary",)),
    )(pos, x, xs)


SPLIT_TILE = 256


def _split_up_kernel(w_ref, perm_ref, wg_ref, wl_ref):
    half = SPLIT_TILE // 2
    for t in range(w_ref.shape[2] // SPLIT_TILE):
        d = jnp.dot(w_ref[0, :, t * SPLIT_TILE:(t + 1) * SPLIT_TILE].astype(BF16), perm_ref[...],
                    preferred_element_type=F32).astype(BF16)
        wg_ref[0, :, t * half:(t + 1) * half] = d[:, :half]
        wl_ref[0, :, t * half:(t + 1) * half] = d[:, half:]


def _split_up_weights(w_up):
    ne, dm, two_ff = w_up.shape
    cols = 1024
    src = jnp.arange(SPLIT_TILE, dtype=I32)[:, None]
    dst = jnp.arange(SPLIT_TILE, dtype=I32)[None, :]
    half = SPLIT_TILE // 2
    perm = (src == jnp.where(dst < half, 2 * dst, 2 * (dst - half) + 1)).astype(BF16)
    return pl.pallas_call(
        _split_up_kernel,
        name="moe_split_up_weights",
        grid=(ne, two_ff // cols),
        in_specs=[
            pl.BlockSpec((1, dm, cols), lambda e, j: (e, 0, j)),
            pl.BlockSpec((SPLIT_TILE, SPLIT_TILE), lambda e, j: (0, 0)),
        ],
        out_specs=[
            pl.BlockSpec((1, dm, cols // 2), lambda e, j: (e, 0, j)),
            pl.BlockSpec((1, dm, cols // 2), lambda e, j: (e, 0, j)),
        ],
        out_shape=[
            jax.ShapeDtypeStruct((ne, dm, two_ff // 2), BF16),
            jax.ShapeDtypeStruct((ne, dm, two_ff // 2), BF16),
        ],
        compiler_params=pltpu.CompilerParams(dimension_semantics=("arbitrary", "arbitrary")),
    )(w_up, perm)


def _moe_kernel(be_ref, nu_ref, xs_ref, wg_ref, wl_ref, wd_ref, bg_ref, bl_ref, bd_ref, ys_ref):
    @pl.when(pl.program_id(0) < nu_ref[0])
    def _():
        x = _load_row_tiles(xs_ref).astype(BF16)
        x_glu = jnp.dot(x, wg_ref[0], preferred_element_type=F32) + bg_ref[0]
        x_lin = jnp.dot(x, wl_ref[0], preferred_element_type=F32) + bl_ref[0]
        x_glu = jnp.minimum(x_glu, SWIGLU_LIMIT)
        x_lin = jnp.clip(x_lin, -SWIGLU_LIMIT, SWIGLU_LIMIT)
        act = x_glu * _sigmoid(SWIGLU_ALPHA * x_glu) * (x_lin + 1.0)
        _store_row_tiles(ys_ref, jnp.dot(act.astype(BF16), wd_ref[0], preferred_element_type=F32) + bd_ref[0])


def _moe(block_e, n_used, xs, wg, wl, wd, bg, bl, bd):
    n_slots = xs.shape[0] // ROW_TILE[0]
    nb = n_slots // MOE_BLOCK
    blk = lambda i, be, nu: (jnp.maximum(jnp.minimum(i, nu[0] - 1), 0), 0)
    wspec = pl.BlockSpec((1, D_MODEL, D_FF), lambda i, be, nu: (be[i], 0, 0))
    bspec = pl.BlockSpec((1, 1, D_FF), lambda i, be, nu: (be[i], 0, 0))
    return pl.pallas_call(
        _moe_kernel,
        name="moe_experts",
        grid_spec=pltpu.PrefetchScalarGridSpec(
            num_scalar_prefetch=2,
            grid=(nb,),
            in_specs=[pl.BlockSpec(_tiled_rows(MOE_BLOCK), blk), wspec, wspec, wspec, bspec, bspec, bspec],
            out_specs=pl.BlockSpec(_tiled_rows(MOE_BLOCK), blk),
        ),
        out_shape=jax.ShapeDtypeStruct(_tiled_rows(n_slots), F32),
        compiler_params=pltpu.CompilerParams(dimension_semantics=("arbitrary",),
                                             vmem_limit_bytes=VMEM_LIMIT),
    )(block_e, n_used, xs, wg, wl, wd, bg, bl, bd)


def _combine_kernel(pos_ref, posn_ref, gt_ref, h1_ref, g_ref, b_ref, ys_ref, o_ref, buf, sem):
    s = pl.program_id(0)
    slot = s % 2
    rows = TOP_K * CHUNK

    def issue(p_ref, to_slot):
        def body(r, carry):
            for kk in range(TOP_K):
                pltpu.make_async_copy(_row_tile(ys_ref, p_ref[0, kk, r]),
                                      _row_tile(buf, to_slot * rows + kk * CHUNK + r),
                                      sem.at[to_slot]).start(priority=kk % 2)
            return carry

        lax.fori_loop(0, CHUNK, body, 0, unroll=DMA_UNROLL)

    @pl.when(s == 0)
    def _():
        issue(pos_ref, 0)

    @pl.when(s + 1 < pl.num_programs(0))
    def _():
        issue(posn_ref, 1 - slot)

    base = pl.multiple_of(slot * rows, rows)
    span = rows * ROW_TILE[0]
    pltpu.make_async_copy(ys_ref.at[pl.ds(0, span)], buf.at[pl.ds(pl.multiple_of(base * ROW_TILE[0], span), span)],
                          sem.at[slot]).wait()
    gt = gt_ref[...]
    ffn = _load_row_tiles(buf, base, CHUNK) * gt[:, 0:1]
    for kk in range(1, TOP_K):
        ffn = ffn + _load_row_tiles(buf, pl.multiple_of(base + kk * CHUNK, CHUNK), CHUNK) * gt[:, kk:kk + 1]
    o_ref[0] = _layer_norm(DEEPNORM_ALPHA * _load_row_tiles(h1_ref) + ffn, g_ref[...], b_ref[...])


def _combine(pos3, gates_t, h1, g, b, ys, bn, lp):
    seq = lp - CHUNK
    nc = lp // CHUNK
    ncs = seq // CHUNK
    n_steps = bn * ncs
    chunk_of = lambda s: (s // ncs) * nc + s % ncs + 1
    nxt = lambda s: jnp.minimum(s + 1, n_steps - 1)
    return pl.pallas_call(
        _combine_kernel,
        name="moe_combine",
        grid=(n_steps,),
        in_specs=[
            pl.BlockSpec((1, TOP_K, CHUNK), lambda s: (chunk_of(s), 0, 0), memory_space=pltpu.SMEM),
            pl.BlockSpec((1, TOP_K, CHUNK), lambda s: (chunk_of(nxt(s)), 0, 0), memory_space=pltpu.SMEM),
            pl.BlockSpec((CHUNK, TOP_K), lambda s: (chunk_of(s), 0)),
            pl.BlockSpec(_tiled_rows(CHUNK), lambda s: (chunk_of(s), 0)),
            pl.BlockSpec((1, D_MODEL), lambda s: (0, 0)),
            pl.BlockSpec((1, D_MODEL), lambda s: (0, 0)),
            pl.BlockSpec(memory_space=pl.ANY),
        ],
        out_specs=pl.BlockSpec((1, CHUNK, D_MODEL), lambda s: (s // ncs, s % ncs, 0)),
        out_shape=jax.ShapeDtypeStruct((bn, seq, D_MODEL), F32),
        scratch_shapes=[pltpu.VMEM(_tiled_rows(2 * TOP_K * CHUNK), F32), pltpu.SemaphoreType.DMA((2,))],
        compiler_params=pltpu.CompilerParams(dimension_semantics=("arbitrary",)),
    )(pos3, pos3, gates_t, h1, g, b, ys)


def kernel(x, meta, ln_in_g, ln_in_b, w_in, gla_gk_w2, gla_gk_b, gla_norm_g, rwkv_mu, rwkv_w0, rwkv_w2, rwkv_a0, rwkv_a2, rwkv_g2, rwkv_k_k, rwkv_k_a, rwkv_r_k, rwkv_ln_g, rwkv_ln_b, w_out, ln1_g, ln1_b, router_w, router_b, exp_w_up, exp_b_up, exp_w_down, exp_b_down, ln2_g, ln2_b):
    bn, seq, _ = x.shape
    assert seq % CHUNK == 0
    lp = seq + CHUNK
    tp = bn * lp
    assert tp % ROUTER_TILE == 0
    row = lambda t: t.reshape(1, -1).astype(F32)

    hcat = jnp.concatenate([jnp.zeros((bn, N_FRONT, D_MODEL), F32),
                            jnp.broadcast_to(meta[None].astype(F32), (bn, N_META, D_MODEL)), x], axis=1)
    hcat = hcat.reshape(tp, D_MODEL)
    gla_in = 2 * GLA_KEY + 2 * GLA_WIDTH + GLA_GATE_RANK
    w = w_in[0]
    w_cols = jnp.concatenate([w[:, :gla_in], jnp.zeros((D_MODEL, 128 - GLA_GATE_RANK), F32), w[:, gla_in:]],
                             axis=1).astype(BF16)
    h, pg, pr = _ln_inproj(hcat, row(ln_in_g), row(ln_in_b), w_cols, lp)

    w2p = jnp.concatenate([gla_gk_w2[0], jnp.zeros((128 - GLA_GATE_RANK, GLA_KEY), F32)], axis=0)
    y_gla = _gla(pg.reshape(bn, lp, GLA_COLS), w2p, row(gla_gk_b[0]), row(gla_norm_g[0]))

    head_id = jnp.arange(RWKV_WIDTH, dtype=I32) // RWKV_HEAD
    seg = (head_id[:, None] == head_id[None, :]).astype(F32)
    y_rwkv = _rwkv(pr.reshape(bn, lp, RWKV_COLS), row(rwkv_mu[0]), row(rwkv_w0[0]), rwkv_w2[0],
                   row(rwkv_a0[0]), rwkv_a2[0], rwkv_g2[0], row(rwkv_k_k[0]), row(rwkv_k_a[0]),
                   row(rwkv_r_k[0]), row(rwkv_ln_g[0]), row(rwkv_ln_b[0]), seg)

    tri = jnp.triu(jnp.ones((ROUTER_TILE, ROUTER_TILE), F32)).astype(BF16)
    rb = jnp.broadcast_to(router_b[0].reshape(N_EXPERTS, 1), (N_EXPERTS, 128))
    h1, idx, gates, rank, cnt = _outproj_router(
        y_gla.reshape(tp, GLA_WIDTH), y_rwkv.reshape(tp, RWKV_WIDTH), h, w_out[0].astype(BF16),
        row(ln1_g[0]), row(ln1_b[0]), router_w[0].T, rb, tri)

    counts = cnt[:, 0].astype(I32)
    padded = (counts + MOE_BLOCK - 1) // MOE_BLOCK * MOE_BLOCK
    ends_p = jnp.cumsum(padded)
    starts_p = ends_p - padded
    e_ids = jnp.arange(N_EXPERTS, dtype=I32)
    start_of = jnp.sum(jnp.where(idx[None] == e_ids[:, None, None], starts_p[:, None, None], 0), axis=0)
    pos = start_of + rank
    nb = tp * TOP_K // MOE_BLOCK + N_EXPERTS
    n_slots = nb * MOE_BLOCK
    block_start = jnp.arange(nb, dtype=I32) * MOE_BLOCK
    block_e = jnp.minimum(jnp.sum((block_start[:, None] >= ends_p[None, :]).astype(I32), axis=1), N_EXPERTS - 1)
    n_used = (ends_p[-1:] // MOE_BLOCK).astype(I32)
    last_block = jnp.maximum(ends_p // MOE_BLOCK - 1, 0).astype(I32)

    xs = _dispatch(pos, h1, _zero_blocks(last_block, n_slots))
    wg, wl = _split_up_weights(exp_w_up[0])
    bg = exp_b_up[0][:, None, 0::2]
    bl = exp_b_up[0][:, None, 1::2]
    ys = _moe(block_e, n_used, xs, wg, wl, exp_w_down[0].astype(BF16), bg, bl, exp_b_down[0][:, None, :])

    pos3 = pos.reshape(TOP_K, tp // CHUNK, CHUNK).transpose(1, 0, 2)
    return _combine(pos3, gates.T, h1, row(ln2_g[0]), row(ln2_b[0]), ys, bn, lp)
```

```python
import functools

import jax
import jax.numpy as jnp
from jax import lax
from jax.experimental import pallas as pl
from jax.experimental.pallas import tpu as pltpu

F32 = jnp.float32
BF16 = jnp.bfloat16
I32 = jnp.int32
HIGHEST = lax.Precision.HIGHEST

D_MODEL = 1024
N_META = 16
CHUNK = 64
N_FRONT = (-N_META) % CHUNK
GLA_HEADS = 4
GLA_DK = 64
GLA_DV = 128
GLA_KEY = GLA_HEADS * GLA_DK
GLA_WIDTH = GLA_HEADS * GLA_DV
GLA_GATE_RANK = 16
GLA_TAU = 16.0
GLA_COLS = 2 * GLA_KEY + 2 * GLA_WIDTH + 128
RWKV_WIDTH = 512
RWKV_HEAD = 64
RWKV_HEADS = RWKV_WIDTH // RWKV_HEAD
RWKV_W_RANK = 64
RWKV_A_RANK = 64
RWKV_G_RANK = 128
RWKV_COLS = 3 * RWKV_WIDTH + RWKV_W_RANK + RWKV_A_RANK + RWKV_G_RANK
N_EXPERTS = 32
TOP_K = 4
D_FF = D_MODEL
SWIGLU_ALPHA = 1.702
SWIGLU_LIMIT = 7.0
MOE_BLOCK = 512
DEPTH = 1
DEEPNORM_ALPHA = (2.0 * DEPTH) ** 0.25
LN_EPS = 1e-5
RWKV_LN_EPS = 64e-5
RMS_EPS = 1e-6

ROUTER_TILE = 768
DMA_UNROLL = 4
VMEM_LIMIT = 56 * 1024 * 1024


def _mm(a, b):
    return jnp.dot(a.astype(BF16), b.astype(BF16), preferred_element_type=F32)


def _mm_nt(a, b):
    return lax.dot_general(a.astype(BF16), b.astype(BF16), (((1,), (1,)), ((), ())),
                           preferred_element_type=F32)


def _mm_tn(a, b):
    return lax.dot_general(a.astype(BF16), b.astype(BF16), (((0,), (0,)), ((), ())),
                           preferred_element_type=F32)


def _mm_f32(a, b):
    return jnp.dot(a, b, preferred_element_type=F32, precision=HIGHEST)


def _layer_norm(x, g, b):
    mu = jnp.mean(x, axis=-1, keepdims=True)
    xc = x - mu
    var = jnp.mean(xc * xc, axis=-1, keepdims=True)
    return xc * lax.rsqrt(var + LN_EPS) * g + b


def _sigmoid(x):
    return 1.0 / (1.0 + jnp.exp(-x))


def _log_sigmoid(x):
    return jnp.minimum(x, 0.0) - jnp.log(1.0 + jnp.exp(-jnp.abs(x)))


ROW_TILE = (8, 128)


def _tiled_rows(n):
    return (n * ROW_TILE[0], ROW_TILE[1])


def _row_tile(ref, i):
    return ref.at[pl.ds(pl.multiple_of(i * ROW_TILE[0], ROW_TILE[0]), ROW_TILE[0])]


def _store_row_tiles(ref, x, row0=0):
    n = x.shape[0]
    for j in range(ROW_TILE[0]):
        ref[pl.ds(row0 * ROW_TILE[0] + j, n, stride=ROW_TILE[0]), :] = x[:, j * ROW_TILE[1]:(j + 1) * ROW_TILE[1]]


def _load_row_tiles(ref, row0=0, n=None):
    n = ref.shape[0] // ROW_TILE[0] if n is None else n
    return jnp.concatenate([ref[pl.ds(row0 * ROW_TILE[0] + j, n, stride=ROW_TILE[0]), :]
                            for j in range(ROW_TILE[0])], axis=1)


def _tri_masks(n):
    r = lax.broadcasted_iota(I32, (n, n), 0)
    c = lax.broadcasted_iota(I32, (n, n), 1)
    return r >= c, r > c, r == c


def _ln_inproj_kernel(x_ref, g_ref, b_ref, w_ref, h_ref, pg_ref, pr_ref, *, tiles_per_seq):
    i = pl.program_id(0)
    y = _layer_norm(x_ref[...], g_ref[...], b_ref[...])
    row = lax.broadcasted_iota(I32, (y.shape[0], 1), 0)
    is_front = jnp.logical_and(i % tiles_per_seq == 0, row < N_FRONT)
    y = jnp.where(is_front, 0.0, y)
    h_ref[...] = y
    p = _mm(y, w_ref[...])
    pg_ref[...] = p[:, :GLA_COLS]
    pr_ref[...] = p[:, GLA_COLS:]


def _ln_inproj(hcat, g, b, w, lp):
    tp = hcat.shape[0]
    tiles_per_seq = 1
    for cand in range(1, lp // 8 + 1):
        if lp % cand == 0 and (lp // cand) % 8 == 0 and lp // cand >= N_FRONT and lp // cand <= 384:
            tiles_per_seq = cand
            break
    tm = lp // tiles_per_seq
    ncols = GLA_COLS + RWKV_COLS
    return pl.pallas_call(
        functools.partial(_ln_inproj_kernel, tiles_per_seq=tiles_per_seq),
        name="ln_inproj",
        grid=(tp // tm,),
        in_specs=[
            pl.BlockSpec((tm, D_MODEL), lambda i: (i, 0)),
            pl.BlockSpec((1, D_MODEL), lambda i: (0, 0)),
            pl.BlockSpec((1, D_MODEL), lambda i: (0, 0)),
            pl.BlockSpec((D_MODEL, ncols), lambda i: (0, 0)),
        ],
        out_specs=[
            pl.BlockSpec((tm, D_MODEL), lambda i: (i, 0)),
            pl.BlockSpec((tm, GLA_COLS), lambda i: (i, 0)),
            pl.BlockSpec((tm, RWKV_COLS), lambda i: (i, 0)),
        ],
        out_shape=[
            jax.ShapeDtypeStruct((tp, D_MODEL), F32),
            jax.ShapeDtypeStruct((tp, GLA_COLS), F32),
            jax.ShapeDtypeStruct((tp, RWKV_COLS), F32),
        ],
        compiler_params=pltpu.CompilerParams(dimension_semantics=("arbitrary",),
                                             vmem_limit_bytes=VMEM_LIMIT),
    )(hcat, g, b, w)


GLA_GROUP = 11


def _gla_kernel(pg_ref, w2_ref, gkb_ref, ng_ref, o_ref, st_ref, qe_ref, oi_ref, kvt_ref, el_ref):
    c = pl.program_id(1)

    @pl.when(c == 0)
    def _():
        st_ref[...] = jnp.zeros_like(st_ref)

    rows_n = pg_ref.shape[1]
    ng = rows_n // CHUNK
    g_off = 2 * GLA_KEY + GLA_WIDTH
    p = pg_ref[0]
    gl = p[:, g_off + GLA_WIDTH:]
    lg = _log_sigmoid(_mm_f32(gl, w2_ref[...]) + gkb_ref[...]) * (1.0 / GLA_TAU)
    row = lax.broadcasted_iota(I32, (rows_n, 1), 0)
    lg = jnp.where(jnp.logical_and(c == 0, row < N_FRONT), 0.0, lg)
    incl, _, _ = _tri_masks(CHUNK)
    tril = incl.astype(BF16)
    bc = jnp.concatenate([_mm_split3(tril, lg[i * CHUNK:(i + 1) * CHUNK]) for i in range(ng)], axis=0)
    g3 = lambda t: t.reshape(ng, CHUNK, t.shape[-1])
    bc = g3(bc)
    b_last = bc[:, CHUNK - 1:CHUNK, :]
    k = g3(p[:, GLA_KEY:2 * GLA_KEY])
    qe = g3(p[:, 0:GLA_KEY]) * (GLA_DK ** -0.5) * jnp.exp(bc)
    ke = k * jnp.exp(-bc)
    kl = k * jnp.exp(b_last - bc)
    e_last = jnp.exp(b_last)
    v = g3(p[:, 2 * GLA_KEY:g_off]).astype(BF16)
    for h in range(GLA_HEADS):
        ks = slice(h * GLA_DK, (h + 1) * GLA_DK)
        vs = slice(h * GLA_DV, (h + 1) * GLA_DV)
        qh = qe[:, :, ks].astype(BF16)
        a = jnp.einsum('gtd,gsd->gts', qh, ke[:, :, ks].astype(BF16), preferred_element_type=F32)
        a = jnp.where(incl, a, 0.0).astype(BF16)
        oi_ref[:, h] = jnp.einsum('gts,gsv->gtv', a, v[:, :, vs], preferred_element_type=F32)
        kvt_ref[:, h] = jnp.einsum('gtv,gtd->gvd', v[:, :, vs], kl[:, :, ks].astype(BF16),
                                   preferred_element_type=F32)
        qe_ref[:, h] = qh
        el_ref[:, h] = e_last[:, :, ks]

    def chunk_body(ci, carry):
        st = st_ref[...]
        oi_ref[ci] = oi_ref[ci] + jnp.einsum('htd,hvd->htv', qe_ref[ci], st.astype(BF16),
                                             preferred_element_type=F32)
        st_ref[...] = st * el_ref[ci] + kvt_ref[ci]
        return carry

    lax.fori_loop(0, ng, chunk_body, 0)

    for ci in range(ng):
        o = oi_ref[ci]
        o = o * lax.rsqrt(jnp.mean(o * o, axis=-1, keepdims=True) + RMS_EPS) * ng_ref[...]
        o = jnp.concatenate([o[h] for h in range(GLA_HEADS)], axis=1)
        rows = slice(ci * CHUNK, (ci + 1) * CHUNK)
        gate = pg_ref[0, rows, g_off:g_off + GLA_WIDTH]
        o_ref[0, rows, :] = o * (gate * _sigmoid(gate))


def _gla(pg, w2p, gkb, ng_w):
    bn, lp, _ = pg.shape
    nc = lp // CHUNK
    ng = max(d for d in range(1, GLA_GROUP + 1) if nc % d == 0)
    rows = ng * CHUNK
    hh = GLA_HEADS
    return pl.pallas_call(
        _gla_kernel,
        name="gla_mixer",
        grid=(bn, nc // ng),
        in_specs=[
            pl.BlockSpec((1, rows, GLA_COLS), lambda b, c: (b, c, 0)),
            pl.BlockSpec((128, GLA_KEY), lambda b, c: (0, 0)),
            pl.BlockSpec((1, GLA_KEY), lambda b, c: (0, 0)),
            pl.BlockSpec((1, GLA_DV), lambda b, c: (0, 0)),
        ],
        out_specs=pl.BlockSpec((1, rows, GLA_WIDTH), lambda b, c: (b, c, 0)),
        out_shape=jax.ShapeDtypeStruct((bn, lp, GLA_WIDTH), F32),
        scratch_shapes=[
            pltpu.VMEM((hh, GLA_DV, GLA_DK), F32),
            pltpu.VMEM((ng, hh, CHUNK, GLA_DK), BF16),
            pltpu.VMEM((ng, hh, CHUNK, GLA_DV), F32),
            pltpu.VMEM((ng, hh, GLA_DV, GLA_DK), F32),
            pltpu.VMEM((ng, hh, 1, GLA_DK), F32),
        ],
        compiler_params=pltpu.CompilerParams(dimension_semantics=("arbitrary", "arbitrary"),
                                             vmem_limit_bytes=VMEM_LIMIT),
    )(pg, w2p, gkb, ng_w)


RWKV_GROUP = 11
RWKV_SLAB = 256
def _mm_split3(ones_bf16, x):
    hi = x.astype(BF16)
    r1 = x - hi.astype(F32)
    mid = r1.astype(BF16)
    lo = (r1 - mid.astype(F32)).astype(BF16)
    return jnp.dot(jnp.concatenate([ones_bf16] * 3, axis=1), jnp.concatenate([hi, mid, lo], axis=0),
                   preferred_element_type=F32)


def _mm_split2_rhs_ones(x, ones_bf16):
    hi = x.astype(BF16)
    mid = (x - hi.astype(F32)).astype(BF16)
    dot = lambda t: jnp.dot(t, ones_bf16, preferred_element_type=F32)
    return dot(hi) + dot(mid)


def _rwkv_kernel(pr_ref, pv_ref, mu_ref, w0_ref, w2_ref, a0_ref, a2_ref, g2_ref, kk_ref, ka_ref,
                 rk_ref, lng_ref, lnb_ref, seg_ref, o_ref, s_ref, ops_ref, wend_ref, rm_ref, yn_ref,
                 y_ref, bonus_ref, gate_ref):
    c = pl.program_id(1)
    rows_n = pr_ref.shape[1]
    ng = rows_n // CHUNK

    @pl.when(c == 0)
    def _():
        s_ref[...] = jnp.zeros_like(s_ref)

    p = pr_ref[0]
    prev_row = jnp.where(c > 0, pv_ref[0][7:8, :], 0.0)
    row = lax.broadcasted_iota(I32, (rows_n, 1), 0)
    prev = jnp.where(row == 0, prev_row, pltpu.roll(p, 1, 0))
    p = p + (prev - p) * mu_ref[...]
    W = RWKV_WIDTH
    r = p[:, 0:W]
    k = p[:, W:2 * W]
    v = p[:, 2 * W:3 * W]
    w_low = p[:, 3 * W:3 * W + RWKV_W_RANK]
    a_low = p[:, 3 * W + RWKV_W_RANK:3 * W + RWKV_W_RANK + RWKV_A_RANK]
    g_low = p[:, 3 * W + RWKV_W_RANK + RWKV_A_RANK:]
    wx = w0_ref[...] + _mm(jnp.tanh(w_low), w2_ref[...])
    w = _log_sigmoid(wx) - 0.5
    logd = -jnp.exp(w)
    a = _sigmoid(a0_ref[...] + _mm(a_low, a2_ref[...]))
    g = _mm(_sigmoid(g_low), g2_ref[...])
    seg = seg_ref[...]
    kk = k * kk_ref[...]
    kk = kk * lax.rsqrt(jnp.maximum(_mm_split2_rhs_ones(kk * kk, seg), 1e-24))
    k = k * (1.0 + (a - 1.0) * ka_ref[...])
    bonus = _mm_split2_rhs_ones(r * k * rk_ref[...], seg) * v

    bonus_ref[...] = bonus
    gate_ref[...] = g

    incl, strict, diag = _tri_masks(CHUNK)
    tril = incl.astype(BF16)
    c_in = jnp.concatenate([_mm_split3(tril, logd[i * CHUNK:(i + 1) * CHUNK]) for i in range(ng)], axis=0)
    g3 = lambda t: t.reshape(ng, CHUNK, W)
    logd, c_in, r, k, v, kk, a = g3(logd), g3(c_in), g3(r), g3(k), g3(v), g3(kk), g3(a)
    c_last = c_in[:, CHUNK - 1:CHUNK, :]
    e_neg = jnp.exp(-c_in)
    e_end = jnp.exp(c_last - c_in)
    kka = kk * a
    per_head = (-kk * jnp.exp(c_in - logd), r * jnp.exp(c_in), kka * e_neg, k * e_neg, kka * e_end,
                k * e_end, v)
    GW = RWKV_SLAB
    n_slab = W // GW
    for i, t in enumerate(per_head):
        t = t.astype(BF16)
        for sl in range(n_slab):
            ops_ref[i, sl] = t[:, :, sl * GW:(sl + 1) * GW]
    w_end = jnp.exp(c_last)
    for sl in range(n_slab):
        wend_ref[sl] = w_end[:, :, sl * GW:(sl + 1) * GW]

    hpl = GW // RWKV_HEAD
    lane_head = lax.broadcasted_iota(I32, (1, 1, GW), 2) // RWKV_HEAD
    head_masks = [lane_head == h for h in range(hpl)]
    rr = lax.broadcasted_iota(I32, (CHUNK, GW), 0)
    cc = lax.broadcasted_iota(I32, (CHUNK, GW), 1) % RWKV_HEAD
    incl4, strict4, eye4 = rr >= cc, rr > cc, (rr == cc).astype(F32)

    def block_diag(x):
        x = x.astype(BF16)
        return jnp.concatenate([jnp.where(m, x, jnp.zeros_like(x)) for m in head_masks], axis=1)

    def head_blocks(full):
        out = jnp.where(head_masks[0], full[:, :RWKV_HEAD], 0.0)
        for h in range(1, hpl):
            out = out + jnp.where(head_masks[h], full[:, h * RWKV_HEAD:(h + 1) * RWKV_HEAD], 0.0)
        return out

    bmm = lambda x, y: jnp.einsum('gts,gsd->gtd', x.astype(BF16), y.astype(BF16), preferred_element_type=F32)
    bmm_nt = lambda x, y: jnp.einsum('gtd,gsd->gts', x.astype(BF16), y.astype(BF16), preferred_element_type=F32)
    bmm_tn = lambda x, y: jnp.einsum('gtk,gtd->gkd', x.astype(BF16), y.astype(BF16), preferred_element_type=F32)

    def slab_body(sl, carry):
        at, rt, bt, kt, bh, kh, vv = [ops_ref[i, sl] for i in range(7)]
        gm = bmm_nt(jnp.concatenate([at, rt], axis=1),
                    jnp.concatenate([block_diag(bt), block_diag(kt)], axis=1))
        a_ab = jnp.where(strict4, gm[:, :CHUNK, :GW], 0.0)
        a_ak = jnp.where(strict4, gm[:, :CHUNK, GW:], 0.0)
        a_rb = jnp.where(incl4, gm[:, CHUNK:, :GW], 0.0)
        a_rk = jnp.where(incl4, gm[:, CHUNK:, GW:], 0.0)
        akv = bmm(jnp.concatenate([a_ak, a_rk], axis=1), block_diag(vv))
        xk = eye4 + a_ab
        pk = bmm(a_ab, block_diag(a_ab))
        for _ in range(4):
            both = bmm(jnp.concatenate([xk, pk], axis=1), block_diag(pk))
            xk = xk + both[:, :CHUNK]
            pk = both[:, CHUNK:]
        xk = xk + bmm(xk, block_diag(pk))
        pq = bmm(xk, jnp.concatenate([block_diag(at), block_diag(akv[:, :CHUNK])], axis=2))
        p_bd, q_bd = block_diag(pq[:, :, :GW]), block_diag(pq[:, :, GW:])
        ry = bmm(a_rb, jnp.concatenate([p_bd, q_bd], axis=2)) + jnp.concatenate(
            [rt.astype(F32), akv[:, CHUNK:]], axis=2)
        m_sbs = head_blocks(bmm_tn(bh, pq[:, :, :GW])) + eye4 * wend_ref[sl]
        n_sbs = head_blocks(bmm_tn(jnp.concatenate([bh, kh], axis=1),
                                   jnp.concatenate([pq[:, :, GW:].astype(BF16), vv], axis=1)))
        rm_ref[sl] = jnp.concatenate([ry[:, :, :GW], m_sbs], axis=1).astype(BF16)
        yn_ref[sl] = jnp.concatenate([ry[:, :, GW:], n_sbs], axis=1)
        return carry

    lax.fori_loop(0, n_slab, slab_body, 0)

    def chunk_body(ci, carry):
        for sl in range(n_slab):
            lanes = slice(sl * GW, (sl + 1) * GW)
            st = s_ref[:, lanes]
            st_bd = jnp.concatenate([jnp.where(m[0], st, 0.0) for m in head_masks], axis=0).astype(BF16)
            res = jnp.dot(rm_ref[sl, ci], st_bd, preferred_element_type=F32) + yn_ref[sl, ci]
            y_ref[ci, :, lanes] = res[:CHUNK]
            s_ref[:, lanes] = res[CHUNK:]
        return carry

    lax.fori_loop(0, ng, chunk_body, 0)

    y = y_ref[...].reshape(rows_n, W)
    mean = _mm_split2_rhs_ones(y, seg) * (1.0 / RWKV_HEAD)
    yc = y - mean
    var = _mm_split2_rhs_ones(yc * yc, seg) * (1.0 / RWKV_HEAD)
    yn = yc * lax.rsqrt(var + RWKV_LN_EPS)
    o_ref[0] = (yn * lng_ref[...] + lnb_ref[...] + bonus_ref[...]) * gate_ref[...]


def _rwkv(pr, mu, w0, w2, a0, a2, g2, k_k, k_a, r_k, ln_g, ln_b, seg):
    bn, lp, _ = pr.shape
    nc = lp // CHUNK
    ng = max(d for d in range(1, RWKV_GROUP + 1) if nc % d == 0)
    rows = ng * CHUNK
    vec = lambda n: pl.BlockSpec((1, n), lambda b, c: (0, 0))
    mat = lambda m, n: pl.BlockSpec((m, n), lambda b, c: (0, 0))
    hd = RWKV_HEAD
    n_slab = RWKV_WIDTH // RWKV_SLAB
    return pl.pallas_call(
        _rwkv_kernel,
        name="rwkv_mixer",
        grid=(bn, nc // ng),
        in_specs=[
            pl.BlockSpec((1, rows, RWKV_COLS), lambda b, c: (b, c, 0)),
            pl.BlockSpec((1, 8, RWKV_COLS), lambda b, c: (b, jnp.maximum(c * (rows // 8) - 1, 0), 0)),
            vec(RWKV_COLS), vec(RWKV_WIDTH), mat(RWKV_W_RANK, RWKV_WIDTH), vec(RWKV_WIDTH),
            mat(RWKV_A_RANK, RWKV_WIDTH), mat(RWKV_G_RANK, RWKV_WIDTH), vec(RWKV_WIDTH), vec(RWKV_WIDTH),
            vec(RWKV_WIDTH), vec(RWKV_WIDTH), vec(RWKV_WIDTH), mat(RWKV_WIDTH, RWKV_WIDTH),
        ],
        out_specs=pl.BlockSpec((1, rows, RWKV_WIDTH), lambda b, c: (b, c, 0)),
        out_shape=jax.ShapeDtypeStruct((bn, lp, RWKV_WIDTH), F32),
        scratch_shapes=[
            pltpu.VMEM((hd, RWKV_WIDTH), F32),
            pltpu.VMEM((7, n_slab, ng, CHUNK, RWKV_SLAB), BF16),
            pltpu.VMEM((n_slab, ng, 1, RWKV_SLAB), F32),
            pltpu.VMEM((n_slab, ng, 2 * CHUNK, RWKV_SLAB), BF16),
            pltpu.VMEM((n_slab, ng, 2 * CHUNK, RWKV_SLAB), F32),
            pltpu.VMEM((ng, CHUNK, RWKV_WIDTH), F32),
            pltpu.VMEM((rows, RWKV_WIDTH), F32),
            pltpu.VMEM((rows, RWKV_WIDTH), F32),
        ],
        compiler_params=pltpu.CompilerParams(dimension_semantics=("arbitrary", "arbitrary"),
                                             vmem_limit_bytes=VMEM_LIMIT),
    )(pr, pr, mu, w0, w2, a0, a2, g2, k_k, k_a, r_k, ln_g, ln_b, seg.astype(BF16))


def _outproj_router_kernel(yg_ref, yr_ref, h_ref, wo_ref, g_ref, b_ref, rwt_ref, rb_ref, tri_ref,
                           h1_ref, idx_ref, gate_ref, rank_ref, cnt_ref, base_ref):
    i = pl.program_id(0)

    @pl.when(i == 0)
    def _():
        base_ref[...] = jnp.zeros_like(base_ref)

    wo = wo_ref[...]
    mix = _mm(yg_ref[...], wo[:GLA_WIDTH]) + _mm(yr_ref[...], wo[GLA_WIDTH:])
    h1 = _layer_norm(DEEPNORM_ALPHA * h_ref[...] + mix, g_ref[...], b_ref[...])
    _store_row_tiles(h1_ref, h1)
    work = lax.dot_general(rwt_ref[...], h1, (((1,), (1,)), ((), ())), preferred_element_type=F32,
                           precision=HIGHEST) + rb_ref[...][:, 0:1]
    tm = work.shape[1]
    e_iota = lax.broadcasted_iota(I32, (N_EXPERTS, tm), 0)
    base = base_ref[...][:, 0:1]
    prior = jnp.zeros((N_EXPERTS, 1), F32)
    tri = tri_ref[...]
    vals = []
    for kk in range(TOP_K):
        m = jnp.max(work, axis=0, keepdims=True)
        sel = jnp.min(jnp.where(work == m, e_iota, N_EXPERTS), axis=0, keepdims=True)
        onehot = e_iota == sel
        work = jnp.where(onehot, -jnp.inf, work)
        oh = onehot.astype(F32)
        cnt = jnp.dot(oh.astype(BF16), tri, preferred_element_type=F32)
        rank = jnp.sum(oh * (base + prior + cnt - 1.0), axis=0, keepdims=True)
        prior = prior + cnt[:, tm - 1:tm]
        vals.append(m)
        idx_ref[kk:kk + 1, :] = sel
        rank_ref[kk:kk + 1, :] = rank.astype(I32)
    es = [jnp.exp(vv - vals[0]) for vv in vals]
    den = es[0] + es[1] + es[2] + es[3]
    for kk in range(TOP_K):
        gate_ref[kk:kk + 1, :] = es[kk] / den
    new_base = base + prior
    base_ref[...] = jnp.broadcast_to(new_base, base_ref.shape)
    cnt_ref[...] = jnp.broadcast_to(new_base, cnt_ref.shape)


def _outproj_router(yg, yr, h, wo, g, b, rwt, rb, tri):
    tp = h.shape[0]
    tm = ROUTER_TILE
    const = lambda m, n: pl.BlockSpec((m, n), lambda i: (0, 0))
    return pl.pallas_call(
        _outproj_router_kernel,
        name="outproj_router",
        grid=(tp // tm,),
        in_specs=[
            pl.BlockSpec((tm, GLA_WIDTH), lambda i: (i, 0)),
            pl.BlockSpec((tm, RWKV_WIDTH), lambda i: (i, 0)),
            pl.BlockSpec((tm, D_MODEL), lambda i: (i, 0)),
            const(D_MODEL, D_MODEL), const(1, D_MODEL), const(1, D_MODEL),
            const(N_EXPERTS, D_MODEL), const(N_EXPERTS, 128), const(tm, tm),
        ],
        out_specs=[
            pl.BlockSpec(_tiled_rows(tm), lambda i: (i, 0)),
            pl.BlockSpec((TOP_K, tm), lambda i: (0, i)),
            pl.BlockSpec((TOP_K, tm), lambda i: (0, i)),
            pl.BlockSpec((TOP_K, tm), lambda i: (0, i)),
            pl.BlockSpec((N_EXPERTS, 128), lambda i: (0, 0)),
        ],
        out_shape=[
            jax.ShapeDtypeStruct(_tiled_rows(tp), F32),
            jax.ShapeDtypeStruct((TOP_K, tp), I32),
            jax.ShapeDtypeStruct((TOP_K, tp), F32),
            jax.ShapeDtypeStruct((TOP_K, tp), I32),
            jax.ShapeDtypeStruct((N_EXPERTS, 128), F32),
        ],
        scratch_shapes=[pltpu.VMEM((N_EXPERTS, 128), F32)],
        compiler_params=pltpu.CompilerParams(dimension_semantics=("arbitrary",),
                                             vmem_limit_bytes=VMEM_LIMIT),
    )(yg, yr, h, wo, g, b, rwt, rb, tri)


def _zero_blocks_kernel(last_ref, o_ref):
    o_ref[...] = jnp.zeros_like(o_ref)


def _zero_blocks(last_block, n_slots):
    return pl.pallas_call(
        _zero_blocks_kernel,
        name="moe_zero_blocks",
        grid_spec=pltpu.PrefetchScalarGridSpec(
            num_scalar_prefetch=1,
            grid=(N_EXPERTS,),
            in_specs=[],
            out_specs=pl.BlockSpec(_tiled_rows(MOE_BLOCK), lambda e, last: (last[e], 0)),
        ),
        out_shape=jax.ShapeDtypeStruct(_tiled_rows(n_slots), F32),
        compiler_params=pltpu.CompilerParams(dimension_semantics=("arbitrary",)),
    )(last_block)


def _dispatch_kernel(pos_ref, x_ref, xs_in_ref, xs_ref, sem):
    del xs_in_ref
    tm = pos_ref.shape[0] // TOP_K

    def start(r, carry):
        for kk in range(TOP_K):
            pltpu.make_async_copy(_row_tile(x_ref, r), _row_tile(xs_ref, pos_ref[r * TOP_K + kk]),
                                  sem).start(priority=kk % 2)
        return carry

    lax.fori_loop(0, tm, start, 0, unroll=DMA_UNROLL)
    for kk in range(TOP_K):
        pltpu.make_async_copy(x_ref, xs_ref.at[pl.ds(0, x_ref.shape[0])], sem).wait()


def _dispatch(pos, x, xs):
    tp = pos.shape[0] // TOP_K
    tm = ROUTER_TILE
    return pl.pallas_call(
        _dispatch_kernel,
        name="moe_dispatch",
        grid=(tp // tm,),
        in_specs=[
            pl.BlockSpec((tm * TOP_K,), lambda i: (i,), memory_space=pltpu.SMEM),
            pl.BlockSpec(_tiled_rows(tm), lambda i: (i, 0)),
            pl.BlockSpec(memory_space=pl.ANY),
        ],
        out_specs=pl.BlockSpec(memory_space=pl.ANY),
        out_shape=jax.ShapeDtypeStruct(xs.shape, xs.dtype),
        scratch_shapes=[pltpu.SemaphoreType.DMA(())],
        input_output_aliases={2: 0},
        compiler_params=pltpu.CompilerParams(dimension_semantics=("arbitrary",)),
    )(pos, x, xs)


SPLIT_TILE = 256


def _moe_kernel(be_ref, nu_ref, xs_ref, wu_ref, wd_ref, perm_ref, bg_ref, bl_ref, bd_ref, ys_ref,
                wg_s, wl_s, wd_s):
    i = pl.program_id(0)

    @pl.when(i < nu_ref[0])
    def _():
        @pl.when(jnp.logical_or(i == 0, be_ref[i] != be_ref[jnp.maximum(i - 1, 0)]))
        def _():
            half = SPLIT_TILE // 2
            for t in range(wu_ref.shape[2] // SPLIT_TILE):
                d = jnp.dot(wu_ref[0, :, t * SPLIT_TILE:(t + 1) * SPLIT_TILE].astype(BF16), perm_ref[...],
                            preferred_element_type=F32).astype(BF16)
                wg_s[:, t * half:(t + 1) * half] = d[:, :half]
                wl_s[:, t * half:(t + 1) * half] = d[:, half:]
            wd_s[...] = wd_ref[0].astype(BF16)

        x = _load_row_tiles(xs_ref).astype(BF16)
        x_glu = jnp.dot(x, wg_s[...], preferred_element_type=F32) + bg_ref[0]
        x_lin = jnp.dot(x, wl_s[...], preferred_element_type=F32) + bl_ref[0]
        x_glu = jnp.minimum(x_glu, SWIGLU_LIMIT)
        x_lin = jnp.clip(x_lin, -SWIGLU_LIMIT, SWIGLU_LIMIT)
        act = x_glu * _sigmoid(SWIGLU_ALPHA * x_glu) * (x_lin + 1.0)
        _store_row_tiles(ys_ref, jnp.dot(act.astype(BF16), wd_s[...], preferred_element_type=F32) + bd_ref[0])


def _moe(block_e, n_used, xs, w_up, w_down, bg, bl, bd):
    n_slots = xs.shape[0] // ROW_TILE[0]
    nb = n_slots // MOE_BLOCK
    src = jnp.arange(SPLIT_TILE, dtype=I32)[:, None]
    dst = jnp.arange(SPLIT_TILE, dtype=I32)[None, :]
    half = SPLIT_TILE // 2
    perm = (src == jnp.where(dst < half, 2 * dst, 2 * (dst - half) + 1)).astype(BF16)
    blk = lambda i, be, nu: (jnp.maximum(jnp.minimum(i, nu[0] - 1), 0), 0)
    bspec = pl.BlockSpec((1, 1, D_FF), lambda i, be, nu: (be[i], 0, 0))
    return pl.pallas_call(
        _moe_kernel,
        name="moe_experts",
        grid_spec=pltpu.PrefetchScalarGridSpec(
            num_scalar_prefetch=2,
            grid=(nb,),
            in_specs=[
                pl.BlockSpec(_tiled_rows(MOE_BLOCK), blk),
                pl.BlockSpec((1, D_MODEL, 2 * D_FF), lambda i, be, nu: (be[i], 0, 0)),
                pl.BlockSpec((1, D_FF, D_MODEL), lambda i, be, nu: (be[i], 0, 0)),
                pl.BlockSpec((SPLIT_TILE, SPLIT_TILE), lambda i, be, nu: (0, 0)),
                bspec, bspec, bspec,
            ],
            out_specs=pl.BlockSpec(_tiled_rows(MOE_BLOCK), blk),
            scratch_shapes=[
                pltpu.VMEM((D_MODEL, D_FF), BF16),
                pltpu.VMEM((D_MODEL, D_FF), BF16),
                pltpu.VMEM((D_FF, D_MODEL), BF16),
            ],
        ),
        out_shape=jax.ShapeDtypeStruct(_tiled_rows(n_slots), F32),
        compiler_params=pltpu.CompilerParams(dimension_semantics=("arbitrary",),
                                             vmem_limit_bytes=VMEM_LIMIT),
    )(block_e, n_used, xs, w_up, w_down, perm, bg, bl, bd)


def _combine_kernel(pos_ref, posn_ref, gt_ref, h1_ref, g_ref, b_ref, ys_ref, o_ref, buf, sem):
    s = pl.program_id(0)
    slot = s % 2
    rows = TOP_K * CHUNK

    def issue(p_ref, to_slot):
        def body(r, carry):
            for kk in range(TOP_K):
                pltpu.make_async_copy(_row_tile(ys_ref, p_ref[r * TOP_K + kk]),
                                      _row_tile(buf, to_slot * rows + kk * CHUNK + r),
                                      sem.at[to_slot]).start(priority=kk % 2)
            return carry

        lax.fori_loop(0, CHUNK, body, 0, unroll=DMA_UNROLL)

    @pl.when(s == 0)
    def _():
        issue(pos_ref, 0)

    @pl.when(s + 1 < pl.num_programs(0))
    def _():
        issue(posn_ref, 1 - slot)

    base = pl.multiple_of(slot * rows, rows)
    span = rows * ROW_TILE[0]
    pltpu.make_async_copy(ys_ref.at[pl.ds(0, span)], buf.at[pl.ds(pl.multiple_of(base * ROW_TILE[0], span), span)],
                          sem.at[slot]).wait()
    gt = gt_ref[...]
    ffn = _load_row_tiles(buf, base, CHUNK) * gt[:, 0:1]
    for kk in range(1, TOP_K):
        ffn = ffn + _load_row_tiles(buf, pl.multiple_of(base + kk * CHUNK, CHUNK), CHUNK) * gt[:, kk:kk + 1]
    o_ref[0] = _layer_norm(DEEPNORM_ALPHA * _load_row_tiles(h1_ref) + ffn, g_ref[...], b_ref[...])


def _combine(pos_flat, gates_t, h1, g, b, ys, bn, lp):
    seq = lp - CHUNK
    nc = lp // CHUNK
    ncs = seq // CHUNK
    n_steps = bn * ncs
    chunk_of = lambda s: (s // ncs) * nc + s % ncs + 1
    nxt = lambda s: jnp.minimum(s + 1, n_steps - 1)
    return pl.pallas_call(
        _combine_kernel,
        name="moe_combine",
        grid=(n_steps,),
        in_specs=[
            pl.BlockSpec((TOP_K * CHUNK,), lambda s: (chunk_of(s),), memory_space=pltpu.SMEM),
            pl.BlockSpec((TOP_K * CHUNK,), lambda s: (chunk_of(nxt(s)),), memory_space=pltpu.SMEM),
            pl.BlockSpec((CHUNK, TOP_K), lambda s: (chunk_of(s), 0)),
            pl.BlockSpec(_tiled_rows(CHUNK), lambda s: (chunk_of(s), 0)),
            pl.BlockSpec((1, D_MODEL), lambda s: (0, 0)),
            pl.BlockSpec((1, D_MODEL), lambda s: (0, 0)),
            pl.BlockSpec(memory_space=pl.ANY),
        ],
        out_specs=pl.BlockSpec((1, CHUNK, D_MODEL), lambda s: (s // ncs, s % ncs, 0)),
        out_shape=jax.ShapeDtypeStruct((bn, seq, D_MODEL), F32),
        scratch_shapes=[pltpu.VMEM(_tiled_rows(2 * TOP_K * CHUNK), F32), pltpu.SemaphoreType.DMA((2,))],
        compiler_params=pltpu.CompilerParams(dimension_semantics=("arbitrary",)),
    )(pos_flat, pos_flat, gates_t, h1, g, b, ys)


def kernel(x, meta, ln_in_g, ln_in_b, w_in, gla_gk_w2, gla_gk_b, gla_norm_g, rwkv_mu, rwkv_w0, rwkv_w2, rwkv_a0, rwkv_a2, rwkv_g2, rwkv_k_k, rwkv_k_a, rwkv_r_k, rwkv_ln_g, rwkv_ln_b, w_out, ln1_g, ln1_b, router_w, router_b, exp_w_up, exp_b_up, exp_w_down, exp_b_down, ln2_g, ln2_b):
    bn, seq, _ = x.shape
    assert seq % CHUNK == 0
    lp = seq + CHUNK
    tp = bn * lp
    assert tp % ROUTER_TILE == 0
    row = lambda t: t.reshape(1, -1).astype(F32)

    hcat = jnp.concatenate([jnp.zeros((bn, N_FRONT, D_MODEL), F32),
                            jnp.broadcast_to(meta[None].astype(F32), (bn, N_META, D_MODEL)), x], axis=1)
    hcat = hcat.reshape(tp, D_MODEL)
    gla_in = 2 * GLA_KEY + 2 * GLA_WIDTH + GLA_GATE_RANK
    w = w_in[0]
    w_cols = jnp.concatenate([w[:, :gla_in], jnp.zeros((D_MODEL, 128 - GLA_GATE_RANK), F32), w[:, gla_in:]],
                             axis=1).astype(BF16)
    h, pg, pr = _ln_inproj(hcat, row(ln_in_g), row(ln_in_b), w_cols, lp)

    w2p = jnp.concatenate([gla_gk_w2[0], jnp.zeros((128 - GLA_GATE_RANK, GLA_KEY), F32)], axis=0)
    y_gla = _gla(pg.reshape(bn, lp, GLA_COLS), w2p, row(gla_gk_b[0]), row(gla_norm_g[0]))

    head_id = jnp.arange(RWKV_WIDTH, dtype=I32) // RWKV_HEAD
    seg = (head_id[:, None] == head_id[None, :]).astype(F32)
    y_rwkv = _rwkv(pr.reshape(bn, lp, RWKV_COLS), row(rwkv_mu[0]), row(rwkv_w0[0]), rwkv_w2[0],
                   row(rwkv_a0[0]), rwkv_a2[0], rwkv_g2[0], row(rwkv_k_k[0]), row(rwkv_k_a[0]),
                   row(rwkv_r_k[0]), row(rwkv_ln_g[0]), row(rwkv_ln_b[0]), seg)

    tri = jnp.triu(jnp.ones((ROUTER_TILE, ROUTER_TILE), F32)).astype(BF16)
    rb = jnp.broadcast_to(router_b[0].reshape(N_EXPERTS, 1), (N_EXPERTS, 128))
    h1, idx, gates, rank, cnt = _outproj_router(
        y_gla.reshape(tp, GLA_WIDTH), y_rwkv.reshape(tp, RWKV_WIDTH), h, w_out[0].astype(BF16),
        row(ln1_g[0]), row(ln1_b[0]), router_w[0].T, rb, tri)

    counts = cnt[:, 0].astype(I32)
    padded = (counts + MOE_BLOCK - 1) // MOE_BLOCK * MOE_BLOCK
    ends_p = jnp.cumsum(padded)
    starts_p = ends_p - padded
    e_ids = jnp.arange(N_EXPERTS, dtype=I32)
    start_of = jnp.sum(jnp.where(idx[None] == e_ids[:, None, None], starts_p[:, None, None], 0), axis=0)
    pos = start_of + rank
    nb = tp * TOP_K // MOE_BLOCK + N_EXPERTS
    n_slots = nb * MOE_BLOCK
    block_start = jnp.arange(nb, dtype=I32) * MOE_BLOCK
    block_e = jnp.minimum(jnp.sum((block_start[:, None] >= ends_p[None, :]).astype(I32), axis=1), N_EXPERTS - 1)
    n_used = (ends_p[-1:] // MOE_BLOCK).astype(I32)
    last_block = jnp.maximum(ends_p // MOE_BLOCK - 1, 0).astype(I32)

    pos_flat = pos.T.reshape(tp * TOP_K)
    xs = _dispatch(pos_flat, h1, _zero_blocks(last_block, n_slots))
    bg = exp_b_up[0][:, None, 0::2]
    bl = exp_b_up[0][:, None, 1::2]
    ys = _moe(block_e, n_used, xs, exp_w_up[0], exp_w_down[0], bg, bl, exp_b_down[0][:, None, :])

    return _combine(pos_flat, gates.T, h1, row(ln2_g[0]), row(ln2_b[0]), ys, bn, lp)
```

```python
import functools

import jax
import jax.numpy as jnp
from jax import lax
from jax.experimental import pallas as pl
from jax.experimental.pallas import tpu as pltpu

F32 = jnp.float32
BF16 = jnp.bfloat16
I32 = jnp.int32
HIGHEST = lax.Precision.HIGHEST

D_MODEL = 1024
N_META = 16
CHUNK = 64
N_FRONT = (-N_META) % CHUNK
GLA_HEADS = 4
GLA_DK = 64
GLA_DV = 128
GLA_KEY = GLA_HEADS * GLA_DK
GLA_WIDTH = GLA_HEADS * GLA_DV
GLA_GATE_RANK = 16
GLA_TAU = 16.0
GLA_COLS = 2 * GLA_KEY + 2 * GLA_WIDTH + 128
RWKV_WIDTH = 512
RWKV_HEAD = 64
RWKV_HEADS = RWKV_WIDTH // RWKV_HEAD
RWKV_W_RANK = 64
RWKV_A_RANK = 64
RWKV_G_RANK = 128
RWKV_COLS = 3 * RWKV_WIDTH + RWKV_W_RANK + RWKV_A_RANK + RWKV_G_RANK
N_EXPERTS = 32
TOP_K = 4
D_FF = D_MODEL
SWIGLU_ALPHA = 1.702
SWIGLU_LIMIT = 7.0
MOE_BLOCK = 512
DEPTH = 1
DEEPNORM_ALPHA = (2.0 * DEPTH) ** 0.25
LN_EPS = 1e-5
RWKV_LN_EPS = 64e-5
RMS_EPS = 1e-6

ROUTER_TILE = 768
DMA_UNROLL = 4
VMEM_LIMIT = 56 * 1024 * 1024


def _mm(a, b):
    return jnp.dot(a.astype(BF16), b.astype(BF16), preferred_element_type=F32)


def _mm_nt(a, b):
    return lax.dot_general(a.astype(BF16), b.astype(BF16), (((1,), (1,)), ((), ())),
                           preferred_element_type=F32)


def _mm_tn(a, b):
    return lax.dot_general(a.astype(BF16), b.astype(BF16), (((0,), (0,)), ((), ())),
                           preferred_element_type=F32)


def _mm_f32(a, b):
    return jnp.dot(a, b, preferred_element_type=F32, precision=HIGHEST)


def _layer_norm(x, g, b):
    mu = jnp.mean(x, axis=-1, keepdims=True)
    xc = x - mu
    var = jnp.mean(xc * xc, axis=-1, keepdims=True)
    return xc * lax.rsqrt(var + LN_EPS) * g + b


def _sigmoid(x):
    return 1.0 / (1.0 + jnp.exp(-x))


def _log_sigmoid(x):
    return jnp.minimum(x, 0.0) - jnp.log(1.0 + jnp.exp(-jnp.abs(x)))


ROW_TILE = (8, 128)


def _tiled_rows(n):
    return (n * ROW_TILE[0], ROW_TILE[1])


def _row_tile(ref, i):
    return ref.at[pl.ds(pl.multiple_of(i * ROW_TILE[0], ROW_TILE[0]), ROW_TILE[0])]


def _store_row_tiles(ref, x, row0=0):
    n = x.shape[0]
    for j in range(ROW_TILE[0]):
        ref[pl.ds(row0 * ROW_TILE[0] + j, n, stride=ROW_TILE[0]), :] = x[:, j * ROW_TILE[1]:(j + 1) * ROW_TILE[1]]


def _load_row_tiles(ref, row0=0, n=None):
    n = ref.shape[0] // ROW_TILE[0] if n is None else n
    return jnp.concatenate([ref[pl.ds(row0 * ROW_TILE[0] + j, n, stride=ROW_TILE[0]), :]
                            for j in range(ROW_TILE[0])], axis=1)


def _tri_masks(n):
    r = lax.broadcasted_iota(I32, (n, n), 0)
    c = lax.broadcasted_iota(I32, (n, n), 1)
    return r >= c, r > c, r == c


def _ln_inproj_kernel(x_ref, g_ref, b_ref, w_ref, h_ref, pg_ref, pr_ref, *, tiles_per_seq):
    i = pl.program_id(0)
    y = _layer_norm(x_ref[...], g_ref[...], b_ref[...])
    row = lax.broadcasted_iota(I32, (y.shape[0], 1), 0)
    is_front = jnp.logical_and(i % tiles_per_seq == 0, row < N_FRONT)
    y = jnp.where(is_front, 0.0, y)
    h_ref[...] = y
    p = _mm(y, w_ref[...])
    pg_ref[...] = p[:, :GLA_COLS]
    pr_ref[...] = p[:, GLA_COLS:]


def _ln_inproj(hcat, g, b, w, lp):
    tp = hcat.shape[0]
    tiles_per_seq = 1
    for cand in range(1, lp // 8 + 1):
        if lp % cand == 0 and (lp // cand) % 8 == 0 and lp // cand >= N_FRONT and lp // cand <= 384:
            tiles_per_seq = cand
            break
    tm = lp // tiles_per_seq
    ncols = GLA_COLS + RWKV_COLS
    return pl.pallas_call(
        functools.partial(_ln_inproj_kernel, tiles_per_seq=tiles_per_seq),
        name="ln_inproj",
        grid=(tp // tm,),
        in_specs=[
            pl.BlockSpec((tm, D_MODEL), lambda i: (i, 0)),
            pl.BlockSpec((1, D_MODEL), lambda i: (0, 0)),
            pl.BlockSpec((1, D_MODEL), lambda i: (0, 0)),
            pl.BlockSpec((D_MODEL, ncols), lambda i: (0, 0)),
        ],
        out_specs=[
            pl.BlockSpec((tm, D_MODEL), lambda i: (i, 0)),
            pl.BlockSpec((tm, GLA_COLS), lambda i: (i, 0)),
            pl.BlockSpec((tm, RWKV_COLS), lambda i: (i, 0)),
        ],
        out_shape=[
            jax.ShapeDtypeStruct((tp, D_MODEL), F32),
            jax.ShapeDtypeStruct((tp, GLA_COLS), F32),
            jax.ShapeDtypeStruct((tp, RWKV_COLS), F32),
        ],
        compiler_params=pltpu.CompilerParams(dimension_semantics=("arbitrary",),
                                             vmem_limit_bytes=VMEM_LIMIT),
    )(hcat, g, b, w)


GLA_GROUP = 11


def _gla_kernel(pg_ref, w2_ref, gkb_ref, ng_ref, o_ref, st_ref, qe_ref, oi_ref, kvt_ref, el_ref):
    c = pl.program_id(1)

    @pl.when(c == 0)
    def _():
        st_ref[...] = jnp.zeros_like(st_ref)

    rows_n = pg_ref.shape[1]
    ng = rows_n // CHUNK
    g_off = 2 * GLA_KEY + GLA_WIDTH
    p = pg_ref[0]
    gl = p[:, g_off + GLA_WIDTH:]
    lg = _log_sigmoid(_mm_f32(gl, w2_ref[...]) + gkb_ref[...]) * (1.0 / GLA_TAU)
    row = lax.broadcasted_iota(I32, (rows_n, 1), 0)
    lg = jnp.where(jnp.logical_and(c == 0, row < N_FRONT), 0.0, lg)
    incl, _, _ = _tri_masks(CHUNK)
    tril = incl.astype(BF16)
    bc = jnp.concatenate([_mm_split3(tril, lg[i * CHUNK:(i + 1) * CHUNK]) for i in range(ng)], axis=0)
    g3 = lambda t: t.reshape(ng, CHUNK, t.shape[-1])
    bc = g3(bc)
    b_last = bc[:, CHUNK - 1:CHUNK, :]
    k = g3(p[:, GLA_KEY:2 * GLA_KEY])
    qe = g3(p[:, 0:GLA_KEY]) * (GLA_DK ** -0.5) * jnp.exp(bc)
    ke = k * jnp.exp(-bc)
    kl = k * jnp.exp(b_last - bc)
    e_last = jnp.exp(b_last)
    v = g3(p[:, 2 * GLA_KEY:g_off]).astype(BF16)
    for h in range(GLA_HEADS):
        ks = slice(h * GLA_DK, (h + 1) * GLA_DK)
        vs = slice(h * GLA_DV, (h + 1) * GLA_DV)
        qh = qe[:, :, ks].astype(BF16)
        a = jnp.einsum('gtd,gsd->gts', qh, ke[:, :, ks].astype(BF16), preferred_element_type=F32)
        a = jnp.where(incl, a, 0.0).astype(BF16)
        oi_ref[:, h] = jnp.einsum('gts,gsv->gtv', a, v[:, :, vs], preferred_element_type=F32)
        kvt_ref[:, h] = jnp.einsum('gtv,gtd->gvd', v[:, :, vs], kl[:, :, ks].astype(BF16),
                                   preferred_element_type=F32)
        qe_ref[:, h] = qh
        el_ref[:, h] = e_last[:, :, ks]

    def chunk_body(ci, carry):
        st = st_ref[...]
        oi_ref[ci] = oi_ref[ci] + jnp.einsum('htd,hvd->htv', qe_ref[ci], st.astype(BF16),
                                             preferred_element_type=F32)
        st_ref[...] = st * el_ref[ci] + kvt_ref[ci]
        return carry

    lax.fori_loop(0, ng, chunk_body, 0)

    for ci in range(ng):
        o = oi_ref[ci]
        o = o * lax.rsqrt(jnp.mean(o * o, axis=-1, keepdims=True) + RMS_EPS) * ng_ref[...]
        o = jnp.concatenate([o[h] for h in range(GLA_HEADS)], axis=1)
        rows = slice(ci * CHUNK, (ci + 1) * CHUNK)
        gate = pg_ref[0, rows, g_off:g_off + GLA_WIDTH]
        o_ref[0, rows, :] = o * (gate * _sigmoid(gate))


def _gla(pg, w2p, gkb, ng_w):
    bn, lp, _ = pg.shape
    nc = lp // CHUNK
    ng = max(d for d in range(1, GLA_GROUP + 1) if nc % d == 0)
    rows = ng * CHUNK
    hh = GLA_HEADS
    return pl.pallas_call(
        _gla_kernel,
        name="gla_mixer",
        grid=(bn, nc // ng),
        in_specs=[
            pl.BlockSpec((1, rows, GLA_COLS), lambda b, c: (b, c, 0)),
            pl.BlockSpec((128, GLA_KEY), lambda b, c: (0, 0)),
            pl.BlockSpec((1, GLA_KEY), lambda b, c: (0, 0)),
            pl.BlockSpec((1, GLA_DV), lambda b, c: (0, 0)),
        ],
        out_specs=pl.BlockSpec((1, rows, GLA_WIDTH), lambda b, c: (b, c, 0)),
        out_shape=jax.ShapeDtypeStruct((bn, lp, GLA_WIDTH), F32),
        scratch_shapes=[
            pltpu.VMEM((hh, GLA_DV, GLA_DK), F32),
            pltpu.VMEM((ng, hh, CHUNK, GLA_DK), BF16),
            pltpu.VMEM((ng, hh, CHUNK, GLA_DV), F32),
            pltpu.VMEM((ng, hh, GLA_DV, GLA_DK), F32),
            pltpu.VMEM((ng, hh, 1, GLA_DK), F32),
        ],
        compiler_params=pltpu.CompilerParams(dimension_semantics=("arbitrary", "arbitrary"),
                                             vmem_limit_bytes=VMEM_LIMIT),
    )(pg, w2p, gkb, ng_w)


RWKV_GROUP = 11
RWKV_SLAB = 256
def _mm_split3(ones_bf16, x):
    hi = x.astype(BF16)
    r1 = x - hi.astype(F32)
    mid = r1.astype(BF16)
    lo = (r1 - mid.astype(F32)).astype(BF16)
    return jnp.dot(jnp.concatenate([ones_bf16] * 3, axis=1), jnp.concatenate([hi, mid, lo], axis=0),
                   preferred_element_type=F32)


def _mm_split2_rhs_ones(x, ones_bf16):
    hi = x.astype(BF16)
    mid = (x - hi.astype(F32)).astype(BF16)
    dot = lambda t: jnp.dot(t, ones_bf16, preferred_element_type=F32)
    return dot(hi) + dot(mid)


def _rwkv_kernel(pr_ref, pv_ref, mu_ref, w0_ref, w2_ref, a0_ref, a2_ref, g2_ref, kk_ref, ka_ref,
                 rk_ref, lng_ref, lnb_ref, seg_ref, o_ref, s_ref, ops_ref, wend_ref, rm_ref, yn_ref,
                 y_ref, bonus_ref, gate_ref):
    c = pl.program_id(1)
    rows_n = pr_ref.shape[1]
    ng = rows_n // CHUNK

    @pl.when(c == 0)
    def _():
        s_ref[...] = jnp.zeros_like(s_ref)

    p = pr_ref[0]
    prev_row = jnp.where(c > 0, pv_ref[0][7:8, :], 0.0)
    row = lax.broadcasted_iota(I32, (rows_n, 1), 0)
    prev = jnp.where(row == 0, prev_row, pltpu.roll(p, 1, 0))
    p = p + (prev - p) * mu_ref[...]
    W = RWKV_WIDTH
    r = p[:, 0:W]
    k = p[:, W:2 * W]
    v = p[:, 2 * W:3 * W]
    w_low = p[:, 3 * W:3 * W + RWKV_W_RANK]
    a_low = p[:, 3 * W + RWKV_W_RANK:3 * W + RWKV_W_RANK + RWKV_A_RANK]
    g_low = p[:, 3 * W + RWKV_W_RANK + RWKV_A_RANK:]
    wx = w0_ref[...] + _mm(jnp.tanh(w_low), w2_ref[...])
    w = _log_sigmoid(wx) - 0.5
    logd = -jnp.exp(w)
    a = _sigmoid(a0_ref[...] + _mm(a_low, a2_ref[...]))
    g = _mm(_sigmoid(g_low), g2_ref[...])
    seg = seg_ref[...]
    kk = k * kk_ref[...]
    kk = kk * lax.rsqrt(jnp.maximum(_mm_split2_rhs_ones(kk * kk, seg), 1e-24))
    k = k * (1.0 + (a - 1.0) * ka_ref[...])
    bonus = _mm_split2_rhs_ones(r * k * rk_ref[...], seg) * v

    bonus_ref[...] = bonus
    gate_ref[...] = g

    incl, strict, diag = _tri_masks(CHUNK)
    tril = incl.astype(BF16)
    c_in = jnp.concatenate([_mm_split3(tril, logd[i * CHUNK:(i + 1) * CHUNK]) for i in range(ng)], axis=0)
    g3 = lambda t: t.reshape(ng, CHUNK, W)
    logd, c_in, r, k, v, kk, a = g3(logd), g3(c_in), g3(r), g3(k), g3(v), g3(kk), g3(a)
    c_last = c_in[:, CHUNK - 1:CHUNK, :]
    e_neg = jnp.exp(-c_in)
    e_end = jnp.exp(c_last - c_in)
    kka = kk * a
    per_head = (-kk * jnp.exp(c_in - logd), r * jnp.exp(c_in), kka * e_neg, k * e_neg, kka * e_end,
                k * e_end, v)
    GW = RWKV_SLAB
    n_slab = W // GW
    for i, t in enumerate(per_head):
        t = t.astype(BF16)
        for sl in range(n_slab):
            ops_ref[i, sl] = t[:, :, sl * GW:(sl + 1) * GW]
    w_end = jnp.exp(c_last)
    for sl in range(n_slab):
        wend_ref[sl] = w_end[:, :, sl * GW:(sl + 1) * GW]

    hpl = GW // RWKV_HEAD
    lane_head = lax.broadcasted_iota(I32, (1, 1, GW), 2) // RWKV_HEAD
    head_masks = [lane_head == h for h in range(hpl)]
    rr = lax.broadcasted_iota(I32, (CHUNK, GW), 0)
    cc = lax.broadcasted_iota(I32, (CHUNK, GW), 1) % RWKV_HEAD
    incl4, strict4, eye4 = rr >= cc, rr > cc, (rr == cc).astype(F32)

    def block_diag(x):
        x = x.astype(BF16)
        return jnp.concatenate([jnp.where(m, x, jnp.zeros_like(x)) for m in head_masks], axis=1)

    def head_blocks(full):
        out = jnp.where(head_masks[0], full[:, :RWKV_HEAD], 0.0)
        for h in range(1, hpl):
            out = out + jnp.where(head_masks[h], full[:, h * RWKV_HEAD:(h + 1) * RWKV_HEAD], 0.0)
        return out

    bmm = lambda x, y: jnp.einsum('gts,gsd->gtd', x.astype(BF16), y.astype(BF16), preferred_element_type=F32)
    bmm_nt = lambda x, y: jnp.einsum('gtd,gsd->gts', x.astype(BF16), y.astype(BF16), preferred_element_type=F32)
    bmm_tn = lambda x, y: jnp.einsum('gtk,gtd->gkd', x.astype(BF16), y.astype(BF16), preferred_element_type=F32)

    def slab_body(sl, carry):
        at, rt, bt, kt, bh, kh, vv = [ops_ref[i, sl] for i in range(7)]
        gm = bmm_nt(jnp.concatenate([at, rt], axis=1),
                    jnp.concatenate([block_diag(bt), block_diag(kt)], axis=1))
        a_ab = jnp.where(strict4, gm[:, :CHUNK, :GW], 0.0)
        a_ak = jnp.where(strict4, gm[:, :CHUNK, GW:], 0.0)
        a_rb = jnp.where(incl4, gm[:, CHUNK:, :GW], 0.0)
        a_rk = jnp.where(incl4, gm[:, CHUNK:, GW:], 0.0)
        akv = bmm(jnp.concatenate([a_ak, a_rk], axis=1), block_diag(vv))
        xk = eye4 + a_ab
        pk = bmm(a_ab, block_diag(a_ab))
        for _ in range(4):
            both = bmm(jnp.concatenate([xk, pk], axis=1), block_diag(pk))
            xk = xk + both[:, :CHUNK]
            pk = both[:, CHUNK:]
        xk = xk + bmm(xk, block_diag(pk))
        pq = bmm(xk, jnp.concatenate([block_diag(at), block_diag(akv[:, :CHUNK])], axis=2))
        p_bd, q_bd = block_diag(pq[:, :, :GW]), block_diag(pq[:, :, GW:])
        ry = bmm(a_rb, jnp.concatenate([p_bd, q_bd], axis=2)) + jnp.concatenate(
            [rt.astype(F32), akv[:, CHUNK:]], axis=2)
        m_sbs = head_blocks(bmm_tn(bh, pq[:, :, :GW])) + eye4 * wend_ref[sl]
        n_sbs = head_blocks(bmm_tn(jnp.concatenate([bh, kh], axis=1),
                                   jnp.concatenate([pq[:, :, GW:].astype(BF16), vv], axis=1)))
        rm_ref[sl] = jnp.concatenate([ry[:, :, :GW], m_sbs], axis=1).astype(BF16)
        yn_ref[sl] = jnp.concatenate([ry[:, :, GW:], n_sbs], axis=1)
        return carry

    lax.fori_loop(0, n_slab, slab_body, 0)

    def chunk_body(ci, carry):
        for sl in range(n_slab):
            lanes = slice(sl * GW, (sl + 1) * GW)
            st = s_ref[:, lanes]
            st_bd = jnp.concatenate([jnp.where(m[0], st, 0.0) for m in head_masks], axis=0).astype(BF16)
            res = jnp.dot(rm_ref[sl, ci], st_bd, preferred_element_type=F32) + yn_ref[sl, ci]
            y_ref[ci, :, lanes] = res[:CHUNK]
            s_ref[:, lanes] = res[CHUNK:]
        return carry

    lax.fori_loop(0, ng, chunk_body, 0)

    y = y_ref[...].reshape(rows_n, W)
    mean = _mm_split2_rhs_ones(y, seg) * (1.0 / RWKV_HEAD)
    yc = y - mean
    var = _mm_split2_rhs_ones(yc * yc, seg) * (1.0 / RWKV_HEAD)
    yn = yc * lax.rsqrt(var + RWKV_LN_EPS)
    o_ref[0] = (yn * lng_ref[...] + lnb_ref[...] + bonus_ref[...]) * gate_ref[...]


def _rwkv(pr, mu, w0, w2, a0, a2, g2, k_k, k_a, r_k, ln_g, ln_b, seg):
    bn, lp, _ = pr.shape
    nc = lp // CHUNK
    ng = max(d for d in range(1, RWKV_GROUP + 1) if nc % d == 0)
    rows = ng * CHUNK
    vec = lambda n: pl.BlockSpec((1, n), lambda b, c: (0, 0))
    mat = lambda m, n: pl.BlockSpec((m, n), lambda b, c: (0, 0))
    hd = RWKV_HEAD
    n_slab = RWKV_WIDTH // RWKV_SLAB
    return pl.pallas_call(
        _rwkv_kernel,
        name="rwkv_mixer",
        grid=(bn, nc // ng),
        in_specs=[
            pl.BlockSpec((1, rows, RWKV_COLS), lambda b, c: (b, c, 0)),
            pl.BlockSpec((1, 8, RWKV_COLS), lambda b, c: (b, jnp.maximum(c * (rows // 8) - 1, 0), 0)),
            vec(RWKV_COLS), vec(RWKV_WIDTH), mat(RWKV_W_RANK, RWKV_WIDTH), vec(RWKV_WIDTH),
            mat(RWKV_A_RANK, RWKV_WIDTH), mat(RWKV_G_RANK, RWKV_WIDTH), vec(RWKV_WIDTH), vec(RWKV_WIDTH),
            vec(RWKV_WIDTH), vec(RWKV_WIDTH), vec(RWKV_WIDTH), mat(RWKV_WIDTH, RWKV_WIDTH),
        ],
        out_specs=pl.BlockSpec((1, rows, RWKV_WIDTH), lambda b, c: (b, c, 0)),
        out_shape=jax.ShapeDtypeStruct((bn, lp, RWKV_WIDTH), F32),
        scratch_shapes=[
            pltpu.VMEM((hd, RWKV_WIDTH), F32),
            pltpu.VMEM((7, n_slab, ng, CHUNK, RWKV_SLAB), BF16),
            pltpu.VMEM((n_slab, ng, 1, RWKV_SLAB), F32),
            pltpu.VMEM((n_slab, ng, 2 * CHUNK, RWKV_SLAB), BF16),
            pltpu.VMEM((n_slab, ng, 2 * CHUNK, RWKV_SLAB), F32),
            pltpu.VMEM((ng, CHUNK, RWKV_WIDTH), F32),
            pltpu.VMEM((rows, RWKV_WIDTH), F32),
            pltpu.VMEM((rows, RWKV_WIDTH), F32),
        ],
        compiler_params=pltpu.CompilerParams(dimension_semantics=("arbitrary", "arbitrary"),
                                             vmem_limit_bytes=VMEM_LIMIT),
    )(pr, pr, mu, w0, w2, a0, a2, g2, k_k, k_a, r_k, ln_g, ln_b, seg.astype(BF16))


def _outproj_router_kernel(yg_ref, yr_ref, h_ref, wo_ref, g_ref, b_ref, rwt_ref, rb_ref, tri_ref,
                           h1_ref, idx_ref, gate_ref, rank_ref, cnt_ref, base_ref):
    i = pl.program_id(0)

    @pl.when(i == 0)
    def _():
        base_ref[...] = jnp.zeros_like(base_ref)

    wo = wo_ref[...]
    mix = _mm(yg_ref[...], wo[:GLA_WIDTH]) + _mm(yr_ref[...], wo[GLA_WIDTH:])
    h1 = _layer_norm(DEEPNORM_ALPHA * h_ref[...] + mix, g_ref[...], b_ref[...])
    _store_row_tiles(h1_ref, h1)
    h_hi = h1.astype(BF16)
    h_lo = (h1 - h_hi.astype(F32)).astype(BF16)
    nt = (((1,), (1,)), ((), ()))
    part = lax.dot_general(rwt_ref[...], h_hi, nt, preferred_element_type=F32)
    work = (part[:N_EXPERTS] + part[N_EXPERTS:]
            + lax.dot_general(rwt_ref[:N_EXPERTS, :], h_lo, nt, preferred_element_type=F32)
            + rb_ref[...][:, 0:1])
    tm = work.shape[1]
    e_iota = lax.broadcasted_iota(I32, (N_EXPERTS, tm), 0)
    base = base_ref[...][:, 0:1]
    vals, onehots = [], []
    for kk in range(TOP_K):
        m = jnp.max(work, axis=0, keepdims=True)
        sel = jnp.min(jnp.where(work == m, e_iota, N_EXPERTS), axis=0, keepdims=True)
        onehot = e_iota == sel
        work = jnp.where(onehot, -jnp.inf, work)
        vals.append(m)
        onehots.append(onehot.astype(F32))
        idx_ref[kk:kk + 1, :] = sel
    cnt_all = jnp.dot(jnp.concatenate(onehots, axis=0).astype(BF16), tri_ref[...], preferred_element_type=F32)
    prior = jnp.zeros((N_EXPERTS, 1), F32)
    for kk in range(TOP_K):
        cnt = cnt_all[kk * N_EXPERTS:(kk + 1) * N_EXPERTS]
        rank = jnp.sum(onehots[kk] * (base + prior + cnt - 1.0), axis=0, keepdims=True)
        prior = prior + cnt[:, tm - 1:tm]
        rank_ref[kk:kk + 1, :] = rank.astype(I32)
    es = [jnp.exp(vv - vals[0]) for vv in vals]
    den = es[0] + es[1] + es[2] + es[3]
    for kk in range(TOP_K):
        gate_ref[kk:kk + 1, :] = es[kk] / den
    new_base = base + prior
    base_ref[...] = jnp.broadcast_to(new_base, base_ref.shape)
    cnt_ref[...] = jnp.broadcast_to(new_base, cnt_ref.shape)


def _outproj_router(yg, yr, h, wo, g, b, rwt, rb, tri):
    tp = h.shape[0]
    tm = ROUTER_TILE
    const = lambda m, n: pl.BlockSpec((m, n), lambda i: (0, 0))
    return pl.pallas_call(
        _outproj_router_kernel,
        name="outproj_router",
        grid=(tp // tm,),
        in_specs=[
            pl.BlockSpec((tm, GLA_WIDTH), lambda i: (i, 0)),
            pl.BlockSpec((tm, RWKV_WIDTH), lambda i: (i, 0)),
            pl.BlockSpec((tm, D_MODEL), lambda i: (i, 0)),
            const(D_MODEL, D_MODEL), const(1, D_MODEL), const(1, D_MODEL),
            const(2 * N_EXPERTS, D_MODEL), const(N_EXPERTS, 128), const(tm, tm),
        ],
        out_specs=[
            pl.BlockSpec(_tiled_rows(tm), lambda i: (i, 0)),
            pl.BlockSpec((TOP_K, tm), lambda i: (0, i)),
            pl.BlockSpec((TOP_K, tm), lambda i: (0, i)),
            pl.BlockSpec((TOP_K, tm), lambda i: (0, i)),
            pl.BlockSpec((N_EXPERTS, 128), lambda i: (0, 0)),
        ],
        out_shape=[
            jax.ShapeDtypeStruct(_tiled_rows(tp), F32),
            jax.ShapeDtypeStruct((TOP_K, tp), I32),
            jax.ShapeDtypeStruct((TOP_K, tp), F32),
            jax.ShapeDtypeStruct((TOP_K, tp), I32),
            jax.ShapeDtypeStruct((N_EXPERTS, 128), F32),
        ],
        scratch_shapes=[pltpu.VMEM((N_EXPERTS, 128), F32)],
        compiler_params=pltpu.CompilerParams(dimension_semantics=("arbitrary",),
                                             vmem_limit_bytes=VMEM_LIMIT),
    )(yg, yr, h, wo, g, b, rwt, rb, tri)


def _zero_blocks_kernel(last_ref, o_ref):
    o_ref[...] = jnp.zeros_like(o_ref)


def _zero_blocks(last_block, n_slots):
    return pl.pallas_call(
        _zero_blocks_kernel,
        name="moe_zero_blocks",
        grid_spec=pltpu.PrefetchScalarGridSpec(
            num_scalar_prefetch=1,
            grid=(N_EXPERTS,),
            in_specs=[],
            out_specs=pl.BlockSpec(_tiled_rows(MOE_BLOCK), lambda e, last: (last[e], 0)),
        ),
        out_shape=jax.ShapeDtypeStruct(_tiled_rows(n_slots), F32),
        compiler_params=pltpu.CompilerParams(dimension_semantics=("arbitrary",)),
    )(last_block)


def _dispatch_kernel(pos_ref, x_ref, xs_in_ref, xs_ref, sem):
    del xs_in_ref
    tm = pos_ref.shape[0] // TOP_K

    def start(r, carry):
        for kk in range(TOP_K):
            pltpu.make_async_copy(_row_tile(x_ref, r), _row_tile(xs_ref, pos_ref[r * TOP_K + kk]),
                                  sem).start(priority=kk % 2)
        return carry

    lax.fori_loop(0, tm, start, 0, unroll=DMA_UNROLL)
    for kk in range(TOP_K):
        pltpu.make_async_copy(x_ref, xs_ref.at[pl.ds(0, x_ref.shape[0])], sem).wait()


def _dispatch(pos, x, xs):
    tp = pos.shape[0] // TOP_K
    tm = ROUTER_TILE
    return pl.pallas_call(
        _dispatch_kernel,
        name="moe_dispatch",
        grid=(tp // tm,),
        in_specs=[
            pl.BlockSpec((tm * TOP_K,), lambda i: (i,), memory_space=pltpu.SMEM),
            pl.BlockSpec(_tiled_rows(tm), lambda i: (i, 0)),
            pl.BlockSpec(memory_space=pl.ANY),
        ],
        out_specs=pl.BlockSpec(memory_space=pl.ANY),
        out_shape=jax.ShapeDtypeStruct(xs.shape, xs.dtype),
        scratch_shapes=[pltpu.SemaphoreType.DMA(())],
        input_output_aliases={2: 0},
        compiler_params=pltpu.CompilerParams(dimension_semantics=("arbitrary",)),
    )(pos, x, xs)


SPLIT_TILE = 256


def _moe_kernel(be_ref, nu_ref, xs_ref, wu_ref, wd_ref, perm_ref, bg_ref, bl_ref, bd_ref, ys_ref,
                wg_s, wl_s, wd_s):
    i = pl.program_id(0)

    @pl.when(i < nu_ref[0])
    def _():
        @pl.when(jnp.logical_or(i == 0, be_ref[i] != be_ref[jnp.maximum(i - 1, 0)]))
        def _():
            half = SPLIT_TILE // 2
            for t in range(wu_ref.shape[2] // SPLIT_TILE):
                d = jnp.dot(wu_ref[0, :, t * SPLIT_TILE:(t + 1) * SPLIT_TILE].astype(BF16), perm_ref[...],
                            preferred_element_type=F32).astype(BF16)
                wg_s[:, t * half:(t + 1) * half] = d[:, :half]
                wl_s[:, t * half:(t + 1) * half] = d[:, half:]
            wd_s[...] = wd_ref[0].astype(BF16)

        x = _load_row_tiles(xs_ref).astype(BF16)
        x_glu = jnp.dot(x, wg_s[...], preferred_element_type=F32) + bg_ref[0]
        x_lin = jnp.dot(x, wl_s[...], preferred_element_type=F32) + bl_ref[0]
        x_glu = jnp.minimum(x_glu, SWIGLU_LIMIT)
        x_lin = jnp.clip(x_lin, -SWIGLU_LIMIT, SWIGLU_LIMIT)
        act = x_glu * _sigmoid(SWIGLU_ALPHA * x_glu) * (x_lin + 1.0)
        _store_row_tiles(ys_ref, jnp.dot(act.astype(BF16), wd_s[...], preferred_element_type=F32) + bd_ref[0])


def _moe(block_e, n_used, xs, w_up, w_down, bg, bl, bd):
    n_slots = xs.shape[0] // ROW_TILE[0]
    nb = n_slots // MOE_BLOCK
    src = jnp.arange(SPLIT_TILE, dtype=I32)[:, None]
    dst = jnp.arange(SPLIT_TILE, dtype=I32)[None, :]
    half = SPLIT_TILE // 2
    perm = (src == jnp.where(dst < half, 2 * dst, 2 * (dst - half) + 1)).astype(BF16)
    blk = lambda i, be, nu: (jnp.maximum(jnp.minimum(i, nu[0] - 1), 0), 0)
    bspec = pl.BlockSpec((1, 1, D_FF), lambda i, be, nu: (be[i], 0, 0))
    return pl.pallas_call(
        _moe_kernel,
        name="moe_experts",
        grid_spec=pltpu.PrefetchScalarGridSpec(
            num_scalar_prefetch=2,
            grid=(nb,),
            in_specs=[
                pl.BlockSpec(_tiled_rows(MOE_BLOCK), blk),
                pl.BlockSpec((1, D_MODEL, 2 * D_FF), lambda i, be, nu: (be[i], 0, 0)),
                pl.BlockSpec((1, D_FF, D_MODEL), lambda i, be, nu: (be[i], 0, 0)),
                pl.BlockSpec((SPLIT_TILE, SPLIT_TILE), lambda i, be, nu: (0, 0)),
                bspec, bspec, bspec,
            ],
            out_specs=pl.BlockSpec(_tiled_rows(MOE_BLOCK), blk),
            scratch_shapes=[
                pltpu.VMEM((D_MODEL, D_FF), BF16),
                pltpu.VMEM((D_MODEL, D_FF), BF16),
                pltpu.VMEM((D_FF, D_MODEL), BF16),
            ],
        ),
        out_shape=jax.ShapeDtypeStruct(_tiled_rows(n_slots), F32),
        compiler_params=pltpu.CompilerParams(dimension_semantics=("arbitrary",),
                                             vmem_limit_bytes=VMEM_LIMIT),
    )(block_e, n_used, xs, w_up, w_down, perm, bg, bl, bd)


def _combine_kernel(pos_ref, posn_ref, gt_ref, h1_ref, g_ref, b_ref, ys_ref, o_ref, buf, sem):
    s = pl.program_id(0)
    rows = TOP_K * CHUNK

    def issue_row(p_ref, to_slot, r):
        for kk in range(TOP_K):
            pltpu.make_async_copy(_row_tile(ys_ref, p_ref[r * TOP_K + kk]),
                                  _row_tile(buf, to_slot * rows + kk * CHUNK + r),
                                  sem.at[to_slot]).start(priority=kk % 2)

    def wait_slot(which):
        span = rows * ROW_TILE[0]
        pltpu.make_async_copy(ys_ref.at[pl.ds(0, span)], buf.at[pl.ds(which * span, span)], sem.at[which]).wait()

    @pl.when(s == 0)
    def _():
        def body(r, carry):
            issue_row(pos_ref, 0, r)
            return carry

        lax.fori_loop(0, CHUNK, body, 0, unroll=DMA_UNROLL)

    def step(cur, nxt):
        wait_slot(cur)
        gt = gt_ref[...]
        quarter = CHUNK // TOP_K
        ffn = None
        for kk in range(TOP_K):
            part = _load_row_tiles(buf, cur * rows + kk * CHUNK, CHUNK) * gt[:, kk:kk + 1]
            ffn = part if ffn is None else ffn + part
            for r in range(kk * quarter, (kk + 1) * quarter):
                issue_row(posn_ref, nxt, r)
        o_ref[0] = _layer_norm(DEEPNORM_ALPHA * _load_row_tiles(h1_ref) + ffn, g_ref[...], b_ref[...])

        @pl.when(s == pl.num_programs(0) - 1)
        def _():
            wait_slot(nxt)

    @pl.when(s % 2 == 0)
    def _():
        step(0, 1)

    @pl.when(s % 2 == 1)
    def _():
        step(1, 0)


def _combine(pos_flat, gates_t, h1, g, b, ys, bn, lp):
    seq = lp - CHUNK
    nc = lp // CHUNK
    ncs = seq // CHUNK
    n_steps = bn * ncs
    chunk_of = lambda s: (s // ncs) * nc + s % ncs + 1
    nxt = lambda s: jnp.minimum(s + 1, n_steps - 1)
    return pl.pallas_call(
        _combine_kernel,
        name="moe_combine",
        grid=(n_steps,),
        in_specs=[
            pl.BlockSpec((TOP_K * CHUNK,), lambda s: (chunk_of(s),), memory_space=pltpu.SMEM),
            pl.BlockSpec((TOP_K * CHUNK,), lambda s: (chunk_of(nxt(s)),), memory_space=pltpu.SMEM),
            pl.BlockSpec((CHUNK, TOP_K), lambda s: (chunk_of(s), 0)),
            pl.BlockSpec(_tiled_rows(CHUNK), lambda s: (chunk_of(s), 0)),
            pl.BlockSpec((1, D_MODEL), lambda s: (0, 0)),
            pl.BlockSpec((1, D_MODEL), lambda s: (0, 0)),
            pl.BlockSpec(memory_space=pl.ANY),
        ],
        out_specs=pl.BlockSpec((1, CHUNK, D_MODEL), lambda s: (s // ncs, s % ncs, 0)),
        out_shape=jax.ShapeDtypeStruct((bn, seq, D_MODEL), F32),
        scratch_shapes=[pltpu.VMEM(_tiled_rows(2 * TOP_K * CHUNK), F32), pltpu.SemaphoreType.DMA((2,))],
        compiler_params=pltpu.CompilerParams(dimension_semantics=("arbitrary",)),
    )(pos_flat, pos_flat, gates_t, h1, g, b, ys)


def kernel(x, meta, ln_in_g, ln_in_b, w_in, gla_gk_w2, gla_gk_b, gla_norm_g, rwkv_mu, rwkv_w0, rwkv_w2, rwkv_a0, rwkv_a2, rwkv_g2, rwkv_k_k, rwkv_k_a, rwkv_r_k, rwkv_ln_g, rwkv_ln_b, w_out, ln1_g, ln1_b, router_w, router_b, exp_w_up, exp_b_up, exp_w_down, exp_b_down, ln2_g, ln2_b):
    bn, seq, _ = x.shape
    assert seq % CHUNK == 0
    lp = seq + CHUNK
    tp = bn * lp
    assert tp % ROUTER_TILE == 0
    row = lambda t: t.reshape(1, -1).astype(F32)

    hcat = jnp.concatenate([jnp.zeros((bn, N_FRONT, D_MODEL), F32),
                            jnp.broadcast_to(meta[None].astype(F32), (bn, N_META, D_MODEL)), x], axis=1)
    hcat = hcat.reshape(tp, D_MODEL)
    gla_in = 2 * GLA_KEY + 2 * GLA_WIDTH + GLA_GATE_RANK
    w = w_in[0]
    w_cols = jnp.concatenate([w[:, :gla_in], jnp.zeros((D_MODEL, 128 - GLA_GATE_RANK), F32), w[:, gla_in:]],
                             axis=1).astype(BF16)
    h, pg, pr = _ln_inproj(hcat, row(ln_in_g), row(ln_in_b), w_cols, lp)

    w2p = jnp.concatenate([gla_gk_w2[0], jnp.zeros((128 - GLA_GATE_RANK, GLA_KEY), F32)], axis=0)
    y_gla = _gla(pg.reshape(bn, lp, GLA_COLS), w2p, row(gla_gk_b[0]), row(gla_norm_g[0]))

    head_id = jnp.arange(RWKV_WIDTH, dtype=I32) // RWKV_HEAD
    seg = (head_id[:, None] == head_id[None, :]).astype(F32)
    y_rwkv = _rwkv(pr.reshape(bn, lp, RWKV_COLS), row(rwkv_mu[0]), row(rwkv_w0[0]), rwkv_w2[0],
                   row(rwkv_a0[0]), rwkv_a2[0], rwkv_g2[0], row(rwkv_k_k[0]), row(rwkv_k_a[0]),
                   row(rwkv_r_k[0]), row(rwkv_ln_g[0]), row(rwkv_ln_b[0]), seg)

    tri = jnp.triu(jnp.ones((ROUTER_TILE, ROUTER_TILE), F32)).astype(BF16)
    rb = jnp.broadcast_to(router_b[0].reshape(N_EXPERTS, 1), (N_EXPERTS, 128))
    rw_t = router_w[0].T
    rw_hi = rw_t.astype(BF16)
    rwt = jnp.concatenate([rw_hi, (rw_t - rw_hi.astype(F32)).astype(BF16)], axis=0)
    h1, idx, gates, rank, cnt = _outproj_router(
        y_gla.reshape(tp, GLA_WIDTH), y_rwkv.reshape(tp, RWKV_WIDTH), h, w_out[0].astype(BF16),
        row(ln1_g[0]), row(ln1_b[0]), rwt, rb, tri)

    counts = cnt[:, 0].astype(I32)
    padded = (counts + MOE_BLOCK - 1) // MOE_BLOCK * MOE_BLOCK
    ends_p = jnp.cumsum(padded)
    starts_p = ends_p - padded
    e_ids = jnp.arange(N_EXPERTS, dtype=I32)
    start_of = jnp.sum(jnp.where(idx[None] == e_ids[:, None, None], starts_p[:, None, None], 0), axis=0)
    pos = start_of + rank
    nb = tp * TOP_K // MOE_BLOCK + N_EXPERTS
    n_slots = nb * MOE_BLOCK
    block_start = jnp.arange(nb, dtype=I32) * MOE_BLOCK
    block_e = jnp.minimum(jnp.sum((block_start[:, None] >= ends_p[None, :]).astype(I32), axis=1), N_EXPERTS - 1)
    n_used = (ends_p[-1:] // MOE_BLOCK).astype(I32)
    last_block = jnp.maximum(ends_p // MOE_BLOCK - 1, 0).astype(I32)

    pos_flat = pos.T.reshape(tp * TOP_K)
    xs = _dispatch(pos_flat, h1, _zero_blocks(last_block, n_slots))
    bg = exp_b_up[0][:, None, 0::2]
    bl = exp_b_up[0][:, None, 1::2]
    ys = _moe(block_e, n_used, xs, exp_w_up[0], exp_w_down[0], bg, bl, exp_b_down[0][:, None, :])

    return _combine(pos_flat, gates.T, h1, row(ln2_g[0]), row(ln2_b[0]), ys, bn, lp)
```

```python
import functools

import jax
import jax.numpy as jnp
from jax import lax
from jax.experimental import pallas as pl
from jax.experimental.pallas import tpu as pltpu

F32 = jnp.float32
BF16 = jnp.bfloat16
I32 = jnp.int32
HIGHEST = lax.Precision.HIGHEST

D_MODEL = 1024
N_META = 16
CHUNK = 64
N_FRONT = (-N_META) % CHUNK
GLA_HEADS = 4
GLA_DK = 64
GLA_DV = 128
GLA_KEY = GLA_HEADS * GLA_DK
GLA_WIDTH = GLA_HEADS * GLA_DV
GLA_GATE_RANK = 16
GLA_TAU = 16.0
GLA_COLS = 2 * GLA_KEY + 2 * GLA_WIDTH + 128
RWKV_WIDTH = 512
RWKV_HEAD = 64
RWKV_HEADS = RWKV_WIDTH // RWKV_HEAD
RWKV_W_RANK = 64
RWKV_A_RANK = 64
RWKV_G_RANK = 128
RWKV_COLS = 3 * RWKV_WIDTH + RWKV_W_RANK + RWKV_A_RANK + RWKV_G_RANK
N_EXPERTS = 32
TOP_K = 4
D_FF = D_MODEL
SWIGLU_ALPHA = 1.702
SWIGLU_LIMIT = 7.0
MOE_BLOCK = 512
DEPTH = 1
DEEPNORM_ALPHA = (2.0 * DEPTH) ** 0.25
LN_EPS = 1e-5
RWKV_LN_EPS = 64e-5
RMS_EPS = 1e-6

ROUTER_TILE = 768
DMA_UNROLL = 4
VMEM_LIMIT = 56 * 1024 * 1024


def _mm(a, b):
    return jnp.dot(a.astype(BF16), b.astype(BF16), preferred_element_type=F32)


def _mm_nt(a, b):
    return lax.dot_general(a.astype(BF16), b.astype(BF16), (((1,), (1,)), ((), ())),
                           preferred_element_type=F32)


def _mm_tn(a, b):
    return lax.dot_general(a.astype(BF16), b.astype(BF16), (((0,), (0,)), ((), ())),
                           preferred_element_type=F32)


def _mm_f32(a, b):
    return jnp.dot(a, b, preferred_element_type=F32, precision=HIGHEST)


def _layer_norm(x, g, b):
    mu = jnp.mean(x, axis=-1, keepdims=True)
    xc = x - mu
    var = jnp.mean(xc * xc, axis=-1, keepdims=True)
    return xc * lax.rsqrt(var + LN_EPS) * g + b


def _sigmoid(x):
    return 1.0 / (1.0 + jnp.exp(-x))


def _log_sigmoid(x):
    return jnp.minimum(x, 0.0) - jnp.log(1.0 + jnp.exp(-jnp.abs(x)))


ROW_TILE = (8, 128)


def _tiled_rows(n):
    return (n * ROW_TILE[0], ROW_TILE[1])


def _row_tile(ref, i):
    return ref.at[pl.ds(pl.multiple_of(i * ROW_TILE[0], ROW_TILE[0]), ROW_TILE[0])]


def _store_row_tiles(ref, x, row0=0):
    n = x.shape[0]
    for j in range(ROW_TILE[0]):
        ref[pl.ds(row0 * ROW_TILE[0] + j, n, stride=ROW_TILE[0]), :] = x[:, j * ROW_TILE[1]:(j + 1) * ROW_TILE[1]]


def _load_row_tiles(ref, row0=0, n=None):
    n = ref.shape[0] // ROW_TILE[0] if n is None else n
    return jnp.concatenate([ref[pl.ds(row0 * ROW_TILE[0] + j, n, stride=ROW_TILE[0]), :]
                            for j in range(ROW_TILE[0])], axis=1)


def _tri_masks(n):
    r = lax.broadcasted_iota(I32, (n, n), 0)
    c = lax.broadcasted_iota(I32, (n, n), 1)
    return r >= c, r > c, r == c


def _ln_inproj_kernel(x_ref, g_ref, b_ref, w_ref, h_ref, pg_ref, pr_ref, *, tiles_per_seq):
    i = pl.program_id(0)
    y = _layer_norm(x_ref[...], g_ref[...], b_ref[...])
    row = lax.broadcasted_iota(I32, (y.shape[0], 1), 0)
    is_front = jnp.logical_and(i % tiles_per_seq == 0, row < N_FRONT)
    y = jnp.where(is_front, 0.0, y)
    h_ref[...] = y
    p = _mm(y, w_ref[...])
    pg_ref[...] = p[:, :GLA_COLS]
    pr_ref[...] = p[:, GLA_COLS:]


def _ln_inproj(hcat, g, b, w, lp):
    tp = hcat.shape[0]
    tiles_per_seq = 1
    for cand in range(1, lp // 8 + 1):
        if lp % cand == 0 and (lp // cand) % 8 == 0 and lp // cand >= N_FRONT and lp // cand <= 384:
            tiles_per_seq = cand
            break
    tm = lp // tiles_per_seq
    ncols = GLA_COLS + RWKV_COLS
    return pl.pallas_call(
        functools.partial(_ln_inproj_kernel, tiles_per_seq=tiles_per_seq),
        name="ln_inproj",
        grid=(tp // tm,),
        in_specs=[
            pl.BlockSpec((tm, D_MODEL), lambda i: (i, 0)),
            pl.BlockSpec((1, D_MODEL), lambda i: (0, 0)),
            pl.BlockSpec((1, D_MODEL), lambda i: (0, 0)),
            pl.BlockSpec((D_MODEL, ncols), lambda i: (0, 0)),
        ],
        out_specs=[
            pl.BlockSpec((tm, D_MODEL), lambda i: (i, 0)),
            pl.BlockSpec((tm, GLA_COLS), lambda i: (i, 0)),
            pl.BlockSpec((tm, RWKV_COLS), lambda i: (i, 0)),
        ],
        out_shape=[
            jax.ShapeDtypeStruct((tp, D_MODEL), F32),
            jax.ShapeDtypeStruct((tp, GLA_COLS), F32),
            jax.ShapeDtypeStruct((tp, RWKV_COLS), F32),
        ],
        compiler_params=pltpu.CompilerParams(dimension_semantics=("arbitrary",),
                                             vmem_limit_bytes=VMEM_LIMIT),
    )(hcat, g, b, w)


GLA_GROUP = 11


def _gla_kernel(pg_ref, w2_ref, gkb_ref, ng_ref, o_ref, st_ref, qe_ref, oi_ref, kvt_ref, el_ref):
    c = pl.program_id(1)

    @pl.when(c == 0)
    def _():
        st_ref[...] = jnp.zeros_like(st_ref)

    rows_n = pg_ref.shape[1]
    ng = rows_n // CHUNK
    g_off = 2 * GLA_KEY + GLA_WIDTH
    p = pg_ref[0]
    gl = p[:, g_off + GLA_WIDTH:]
    lg = _log_sigmoid(_mm_f32(gl, w2_ref[...]) + gkb_ref[...]) * (1.0 / GLA_TAU)
    row = lax.broadcasted_iota(I32, (rows_n, 1), 0)
    lg = jnp.where(jnp.logical_and(c == 0, row < N_FRONT), 0.0, lg)
    incl, _, _ = _tri_masks(CHUNK)
    tril = incl.astype(BF16)
    bc = jnp.concatenate([_mm_split3(tril, lg[i * CHUNK:(i + 1) * CHUNK]) for i in range(ng)], axis=0)
    g3 = lambda t: t.reshape(ng, CHUNK, t.shape[-1])
    bc = g3(bc)
    b_last = bc[:, CHUNK - 1:CHUNK, :]
    k = g3(p[:, GLA_KEY:2 * GLA_KEY])
    qe = g3(p[:, 0:GLA_KEY]) * (GLA_DK ** -0.5) * jnp.exp(bc)
    ke = k * jnp.exp(-bc)
    kl = k * jnp.exp(b_last - bc)
    e_last = jnp.exp(b_last)
    v = g3(p[:, 2 * GLA_KEY:g_off]).astype(BF16)
    for h in range(GLA_HEADS):
        ks = slice(h * GLA_DK, (h + 1) * GLA_DK)
        vs = slice(h * GLA_DV, (h + 1) * GLA_DV)
        qh = qe[:, :, ks].astype(BF16)
        a = jnp.einsum('gtd,gsd->gts', qh, ke[:, :, ks].astype(BF16), preferred_element_type=F32)
        a = jnp.where(incl, a, 0.0).astype(BF16)
        oi_ref[:, h] = jnp.einsum('gts,gsv->gtv', a, v[:, :, vs], preferred_element_type=F32)
        kvt_ref[:, h] = jnp.einsum('gtv,gtd->gvd', v[:, :, vs], kl[:, :, ks].astype(BF16),
                                   preferred_element_type=F32)
        qe_ref[:, h] = qh
        el_ref[:, h] = e_last[:, :, ks]

    def chunk_body(ci, carry):
        st = st_ref[...]
        oi_ref[ci] = oi_ref[ci] + jnp.einsum('htd,hvd->htv', qe_ref[ci], st.astype(BF16),
                                             preferred_element_type=F32)
        st_ref[...] = st * el_ref[ci] + kvt_ref[ci]
        return carry

    lax.fori_loop(0, ng, chunk_body, 0)

    for ci in range(ng):
        o = oi_ref[ci]
        o = o * lax.rsqrt(jnp.mean(o * o, axis=-1, keepdims=True) + RMS_EPS) * ng_ref[...]
        o = jnp.concatenate([o[h] for h in range(GLA_HEADS)], axis=1)
        rows = slice(ci * CHUNK, (ci + 1) * CHUNK)
        gate = pg_ref[0, rows, g_off:g_off + GLA_WIDTH]
        o_ref[0, rows, :] = o * (gate * _sigmoid(gate))


def _gla(pg, w2p, gkb, ng_w):
    bn, lp, _ = pg.shape
    nc = lp // CHUNK
    ng = max(d for d in range(1, GLA_GROUP + 1) if nc % d == 0)
    rows = ng * CHUNK
    hh = GLA_HEADS
    return pl.pallas_call(
        _gla_kernel,
        name="gla_mixer",
        grid=(bn, nc // ng),
        in_specs=[
            pl.BlockSpec((1, rows, GLA_COLS), lambda b, c: (b, c, 0)),
            pl.BlockSpec((128, GLA_KEY), lambda b, c: (0, 0)),
            pl.BlockSpec((1, GLA_KEY), lambda b, c: (0, 0)),
            pl.BlockSpec((1, GLA_DV), lambda b, c: (0, 0)),
        ],
        out_specs=pl.BlockSpec((1, rows, GLA_WIDTH), lambda b, c: (b, c, 0)),
        out_shape=jax.ShapeDtypeStruct((bn, lp, GLA_WIDTH), F32),
        scratch_shapes=[
            pltpu.VMEM((hh, GLA_DV, GLA_DK), F32),
            pltpu.VMEM((ng, hh, CHUNK, GLA_DK), BF16),
            pltpu.VMEM((ng, hh, CHUNK, GLA_DV), F32),
            pltpu.VMEM((ng, hh, GLA_DV, GLA_DK), F32),
            pltpu.VMEM((ng, hh, 1, GLA_DK), F32),
        ],
        compiler_params=pltpu.CompilerParams(dimension_semantics=("arbitrary", "arbitrary"),
                                             vmem_limit_bytes=VMEM_LIMIT),
    )(pg, w2p, gkb, ng_w)


RWKV_GROUP = 11
RWKV_SLAB = 256
def _mm_split3(ones_bf16, x):
    hi = x.astype(BF16)
    r1 = x - hi.astype(F32)
    mid = r1.astype(BF16)
    lo = (r1 - mid.astype(F32)).astype(BF16)
    return jnp.dot(jnp.concatenate([ones_bf16] * 3, axis=1), jnp.concatenate([hi, mid, lo], axis=0),
                   preferred_element_type=F32)


def _mm_split2_rhs_ones(x, ones_bf16):
    hi = x.astype(BF16)
    mid = (x - hi.astype(F32)).astype(BF16)
    dot = lambda t: jnp.dot(t, ones_bf16, preferred_element_type=F32)
    return dot(hi) + dot(mid)


def _rwkv_kernel(pr_ref, pv_ref, mu_ref, w0_ref, w2_ref, a0_ref, a2_ref, g2_ref, kk_ref, ka_ref,
                 rk_ref, lng_ref, lnb_ref, seg_ref, o_ref, s_ref, ops_ref, wend_ref, rm_ref, yn_ref,
                 y_ref, bonus_ref, gate_ref):
    c = pl.program_id(1)
    rows_n = pr_ref.shape[1]
    ng = rows_n // CHUNK

    @pl.when(c == 0)
    def _():
        s_ref[...] = jnp.zeros_like(s_ref)

    p = pr_ref[0]
    prev_row = jnp.where(c > 0, pv_ref[0][7:8, :], 0.0)
    row = lax.broadcasted_iota(I32, (rows_n, 1), 0)
    prev = jnp.where(row == 0, prev_row, pltpu.roll(p, 1, 0))
    p = p + (prev - p) * mu_ref[...]
    W = RWKV_WIDTH
    r = p[:, 0:W]
    k = p[:, W:2 * W]
    v = p[:, 2 * W:3 * W]
    w_low = p[:, 3 * W:3 * W + RWKV_W_RANK]
    a_low = p[:, 3 * W + RWKV_W_RANK:3 * W + RWKV_W_RANK + RWKV_A_RANK]
    g_low = p[:, 3 * W + RWKV_W_RANK + RWKV_A_RANK:]
    wx = w0_ref[...] + _mm(jnp.tanh(w_low), w2_ref[...])
    w = _log_sigmoid(wx) - 0.5
    logd = -jnp.exp(w)
    a = _sigmoid(a0_ref[...] + _mm(a_low, a2_ref[...]))
    g = _mm(_sigmoid(g_low), g2_ref[...])
    seg = seg_ref[...]
    kk = k * kk_ref[...]
    kk = kk * lax.rsqrt(jnp.maximum(_mm_split2_rhs_ones(kk * kk, seg), 1e-24))
    k = k * (1.0 + (a - 1.0) * ka_ref[...])
    bonus = _mm_split2_rhs_ones(r * k * rk_ref[...], seg) * v

    bonus_ref[...] = bonus
    gate_ref[...] = g

    incl, strict, diag = _tri_masks(CHUNK)
    tril = incl.astype(BF16)
    c_in = jnp.concatenate([_mm_split3(tril, logd[i * CHUNK:(i + 1) * CHUNK]) for i in range(ng)], axis=0)
    g3 = lambda t: t.reshape(ng, CHUNK, W)
    logd, c_in, r, k, v, kk, a = g3(logd), g3(c_in), g3(r), g3(k), g3(v), g3(kk), g3(a)
    c_last = c_in[:, CHUNK - 1:CHUNK, :]
    e_neg = jnp.exp(-c_in)
    e_end = jnp.exp(c_last - c_in)
    kka = kk * a
    per_head = (-kk * jnp.exp(c_in - logd), r * jnp.exp(c_in), kka * e_neg, k * e_neg, kka * e_end,
                k * e_end, v)
    GW = RWKV_SLAB
    n_slab = W // GW
    for i, t in enumerate(per_head):
        t = t.astype(BF16)
        for sl in range(n_slab):
            ops_ref[i, sl] = t[:, :, sl * GW:(sl + 1) * GW]
    w_end = jnp.exp(c_last)
    for sl in range(n_slab):
        wend_ref[sl] = w_end[:, :, sl * GW:(sl + 1) * GW]

    hpl = GW // RWKV_HEAD
    lane_head = lax.broadcasted_iota(I32, (1, 1, GW), 2) // RWKV_HEAD
    head_masks = [lane_head == h for h in range(hpl)]
    rr = lax.broadcasted_iota(I32, (CHUNK, GW), 0)
    cc = lax.broadcasted_iota(I32, (CHUNK, GW), 1) % RWKV_HEAD
    incl4, strict4, eye4 = rr >= cc, rr > cc, (rr == cc).astype(F32)

    def block_diag(x):
        x = x.astype(BF16)
        return jnp.concatenate([jnp.where(m, x, jnp.zeros_like(x)) for m in head_masks], axis=1)

    def head_blocks(full):
        out = jnp.where(head_masks[0], full[:, :RWKV_HEAD], 0.0)
        for h in range(1, hpl):
            out = out + jnp.where(head_masks[h], full[:, h * RWKV_HEAD:(h + 1) * RWKV_HEAD], 0.0)
        return out

    bmm = lambda x, y: jnp.einsum('gts,gsd->gtd', x.astype(BF16), y.astype(BF16), preferred_element_type=F32)
    bmm_nt = lambda x, y: jnp.einsum('gtd,gsd->gts', x.astype(BF16), y.astype(BF16), preferred_element_type=F32)
    bmm_tn = lambda x, y: jnp.einsum('gtk,gtd->gkd', x.astype(BF16), y.astype(BF16), preferred_element_type=F32)

    def slab_body(sl, carry):
        at, rt, bt, kt, bh, kh, vv = [ops_ref[i, sl] for i in range(7)]
        gm = bmm_nt(jnp.concatenate([at, rt], axis=1),
                    jnp.concatenate([block_diag(bt), block_diag(kt)], axis=1))
        a_ab = jnp.where(strict4, gm[:, :CHUNK, :GW], 0.0)
        a_ak = jnp.where(strict4, gm[:, :CHUNK, GW:], 0.0)
        a_rb = jnp.where(incl4, gm[:, CHUNK:, :GW], 0.0)
        a_rk = jnp.where(incl4, gm[:, CHUNK:, GW:], 0.0)
        akv = bmm(jnp.concatenate([a_ak, a_rk], axis=1), block_diag(vv))
        xk = eye4 + a_ab
        pk = bmm(a_ab, block_diag(a_ab))
        for _ in range(4):
            both = bmm(jnp.concatenate([xk, pk], axis=1), block_diag(pk))
            xk = xk + both[:, :CHUNK]
            pk = both[:, CHUNK:]
        xk = xk + bmm(xk, block_diag(pk))
        pq = bmm(xk, jnp.concatenate([block_diag(at), block_diag(akv[:, :CHUNK])], axis=2))
        p_bd, q_bd = block_diag(pq[:, :, :GW]), block_diag(pq[:, :, GW:])
        ry = bmm(a_rb, jnp.concatenate([p_bd, q_bd], axis=2)) + jnp.concatenate(
            [rt.astype(F32), akv[:, CHUNK:]], axis=2)
        m_sbs = head_blocks(bmm_tn(bh, pq[:, :, :GW])) + eye4 * wend_ref[sl]
        n_sbs = head_blocks(bmm_tn(jnp.concatenate([bh, kh], axis=1),
                                   jnp.concatenate([pq[:, :, GW:].astype(BF16), vv], axis=1)))
        rm_ref[sl] = jnp.concatenate([ry[:, :, :GW], m_sbs], axis=1).astype(BF16)
        yn_ref[sl] = jnp.concatenate([ry[:, :, GW:], n_sbs], axis=1)
        return carry

    lax.fori_loop(0, n_slab, slab_body, 0)

    def chunk_body(ci, carry):
        for sl in range(n_slab):
            lanes = slice(sl * GW, (sl + 1) * GW)
            st = s_ref[:, lanes]
            st_bd = jnp.concatenate([jnp.where(m[0], st, 0.0) for m in head_masks], axis=0).astype(BF16)
            res = jnp.dot(rm_ref[sl, ci], st_bd, preferred_element_type=F32) + yn_ref[sl, ci]
            y_ref[ci, :, lanes] = res[:CHUNK]
            s_ref[:, lanes] = res[CHUNK:]
        return carry

    lax.fori_loop(0, ng, chunk_body, 0)

    y = y_ref[...].reshape(rows_n, W)
    mean = _mm_split2_rhs_ones(y, seg) * (1.0 / RWKV_HEAD)
    yc = y - mean
    var = _mm_split2_rhs_ones(yc * yc, seg) * (1.0 / RWKV_HEAD)
    yn = yc * lax.rsqrt(var + RWKV_LN_EPS)
    o_ref[0] = (yn * lng_ref[...] + lnb_ref[...] + bonus_ref[...]) * gate_ref[...]


def _rwkv(pr, mu, w0, w2, a0, a2, g2, k_k, k_a, r_k, ln_g, ln_b, seg):
    bn, lp, _ = pr.shape
    nc = lp // CHUNK
    ng = max(d for d in range(1, RWKV_GROUP + 1) if nc % d == 0)
    rows = ng * CHUNK
    vec = lambda n: pl.BlockSpec((1, n), lambda b, c: (0, 0))
    mat = lambda m, n: pl.BlockSpec((m, n), lambda b, c: (0, 0))
    hd = RWKV_HEAD
    n_slab = RWKV_WIDTH // RWKV_SLAB
    return pl.pallas_call(
        _rwkv_kernel,
        name="rwkv_mixer",
        grid=(bn, nc // ng),
        in_specs=[
            pl.BlockSpec((1, rows, RWKV_COLS), lambda b, c: (b, c, 0)),
            pl.BlockSpec((1, 8, RWKV_COLS), lambda b, c: (b, jnp.maximum(c * (rows // 8) - 1, 0), 0)),
            vec(RWKV_COLS), vec(RWKV_WIDTH), mat(RWKV_W_RANK, RWKV_WIDTH), vec(RWKV_WIDTH),
            mat(RWKV_A_RANK, RWKV_WIDTH), mat(RWKV_G_RANK, RWKV_WIDTH), vec(RWKV_WIDTH), vec(RWKV_WIDTH),
            vec(RWKV_WIDTH), vec(RWKV_WIDTH), vec(RWKV_WIDTH), mat(RWKV_WIDTH, RWKV_WIDTH),
        ],
        out_specs=pl.BlockSpec((1, rows, RWKV_WIDTH), lambda b, c: (b, c, 0)),
        out_shape=jax.ShapeDtypeStruct((bn, lp, RWKV_WIDTH), F32),
        scratch_shapes=[
            pltpu.VMEM((hd, RWKV_WIDTH), F32),
            pltpu.VMEM((7, n_slab, ng, CHUNK, RWKV_SLAB), BF16),
            pltpu.VMEM((n_slab, ng, 1, RWKV_SLAB), F32),
            pltpu.VMEM((n_slab, ng, 2 * CHUNK, RWKV_SLAB), BF16),
            pltpu.VMEM((n_slab, ng, 2 * CHUNK, RWKV_SLAB), F32),
            pltpu.VMEM((ng, CHUNK, RWKV_WIDTH), F32),
            pltpu.VMEM((rows, RWKV_WIDTH), F32),
            pltpu.VMEM((rows, RWKV_WIDTH), F32),
        ],
        compiler_params=pltpu.CompilerParams(dimension_semantics=("arbitrary", "arbitrary"),
                                             vmem_limit_bytes=VMEM_LIMIT),
    )(pr, pr, mu, w0, w2, a0, a2, g2, k_k, k_a, r_k, ln_g, ln_b, seg.astype(BF16))


def _outproj_router_kernel(yg_ref, yr_ref, h_ref, wo_ref, g_ref, b_ref, rwt_ref, rb_ref, tri_ref,
                           h1_ref, idx_ref, gate_ref, rank_ref, cnt_ref, base_ref):
    i = pl.program_id(0)

    @pl.when(i == 0)
    def _():
        base_ref[...] = jnp.zeros_like(base_ref)

    wo = wo_ref[...]
    mix = _mm(yg_ref[...], wo[:GLA_WIDTH]) + _mm(yr_ref[...], wo[GLA_WIDTH:])
    h1 = _layer_norm(DEEPNORM_ALPHA * h_ref[...] + mix, g_ref[...], b_ref[...])
    _store_row_tiles(h1_ref, h1)
    h_hi = h1.astype(BF16)
    h_lo = (h1 - h_hi.astype(F32)).astype(BF16)
    nt = (((1,), (1,)), ((), ()))
    part = lax.dot_general(rwt_ref[...], h_hi, nt, preferred_element_type=F32)
    work = (part[:N_EXPERTS] + part[N_EXPERTS:]
            + lax.dot_general(rwt_ref[:N_EXPERTS, :], h_lo, nt, preferred_element_type=F32)
            + rb_ref[...][:, 0:1])
    tm = work.shape[1]
    e_iota = lax.broadcasted_iota(I32, (N_EXPERTS, tm), 0)
    base = base_ref[...][:, 0:1]
    vals, onehots = [], []
    for kk in range(TOP_K):
        m = jnp.max(work, axis=0, keepdims=True)
        sel = jnp.min(jnp.where(work == m, e_iota, N_EXPERTS), axis=0, keepdims=True)
        onehot = e_iota == sel
        work = jnp.where(onehot, -jnp.inf, work)
        vals.append(m)
        onehots.append(onehot.astype(F32))
        idx_ref[kk:kk + 1, :] = sel
    cnt_all = jnp.dot(jnp.concatenate(onehots, axis=0).astype(BF16), tri_ref[...], preferred_element_type=F32)
    prior = jnp.zeros((N_EXPERTS, 1), F32)
    for kk in range(TOP_K):
        cnt = cnt_all[kk * N_EXPERTS:(kk + 1) * N_EXPERTS]
        rank = jnp.sum(onehots[kk] * (base + prior + cnt - 1.0), axis=0, keepdims=True)
        prior = prior + cnt[:, tm - 1:tm]
        rank_ref[kk:kk + 1, :] = rank.astype(I32)
    es = [jnp.exp(vv - vals[0]) for vv in vals]
    den = es[0] + es[1] + es[2] + es[3]
    for kk in range(TOP_K):
        gate_ref[kk:kk + 1, :] = es[kk] / den
    new_base = base + prior
    base_ref[...] = jnp.broadcast_to(new_base, base_ref.shape)
    cnt_ref[...] = jnp.broadcast_to(new_base, cnt_ref.shape)


def _outproj_router(yg, yr, h, wo, g, b, rwt, rb, tri):
    tp = h.shape[0]
    tm = ROUTER_TILE
    const = lambda m, n: pl.BlockSpec((m, n), lambda i: (0, 0))
    return pl.pallas_call(
        _outproj_router_kernel,
        name="outproj_router",
        grid=(tp // tm,),
        in_specs=[
            pl.BlockSpec((tm, GLA_WIDTH), lambda i: (i, 0)),
            pl.BlockSpec((tm, RWKV_WIDTH), lambda i: (i, 0)),
            pl.BlockSpec((tm, D_MODEL), lambda i: (i, 0)),
            const(D_MODEL, D_MODEL), const(1, D_MODEL), const(1, D_MODEL),
            const(2 * N_EXPERTS, D_MODEL), const(N_EXPERTS, 128), const(tm, tm),
        ],
        out_specs=[
            pl.BlockSpec(_tiled_rows(tm), lambda i: (i, 0)),
            pl.BlockSpec((TOP_K, tm), lambda i: (0, i)),
            pl.BlockSpec((TOP_K, tm), lambda i: (0, i)),
            pl.BlockSpec((TOP_K, tm), lambda i: (0, i)),
            pl.BlockSpec((N_EXPERTS, 128), lambda i: (0, 0)),
        ],
        out_shape=[
            jax.ShapeDtypeStruct(_tiled_rows(tp), F32),
            jax.ShapeDtypeStruct((TOP_K, tp), I32),
            jax.ShapeDtypeStruct((TOP_K, tp), F32),
            jax.ShapeDtypeStruct((TOP_K, tp), I32),
            jax.ShapeDtypeStruct((N_EXPERTS, 128), F32),
        ],
        scratch_shapes=[pltpu.VMEM((N_EXPERTS, 128), F32)],
        compiler_params=pltpu.CompilerParams(dimension_semantics=("arbitrary",),
                                             vmem_limit_bytes=VMEM_LIMIT),
    )(yg, yr, h, wo, g, b, rwt, rb, tri)


def _zero_blocks_kernel(last_ref, o_ref):
    o_ref[...] = jnp.zeros_like(o_ref)


def _zero_blocks(last_block, n_slots):
    return pl.pallas_call(
        _zero_blocks_kernel,
        name="moe_zero_blocks",
        grid_spec=pltpu.PrefetchScalarGridSpec(
            num_scalar_prefetch=1,
            grid=(N_EXPERTS,),
            in_specs=[],
            out_specs=pl.BlockSpec(_tiled_rows(MOE_BLOCK), lambda e, last: (last[e], 0)),
        ),
        out_shape=jax.ShapeDtypeStruct(_tiled_rows(n_slots), F32),
        compiler_params=pltpu.CompilerParams(dimension_semantics=("arbitrary",)),
    )(last_block)


def _dispatch_kernel(pos_ref, x_ref, xs_in_ref, xs_ref, sem):
    del xs_in_ref
    tm = pos_ref.shape[0] // TOP_K

    def start(r, carry):
        for kk in range(TOP_K):
            pltpu.make_async_copy(_row_tile(x_ref, r), _row_tile(xs_ref, pos_ref[r * TOP_K + kk]),
                                  sem).start(priority=kk % 2)
        return carry

    lax.fori_loop(0, tm, start, 0, unroll=DMA_UNROLL)
    for kk in range(TOP_K):
        pltpu.make_async_copy(x_ref, xs_ref.at[pl.ds(0, x_ref.shape[0])], sem).wait()


def _dispatch(pos, x, xs):
    tp = pos.shape[0] // TOP_K
    tm = ROUTER_TILE
    return pl.pallas_call(
        _dispatch_kernel,
        name="moe_dispatch",
        grid=(tp // tm,),
        in_specs=[
            pl.BlockSpec((tm * TOP_K,), lambda i: (i,), memory_space=pltpu.SMEM),
            pl.BlockSpec(_tiled_rows(tm), lambda i: (i, 0)),
            pl.BlockSpec(memory_space=pl.ANY),
        ],
        out_specs=pl.BlockSpec(memory_space=pl.ANY),
        out_shape=jax.ShapeDtypeStruct(xs.shape, xs.dtype),
        scratch_shapes=[pltpu.SemaphoreType.DMA(())],
        input_output_aliases={2: 0},
        compiler_params=pltpu.CompilerParams(dimension_semantics=("arbitrary",)),
    )(pos, x, xs)


SPLIT_TILE = 256


def _moe_kernel(be_ref, nu_ref, xs_ref, wu_ref, wd_ref, perm_ref, bg_ref, bl_ref, bd_ref, ys_ref,
                wg_s, wl_s, wd_s):
    i = pl.program_id(0)

    @pl.when(i < nu_ref[0])
    def _():
        @pl.when(jnp.logical_or(i == 0, be_ref[i] != be_ref[jnp.maximum(i - 1, 0)]))
        def _():
            half = SPLIT_TILE // 2
            for t in range(wu_ref.shape[2] // SPLIT_TILE):
                d = jnp.dot(wu_ref[0, :, t * SPLIT_TILE:(t + 1) * SPLIT_TILE].astype(BF16), perm_ref[...],
                            preferred_element_type=F32).astype(BF16)
                wg_s[:, t * half:(t + 1) * half] = d[:, :half]
                wl_s[:, t * half:(t + 1) * half] = d[:, half:]
            wd_s[...] = wd_ref[0].astype(BF16)

        x = _load_row_tiles(xs_ref).astype(BF16)
        x_glu = jnp.dot(x, wg_s[...], preferred_element_type=F32) + bg_ref[0]
        x_lin = jnp.dot(x, wl_s[...], preferred_element_type=F32) + bl_ref[0]
        x_glu = jnp.minimum(x_glu, SWIGLU_LIMIT)
        x_lin = jnp.clip(x_lin, -SWIGLU_LIMIT, SWIGLU_LIMIT)
        act = x_glu * _sigmoid(SWIGLU_ALPHA * x_glu) * (x_lin + 1.0)
        _store_row_tiles(ys_ref, jnp.dot(act.astype(BF16), wd_s[...], preferred_element_type=F32) + bd_ref[0])


def _moe(block_e, n_used, xs, w_up, w_down, bg, bl, bd):
    n_slots = xs.shape[0] // ROW_TILE[0]
    nb = n_slots // MOE_BLOCK
    src = jnp.arange(SPLIT_TILE, dtype=I32)[:, None]
    dst = jnp.arange(SPLIT_TILE, dtype=I32)[None, :]
    half = SPLIT_TILE // 2
    perm = (src == jnp.where(dst < half, 2 * dst, 2 * (dst - half) + 1)).astype(BF16)
    blk = lambda i, be, nu: (jnp.maximum(jnp.minimum(i, nu[0] - 1), 0), 0)
    bspec = pl.BlockSpec((1, 1, D_FF), lambda i, be, nu: (be[i], 0, 0))
    return pl.pallas_call(
        _moe_kernel,
        name="moe_experts",
        grid_spec=pltpu.PrefetchScalarGridSpec(
            num_scalar_prefetch=2,
            grid=(nb,),
            in_specs=[
                pl.BlockSpec(_tiled_rows(MOE_BLOCK), blk),
                pl.BlockSpec((1, D_MODEL, 2 * D_FF), lambda i, be, nu: (be[i], 0, 0)),
                pl.BlockSpec((1, D_FF, D_MODEL), lambda i, be, nu: (be[i], 0, 0)),
                pl.BlockSpec((SPLIT_TILE, SPLIT_TILE), lambda i, be, nu: (0, 0)),
                bspec, bspec, bspec,
            ],
            out_specs=pl.BlockSpec(_tiled_rows(MOE_BLOCK), blk),
            scratch_shapes=[
                pltpu.VMEM((D_MODEL, D_FF), BF16),
                pltpu.VMEM((D_MODEL, D_FF), BF16),
                pltpu.VMEM((D_FF, D_MODEL), BF16),
            ],
        ),
        out_shape=jax.ShapeDtypeStruct(_tiled_rows(n_slots), F32),
        compiler_params=pltpu.CompilerParams(dimension_semantics=("arbitrary",),
                                             vmem_limit_bytes=VMEM_LIMIT),
    )(block_e, n_used, xs, w_up, w_down, perm, bg, bl, bd)


COMBINE_SLOTS = 3


def _combine_kernel(pos_ref, posn_ref, posa_ref, gt_ref, h1_ref, g_ref, b_ref, ys_ref, o_ref, buf, sem):
    s = pl.program_id(0)
    rows = TOP_K * CHUNK

    def issue_row(p_ref, to_slot, r):
        for kk in range(TOP_K):
            pltpu.make_async_copy(_row_tile(ys_ref, p_ref[r * TOP_K + kk]),
                                  _row_tile(buf, to_slot * rows + kk * CHUNK + r),
                                  sem.at[to_slot]).start(priority=kk % 2)

    def wait_slot(which):
        span = rows * ROW_TILE[0]
        pltpu.make_async_copy(ys_ref.at[pl.ds(0, span)], buf.at[pl.ds(which * span, span)], sem.at[which]).wait()

    @pl.when(s == 0)
    def _():
        def body(r, carry):
            issue_row(pos_ref, 0, r)
            issue_row(posn_ref, 1, r)
            return carry

        lax.fori_loop(0, CHUNK, body, 0, unroll=DMA_UNROLL)

    def step(cur):
        ahead = (cur + COMBINE_SLOTS - 1) % COMBINE_SLOTS
        wait_slot(cur)
        gt = gt_ref[...]
        quarter = CHUNK // TOP_K
        ffn = None
        for kk in range(TOP_K):
            part = _load_row_tiles(buf, cur * rows + kk * CHUNK, CHUNK) * gt[:, kk:kk + 1]
            ffn = part if ffn is None else ffn + part
            for r in range(kk * quarter, (kk + 1) * quarter):
                issue_row(posa_ref, ahead, r)
        o_ref[0] = _layer_norm(DEEPNORM_ALPHA * _load_row_tiles(h1_ref) + ffn, g_ref[...], b_ref[...])

        @pl.when(s == pl.num_programs(0) - 1)
        def _():
            wait_slot(ahead)
            wait_slot((cur + 1) % COMBINE_SLOTS)

    for cur in range(COMBINE_SLOTS):
        pl.when(s % COMBINE_SLOTS == cur)(functools.partial(step, cur))


def _combine(pos_flat, gates_t, h1, g, b, ys, bn, lp):
    seq = lp - CHUNK
    nc = lp // CHUNK
    ncs = seq // CHUNK
    n_steps = bn * ncs
    chunk_of = lambda s: (s // ncs) * nc + s % ncs + 1
    later = lambda s, d: jnp.minimum(s + d, n_steps - 1)
    return pl.pallas_call(
        _combine_kernel,
        name="moe_combine",
        grid=(n_steps,),
        in_specs=[
            pl.BlockSpec((TOP_K * CHUNK,), lambda s: (chunk_of(s),), memory_space=pltpu.SMEM),
            pl.BlockSpec((TOP_K * CHUNK,), lambda s: (chunk_of(later(s, 1)),), memory_space=pltpu.SMEM),
            pl.BlockSpec((TOP_K * CHUNK,), lambda s: (chunk_of(later(s, 2)),), memory_space=pltpu.SMEM),
            pl.BlockSpec((CHUNK, TOP_K), lambda s: (chunk_of(s), 0)),
            pl.BlockSpec(_tiled_rows(CHUNK), lambda s: (chunk_of(s), 0)),
            pl.BlockSpec((1, D_MODEL), lambda s: (0, 0)),
            pl.BlockSpec((1, D_MODEL), lambda s: (0, 0)),
            pl.BlockSpec(memory_space=pl.ANY),
        ],
        out_specs=pl.BlockSpec((1, CHUNK, D_MODEL), lambda s: (s // ncs, s % ncs, 0)),
        out_shape=jax.ShapeDtypeStruct((bn, seq, D_MODEL), F32),
        scratch_shapes=[pltpu.VMEM(_tiled_rows(COMBINE_SLOTS * TOP_K * CHUNK), F32),
                        pltpu.SemaphoreType.DMA((COMBINE_SLOTS,))],
        compiler_params=pltpu.CompilerParams(dimension_semantics=("arbitrary",)),
    )(pos_flat, pos_flat, pos_flat, gates_t, h1, g, b, ys)


def kernel(x, meta, ln_in_g, ln_in_b, w_in, gla_gk_w2, gla_gk_b, gla_norm_g, rwkv_mu, rwkv_w0, rwkv_w2, rwkv_a0, rwkv_a2, rwkv_g2, rwkv_k_k, rwkv_k_a, rwkv_r_k, rwkv_ln_g, rwkv_ln_b, w_out, ln1_g, ln1_b, router_w, router_b, exp_w_up, exp_b_up, exp_w_down, exp_b_down, ln2_g, ln2_b):
    bn, seq, _ = x.shape
    assert seq % CHUNK == 0
    lp = seq + CHUNK
    tp = bn * lp
    assert tp % ROUTER_TILE == 0
    row = lambda t: t.reshape(1, -1).astype(F32)

    hcat = jnp.concatenate([jnp.zeros((bn, N_FRONT, D_MODEL), F32),
                            jnp.broadcast_to(meta[None].astype(F32), (bn, N_META, D_MODEL)), x], axis=1)
    hcat = hcat.reshape(tp, D_MODEL)
    gla_in = 2 * GLA_KEY + 2 * GLA_WIDTH + GLA_GATE_RANK
    w = w_in[0]
    w_cols = jnp.concatenate([w[:, :gla_in], jnp.zeros((D_MODEL, 128 - GLA_GATE_RANK), F32), w[:, gla_in:]],
                             axis=1).astype(BF16)
    h, pg, pr = _ln_inproj(hcat, row(ln_in_g), row(ln_in_b), w_cols, lp)

    w2p = jnp.concatenate([gla_gk_w2[0], jnp.zeros((128 - GLA_GATE_RANK, GLA_KEY), F32)], axis=0)
    y_gla = _gla(pg.reshape(bn, lp, GLA_COLS), w2p, row(gla_gk_b[0]), row(gla_norm_g[0]))

    head_id = jnp.arange(RWKV_WIDTH, dtype=I32) // RWKV_HEAD
    seg = (head_id[:, None] == head_id[None, :]).astype(F32)
    y_rwkv = _rwkv(pr.reshape(bn, lp, RWKV_COLS), row(rwkv_mu[0]), row(rwkv_w0[0]), rwkv_w2[0],
                   row(rwkv_a0[0]), rwkv_a2[0], rwkv_g2[0], row(rwkv_k_k[0]), row(rwkv_k_a[0]),
                   row(rwkv_r_k[0]), row(rwkv_ln_g[0]), row(rwkv_ln_b[0]), seg)

    tri = jnp.triu(jnp.ones((ROUTER_TILE, ROUTER_TILE), F32)).astype(BF16)
    rb = jnp.broadcast_to(router_b[0].reshape(N_EXPERTS, 1), (N_EXPERTS, 128))
    rw_t = router_w[0].T
    rw_hi = rw_t.astype(BF16)
    rwt = jnp.concatenate([rw_hi, (rw_t - rw_hi.astype(F32)).astype(BF16)], axis=0)
    h1, idx, gates, rank, cnt = _outproj_router(
        y_gla.reshape(tp, GLA_WIDTH), y_rwkv.reshape(tp, RWKV_WIDTH), h, w_out[0].astype(BF16),
        row(ln1_g[0]), row(ln1_b[0]), rwt, rb, tri)

    counts = cnt[:, 0].astype(I32)
    padded = (counts + MOE_BLOCK - 1) // MOE_BLOCK * MOE_BLOCK
    ends_p = jnp.cumsum(padded)
    starts_p = ends_p - padded
    e_ids = jnp.arange(N_EXPERTS, dtype=I32)
    start_of = jnp.sum(jnp.where(idx[None] == e_ids[:, None, None], starts_p[:, None, None], 0), axis=0)
    pos = start_of + rank
    nb = tp * TOP_K // MOE_BLOCK + N_EXPERTS
    n_slots = nb * MOE_BLOCK
    block_start = jnp.arange(nb, dtype=I32) * MOE_BLOCK
    block_e = jnp.minimum(jnp.sum((block_start[:, None] >= ends_p[None, :]).astype(I32), axis=1), N_EXPERTS - 1)
    n_used = (ends_p[-1:] // MOE_BLOCK).astype(I32)
    last_block = jnp.maximum(ends_p // MOE_BLOCK - 1, 0).astype(I32)

    pos_flat = pos.T.reshape(tp * TOP_K)
    xs = _dispatch(pos_flat, h1, _zero_blocks(last_block, n_slots))
    bg = exp_b_up[0][:, None, 0::2]
    bl = exp_b_up[0][:, None, 1::2]
    ys = _moe(block_e, n_used, xs, exp_w_up[0], exp_w_down[0], bg, bl, exp_b_down[0][:, None, :])

    return _combine(pos_flat, gates.T, h1, row(ln2_g[0]), row(ln2_b[0]), ys, bn, lp)
```

```python
import functools

import jax
import jax.numpy as jnp
from jax import lax
from jax.experimental import pallas as pl
from jax.experimental.pallas import tpu as pltpu

F32 = jnp.float32
BF16 = jnp.bfloat16
I32 = jnp.int32
HIGHEST = lax.Precision.HIGHEST

D_MODEL = 1024
N_META = 16
CHUNK = 64
N_FRONT = (-N_META) % CHUNK
GLA_HEADS = 4
GLA_DK = 64
GLA_DV = 128
GLA_KEY = GLA_HEADS * GLA_DK
GLA_WIDTH = GLA_HEADS * GLA_DV
GLA_GATE_RANK = 16
GLA_TAU = 16.0
GLA_COLS = 2 * GLA_KEY + 2 * GLA_WIDTH + 128
RWKV_WIDTH = 512
RWKV_HEAD = 64
RWKV_HEADS = RWKV_WIDTH // RWKV_HEAD
RWKV_W_RANK = 64
RWKV_A_RANK = 64
RWKV_G_RANK = 128
RWKV_COLS = 3 * RWKV_WIDTH + RWKV_W_RANK + RWKV_A_RANK + RWKV_G_RANK
N_EXPERTS = 32
TOP_K = 4
D_FF = D_MODEL
SWIGLU_ALPHA = 1.702
SWIGLU_LIMIT = 7.0
MOE_BLOCK = 512
DEPTH = 1
DEEPNORM_ALPHA = (2.0 * DEPTH) ** 0.25
LN_EPS = 1e-5
RWKV_LN_EPS = 64e-5
RMS_EPS = 1e-6

ROUTER_TILE = 768
DMA_UNROLL = 4
VMEM_LIMIT = 56 * 1024 * 1024


def _mm(a, b):
    return jnp.dot(a.astype(BF16), b.astype(BF16), preferred_element_type=F32)


def _mm_nt(a, b):
    return lax.dot_general(a.astype(BF16), b.astype(BF16), (((1,), (1,)), ((), ())),
                           preferred_element_type=F32)


def _mm_tn(a, b):
    return lax.dot_general(a.astype(BF16), b.astype(BF16), (((0,), (0,)), ((), ())),
                           preferred_element_type=F32)


def _mm_f32(a, b):
    return jnp.dot(a, b, preferred_element_type=F32, precision=HIGHEST)


def _layer_norm(x, g, b):
    mu = jnp.mean(x, axis=-1, keepdims=True)
    xc = x - mu
    var = jnp.mean(xc * xc, axis=-1, keepdims=True)
    return xc * lax.rsqrt(var + LN_EPS) * g + b


def _sigmoid(x):
    return 1.0 / (1.0 + jnp.exp(-x))


def _log_sigmoid(x):
    return jnp.minimum(x, 0.0) - jnp.log(1.0 + jnp.exp(-jnp.abs(x)))


ROW_TILE = (8, 128)


def _tiled_rows(n):
    return (n * ROW_TILE[0], ROW_TILE[1])


def _row_tile(ref, i):
    return ref.at[pl.ds(pl.multiple_of(i * ROW_TILE[0], ROW_TILE[0]), ROW_TILE[0])]


def _store_row_tiles(ref, x, row0=0):
    n = x.shape[0]
    for j in range(ROW_TILE[0]):
        ref[pl.ds(row0 * ROW_TILE[0] + j, n, stride=ROW_TILE[0]), :] = x[:, j * ROW_TILE[1]:(j + 1) * ROW_TILE[1]]


def _load_row_tiles(ref, row0=0, n=None):
    n = ref.shape[0] // ROW_TILE[0] if n is None else n
    return jnp.concatenate([ref[pl.ds(row0 * ROW_TILE[0] + j, n, stride=ROW_TILE[0]), :]
                            for j in range(ROW_TILE[0])], axis=1)


def _tri_masks(n):
    r = lax.broadcasted_iota(I32, (n, n), 0)
    c = lax.broadcasted_iota(I32, (n, n), 1)
    return r >= c, r > c, r == c


def _ln_inproj_kernel(x_ref, g_ref, b_ref, w_ref, h_ref, pg_ref, pr_ref, *, tiles_per_seq):
    i = pl.program_id(0)
    y = _layer_norm(x_ref[...], g_ref[...], b_ref[...])
    row = lax.broadcasted_iota(I32, (y.shape[0], 1), 0)
    is_front = jnp.logical_and(i % tiles_per_seq == 0, row < N_FRONT)
    y = jnp.where(is_front, 0.0, y)
    h_ref[...] = y
    p = _mm(y, w_ref[...])
    pg_ref[...] = p[:, :GLA_COLS]
    pr_ref[...] = p[:, GLA_COLS:]


def _ln_inproj(hcat, g, b, w, lp):
    tp = hcat.shape[0]
    tiles_per_seq = 1
    for cand in range(1, lp // 8 + 1):
        if lp % cand == 0 and (lp // cand) % 8 == 0 and lp // cand >= N_FRONT and lp // cand <= 384:
            tiles_per_seq = cand
            break
    tm = lp // tiles_per_seq
    ncols = GLA_COLS + RWKV_COLS
    return pl.pallas_call(
        functools.partial(_ln_inproj_kernel, tiles_per_seq=tiles_per_seq),
        name="ln_inproj",
        grid=(tp // tm,),
        in_specs=[
            pl.BlockSpec((tm, D_MODEL), lambda i: (i, 0)),
            pl.BlockSpec((1, D_MODEL), lambda i: (0, 0)),
            pl.BlockSpec((1, D_MODEL), lambda i: (0, 0)),
            pl.BlockSpec((D_MODEL, ncols), lambda i: (0, 0)),
        ],
        out_specs=[
            pl.BlockSpec((tm, D_MODEL), lambda i: (i, 0)),
            pl.BlockSpec((tm, GLA_COLS), lambda i: (i, 0)),
            pl.BlockSpec((tm, RWKV_COLS), lambda i: (i, 0)),
        ],
        out_shape=[
            jax.ShapeDtypeStruct((tp, D_MODEL), F32),
            jax.ShapeDtypeStruct((tp, GLA_COLS), F32),
            jax.ShapeDtypeStruct((tp, RWKV_COLS), F32),
        ],
        compiler_params=pltpu.CompilerParams(dimension_semantics=("arbitrary",),
                                             vmem_limit_bytes=VMEM_LIMIT),
    )(hcat, g, b, w)


GLA_GROUP = 11


def _gla_kernel(pg_ref, w2_ref, gkb_ref, ng_ref, o_ref, st_ref, qe_ref, oi_ref, kvt_ref, el_ref):
    c = pl.program_id(1)

    @pl.when(c == 0)
    def _():
        st_ref[...] = jnp.zeros_like(st_ref)

    rows_n = pg_ref.shape[1]
    ng = rows_n // CHUNK
    g_off = 2 * GLA_KEY + GLA_WIDTH
    p = pg_ref[0]
    gl = p[:, g_off + GLA_WIDTH:]
    lg = _log_sigmoid(_mm_f32(gl, w2_ref[...]) + gkb_ref[...]) * (1.0 / GLA_TAU)
    row = lax.broadcasted_iota(I32, (rows_n, 1), 0)
    lg = jnp.where(jnp.logical_and(c == 0, row < N_FRONT), 0.0, lg)
    incl, _, _ = _tri_masks(CHUNK)
    tril = incl.astype(BF16)
    bc = jnp.concatenate([_mm_split3(tril, lg[i * CHUNK:(i + 1) * CHUNK]) for i in range(ng)], axis=0)
    g3 = lambda t: t.reshape(ng, CHUNK, t.shape[-1])
    bc = g3(bc)
    b_last = bc[:, CHUNK - 1:CHUNK, :]
    k = g3(p[:, GLA_KEY:2 * GLA_KEY])
    qe = g3(p[:, 0:GLA_KEY]) * (GLA_DK ** -0.5) * jnp.exp(bc)
    ke = k * jnp.exp(-bc)
    kl = k * jnp.exp(b_last - bc)
    e_last = jnp.exp(b_last)
    v = g3(p[:, 2 * GLA_KEY:g_off]).astype(BF16)
    for h in range(GLA_HEADS):
        ks = slice(h * GLA_DK, (h + 1) * GLA_DK)
        vs = slice(h * GLA_DV, (h + 1) * GLA_DV)
        qh = qe[:, :, ks].astype(BF16)
        a = jnp.einsum('gtd,gsd->gts', qh, ke[:, :, ks].astype(BF16), preferred_element_type=F32)
        a = jnp.where(incl, a, 0.0).astype(BF16)
        oi_ref[:, h] = jnp.einsum('gts,gsv->gtv', a, v[:, :, vs], preferred_element_type=F32)
        kvt_ref[:, h] = jnp.einsum('gtv,gtd->gvd', v[:, :, vs], kl[:, :, ks].astype(BF16),
                                   preferred_element_type=F32)
        qe_ref[:, h] = qh
        el_ref[:, h] = e_last[:, :, ks]

    def chunk_body(ci, carry):
        st = st_ref[...]
        oi_ref[ci] = oi_ref[ci] + jnp.einsum('htd,hvd->htv', qe_ref[ci], st.astype(BF16),
                                             preferred_element_type=F32)
        st_ref[...] = st * el_ref[ci] + kvt_ref[ci]
        return carry

    lax.fori_loop(0, ng, chunk_body, 0)

    for ci in range(ng):
        o = oi_ref[ci]
        o = o * lax.rsqrt(jnp.mean(o * o, axis=-1, keepdims=True) + RMS_EPS) * ng_ref[...]
        o = jnp.concatenate([o[h] for h in range(GLA_HEADS)], axis=1)
        rows = slice(ci * CHUNK, (ci + 1) * CHUNK)
        gate = pg_ref[0, rows, g_off:g_off + GLA_WIDTH]
        o_ref[0, rows, :] = o * (gate * _sigmoid(gate))


def _gla(pg, w2p, gkb, ng_w):
    bn, lp, _ = pg.shape
    nc = lp // CHUNK
    ng = max(d for d in range(1, GLA_GROUP + 1) if nc % d == 0)
    rows = ng * CHUNK
    hh = GLA_HEADS
    return pl.pallas_call(
        _gla_kernel,
        name="gla_mixer",
        grid=(bn, nc // ng),
        in_specs=[
            pl.BlockSpec((1, rows, GLA_COLS), lambda b, c: (b, c, 0)),
            pl.BlockSpec((128, GLA_KEY), lambda b, c: (0, 0)),
            pl.BlockSpec((1, GLA_KEY), lambda b, c: (0, 0)),
            pl.BlockSpec((1, GLA_DV), lambda b, c: (0, 0)),
        ],
        out_specs=pl.BlockSpec((1, rows, GLA_WIDTH), lambda b, c: (b, c, 0)),
        out_shape=jax.ShapeDtypeStruct((bn, lp, GLA_WIDTH), F32),
        scratch_shapes=[
            pltpu.VMEM((hh, GLA_DV, GLA_DK), F32),
            pltpu.VMEM((ng, hh, CHUNK, GLA_DK), BF16),
            pltpu.VMEM((ng, hh, CHUNK, GLA_DV), F32),
            pltpu.VMEM((ng, hh, GLA_DV, GLA_DK), F32),
            pltpu.VMEM((ng, hh, 1, GLA_DK), F32),
        ],
        compiler_params=pltpu.CompilerParams(dimension_semantics=("arbitrary", "arbitrary"),
                                             vmem_limit_bytes=VMEM_LIMIT),
    )(pg, w2p, gkb, ng_w)


RWKV_GROUP = 11
RWKV_SLAB = 256
def _mm_split3(ones_bf16, x):
    hi = x.astype(BF16)
    r1 = x - hi.astype(F32)
    mid = r1.astype(BF16)
    lo = (r1 - mid.astype(F32)).astype(BF16)
    return jnp.dot(jnp.concatenate([ones_bf16] * 3, axis=1), jnp.concatenate([hi, mid, lo], axis=0),
                   preferred_element_type=F32)


def _mm_split2_rhs_ones(x, ones_bf16):
    hi = x.astype(BF16)
    mid = (x - hi.astype(F32)).astype(BF16)
    dot = lambda t: jnp.dot(t, ones_bf16, preferred_element_type=F32)
    return dot(hi) + dot(mid)


def _rwkv_kernel(pr_ref, pv_ref, mu_ref, w0_ref, w2_ref, a0_ref, a2_ref, g2_ref, kk_ref, ka_ref,
                 rk_ref, lng_ref, lnb_ref, seg_ref, o_ref, s_ref, ops_ref, wend_ref, rm_ref, yn_ref,
                 y_ref, bonus_ref, gate_ref):
    c = pl.program_id(1)
    rows_n = pr_ref.shape[1]
    ng = rows_n // CHUNK

    @pl.when(c == 0)
    def _():
        s_ref[...] = jnp.zeros_like(s_ref)

    p = pr_ref[0]
    prev_row = jnp.where(c > 0, pv_ref[0][7:8, :], 0.0)
    row = lax.broadcasted_iota(I32, (rows_n, 1), 0)
    prev = jnp.where(row == 0, prev_row, pltpu.roll(p, 1, 0))
    p = p + (prev - p) * mu_ref[...]
    W = RWKV_WIDTH
    r = p[:, 0:W]
    k = p[:, W:2 * W]
    v = p[:, 2 * W:3 * W]
    w_low = p[:, 3 * W:3 * W + RWKV_W_RANK]
    a_low = p[:, 3 * W + RWKV_W_RANK:3 * W + RWKV_W_RANK + RWKV_A_RANK]
    g_low = p[:, 3 * W + RWKV_W_RANK + RWKV_A_RANK:]
    wx = w0_ref[...] + _mm(jnp.tanh(w_low), w2_ref[...])
    w = _log_sigmoid(wx) - 0.5
    logd = -jnp.exp(w)
    a = _sigmoid(a0_ref[...] + _mm(a_low, a2_ref[...]))
    g = _mm(_sigmoid(g_low), g2_ref[...])
    seg = seg_ref[...]
    kk = k * kk_ref[...]
    kk = kk * lax.rsqrt(jnp.maximum(_mm_split2_rhs_ones(kk * kk, seg), 1e-24))
    k = k * (1.0 + (a - 1.0) * ka_ref[...])
    bonus = _mm_split2_rhs_ones(r * k * rk_ref[...], seg) * v

    bonus_ref[...] = bonus
    gate_ref[...] = g

    incl, strict, diag = _tri_masks(CHUNK)
    tril = incl.astype(BF16)
    c_in = jnp.concatenate([_mm_split3(tril, logd[i * CHUNK:(i + 1) * CHUNK]) for i in range(ng)], axis=0)
    g3 = lambda t: t.reshape(ng, CHUNK, W)
    logd, c_in, r, k, v, kk, a = g3(logd), g3(c_in), g3(r), g3(k), g3(v), g3(kk), g3(a)
    c_last = c_in[:, CHUNK - 1:CHUNK, :]
    e_neg = jnp.exp(-c_in)
    e_end = jnp.exp(c_last - c_in)
    kka = kk * a
    per_head = (-kk * jnp.exp(c_in - logd), r * jnp.exp(c_in), kka * e_neg, k * e_neg, kka * e_end,
                k * e_end, v)
    GW = RWKV_SLAB
    n_slab = W // GW
    for i, t in enumerate(per_head):
        t = t.astype(BF16)
        for sl in range(n_slab):
            ops_ref[i, sl] = t[:, :, sl * GW:(sl + 1) * GW]
    w_end = jnp.exp(c_last)
    for sl in range(n_slab):
        wend_ref[sl] = w_end[:, :, sl * GW:(sl + 1) * GW]

    hpl = GW // RWKV_HEAD
    lane_head = lax.broadcasted_iota(I32, (1, 1, GW), 2) // RWKV_HEAD
    head_masks = [lane_head == h for h in range(hpl)]
    rr = lax.broadcasted_iota(I32, (CHUNK, GW), 0)
    cc = lax.broadcasted_iota(I32, (CHUNK, GW), 1) % RWKV_HEAD
    incl4, strict4, eye4 = rr >= cc, rr > cc, (rr == cc).astype(F32)

    def block_diag(x):
        x = x.astype(BF16)
        return jnp.concatenate([jnp.where(m, x, jnp.zeros_like(x)) for m in head_masks], axis=1)

    def head_blocks(full):
        out = jnp.where(head_masks[0], full[:, :RWKV_HEAD], 0.0)
        for h in range(1, hpl):
            out = out + jnp.where(head_masks[h], full[:, h * RWKV_HEAD:(h + 1) * RWKV_HEAD], 0.0)
        return out

    bmm = lambda x, y: jnp.einsum('gts,gsd->gtd', x.astype(BF16), y.astype(BF16), preferred_element_type=F32)
    bmm_nt = lambda x, y: jnp.einsum('gtd,gsd->gts', x.astype(BF16), y.astype(BF16), preferred_element_type=F32)
    bmm_tn = lambda x, y: jnp.einsum('gtk,gtd->gkd', x.astype(BF16), y.astype(BF16), preferred_element_type=F32)

    def slab_body(sl, carry):
        at, rt, bt, kt, bh, kh, vv = [ops_ref[i, sl] for i in range(7)]
        gm = bmm_nt(jnp.concatenate([at, rt], axis=1),
                    jnp.concatenate([block_diag(bt), block_diag(kt)], axis=1))
        a_ab = jnp.where(strict4, gm[:, :CHUNK, :GW], 0.0)
        a_ak = jnp.where(strict4, gm[:, :CHUNK, GW:], 0.0)
        a_rb = jnp.where(incl4, gm[:, CHUNK:, :GW], 0.0)
        a_rk = jnp.where(incl4, gm[:, CHUNK:, GW:], 0.0)
        akv = bmm(jnp.concatenate([a_ak, a_rk], axis=1), block_diag(vv))
        xk = eye4 + a_ab
        pk = bmm(a_ab, block_diag(a_ab))
        for _ in range(4):
            both = bmm(jnp.concatenate([xk, pk], axis=1), block_diag(pk))
            xk = xk + both[:, :CHUNK]
            pk = both[:, CHUNK:]
        xk = xk + bmm(xk, block_diag(pk))
        pq = bmm(xk, jnp.concatenate([block_diag(at), block_diag(akv[:, :CHUNK])], axis=2))
        p_bd, q_bd = block_diag(pq[:, :, :GW]), block_diag(pq[:, :, GW:])
        ry = bmm(a_rb, jnp.concatenate([p_bd, q_bd], axis=2)) + jnp.concatenate(
            [rt.astype(F32), akv[:, CHUNK:]], axis=2)
        m_sbs = head_blocks(bmm_tn(bh, pq[:, :, :GW])) + eye4 * wend_ref[sl]
        n_sbs = head_blocks(bmm_tn(jnp.concatenate([bh, kh], axis=1),
                                   jnp.concatenate([pq[:, :, GW:].astype(BF16), vv], axis=1)))
        rm_ref[sl] = jnp.concatenate([ry[:, :, :GW], m_sbs], axis=1).astype(BF16)
        yn_ref[sl] = jnp.concatenate([ry[:, :, GW:], n_sbs], axis=1)
        return carry

    lax.fori_loop(0, n_slab, slab_body, 0)

    def chunk_body(ci, carry):
        for sl in range(n_slab):
            lanes = slice(sl * GW, (sl + 1) * GW)
            st = s_ref[:, lanes]
            st_bd = jnp.concatenate([jnp.where(m[0], st, 0.0) for m in head_masks], axis=0).astype(BF16)
            res = jnp.dot(rm_ref[sl, ci], st_bd, preferred_element_type=F32) + yn_ref[sl, ci]
            y_ref[ci, :, lanes] = res[:CHUNK]
            s_ref[:, lanes] = res[CHUNK:]
        return carry

    lax.fori_loop(0, ng, chunk_body, 0)

    y = y_ref[...].reshape(rows_n, W)
    mean = _mm_split2_rhs_ones(y, seg) * (1.0 / RWKV_HEAD)
    yc = y - mean
    var = _mm_split2_rhs_ones(yc * yc, seg) * (1.0 / RWKV_HEAD)
    yn = yc * lax.rsqrt(var + RWKV_LN_EPS)
    o_ref[0] = (yn * lng_ref[...] + lnb_ref[...] + bonus_ref[...]) * gate_ref[...]


def _rwkv(pr, mu, w0, w2, a0, a2, g2, k_k, k_a, r_k, ln_g, ln_b, seg):
    bn, lp, _ = pr.shape
    nc = lp // CHUNK
    ng = max(d for d in range(1, RWKV_GROUP + 1) if nc % d == 0)
    rows = ng * CHUNK
    vec = lambda n: pl.BlockSpec((1, n), lambda b, c: (0, 0))
    mat = lambda m, n: pl.BlockSpec((m, n), lambda b, c: (0, 0))
    hd = RWKV_HEAD
    n_slab = RWKV_WIDTH // RWKV_SLAB
    return pl.pallas_call(
        _rwkv_kernel,
        name="rwkv_mixer",
        grid=(bn, nc // ng),
        in_specs=[
            pl.BlockSpec((1, rows, RWKV_COLS), lambda b, c: (b, c, 0)),
            pl.BlockSpec((1, 8, RWKV_COLS), lambda b, c: (b, jnp.maximum(c * (rows // 8) - 1, 0), 0)),
            vec(RWKV_COLS), vec(RWKV_WIDTH), mat(RWKV_W_RANK, RWKV_WIDTH), vec(RWKV_WIDTH),
            mat(RWKV_A_RANK, RWKV_WIDTH), mat(RWKV_G_RANK, RWKV_WIDTH), vec(RWKV_WIDTH), vec(RWKV_WIDTH),
            vec(RWKV_WIDTH), vec(RWKV_WIDTH), vec(RWKV_WIDTH), mat(RWKV_WIDTH, RWKV_WIDTH),
        ],
        out_specs=pl.BlockSpec((1, rows, RWKV_WIDTH), lambda b, c: (b, c, 0)),
        out_shape=jax.ShapeDtypeStruct((bn, lp, RWKV_WIDTH), F32),
        scratch_shapes=[
            pltpu.VMEM((hd, RWKV_WIDTH), F32),
            pltpu.VMEM((7, n_slab, ng, CHUNK, RWKV_SLAB), BF16),
            pltpu.VMEM((n_slab, ng, 1, RWKV_SLAB), F32),
            pltpu.VMEM((n_slab, ng, 2 * CHUNK, RWKV_SLAB), BF16),
            pltpu.VMEM((n_slab, ng, 2 * CHUNK, RWKV_SLAB), F32),
            pltpu.VMEM((ng, CHUNK, RWKV_WIDTH), F32),
            pltpu.VMEM((rows, RWKV_WIDTH), F32),
            pltpu.VMEM((rows, RWKV_WIDTH), F32),
        ],
        compiler_params=pltpu.CompilerParams(dimension_semantics=("arbitrary", "arbitrary"),
                                             vmem_limit_bytes=VMEM_LIMIT),
    )(pr, pr, mu, w0, w2, a0, a2, g2, k_k, k_a, r_k, ln_g, ln_b, seg.astype(BF16))


def _outproj_router_kernel(yg_ref, yr_ref, h_ref, wo_ref, g_ref, b_ref, rwt_ref, rb_ref, tri_ref,
                           h1_ref, idx_ref, gate_ref, rank_ref, cnt_ref, base_ref):
    i = pl.program_id(0)

    @pl.when(i == 0)
    def _():
        base_ref[...] = jnp.zeros_like(base_ref)

    wo = wo_ref[...]
    mix = _mm(yg_ref[...], wo[:GLA_WIDTH]) + _mm(yr_ref[...], wo[GLA_WIDTH:])
    h1 = _layer_norm(DEEPNORM_ALPHA * h_ref[...] + mix, g_ref[...], b_ref[...])
    _store_row_tiles(h1_ref, h1)
    h_hi = h1.astype(BF16)
    h_lo = (h1 - h_hi.astype(F32)).astype(BF16)
    nt = (((1,), (1,)), ((), ()))
    part = lax.dot_general(rwt_ref[...], h_hi, nt, preferred_element_type=F32)
    work = (part[:N_EXPERTS] + part[N_EXPERTS:]
            + lax.dot_general(rwt_ref[:N_EXPERTS, :], h_lo, nt, preferred_element_type=F32)
            + rb_ref[...][:, 0:1])
    tm = work.shape[1]
    e_iota = lax.broadcasted_iota(I32, (N_EXPERTS, tm), 0)
    base = base_ref[...][:, 0:1]
    vals, onehots = [], []
    for kk in range(TOP_K):
        m = jnp.max(work, axis=0, keepdims=True)
        sel = jnp.min(jnp.where(work == m, e_iota, N_EXPERTS), axis=0, keepdims=True)
        onehot = e_iota == sel
        work = jnp.where(onehot, -jnp.inf, work)
        vals.append(m)
        onehots.append(onehot.astype(F32))
        idx_ref[kk:kk + 1, :] = sel
    cnt_all = jnp.dot(jnp.concatenate(onehots, axis=0).astype(BF16), tri_ref[...], preferred_element_type=F32)
    prior = jnp.zeros((N_EXPERTS, 1), F32)
    for kk in range(TOP_K):
        cnt = cnt_all[kk * N_EXPERTS:(kk + 1) * N_EXPERTS]
        rank = jnp.sum(onehots[kk] * (base + prior + cnt - 1.0), axis=0, keepdims=True)
        prior = prior + cnt[:, tm - 1:tm]
        rank_ref[kk:kk + 1, :] = rank.astype(I32)
    es = [jnp.exp(vv - vals[0]) for vv in vals]
    den = es[0] + es[1] + es[2] + es[3]
    for kk in range(TOP_K):
        gate_ref[kk:kk + 1, :] = es[kk] / den
    new_base = base + prior
    base_ref[...] = jnp.broadcast_to(new_base, base_ref.shape)
    cnt_ref[...] = jnp.broadcast_to(new_base, cnt_ref.shape)


def _outproj_router(yg, yr, h, wo, g, b, rwt, rb, tri):
    tp = h.shape[0]
    tm = ROUTER_TILE
    const = lambda m, n: pl.BlockSpec((m, n), lambda i: (0, 0))
    return pl.pallas_call(
        _outproj_router_kernel,
        name="outproj_router",
        grid=(tp // tm,),
        in_specs=[
            pl.BlockSpec((tm, GLA_WIDTH), lambda i: (i, 0)),
            pl.BlockSpec((tm, RWKV_WIDTH), lambda i: (i, 0)),
            pl.BlockSpec((tm, D_MODEL), lambda i: (i, 0)),
            const(D_MODEL, D_MODEL), const(1, D_MODEL), const(1, D_MODEL),
            const(2 * N_EXPERTS, D_MODEL), const(N_EXPERTS, 128), const(tm, tm),
        ],
        out_specs=[
            pl.BlockSpec(_tiled_rows(tm), lambda i: (i, 0)),
            pl.BlockSpec((TOP_K, tm), lambda i: (0, i)),
            pl.BlockSpec((TOP_K, tm), lambda i: (0, i)),
            pl.BlockSpec((TOP_K, tm), lambda i: (0, i)),
            pl.BlockSpec((N_EXPERTS, 128), lambda i: (0, 0)),
        ],
        out_shape=[
            jax.ShapeDtypeStruct(_tiled_rows(tp), F32),
            jax.ShapeDtypeStruct((TOP_K, tp), I32),
            jax.ShapeDtypeStruct((TOP_K, tp), F32),
            jax.ShapeDtypeStruct((TOP_K, tp), I32),
            jax.ShapeDtypeStruct((N_EXPERTS, 128), F32),
        ],
        scratch_shapes=[pltpu.VMEM((N_EXPERTS, 128), F32)],
        compiler_params=pltpu.CompilerParams(dimension_semantics=("arbitrary",),
                                             vmem_limit_bytes=VMEM_LIMIT),
    )(yg, yr, h, wo, g, b, rwt, rb, tri)


def _zero_blocks_kernel(last_ref, o_ref):
    o_ref[...] = jnp.zeros_like(o_ref)


def _zero_blocks(last_block, n_slots):
    return pl.pallas_call(
        _zero_blocks_kernel,
        name="moe_zero_blocks",
        grid_spec=pltpu.PrefetchScalarGridSpec(
            num_scalar_prefetch=1,
            grid=(N_EXPERTS,),
            in_specs=[],
            out_specs=pl.BlockSpec(_tiled_rows(MOE_BLOCK), lambda e, last: (last[e], 0)),
        ),
        out_shape=jax.ShapeDtypeStruct(_tiled_rows(n_slots), F32),
        compiler_params=pltpu.CompilerParams(dimension_semantics=("arbitrary",)),
    )(last_block)


def _dispatch_kernel(pos_ref, x_ref, xs_in_ref, xs_ref, sem):
    del xs_in_ref
    tm = pos_ref.shape[0] // TOP_K

    def start(r, carry):
        for kk in range(TOP_K):
            pltpu.make_async_copy(_row_tile(x_ref, r), _row_tile(xs_ref, pos_ref[r * TOP_K + kk]),
                                  sem).start(priority=kk % 2)
        return carry

    lax.fori_loop(0, tm, start, 0, unroll=DMA_UNROLL)
    for kk in range(TOP_K):
        pltpu.make_async_copy(x_ref, xs_ref.at[pl.ds(0, x_ref.shape[0])], sem).wait()


def _dispatch(pos, x, xs):
    tp = pos.shape[0] // TOP_K
    tm = ROUTER_TILE
    return pl.pallas_call(
        _dispatch_kernel,
        name="moe_dispatch",
        grid=(tp // tm,),
        in_specs=[
            pl.BlockSpec((tm * TOP_K,), lambda i: (i,), memory_space=pltpu.SMEM),
            pl.BlockSpec(_tiled_rows(tm), lambda i: (i, 0)),
            pl.BlockSpec(memory_space=pl.ANY),
        ],
        out_specs=pl.BlockSpec(memory_space=pl.ANY),
        out_shape=jax.ShapeDtypeStruct(xs.shape, xs.dtype),
        scratch_shapes=[pltpu.SemaphoreType.DMA(())],
        input_output_aliases={2: 0},
        compiler_params=pltpu.CompilerParams(dimension_semantics=("arbitrary",)),
    )(pos, x, xs)


SPLIT_TILE = 256


def _moe_kernel(be_ref, nu_ref, xs_ref, wu_ref, wd_ref, perm_ref, bg_ref, bl_ref, bd_ref, ys_ref,
                wg_s, wl_s, wd_s):
    i = pl.program_id(0)

    @pl.when(i < nu_ref[0])
    def _():
        @pl.when(jnp.logical_or(i == 0, be_ref[i] != be_ref[jnp.maximum(i - 1, 0)]))
        def _():
            half = SPLIT_TILE // 2
            for t in range(wu_ref.shape[2] // SPLIT_TILE):
                d = jnp.dot(wu_ref[0, :, t * SPLIT_TILE:(t + 1) * SPLIT_TILE].astype(BF16), perm_ref[...],
                            preferred_element_type=F32).astype(BF16)
                wg_s[:, t * half:(t + 1) * half] = d[:, :half]
                wl_s[:, t * half:(t + 1) * half] = d[:, half:]
            wd_s[...] = wd_ref[0].astype(BF16)

        x = _load_row_tiles(xs_ref).astype(BF16)
        x_glu = jnp.dot(x, wg_s[...], preferred_element_type=F32) + bg_ref[0]
        x_lin = jnp.dot(x, wl_s[...], preferred_element_type=F32) + bl_ref[0]
        x_glu = jnp.minimum(x_glu, SWIGLU_LIMIT)
        x_lin = jnp.clip(x_lin, -SWIGLU_LIMIT, SWIGLU_LIMIT)
        act = x_glu * _sigmoid(SWIGLU_ALPHA * x_glu) * (x_lin + 1.0)
        _store_row_tiles(ys_ref, jnp.dot(act.astype(BF16), wd_s[...], preferred_element_type=F32) + bd_ref[0])


def _moe(block_e, n_used, xs, w_up, w_down, bg, bl, bd):
    n_slots = xs.shape[0] // ROW_TILE[0]
    nb = n_slots // MOE_BLOCK
    src = jnp.arange(SPLIT_TILE, dtype=I32)[:, None]
    dst = jnp.arange(SPLIT_TILE, dtype=I32)[None, :]
    half = SPLIT_TILE // 2
    perm = (src == jnp.where(dst < half, 2 * dst, 2 * (dst - half) + 1)).astype(BF16)
    blk = lambda i, be, nu: (jnp.maximum(jnp.minimum(i, nu[0] - 1), 0), 0)
    bspec = pl.BlockSpec((1, 1, D_FF), lambda i, be, nu: (be[i], 0, 0))
    return pl.pallas_call(
        _moe_kernel,
        name="moe_experts",
        grid_spec=pltpu.PrefetchScalarGridSpec(
            num_scalar_prefetch=2,
            grid=(nb,),
            in_specs=[
                pl.BlockSpec(_tiled_rows(MOE_BLOCK), blk),
                pl.BlockSpec((1, D_MODEL, 2 * D_FF), lambda i, be, nu: (be[i], 0, 0)),
                pl.BlockSpec((1, D_FF, D_MODEL), lambda i, be, nu: (be[i], 0, 0)),
                pl.BlockSpec((SPLIT_TILE, SPLIT_TILE), lambda i, be, nu: (0, 0)),
                bspec, bspec, bspec,
            ],
            out_specs=pl.BlockSpec(_tiled_rows(MOE_BLOCK), blk),
            scratch_shapes=[
                pltpu.VMEM((D_MODEL, D_FF), BF16),
                pltpu.VMEM((D_MODEL, D_FF), BF16),
                pltpu.VMEM((D_FF, D_MODEL), BF16),
            ],
        ),
        out_shape=jax.ShapeDtypeStruct(_tiled_rows(n_slots), F32),
        compiler_params=pltpu.CompilerParams(dimension_semantics=("arbitrary",),
                                             vmem_limit_bytes=VMEM_LIMIT),
    )(block_e, n_used, xs, w_up, w_down, perm, bg, bl, bd)


COMBINE_SLOTS = 3
COMBINE_CHUNKS = 2


def _combine_kernel(*refs, cps):
    first_pos = refs[0:2]
    ahead_pos = refs[2:2 + cps]
    gts = refs[2 + cps:2 + 2 * cps]
    h1s = refs[2 + 2 * cps:2 + 3 * cps]
    g_ref, b_ref, ys_ref, o_ref, buf, sem = refs[2 + 3 * cps:]
    s = pl.program_id(0)
    rows = TOP_K * CHUNK

    def issue_row(p_ref, to_slot, r):
        for kk in range(TOP_K):
            pltpu.make_async_copy(_row_tile(ys_ref, p_ref[r * TOP_K + kk]),
                                  _row_tile(buf, to_slot * rows + kk * CHUNK + r),
                                  sem.at[to_slot]).start(priority=kk % 2)

    def wait_slot(which):
        span = rows * ROW_TILE[0]
        pltpu.make_async_copy(ys_ref.at[pl.ds(0, span)], buf.at[pl.ds(which * span, span)], sem.at[which]).wait()

    @pl.when(s == 0)
    def _():
        def body(r, carry):
            issue_row(first_pos[0], 0, r)
            issue_row(first_pos[1], 1, r)
            return carry

        lax.fori_loop(0, CHUNK, body, 0, unroll=DMA_UNROLL)

    def reduce_chunk(j, cur):
        ahead = (cur + 2) % COMBINE_SLOTS
        wait_slot(cur)
        gt = gts[j][...]
        quarter = CHUNK // TOP_K
        ffn = None
        for kk in range(TOP_K):
            part = _load_row_tiles(buf, cur * rows + kk * CHUNK, CHUNK) * gt[:, kk:kk + 1]
            ffn = part if ffn is None else ffn + part
            for r in range(kk * quarter, (kk + 1) * quarter):
                issue_row(ahead_pos[j], ahead, r)
        o_ref[0, j * CHUNK:(j + 1) * CHUNK, :] = _layer_norm(
            DEEPNORM_ALPHA * _load_row_tiles(h1s[j]) + ffn, g_ref[...], b_ref[...])

    def step(first_slot):
        for j in range(cps):
            reduce_chunk(j, (first_slot + j) % COMBINE_SLOTS)

        @pl.when(s == pl.num_programs(0) - 1)
        def _():
            last = (first_slot + cps - 1) % COMBINE_SLOTS
            wait_slot((last + 1) % COMBINE_SLOTS)
            wait_slot((last + 2) % COMBINE_SLOTS)

    for first_slot in range(COMBINE_SLOTS):
        pl.when((s * cps) % COMBINE_SLOTS == first_slot)(functools.partial(step, first_slot))


def _combine(pos_flat, gates_t, h1, g, b, ys, bn, lp):
    seq = lp - CHUNK
    nc = lp // CHUNK
    ncs = seq // CHUNK
    cps = COMBINE_CHUNKS if ncs % COMBINE_CHUNKS == 0 else 1
    n_chunks = bn * ncs
    chunk_of = lambda m: (m // ncs) * nc + m % ncs + 1
    clamp = lambda m: jnp.minimum(m, n_chunks - 1)
    pos_spec = lambda f: pl.BlockSpec((TOP_K * CHUNK,), lambda s: (chunk_of(f(s)),), memory_space=pltpu.SMEM)
    per_chunk = lambda shape: [pl.BlockSpec(shape, lambda s, j=j: (chunk_of(s * cps + j), 0)) for j in range(cps)]
    return pl.pallas_call(
        functools.partial(_combine_kernel, cps=cps),
        name="moe_combine",
        grid=(n_chunks // cps,),
        in_specs=[pos_spec(lambda s: 0 * s), pos_spec(lambda s: 0 * s + 1)]
        + [pos_spec(lambda s, j=j: clamp(s * cps + j + 2)) for j in range(cps)]
        + per_chunk((CHUNK, TOP_K)) + per_chunk(_tiled_rows(CHUNK))
        + [
            pl.BlockSpec((1, D_MODEL), lambda s: (0, 0)),
            pl.BlockSpec((1, D_MODEL), lambda s: (0, 0)),
            pl.BlockSpec(memory_space=pl.ANY),
        ],
        out_specs=pl.BlockSpec((1, cps * CHUNK, D_MODEL), lambda s: (s // (ncs // cps), s % (ncs // cps), 0)),
        out_shape=jax.ShapeDtypeStruct((bn, seq, D_MODEL), F32),
        scratch_shapes=[pltpu.VMEM(_tiled_rows(COMBINE_SLOTS * TOP_K * CHUNK), F32),
                        pltpu.SemaphoreType.DMA((COMBINE_SLOTS,))],
        compiler_params=pltpu.CompilerParams(dimension_semantics=("arbitrary",)),
    )(*([pos_flat] * (2 + cps) + [gates_t] * cps + [h1] * cps + [g, b, ys]))


def kernel(x, meta, ln_in_g, ln_in_b, w_in, gla_gk_w2, gla_gk_b, gla_norm_g, rwkv_mu, rwkv_w0, rwkv_w2, rwkv_a0, rwkv_a2, rwkv_g2, rwkv_k_k, rwkv_k_a, rwkv_r_k, rwkv_ln_g, rwkv_ln_b, w_out, ln1_g, ln1_b, router_w, router_b, exp_w_up, exp_b_up, exp_w_down, exp_b_down, ln2_g, ln2_b):
    bn, seq, _ = x.shape
    assert seq % CHUNK == 0
    lp = seq + CHUNK
    tp = bn * lp
    assert tp % ROUTER_TILE == 0
    row = lambda t: t.reshape(1, -1).astype(F32)

    hcat = jnp.concatenate([jnp.zeros((bn, N_FRONT, D_MODEL), F32),
                            jnp.broadcast_to(meta[None].astype(F32), (bn, N_META, D_MODEL)), x], axis=1)
    hcat = hcat.reshape(tp, D_MODEL)
    gla_in = 2 * GLA_KEY + 2 * GLA_WIDTH + GLA_GATE_RANK
    w = w_in[0]
    w_cols = jnp.concatenate([w[:, :gla_in], jnp.zeros((D_MODEL, 128 - GLA_GATE_RANK), F32), w[:, gla_in:]],
                             axis=1).astype(BF16)
    h, pg, pr = _ln_inproj(hcat, row(ln_in_g), row(ln_in_b), w_cols, lp)

    w2p = jnp.concatenate([gla_gk_w2[0], jnp.zeros((128 - GLA_GATE_RANK, GLA_KEY), F32)], axis=0)
    y_gla = _gla(pg.reshape(bn, lp, GLA_COLS), w2p, row(gla_gk_b[0]), row(gla_norm_g[0]))

    head_id = jnp.arange(RWKV_WIDTH, dtype=I32) // RWKV_HEAD
    seg = (head_id[:, None] == head_id[None, :]).astype(F32)
    y_rwkv = _rwkv(pr.reshape(bn, lp, RWKV_COLS), row(rwkv_mu[0]), row(rwkv_w0[0]), rwkv_w2[0],
                   row(rwkv_a0[0]), rwkv_a2[0], rwkv_g2[0], row(rwkv_k_k[0]), row(rwkv_k_a[0]),
                   row(rwkv_r_k[0]), row(rwkv_ln_g[0]), row(rwkv_ln_b[0]), seg)

    tri = jnp.triu(jnp.ones((ROUTER_TILE, ROUTER_TILE), F32)).astype(BF16)
    rb = jnp.broadcast_to(router_b[0].reshape(N_EXPERTS, 1), (N_EXPERTS, 128))
    rw_t = router_w[0].T
    rw_hi = rw_t.astype(BF16)
    rwt = jnp.concatenate([rw_hi, (rw_t - rw_hi.astype(F32)).astype(BF16)], axis=0)
    h1, idx, gates, rank, cnt = _outproj_router(
        y_gla.reshape(tp, GLA_WIDTH), y_rwkv.reshape(tp, RWKV_WIDTH), h, w_out[0].astype(BF16),
        row(ln1_g[0]), row(ln1_b[0]), rwt, rb, tri)

    counts = cnt[:, 0].astype(I32)
    padded = (counts + MOE_BLOCK - 1) // MOE_BLOCK * MOE_BLOCK
    ends_p = jnp.cumsum(padded)
    starts_p = ends_p - padded
    e_ids = jnp.arange(N_EXPERTS, dtype=I32)
    start_of = jnp.sum(jnp.where(idx[None] == e_ids[:, None, None], starts_p[:, None, None], 0), axis=0)
    pos = start_of + rank
    nb = tp * TOP_K // MOE_BLOCK + N_EXPERTS
    n_slots = nb * MOE_BLOCK
    block_start = jnp.arange(nb, dtype=I32) * MOE_BLOCK
    block_e = jnp.minimum(jnp.sum((block_start[:, None] >= ends_p[None, :]).astype(I32), axis=1), N_EXPERTS - 1)
    n_used = (ends_p[-1:] // MOE_BLOCK).astype(I32)
    last_block = jnp.maximum(ends_p // MOE_BLOCK - 1, 0).astype(I32)

    pos_flat = pos.T.reshape(tp * TOP_K)
    xs = _dispatch(pos_flat, h1, _zero_blocks(last_block, n_slots))
    bg = exp_b_up[0][:, None, 0::2]
    bl = exp_b_up[0][:, None, 1::2]
    ys = _moe(block_e, n_used, xs, exp_w_up[0], exp_w_down[0], bg, bl, exp_b_down[0][:, None, :])

    return _combine(pos_flat, gates.T, h1, row(ln2_g[0]), row(ln2_b[0]), ys, bn, lp)
```

```python
import functools

import jax
import jax.numpy as jnp
from jax import lax
from jax.experimental import pallas as pl
from jax.experimental.pallas import tpu as pltpu

F32 = jnp.float32
BF16 = jnp.bfloat16
I32 = jnp.int32
HIGHEST = lax.Precision.HIGHEST

D_MODEL = 1024
N_META = 16
CHUNK = 64
N_FRONT = (-N_META) % CHUNK
GLA_HEADS = 4
GLA_DK = 64
GLA_DV = 128
GLA_KEY = GLA_HEADS * GLA_DK
GLA_WIDTH = GLA_HEADS * GLA_DV
GLA_GATE_RANK = 16
GLA_TAU = 16.0
GLA_COLS = 2 * GLA_KEY + 2 * GLA_WIDTH + 128
RWKV_WIDTH = 512
RWKV_HEAD = 64
RWKV_HEADS = RWKV_WIDTH // RWKV_HEAD
RWKV_W_RANK = 64
RWKV_A_RANK = 64
RWKV_G_RANK = 128
RWKV_COLS = 3 * RWKV_WIDTH + RWKV_W_RANK + RWKV_A_RANK + RWKV_G_RANK
N_EXPERTS = 32
TOP_K = 4
D_FF = D_MODEL
SWIGLU_ALPHA = 1.702
SWIGLU_LIMIT = 7.0
MOE_BLOCK = 512
DEPTH = 1
DEEPNORM_ALPHA = (2.0 * DEPTH) ** 0.25
LN_EPS = 1e-5
RWKV_LN_EPS = 64e-5
RMS_EPS = 1e-6

ROUTER_TILE = 768
DMA_UNROLL = 4
VMEM_LIMIT = 56 * 1024 * 1024


def _mm(a, b):
    return jnp.dot(a.astype(BF16), b.astype(BF16), preferred_element_type=F32)


def _mm_nt(a, b):
    return lax.dot_general(a.astype(BF16), b.astype(BF16), (((1,), (1,)), ((), ())),
                           preferred_element_type=F32)


def _mm_tn(a, b):
    return lax.dot_general(a.astype(BF16), b.astype(BF16), (((0,), (0,)), ((), ())),
                           preferred_element_type=F32)


def _mm_f32(a, b):
    return jnp.dot(a, b, preferred_element_type=F32, precision=HIGHEST)


def _layer_norm(x, g, b):
    mu = jnp.mean(x, axis=-1, keepdims=True)
    xc = x - mu
    var = jnp.mean(xc * xc, axis=-1, keepdims=True)
    return xc * lax.rsqrt(var + LN_EPS) * g + b


def _sigmoid(x):
    return 1.0 / (1.0 + jnp.exp(-x))


def _log_sigmoid(x):
    return jnp.minimum(x, 0.0) - jnp.log(1.0 + jnp.exp(-jnp.abs(x)))


ROW_TILE = (8, 128)


def _tiled_rows(n):
    return (n * ROW_TILE[0], ROW_TILE[1])


def _row_tile(ref, i):
    return ref.at[pl.ds(pl.multiple_of(i * ROW_TILE[0], ROW_TILE[0]), ROW_TILE[0])]


def _store_row_tiles(ref, x, row0=0):
    n = x.shape[0]
    for j in range(ROW_TILE[0]):
        ref[pl.ds(row0 * ROW_TILE[0] + j, n, stride=ROW_TILE[0]), :] = x[:, j * ROW_TILE[1]:(j + 1) * ROW_TILE[1]]


def _load_row_tiles(ref, row0=0, n=None):
    n = ref.shape[0] // ROW_TILE[0] if n is None else n
    return jnp.concatenate([ref[pl.ds(row0 * ROW_TILE[0] + j, n, stride=ROW_TILE[0]), :]
                            for j in range(ROW_TILE[0])], axis=1)


def _tri_masks(n):
    r = lax.broadcasted_iota(I32, (n, n), 0)
    c = lax.broadcasted_iota(I32, (n, n), 1)
    return r >= c, r > c, r == c


def _ln_inproj_kernel(x_ref, g_ref, b_ref, w_ref, h_ref, pg_ref, pr_ref, *, tiles_per_seq):
    i = pl.program_id(0)
    y = _layer_norm(x_ref[...], g_ref[...], b_ref[...])
    row = lax.broadcasted_iota(I32, (y.shape[0], 1), 0)
    is_front = jnp.logical_and(i % tiles_per_seq == 0, row < N_FRONT)
    y = jnp.where(is_front, 0.0, y)
    h_ref[...] = y
    p = _mm(y, w_ref[...])
    pg_ref[...] = p[:, :GLA_COLS]
    pr_ref[...] = p[:, GLA_COLS:]


def _ln_inproj(hcat, g, b, w, lp):
    tp = hcat.shape[0]
    tiles_per_seq = 1
    for cand in range(1, lp // 8 + 1):
        if lp % cand == 0 and (lp // cand) % 8 == 0 and lp // cand >= N_FRONT and lp // cand <= 384:
            tiles_per_seq = cand
            break
    tm = lp // tiles_per_seq
    ncols = GLA_COLS + RWKV_COLS
    return pl.pallas_call(
        functools.partial(_ln_inproj_kernel, tiles_per_seq=tiles_per_seq),
        name="ln_inproj",
        grid=(tp // tm,),
        in_specs=[
            pl.BlockSpec((tm, D_MODEL), lambda i: (i, 0)),
            pl.BlockSpec((1, D_MODEL), lambda i: (0, 0)),
            pl.BlockSpec((1, D_MODEL), lambda i: (0, 0)),
            pl.BlockSpec((D_MODEL, ncols), lambda i: (0, 0)),
        ],
        out_specs=[
            pl.BlockSpec((tm, D_MODEL), lambda i: (i, 0)),
            pl.BlockSpec((tm, GLA_COLS), lambda i: (i, 0)),
            pl.BlockSpec((tm, RWKV_COLS), lambda i: (i, 0)),
        ],
        out_shape=[
            jax.ShapeDtypeStruct((tp, D_MODEL), F32),
            jax.ShapeDtypeStruct((tp, GLA_COLS), F32),
            jax.ShapeDtypeStruct((tp, RWKV_COLS), F32),
        ],
        compiler_params=pltpu.CompilerParams(dimension_semantics=("arbitrary",),
                                             vmem_limit_bytes=VMEM_LIMIT),
    )(hcat, g, b, w)


GLA_GROUP = 11


def _gla_kernel(pg_ref, w2_ref, gkb_ref, ng_ref, o_ref, st_ref, qe_ref, oi_ref, kvt_ref, el_ref):
    c = pl.program_id(1)

    @pl.when(c == 0)
    def _():
        st_ref[...] = jnp.zeros_like(st_ref)

    rows_n = pg_ref.shape[1]
    ng = rows_n // CHUNK
    g_off = 2 * GLA_KEY + GLA_WIDTH
    p = pg_ref[0]
    gl = p[:, g_off + GLA_WIDTH:]
    lg = _log_sigmoid(_mm_f32(gl, w2_ref[...]) + gkb_ref[...]) * (1.0 / GLA_TAU)
    row = lax.broadcasted_iota(I32, (rows_n, 1), 0)
    lg = jnp.where(jnp.logical_and(c == 0, row < N_FRONT), 0.0, lg)
    incl, _, _ = _tri_masks(CHUNK)
    tril = incl.astype(BF16)
    bc = jnp.concatenate([_mm_split3(tril, lg[i * CHUNK:(i + 1) * CHUNK]) for i in range(ng)], axis=0)
    g3 = lambda t: t.reshape(ng, CHUNK, t.shape[-1])
    bc = g3(bc)
    b_last = bc[:, CHUNK - 1:CHUNK, :]
    k = g3(p[:, GLA_KEY:2 * GLA_KEY])
    q = g3(p[:, 0:GLA_KEY]) * (GLA_DK ** -0.5)
    qe = q * jnp.exp(bc)
    b_mid = bc[:, CHUNK // 2 - 1:CHUNK // 2, :]
    qm = q * jnp.exp(bc - b_mid)
    km = k * jnp.exp(b_mid - bc)
    kl = k * jnp.exp(b_last - bc)
    e_last = jnp.exp(b_last)
    v = g3(p[:, 2 * GLA_KEY:g_off]).astype(BF16)
    for h in range(GLA_HEADS):
        ks = slice(h * GLA_DK, (h + 1) * GLA_DK)
        vs = slice(h * GLA_DV, (h + 1) * GLA_DV)
        qh = qe[:, :, ks].astype(BF16)
        a = jnp.einsum('gtd,gsd->gts', qm[:, :, ks].astype(BF16), km[:, :, ks].astype(BF16),
                       preferred_element_type=F32)
        a = jnp.where(incl, a, 0.0).astype(BF16)
        oi_ref[:, h] = jnp.einsum('gts,gsv->gtv', a, v[:, :, vs], preferred_element_type=F32)
        kvt_ref[:, h] = jnp.einsum('gtv,gtd->gvd', v[:, :, vs], kl[:, :, ks].astype(BF16),
                                   preferred_element_type=F32)
        qe_ref[:, h] = qh
        el_ref[:, h] = e_last[:, :, ks]

    def chunk_body(ci, carry):
        st = st_ref[...]
        oi_ref[ci] = oi_ref[ci] + jnp.einsum('htd,hvd->htv', qe_ref[ci], st.astype(BF16),
                                             preferred_element_type=F32)
        st_ref[...] = st * el_ref[ci] + kvt_ref[ci]
        return carry

    lax.fori_loop(0, ng, chunk_body, 0)

    for ci in range(ng):
        o = oi_ref[ci]
        o = o * lax.rsqrt(jnp.mean(o * o, axis=-1, keepdims=True) + RMS_EPS) * ng_ref[...]
        o = jnp.concatenate([o[h] for h in range(GLA_HEADS)], axis=1)
        rows = slice(ci * CHUNK, (ci + 1) * CHUNK)
        gate = pg_ref[0, rows, g_off:g_off + GLA_WIDTH]
        o_ref[0, rows, :] = o * (gate * _sigmoid(gate))


def _gla(pg, w2p, gkb, ng_w):
    bn, lp, _ = pg.shape
    nc = lp // CHUNK
    ng = max(d for d in range(1, GLA_GROUP + 1) if nc % d == 0)
    rows = ng * CHUNK
    hh = GLA_HEADS
    return pl.pallas_call(
        _gla_kernel,
        name="gla_mixer",
        grid=(bn, nc // ng),
        in_specs=[
            pl.BlockSpec((1, rows, GLA_COLS), lambda b, c: (b, c, 0)),
            pl.BlockSpec((128, GLA_KEY), lambda b, c: (0, 0)),
            pl.BlockSpec((1, GLA_KEY), lambda b, c: (0, 0)),
            pl.BlockSpec((1, GLA_DV), lambda b, c: (0, 0)),
        ],
        out_specs=pl.BlockSpec((1, rows, GLA_WIDTH), lambda b, c: (b, c, 0)),
        out_shape=jax.ShapeDtypeStruct((bn, lp, GLA_WIDTH), F32),
        scratch_shapes=[
            pltpu.VMEM((hh, GLA_DV, GLA_DK), F32),
            pltpu.VMEM((ng, hh, CHUNK, GLA_DK), BF16),
            pltpu.VMEM((ng, hh, CHUNK, GLA_DV), F32),
            pltpu.VMEM((ng, hh, GLA_DV, GLA_DK), F32),
            pltpu.VMEM((ng, hh, 1, GLA_DK), F32),
        ],
        compiler_params=pltpu.CompilerParams(dimension_semantics=("arbitrary", "arbitrary"),
                                             vmem_limit_bytes=VMEM_LIMIT),
    )(pg, w2p, gkb, ng_w)


RWKV_GROUP = 11
RWKV_SLAB = 256
def _mm_split3(ones_bf16, x):
    hi = x.astype(BF16)
    r1 = x - hi.astype(F32)
    mid = r1.astype(BF16)
    lo = (r1 - mid.astype(F32)).astype(BF16)
    return jnp.dot(jnp.concatenate([ones_bf16] * 3, axis=1), jnp.concatenate([hi, mid, lo], axis=0),
                   preferred_element_type=F32)


def _mm_split2_rhs_ones(x, ones_bf16):
    hi = x.astype(BF16)
    mid = (x - hi.astype(F32)).astype(BF16)
    dot = lambda t: jnp.dot(t, ones_bf16, preferred_element_type=F32)
    return dot(hi) + dot(mid)


def _rwkv_kernel(pr_ref, pv_ref, mu_ref, w0_ref, w2_ref, a0_ref, a2_ref, g2_ref, kk_ref, ka_ref,
                 rk_ref, lng_ref, lnb_ref, seg_ref, o_ref, s_ref, ops_ref, wend_ref, rm_ref, yn_ref,
                 y_ref, bonus_ref, gate_ref):
    c = pl.program_id(1)
    rows_n = pr_ref.shape[1]
    ng = rows_n // CHUNK

    @pl.when(c == 0)
    def _():
        s_ref[...] = jnp.zeros_like(s_ref)

    p = pr_ref[0]
    prev_row = jnp.where(c > 0, pv_ref[0][7:8, :], 0.0)
    row = lax.broadcasted_iota(I32, (rows_n, 1), 0)
    prev = jnp.where(row == 0, prev_row, pltpu.roll(p, 1, 0))
    p = p + (prev - p) * mu_ref[...]
    W = RWKV_WIDTH
    r = p[:, 0:W]
    k = p[:, W:2 * W]
    v = p[:, 2 * W:3 * W]
    w_low = p[:, 3 * W:3 * W + RWKV_W_RANK]
    a_low = p[:, 3 * W + RWKV_W_RANK:3 * W + RWKV_W_RANK + RWKV_A_RANK]
    g_low = p[:, 3 * W + RWKV_W_RANK + RWKV_A_RANK:]
    wx = w0_ref[...] + _mm(jnp.tanh(w_low), w2_ref[...])
    w = _log_sigmoid(wx) - 0.5
    logd = -jnp.exp(w)
    a = _sigmoid(a0_ref[...] + _mm(a_low, a2_ref[...]))
    g = _mm(_sigmoid(g_low), g2_ref[...])
    seg = seg_ref[...]
    kk = k * kk_ref[...]
    kk = kk * lax.rsqrt(jnp.maximum(_mm_split2_rhs_ones(kk * kk, seg), 1e-24))
    k = k * (1.0 + (a - 1.0) * ka_ref[...])
    bonus = _mm_split2_rhs_ones(r * k * rk_ref[...], seg) * v

    bonus_ref[...] = bonus
    gate_ref[...] = g

    incl, strict, diag = _tri_masks(CHUNK)
    tril = incl.astype(BF16)
    c_in = jnp.concatenate([_mm_split3(tril, logd[i * CHUNK:(i + 1) * CHUNK]) for i in range(ng)], axis=0)
    g3 = lambda t: t.reshape(ng, CHUNK, W)
    logd, c_in, r, k, v, kk, a = g3(logd), g3(c_in), g3(r), g3(k), g3(v), g3(kk), g3(a)
    c_last = c_in[:, CHUNK - 1:CHUNK, :]
    e_neg = jnp.exp(-c_in)
    e_end = jnp.exp(c_last - c_in)
    kka = kk * a
    per_head = (-kk * jnp.exp(c_in - logd), r * jnp.exp(c_in), kka * e_neg, k * e_neg, kka * e_end,
                k * e_end, v)
    GW = RWKV_SLAB
    n_slab = W // GW
    for i, t in enumerate(per_head):
        t = t.astype(BF16)
        for sl in range(n_slab):
            ops_ref[i, sl] = t[:, :, sl * GW:(sl + 1) * GW]
    w_end = jnp.exp(c_last)
    for sl in range(n_slab):
        wend_ref[sl] = w_end[:, :, sl * GW:(sl + 1) * GW]

    hpl = GW // RWKV_HEAD
    lane_head = lax.broadcasted_iota(I32, (1, 1, GW), 2) // RWKV_HEAD
    head_masks = [lane_head == h for h in range(hpl)]
    rr = lax.broadcasted_iota(I32, (CHUNK, GW), 0)
    cc = lax.broadcasted_iota(I32, (CHUNK, GW), 1) % RWKV_HEAD
    incl4, strict4, eye4 = rr >= cc, rr > cc, (rr == cc).astype(F32)

    def block_diag(x):
        x = x.astype(BF16)
        return jnp.concatenate([jnp.where(m, x, jnp.zeros_like(x)) for m in head_masks], axis=1)

    def head_blocks(full):
        out = jnp.where(head_masks[0], full[:, :RWKV_HEAD], 0.0)
        for h in range(1, hpl):
            out = out + jnp.where(head_masks[h], full[:, h * RWKV_HEAD:(h + 1) * RWKV_HEAD], 0.0)
        return out

    bmm = lambda x, y: jnp.einsum('gts,gsd->gtd', x.astype(BF16), y.astype(BF16), preferred_element_type=F32)
    bmm_nt = lambda x, y: jnp.einsum('gtd,gsd->gts', x.astype(BF16), y.astype(BF16), preferred_element_type=F32)
    bmm_tn = lambda x, y: jnp.einsum('gtk,gtd->gkd', x.astype(BF16), y.astype(BF16), preferred_element_type=F32)

    def slab_body(sl, carry):
        at, rt, bt, kt, bh, kh, vv = [ops_ref[i, sl] for i in range(7)]
        gm = bmm_nt(jnp.concatenate([at, rt], axis=1),
                    jnp.concatenate([block_diag(bt), block_diag(kt)], axis=1))
        a_ab = jnp.where(strict4, gm[:, :CHUNK, :GW], 0.0)
        a_ak = jnp.where(strict4, gm[:, :CHUNK, GW:], 0.0)
        a_rb = jnp.where(incl4, gm[:, CHUNK:, :GW], 0.0)
        a_rk = jnp.where(incl4, gm[:, CHUNK:, GW:], 0.0)
        akv = bmm(jnp.concatenate([a_ak, a_rk], axis=1), block_diag(vv))
        xk = eye4 + a_ab
        pk = bmm(a_ab, block_diag(a_ab))
        for _ in range(4):
            both = bmm(jnp.concatenate([xk, pk], axis=1), block_diag(pk))
            xk = xk + both[:, :CHUNK]
            pk = both[:, CHUNK:]
        xk = xk + bmm(xk, block_diag(pk))
        pq = bmm(xk, jnp.concatenate([block_diag(at), block_diag(akv[:, :CHUNK])], axis=2))
        p_bd, q_bd = block_diag(pq[:, :, :GW]), block_diag(pq[:, :, GW:])
        ry = bmm(a_rb, jnp.concatenate([p_bd, q_bd], axis=2)) + jnp.concatenate(
            [rt.astype(F32), akv[:, CHUNK:]], axis=2)
        m_sbs = head_blocks(bmm_tn(bh, pq[:, :, :GW])) + eye4 * wend_ref[sl]
        n_sbs = head_blocks(bmm_tn(jnp.concatenate([bh, kh], axis=1),
                                   jnp.concatenate([pq[:, :, GW:].astype(BF16), vv], axis=1)))
        rm_ref[sl] = jnp.concatenate([ry[:, :, :GW], m_sbs], axis=1).astype(BF16)
        yn_ref[sl] = jnp.concatenate([ry[:, :, GW:], n_sbs], axis=1)
        return carry

    lax.fori_loop(0, n_slab, slab_body, 0)

    def chunk_body(ci, carry):
        for sl in range(n_slab):
            lanes = slice(sl * GW, (sl + 1) * GW)
            st = s_ref[:, lanes]
            st_bd = jnp.concatenate([jnp.where(m[0], st, 0.0) for m in head_masks], axis=0).astype(BF16)
            res = jnp.dot(rm_ref[sl, ci], st_bd, preferred_element_type=F32) + yn_ref[sl, ci]
            y_ref[ci, :, lanes] = res[:CHUNK]
            s_ref[:, lanes] = res[CHUNK:]
        return carry

    lax.fori_loop(0, ng, chunk_body, 0)

    y = y_ref[...].reshape(rows_n, W)
    mean = _mm_split2_rhs_ones(y, seg) * (1.0 / RWKV_HEAD)
    yc = y - mean
    var = _mm_split2_rhs_ones(yc * yc, seg) * (1.0 / RWKV_HEAD)
    yn = yc * lax.rsqrt(var + RWKV_LN_EPS)
    o_ref[0] = (yn * lng_ref[...] + lnb_ref[...] + bonus_ref[...]) * gate_ref[...]


def _rwkv(pr, mu, w0, w2, a0, a2, g2, k_k, k_a, r_k, ln_g, ln_b, seg):
    bn, lp, _ = pr.shape
    nc = lp // CHUNK
    ng = max(d for d in range(1, RWKV_GROUP + 1) if nc % d == 0)
    rows = ng * CHUNK
    vec = lambda n: pl.BlockSpec((1, n), lambda b, c: (0, 0))
    mat = lambda m, n: pl.BlockSpec((m, n), lambda b, c: (0, 0))
    hd = RWKV_HEAD
    n_slab = RWKV_WIDTH // RWKV_SLAB
    return pl.pallas_call(
        _rwkv_kernel,
        name="rwkv_mixer",
        grid=(bn, nc // ng),
        in_specs=[
            pl.BlockSpec((1, rows, RWKV_COLS), lambda b, c: (b, c, 0)),
            pl.BlockSpec((1, 8, RWKV_COLS), lambda b, c: (b, jnp.maximum(c * (rows // 8) - 1, 0), 0)),
            vec(RWKV_COLS), vec(RWKV_WIDTH), mat(RWKV_W_RANK, RWKV_WIDTH), vec(RWKV_WIDTH),
            mat(RWKV_A_RANK, RWKV_WIDTH), mat(RWKV_G_RANK, RWKV_WIDTH), vec(RWKV_WIDTH), vec(RWKV_WIDTH),
            vec(RWKV_WIDTH), vec(RWKV_WIDTH), vec(RWKV_WIDTH), mat(RWKV_WIDTH, RWKV_WIDTH),
        ],
        out_specs=pl.BlockSpec((1, rows, RWKV_WIDTH), lambda b, c: (b, c, 0)),
        out_shape=jax.ShapeDtypeStruct((bn, lp, RWKV_WIDTH), F32),
        scratch_shapes=[
            pltpu.VMEM((hd, RWKV_WIDTH), F32),
            pltpu.VMEM((7, n_slab, ng, CHUNK, RWKV_SLAB), BF16),
            pltpu.VMEM((n_slab, ng, 1, RWKV_SLAB), F32),
            pltpu.VMEM((n_slab, ng, 2 * CHUNK, RWKV_SLAB), BF16),
            pltpu.VMEM((n_slab, ng, 2 * CHUNK, RWKV_SLAB), F32),
            pltpu.VMEM((ng, CHUNK, RWKV_WIDTH), F32),
            pltpu.VMEM((rows, RWKV_WIDTH), F32),
            pltpu.VMEM((rows, RWKV_WIDTH), F32),
        ],
        compiler_params=pltpu.CompilerParams(dimension_semantics=("arbitrary", "arbitrary"),
                                             vmem_limit_bytes=VMEM_LIMIT),
    )(pr, pr, mu, w0, w2, a0, a2, g2, k_k, k_a, r_k, ln_g, ln_b, seg.astype(BF16))


def _outproj_router_kernel(yg_ref, yr_ref, h_ref, wo_ref, g_ref, b_ref, rwt_ref, rb_ref, tri_ref,
                           h1_ref, idx_ref, gate_ref, rank_ref, cnt_ref, base_ref):
    i = pl.program_id(0)

    @pl.when(i == 0)
    def _():
        base_ref[...] = jnp.zeros_like(base_ref)

    wo = wo_ref[...]
    mix = _mm(yg_ref[...], wo[:GLA_WIDTH]) + _mm(yr_ref[...], wo[GLA_WIDTH:])
    h1 = _layer_norm(DEEPNORM_ALPHA * h_ref[...] + mix, g_ref[...], b_ref[...])
    _store_row_tiles(h1_ref, h1)
    h_hi = h1.astype(BF16)
    h_lo = (h1 - h_hi.astype(F32)).astype(BF16)
    nt = (((1,), (1,)), ((), ()))
    part = lax.dot_general(rwt_ref[...], h_hi, nt, preferred_element_type=F32)
    work = (part[:N_EXPERTS] + part[N_EXPERTS:]
            + lax.dot_general(rwt_ref[:N_EXPERTS, :], h_lo, nt, preferred_element_type=F32)
            + rb_ref[...][:, 0:1])
    tm = work.shape[1]
    e_iota = lax.broadcasted_iota(I32, (N_EXPERTS, tm), 0)
    base = base_ref[...][:, 0:1]
    vals, onehots = [], []
    for kk in range(TOP_K):
        m = jnp.max(work, axis=0, keepdims=True)
        sel = jnp.min(jnp.where(work == m, e_iota, N_EXPERTS), axis=0, keepdims=True)
        onehot = e_iota == sel
        work = jnp.where(onehot, -jnp.inf, work)
        vals.append(m)
        onehots.append(onehot.astype(F32))
        idx_ref[kk:kk + 1, :] = sel
    cnt_all = jnp.dot(jnp.concatenate(onehots, axis=0).astype(BF16), tri_ref[...], preferred_element_type=F32)
    prior = jnp.zeros((N_EXPERTS, 1), F32)
    for kk in range(TOP_K):
        cnt = cnt_all[kk * N_EXPERTS:(kk + 1) * N_EXPERTS]
        rank = jnp.sum(onehots[kk] * (base + prior + cnt - 1.0), axis=0, keepdims=True)
        prior = prior + cnt[:, tm - 1:tm]
        rank_ref[kk:kk + 1, :] = rank.astype(I32)
    es = [jnp.exp(vv - vals[0]) for vv in vals]
    den = es[0] + es[1] + es[2] + es[3]
    for kk in range(TOP_K):
        gate_ref[kk:kk + 1, :] = es[kk] / den
    new_base = base + prior
    base_ref[...] = jnp.broadcast_to(new_base, base_ref.shape)
    cnt_ref[...] = jnp.broadcast_to(new_base, cnt_ref.shape)


def _outproj_router(yg, yr, h, wo, g, b, rwt, rb, tri):
    tp = h.shape[0]
    tm = ROUTER_TILE
    const = lambda m, n: pl.BlockSpec((m, n), lambda i: (0, 0))
    return pl.pallas_call(
        _outproj_router_kernel,
        name="outproj_router",
        grid=(tp // tm,),
        in_specs=[
            pl.BlockSpec((tm, GLA_WIDTH), lambda i: (i, 0)),
            pl.BlockSpec((tm, RWKV_WIDTH), lambda i: (i, 0)),
            pl.BlockSpec((tm, D_MODEL), lambda i: (i, 0)),
            const(D_MODEL, D_MODEL), const(1, D_MODEL), const(1, D_MODEL),
            const(2 * N_EXPERTS, D_MODEL), const(N_EXPERTS, 128), const(tm, tm),
        ],
        out_specs=[
            pl.BlockSpec(_tiled_rows(tm), lambda i: (i, 0)),
            pl.BlockSpec((TOP_K, tm), lambda i: (0, i)),
            pl.BlockSpec((TOP_K, tm), lambda i: (0, i)),
            pl.BlockSpec((TOP_K, tm), lambda i: (0, i)),
            pl.BlockSpec((N_EXPERTS, 128), lambda i: (0, 0)),
        ],
        out_shape=[
            jax.ShapeDtypeStruct(_tiled_rows(tp), F32),
            jax.ShapeDtypeStruct((TOP_K, tp), I32),
            jax.ShapeDtypeStruct((TOP_K, tp), F32),
            jax.ShapeDtypeStruct((TOP_K, tp), I32),
            jax.ShapeDtypeStruct((N_EXPERTS, 128), F32),
        ],
        scratch_shapes=[pltpu.VMEM((N_EXPERTS, 128), F32)],
        compiler_params=pltpu.CompilerParams(dimension_semantics=("arbitrary",),
                                             vmem_limit_bytes=VMEM_LIMIT),
    )(yg, yr, h, wo, g, b, rwt, rb, tri)


def _zero_blocks_kernel(last_ref, o_ref):
    o_ref[...] = jnp.zeros_like(o_ref)


def _zero_blocks(last_block, n_slots):
    return pl.pallas_call(
        _zero_blocks_kernel,
        name="moe_zero_blocks",
        grid_spec=pltpu.PrefetchScalarGridSpec(
            num_scalar_prefetch=1,
            grid=(N_EXPERTS,),
            in_specs=[],
            out_specs=pl.BlockSpec(_tiled_rows(MOE_BLOCK), lambda e, last: (last[e], 0)),
        ),
        out_shape=jax.ShapeDtypeStruct(_tiled_rows(n_slots), F32),
        compiler_params=pltpu.CompilerParams(dimension_semantics=("arbitrary",)),
    )(last_block)


def _dispatch_kernel(pos_ref, x_ref, xs_in_ref, xs_ref, sem):
    del xs_in_ref
    tm = pos_ref.shape[0] // TOP_K

    def start(r, carry):
        for kk in range(TOP_K):
            pltpu.make_async_copy(_row_tile(x_ref, r), _row_tile(xs_ref, pos_ref[r * TOP_K + kk]),
                                  sem).start(priority=kk % 2)
        return carry

    lax.fori_loop(0, tm, start, 0, unroll=DMA_UNROLL)
    for kk in range(TOP_K):
        pltpu.make_async_copy(x_ref, xs_ref.at[pl.ds(0, x_ref.shape[0])], sem).wait()


def _dispatch(pos, x, xs):
    tp = pos.shape[0] // TOP_K
    tm = ROUTER_TILE
    return pl.pallas_call(
        _dispatch_kernel,
        name="moe_dispatch",
        grid=(tp // tm,),
        in_specs=[
            pl.BlockSpec((tm * TOP_K,), lambda i: (i,), memory_space=pltpu.SMEM),
            pl.BlockSpec(_tiled_rows(tm), lambda i: (i, 0)),
            pl.BlockSpec(memory_space=pl.ANY),
        ],
        out_specs=pl.BlockSpec(memory_space=pl.ANY),
        out_shape=jax.ShapeDtypeStruct(xs.shape, xs.dtype),
        scratch_shapes=[pltpu.SemaphoreType.DMA(())],
        input_output_aliases={2: 0},
        compiler_params=pltpu.CompilerParams(dimension_semantics=("arbitrary",)),
    )(pos, x, xs)


SPLIT_TILE = 256


def _moe_kernel(be_ref, nu_ref, first_ref, slot_ref, nxt_ref, xs_ref, wu_hbm, wd_hbm, perm_ref, bg_ref,
                bl_ref, bd_ref, ys_ref, wu_buf, wd_buf, wsem, wg_s, wl_s, wd_s):
    i = pl.program_id(0)

    def weight_copies(e, sl):
        return (pltpu.make_async_copy(wu_hbm.at[e], wu_buf.at[sl], wsem.at[0, sl]),
                pltpu.make_async_copy(wd_hbm.at[e], wd_buf.at[sl], wsem.at[1, sl]))

    @pl.when(i < nu_ref[0])
    def _():
        @pl.when(first_ref[i] == 1)
        def _():
            sl = slot_ref[i]

            @pl.when(i == 0)
            def _():
                for c in weight_copies(be_ref[0], sl):
                    c.start()

            for c in weight_copies(be_ref[i], sl):
                c.wait()

            @pl.when(nxt_ref[i] >= 0)
            def _():
                for c in weight_copies(nxt_ref[i], 1 - sl):
                    c.start()

            half = SPLIT_TILE // 2
            for t in range(wu_buf.shape[2] // SPLIT_TILE):
                d = jnp.dot(wu_buf[sl, :, t * SPLIT_TILE:(t + 1) * SPLIT_TILE].astype(BF16), perm_ref[...],
                            preferred_element_type=F32).astype(BF16)
                wg_s[:, t * half:(t + 1) * half] = d[:, :half]
                wl_s[:, t * half:(t + 1) * half] = d[:, half:]
            wd_s[...] = wd_buf[sl].astype(BF16)

        x = _load_row_tiles(xs_ref).astype(BF16)
        x_glu = jnp.dot(x, wg_s[...], preferred_element_type=F32) + bg_ref[0]
        x_lin = jnp.dot(x, wl_s[...], preferred_element_type=F32) + bl_ref[0]
        x_glu = jnp.minimum(x_glu, SWIGLU_LIMIT)
        x_lin = jnp.clip(x_lin, -SWIGLU_LIMIT, SWIGLU_LIMIT)
        act = x_glu * _sigmoid(SWIGLU_ALPHA * x_glu) * (x_lin + 1.0)
        _store_row_tiles(ys_ref, jnp.dot(act.astype(BF16), wd_s[...], preferred_element_type=F32) + bd_ref[0])


def _moe(block_e, n_used, xs, w_up, w_down, bg, bl, bd):
    n_slots = xs.shape[0] // ROW_TILE[0]
    nb = n_slots // MOE_BLOCK
    src = jnp.arange(SPLIT_TILE, dtype=I32)[:, None]
    dst = jnp.arange(SPLIT_TILE, dtype=I32)[None, :]
    half = SPLIT_TILE // 2
    perm = (src == jnp.where(dst < half, 2 * dst, 2 * (dst - half) + 1)).astype(BF16)
    ids = jnp.arange(nb, dtype=I32)
    first = jnp.logical_and(ids < n_used[0], jnp.logical_or(ids == 0, block_e != jnp.roll(block_e, 1)))
    slot = (jnp.cumsum(first.astype(I32)) - 1) % 2
    first_at = jnp.where(first, ids, nb)
    next_first = jnp.concatenate([lax.cummin(first_at[::-1])[::-1][1:], jnp.full((1,), nb, I32)])
    nxt_e = jnp.where(next_first < nb, block_e[jnp.minimum(next_first, nb - 1)], -1).astype(I32)
    blk = lambda i, be, nu, *_: (jnp.maximum(jnp.minimum(i, nu[0] - 1), 0), 0)
    bspec = pl.BlockSpec((1, 1, D_FF), lambda i, be, *_: (be[i], 0, 0))
    return pl.pallas_call(
        _moe_kernel,
        name="moe_experts",
        grid_spec=pltpu.PrefetchScalarGridSpec(
            num_scalar_prefetch=5,
            grid=(nb,),
            in_specs=[
                pl.BlockSpec(_tiled_rows(MOE_BLOCK), blk),
                pl.BlockSpec(memory_space=pl.ANY),
                pl.BlockSpec(memory_space=pl.ANY),
                pl.BlockSpec((SPLIT_TILE, SPLIT_TILE), lambda i, *_: (0, 0)),
                bspec, bspec, bspec,
            ],
            out_specs=pl.BlockSpec(_tiled_rows(MOE_BLOCK), blk),
            scratch_shapes=[
                pltpu.VMEM((2, D_MODEL, 2 * D_FF), F32),
                pltpu.VMEM((2, D_FF, D_MODEL), F32),
                pltpu.SemaphoreType.DMA((2, 2)),
                pltpu.VMEM((D_MODEL, D_FF), BF16),
                pltpu.VMEM((D_MODEL, D_FF), BF16),
                pltpu.VMEM((D_FF, D_MODEL), BF16),
            ],
        ),
        out_shape=jax.ShapeDtypeStruct(_tiled_rows(n_slots), F32),
        compiler_params=pltpu.CompilerParams(dimension_semantics=("arbitrary",),
                                             vmem_limit_bytes=VMEM_LIMIT),
    )(block_e, n_used, first.astype(I32), slot.astype(I32), nxt_e, xs, w_up, w_down, perm, bg, bl, bd)


COMBINE_SLOTS = 3
COMBINE_CHUNKS = 2


def _combine_kernel(*refs, cps):
    first_pos = refs[0:2]
    ahead_pos = refs[2:2 + cps]
    gts = refs[2 + cps:2 + 2 * cps]
    h1s = refs[2 + 2 * cps:2 + 3 * cps]
    g_ref, b_ref, ys_ref, o_ref, buf, sem = refs[2 + 3 * cps:]
    s = pl.program_id(0)
    rows = TOP_K * CHUNK

    def issue_row(p_ref, to_slot, r):
        for kk in range(TOP_K):
            pltpu.make_async_copy(_row_tile(ys_ref, p_ref[r * TOP_K + kk]),
                                  _row_tile(buf, to_slot * rows + kk * CHUNK + r),
                                  sem.at[to_slot]).start(priority=kk % 2)

    def wait_slot(which):
        span = rows * ROW_TILE[0]
        pltpu.make_async_copy(ys_ref.at[pl.ds(0, span)], buf.at[pl.ds(which * span, span)], sem.at[which]).wait()

    @pl.when(s == 0)
    def _():
        def body(r, carry):
            issue_row(first_pos[0], 0, r)
            issue_row(first_pos[1], 1, r)
            return carry

        lax.fori_loop(0, CHUNK, body, 0, unroll=DMA_UNROLL)

    def reduce_chunk(j, cur):
        ahead = (cur + 2) % COMBINE_SLOTS
        wait_slot(cur)
        gt = gts[j][...]
        quarter = CHUNK // TOP_K
        ffn = None
        for kk in range(TOP_K):
            part = _load_row_tiles(buf, cur * rows + kk * CHUNK, CHUNK) * gt[:, kk:kk + 1]
            ffn = part if ffn is None else ffn + part
            for r in range(kk * quarter, (kk + 1) * quarter):
                issue_row(ahead_pos[j], ahead, r)
        o_ref[0, j * CHUNK:(j + 1) * CHUNK, :] = _layer_norm(
            DEEPNORM_ALPHA * _load_row_tiles(h1s[j]) + ffn, g_ref[...], b_ref[...])

    def step(first_slot):
        for j in range(cps):
            reduce_chunk(j, (first_slot + j) % COMBINE_SLOTS)

        @pl.when(s == pl.num_programs(0) - 1)
        def _():
            last = (first_slot + cps - 1) % COMBINE_SLOTS
            wait_slot((last + 1) % COMBINE_SLOTS)
            wait_slot((last + 2) % COMBINE_SLOTS)

    for first_slot in range(COMBINE_SLOTS):
        pl.when((s * cps) % COMBINE_SLOTS == first_slot)(functools.partial(step, first_slot))


def _combine(pos_flat, gates_t, h1, g, b, ys, bn, lp):
    seq = lp - CHUNK
    nc = lp // CHUNK
    ncs = seq // CHUNK
    cps = COMBINE_CHUNKS if ncs % COMBINE_CHUNKS == 0 else 1
    n_chunks = bn * ncs
    chunk_of = lambda m: (m // ncs) * nc + m % ncs + 1
    clamp = lambda m: jnp.minimum(m, n_chunks - 1)
    pos_spec = lambda f: pl.BlockSpec((TOP_K * CHUNK,), lambda s: (chunk_of(f(s)),), memory_space=pltpu.SMEM)
    per_chunk = lambda shape: [pl.BlockSpec(shape, lambda s, j=j: (chunk_of(s * cps + j), 0)) for j in range(cps)]
    return pl.pallas_call(
        functools.partial(_combine_kernel, cps=cps),
        name="moe_combine",
        grid=(n_chunks // cps,),
        in_specs=[pos_spec(lambda s: 0 * s), pos_spec(lambda s: 0 * s + 1)]
        + [pos_spec(lambda s, j=j: clamp(s * cps + j + 2)) for j in range(cps)]
        + per_chunk((CHUNK, TOP_K)) + per_chunk(_tiled_rows(CHUNK))
        + [
            pl.BlockSpec((1, D_MODEL), lambda s: (0, 0)),
            pl.BlockSpec((1, D_MODEL), lambda s: (0, 0)),
            pl.BlockSpec(memory_space=pl.ANY),
        ],
        out_specs=pl.BlockSpec((1, cps * CHUNK, D_MODEL), lambda s: (s // (ncs // cps), s % (ncs // cps), 0)),
        out_shape=jax.ShapeDtypeStruct((bn, seq, D_MODEL), F32),
        scratch_shapes=[pltpu.VMEM(_tiled_rows(COMBINE_SLOTS * TOP_K * CHUNK), F32),
                        pltpu.SemaphoreType.DMA((COMBINE_SLOTS,))],
        compiler_params=pltpu.CompilerParams(dimension_semantics=("arbitrary",)),
    )(*([pos_flat] * (2 + cps) + [gates_t] * cps + [h1] * cps + [g, b, ys]))


def kernel(x, meta, ln_in_g, ln_in_b, w_in, gla_gk_w2, gla_gk_b, gla_norm_g, rwkv_mu, rwkv_w0, rwkv_w2, rwkv_a0, rwkv_a2, rwkv_g2, rwkv_k_k, rwkv_k_a, rwkv_r_k, rwkv_ln_g, rwkv_ln_b, w_out, ln1_g, ln1_b, router_w, router_b, exp_w_up, exp_b_up, exp_w_down, exp_b_down, ln2_g, ln2_b):
    bn, seq, _ = x.shape
    assert seq % CHUNK == 0
    lp = seq + CHUNK
    tp = bn * lp
    assert tp % ROUTER_TILE == 0
    row = lambda t: t.reshape(1, -1).astype(F32)

    hcat = jnp.concatenate([jnp.zeros((bn, N_FRONT, D_MODEL), F32),
                            jnp.broadcast_to(meta[None].astype(F32), (bn, N_META, D_MODEL)), x], axis=1)
    hcat = hcat.reshape(tp, D_MODEL)
    gla_in = 2 * GLA_KEY + 2 * GLA_WIDTH + GLA_GATE_RANK
    w = w_in[0]
    w_cols = jnp.concatenate([w[:, :gla_in], jnp.zeros((D_MODEL, 128 - GLA_GATE_RANK), F32), w[:, gla_in:]],
                             axis=1).astype(BF16)
    h, pg, pr = _ln_inproj(hcat, row(ln_in_g), row(ln_in_b), w_cols, lp)

    w2p = jnp.concatenate([gla_gk_w2[0], jnp.zeros((128 - GLA_GATE_RANK, GLA_KEY), F32)], axis=0)
    y_gla = _gla(pg.reshape(bn, lp, GLA_COLS), w2p, row(gla_gk_b[0]), row(gla_norm_g[0]))

    head_id = jnp.arange(RWKV_WIDTH, dtype=I32) // RWKV_HEAD
    seg = (head_id[:, None] == head_id[None, :]).astype(F32)
    y_rwkv = _rwkv(pr.reshape(bn, lp, RWKV_COLS), row(rwkv_mu[0]), row(rwkv_w0[0]), rwkv_w2[0],
                   row(rwkv_a0[0]), rwkv_a2[0], rwkv_g2[0], row(rwkv_k_k[0]), row(rwkv_k_a[0]),
                   row(rwkv_r_k[0]), row(rwkv_ln_g[0]), row(rwkv_ln_b[0]), seg)

    tri = jnp.triu(jnp.ones((ROUTER_TILE, ROUTER_TILE), F32)).astype(BF16)
    rb = jnp.broadcast_to(router_b[0].reshape(N_EXPERTS, 1), (N_EXPERTS, 128))
    rw_t = router_w[0].T
    rw_hi = rw_t.astype(BF16)
    rwt = jnp.concatenate([rw_hi, (rw_t - rw_hi.astype(F32)).astype(BF16)], axis=0)
    h1, idx, gates, rank, cnt = _outproj_router(
        y_gla.reshape(tp, GLA_WIDTH), y_rwkv.reshape(tp, RWKV_WIDTH), h, w_out[0].astype(BF16),
        row(ln1_g[0]), row(ln1_b[0]), rwt, rb, tri)

    counts = cnt[:, 0].astype(I32)
    padded = (counts + MOE_BLOCK - 1) // MOE_BLOCK * MOE_BLOCK
    ends_p = jnp.cumsum(padded)
    starts_p = ends_p - padded
    e_ids = jnp.arange(N_EXPERTS, dtype=I32)
    start_of = jnp.sum(jnp.where(idx[None] == e_ids[:, None, None], starts_p[:, None, None], 0), axis=0)
    pos = start_of + rank
    nb = tp * TOP_K // MOE_BLOCK + N_EXPERTS
    n_slots = nb * MOE_BLOCK
    block_start = jnp.arange(nb, dtype=I32) * MOE_BLOCK
    block_e = jnp.minimum(jnp.sum((block_start[:, None] >= ends_p[None, :]).astype(I32), axis=1), N_EXPERTS - 1)
    n_used = (ends_p[-1:] // MOE_BLOCK).astype(I32)
    last_block = jnp.maximum(ends_p // MOE_BLOCK - 1, 0).astype(I32)

    pos_flat = pos.T.reshape(tp * TOP_K)
    xs = _dispatch(pos_flat, h1, _zero_blocks(last_block, n_slots))
    bg = exp_b_up[0][:, None, 0::2]
    bl = exp_b_up[0][:, None, 1::2]
    ys = _moe(block_e, n_used, xs, exp_w_up[0], exp_w_down[0], bg, bl, exp_b_down[0][:, None, :])

    return _combine(pos_flat, gates.T, h1, row(ln2_g[0]), row(ln2_b[0]), ys, bn, lp)
```

```python
import functools
import math

import jax
import jax.numpy as jnp
from jax import lax
from jax.experimental import pallas as pl
from jax.experimental.pallas import tpu as pltpu

F32 = jnp.float32
BF16 = jnp.bfloat16
I32 = jnp.int32

D_MODEL = 1024
N_META = 16
CHUNK = 64
N_FRONT = (-N_META) % CHUNK
GLA_HEADS = 4
GLA_DK = 64
GLA_DV = 128
GLA_KEY = GLA_HEADS * GLA_DK
GLA_WIDTH = GLA_HEADS * GLA_DV
GLA_GATE_RANK = 16
GLA_TAU = 16.0
GLA_COLS = 2 * GLA_KEY + 2 * GLA_WIDTH + 128
RWKV_WIDTH = 512
RWKV_HEAD = 64
RWKV_HEADS = RWKV_WIDTH // RWKV_HEAD
RWKV_W_RANK = 64
RWKV_A_RANK = 64
RWKV_G_RANK = 128
RWKV_COLS = 3 * RWKV_WIDTH + RWKV_W_RANK + RWKV_A_RANK + RWKV_G_RANK
N_EXPERTS = 32
TOP_K = 4
D_FF = D_MODEL
SWIGLU_ALPHA = 1.702
SWIGLU_LIMIT = 7.0
MOE_BLOCK = 512
DEPTH = 1
DEEPNORM_ALPHA = (2.0 * DEPTH) ** 0.25
LN_EPS = 1e-5
RWKV_LN_EPS = 64e-5
RMS_EPS = 1e-6

ROUTER_TILE = 768
DMA_UNROLL = 4
VMEM_LIMIT = 56 * 1024 * 1024


def _mm(a, b):
    return jnp.dot(a.astype(BF16), b.astype(BF16), preferred_element_type=F32)


def _layer_norm(x, g, b):
    mu = jnp.mean(x, axis=-1, keepdims=True)
    xc = x - mu
    var = jnp.mean(xc * xc, axis=-1, keepdims=True)
    return xc * lax.rsqrt(var + LN_EPS) * g + b


def _sigmoid(x):
    return 1.0 / (1.0 + jnp.exp(-x))


def _log_sigmoid(x):
    return jnp.minimum(x, 0.0) - jnp.log(1.0 + jnp.exp(-jnp.abs(x)))


ROW_TILE = (8, 128)


def _tiled_rows(n):
    return (n * ROW_TILE[0], ROW_TILE[1])


def _row_tile(ref, i):
    return ref.at[pl.ds(pl.multiple_of(i * ROW_TILE[0], ROW_TILE[0]), ROW_TILE[0])]


def _store_row_tiles(ref, x, row0=0):
    n = x.shape[0]
    for j in range(ROW_TILE[0]):
        ref[pl.ds(row0 * ROW_TILE[0] + j, n, stride=ROW_TILE[0]), :] = x[:, j * ROW_TILE[1]:(j + 1) * ROW_TILE[1]]


def _load_row_tiles(ref, row0=0, n=None):
    n = ref.shape[0] // ROW_TILE[0] if n is None else n
    return jnp.concatenate([ref[pl.ds(row0 * ROW_TILE[0] + j, n, stride=ROW_TILE[0]), :]
                            for j in range(ROW_TILE[0])], axis=1)


def _tri_masks(n):
    r = lax.broadcasted_iota(I32, (n, n), 0)
    c = lax.broadcasted_iota(I32, (n, n), 1)
    return r >= c, r > c, r == c


def _ln_inproj_kernel(x_ref, g_ref, b_ref, w_ref, h_ref, pg_ref, pr_ref, *, tiles_per_seq):
    i = pl.program_id(0)
    y = _layer_norm(x_ref[...], g_ref[...], b_ref[...])
    row = lax.broadcasted_iota(I32, (y.shape[0], 1), 0)
    is_front = jnp.logical_and(i % tiles_per_seq == 0, row < N_FRONT)
    y = jnp.where(is_front, 0.0, y)
    h_ref[...] = y
    p = _mm(y, w_ref[...])
    pg_ref[...] = p[:, :GLA_COLS]
    pr_ref[...] = p[:, GLA_COLS:]


def _ln_inproj(hcat, g, b, w, lp):
    tp = hcat.shape[0]
    tiles_per_seq = 1
    for cand in range(1, lp // 8 + 1):
        if lp % cand == 0 and (lp // cand) % 8 == 0 and lp // cand >= N_FRONT and lp // cand <= 384:
            tiles_per_seq = cand
            break
    tm = lp // tiles_per_seq
    ncols = GLA_COLS + RWKV_COLS
    return pl.pallas_call(
        functools.partial(_ln_inproj_kernel, tiles_per_seq=tiles_per_seq),
        name="ln_inproj",
        grid=(tp // tm,),
        in_specs=[
            pl.BlockSpec((tm, D_MODEL), lambda i: (i, 0)),
            pl.BlockSpec((1, D_MODEL), lambda i: (0, 0)),
            pl.BlockSpec((1, D_MODEL), lambda i: (0, 0)),
            pl.BlockSpec((D_MODEL, ncols), lambda i: (0, 0)),
        ],
        out_specs=[
            pl.BlockSpec((tm, D_MODEL), lambda i: (i, 0)),
            pl.BlockSpec((tm, GLA_COLS), lambda i: (i, 0)),
            pl.BlockSpec((tm, RWKV_COLS), lambda i: (i, 0)),
        ],
        out_shape=[
            jax.ShapeDtypeStruct((tp, D_MODEL), F32),
            jax.ShapeDtypeStruct((tp, GLA_COLS), F32),
            jax.ShapeDtypeStruct((tp, RWKV_COLS), F32),
        ],
        compiler_params=pltpu.CompilerParams(dimension_semantics=("arbitrary",),
                                             vmem_limit_bytes=VMEM_LIMIT),
    )(hcat, g, b, w)


GLA_GROUP = 11


def _gla_kernel(pg_ref, w2_ref, gkb_ref, ng_ref, o_ref, st_ref, qe_ref, oi_ref, kvt_ref, el_ref):
    c = pl.program_id(1)

    @pl.when(c == 0)
    def _():
        st_ref[...] = jnp.zeros_like(st_ref)

    rows_n = pg_ref.shape[1]
    ng = rows_n // CHUNK
    g_off = 2 * GLA_KEY + GLA_WIDTH
    p = pg_ref[0]
    gl3 = p[:, g_off + GLA_WIDTH:]
    gl_hi = gl3.astype(BF16)
    gl_lo = (gl3 - gl_hi.astype(F32)).astype(BF16)
    lane = lax.broadcasted_iota(I32, (1, gl3.shape[1]), 1)
    third = jnp.logical_and(lane >= 2 * GLA_GATE_RANK, lane < 3 * GLA_GATE_RANK)
    gate_pre = jnp.dot(jnp.where(third, gl_lo, gl_hi), w2_ref[...], preferred_element_type=F32)
    lg = _log_sigmoid(gate_pre + gkb_ref[...]) * (1.0 / GLA_TAU)
    row = lax.broadcasted_iota(I32, (rows_n, 1), 0)
    lg = jnp.where(jnp.logical_and(c == 0, row < N_FRONT), 0.0, lg)
    incl, _, _ = _tri_masks(CHUNK)
    tril = incl.astype(BF16)
    bc = jnp.concatenate([_mm_split3(tril, lg[i * CHUNK:(i + 1) * CHUNK]) for i in range(ng)], axis=0)
    g3 = lambda t: t.reshape(ng, CHUNK, t.shape[-1])
    bc = g3(bc)
    b_last = bc[:, CHUNK - 1:CHUNK, :]
    k = g3(p[:, GLA_KEY:2 * GLA_KEY])
    q = g3(p[:, 0:GLA_KEY]) * (GLA_DK ** -0.5)
    qe = q * jnp.exp(bc)
    b_mid = bc[:, CHUNK // 2 - 1:CHUNK // 2, :]
    qm = q * jnp.exp(bc - b_mid)
    km = k * jnp.exp(b_mid - bc)
    kl = k * jnp.exp(b_last - bc)
    e_last = jnp.exp(b_last)
    v = g3(p[:, 2 * GLA_KEY:g_off]).astype(BF16)
    for h in range(GLA_HEADS):
        ks = slice(h * GLA_DK, (h + 1) * GLA_DK)
        vs = slice(h * GLA_DV, (h + 1) * GLA_DV)
        qh = qe[:, :, ks].astype(BF16)
        a = jnp.einsum('gtd,gsd->gts', qm[:, :, ks].astype(BF16), km[:, :, ks].astype(BF16),
                       preferred_element_type=F32)
        a = jnp.where(incl, a, 0.0).astype(BF16)
        oi_ref[:, h] = jnp.einsum('gts,gsv->gtv', a, v[:, :, vs], preferred_element_type=F32)
        kvt_ref[:, h] = jnp.einsum('gtv,gtd->gvd', v[:, :, vs], kl[:, :, ks].astype(BF16),
                                   preferred_element_type=F32)
        qe_ref[:, h] = qh
        el_ref[:, h] = e_last[:, :, ks]

    def chunk_body(ci, carry):
        st = st_ref[...]
        oi_ref[ci] = oi_ref[ci] + jnp.einsum('htd,hvd->htv', qe_ref[ci], st.astype(BF16),
                                             preferred_element_type=F32)
        st_ref[...] = st * el_ref[ci] + kvt_ref[ci]
        return carry

    lax.fori_loop(0, ng, chunk_body, 0)

    for ci in range(ng):
        o = oi_ref[ci]
        o = o * lax.rsqrt(jnp.mean(o * o, axis=-1, keepdims=True) + RMS_EPS) * ng_ref[...]
        o = jnp.concatenate([o[h] for h in range(GLA_HEADS)], axis=1)
        rows = slice(ci * CHUNK, (ci + 1) * CHUNK)
        gate = pg_ref[0, rows, g_off:g_off + GLA_WIDTH]
        o_ref[0, rows, :] = o * (gate * _sigmoid(gate))


def _gla(pg, w2p, gkb, ng_w):
    bn, lp, _ = pg.shape
    nc = lp // CHUNK
    ng = max(d for d in range(1, GLA_GROUP + 1) if nc % d == 0)
    rows = ng * CHUNK
    hh = GLA_HEADS
    return pl.pallas_call(
        _gla_kernel,
        name="gla_mixer",
        grid=(bn, nc // ng),
        in_specs=[
            pl.BlockSpec((1, rows, GLA_COLS), lambda b, c: (b, c, 0)),
            pl.BlockSpec((128, GLA_KEY), lambda b, c: (0, 0)),
            pl.BlockSpec((1, GLA_KEY), lambda b, c: (0, 0)),
            pl.BlockSpec((1, GLA_DV), lambda b, c: (0, 0)),
        ],
        out_specs=pl.BlockSpec((1, rows, GLA_WIDTH), lambda b, c: (b, c, 0)),
        out_shape=jax.ShapeDtypeStruct((bn, lp, GLA_WIDTH), F32),
        scratch_shapes=[
            pltpu.VMEM((hh, GLA_DV, GLA_DK), F32),
            pltpu.VMEM((ng, hh, CHUNK, GLA_DK), BF16),
            pltpu.VMEM((ng, hh, CHUNK, GLA_DV), F32),
            pltpu.VMEM((ng, hh, GLA_DV, GLA_DK), F32),
            pltpu.VMEM((ng, hh, 1, GLA_DK), F32),
        ],
        compiler_params=pltpu.CompilerParams(dimension_semantics=("arbitrary", "arbitrary"),
                                             vmem_limit_bytes=VMEM_LIMIT),
    )(pg, w2p, gkb, ng_w)


RWKV_GROUP = 11
RWKV_SLAB = 256
def _mm_split3(ones_bf16, x):
    hi = x.astype(BF16)
    r1 = x - hi.astype(F32)
    mid = r1.astype(BF16)
    lo = (r1 - mid.astype(F32)).astype(BF16)
    return jnp.dot(jnp.concatenate([ones_bf16] * 3, axis=1), jnp.concatenate([hi, mid, lo], axis=0),
                   preferred_element_type=F32)


def _mm_split2_rhs_ones(x, ones_bf16):
    hi = x.astype(BF16)
    mid = (x - hi.astype(F32)).astype(BF16)
    dot = lambda t: jnp.dot(t, ones_bf16, preferred_element_type=F32)
    return dot(hi) + dot(mid)


def _rwkv_kernel(pr_ref, pv_ref, mu_ref, w0_ref, w2_ref, a0_ref, a2_ref, g2_ref, kk_ref, ka_ref,
                 rk_ref, lng_ref, lnb_ref, seg_ref, o_ref, s_ref, ops_ref, wend_ref, rm_ref, yn_ref,
                 y_ref, bonus_ref, gate_ref):
    c = pl.program_id(1)
    rows_n = pr_ref.shape[1]
    ng = rows_n // CHUNK

    @pl.when(c == 0)
    def _():
        s_ref[...] = jnp.zeros_like(s_ref)

    p = pr_ref[0]
    prev_row = jnp.where(c > 0, pv_ref[0][7:8, :], 0.0)
    rolled = pltpu.roll(p, 1, 0)
    row8 = lax.broadcasted_iota(I32, (8, 1), 0)
    prev = jnp.concatenate([jnp.where(row8 == 0, prev_row, rolled[:8]), rolled[8:]], axis=0)
    p = p + (prev - p) * mu_ref[...]
    W = RWKV_WIDTH
    r = p[:, 0:W]
    k = p[:, W:2 * W]
    v = p[:, 2 * W:3 * W]
    w_low = p[:, 3 * W:3 * W + RWKV_W_RANK]
    a_low = p[:, 3 * W + RWKV_W_RANK:3 * W + RWKV_W_RANK + RWKV_A_RANK]
    g_low = p[:, 3 * W + RWKV_W_RANK + RWKV_A_RANK:]
    wx = w0_ref[...] + _mm(jnp.tanh(w_low), w2_ref[...])
    logd = _sigmoid(wx) * (-math.exp(-0.5))
    a = _sigmoid(a0_ref[...] + _mm(a_low, a2_ref[...]))
    g = _mm(_sigmoid(g_low), g2_ref[...])
    seg = seg_ref[...]
    kk = k * kk_ref[...]
    kk = kk * lax.rsqrt(jnp.maximum(_mm_split2_rhs_ones(kk * kk, seg), 1e-24))
    k = k * (1.0 + (a - 1.0) * ka_ref[...])
    bonus = _mm_split2_rhs_ones(r * k * rk_ref[...], seg) * v

    bonus_ref[...] = bonus
    gate_ref[...] = g

    incl, strict, diag = _tri_masks(CHUNK)
    tril = incl.astype(BF16)
    c_in = jnp.concatenate([_mm_split3(tril, logd[i * CHUNK:(i + 1) * CHUNK]) for i in range(ng)], axis=0)
    g3 = lambda t: t.reshape(ng, CHUNK, W)
    logd, c_in, r, k, v, kk, a = g3(logd), g3(c_in), g3(r), g3(k), g3(v), g3(kk), g3(a)
    c_last = c_in[:, CHUNK - 1:CHUNK, :]
    e_neg = jnp.exp(-c_in)
    e_end = jnp.exp(c_last - c_in)
    kka = kk * a
    per_head = (-kk * jnp.exp(c_in - logd), r * jnp.exp(c_in), kka * e_neg, k * e_neg, kka * e_end,
                k * e_end, v)
    GW = RWKV_SLAB
    n_slab = W // GW
    for i, t in enumerate(per_head):
        t = t.astype(BF16)
        for sl in range(n_slab):
            ops_ref[i, sl] = t[:, :, sl * GW:(sl + 1) * GW]
    w_end = jnp.exp(c_last)
    for sl in range(n_slab):
        wend_ref[sl] = w_end[:, :, sl * GW:(sl + 1) * GW]

    hpl = GW // RWKV_HEAD
    lane_head = lax.broadcasted_iota(I32, (1, 1, GW), 2) // RWKV_HEAD
    head_masks = [lane_head == h for h in range(hpl)]
    rr = lax.broadcasted_iota(I32, (CHUNK, GW), 0)
    cc = lax.broadcasted_iota(I32, (CHUNK, GW), 1) % RWKV_HEAD
    incl4, strict4, eye4 = rr >= cc, rr > cc, (rr == cc).astype(F32)

    def block_diag(x):
        x = x.astype(BF16)
        return jnp.concatenate([jnp.where(m, x, jnp.zeros_like(x)) for m in head_masks], axis=1)

    def head_blocks(full):
        out = jnp.where(head_masks[0], full[:, :RWKV_HEAD], 0.0)
        for h in range(1, hpl):
            out = out + jnp.where(head_masks[h], full[:, h * RWKV_HEAD:(h + 1) * RWKV_HEAD], 0.0)
        return out

    bmm = lambda x, y: jnp.einsum('gts,gsd->gtd', x.astype(BF16), y.astype(BF16), preferred_element_type=F32)
    bmm_nt = lambda x, y: jnp.einsum('gtd,gsd->gts', x.astype(BF16), y.astype(BF16), preferred_element_type=F32)
    bmm_tn = lambda x, y: jnp.einsum('gtk,gtd->gkd', x.astype(BF16), y.astype(BF16), preferred_element_type=F32)

    def slab_body(sl, carry):
        at, rt, bt, kt, bh, kh, vv = [ops_ref[i, sl] for i in range(7)]
        gm = bmm_nt(jnp.concatenate([at, rt], axis=1),
                    jnp.concatenate([block_diag(bt), block_diag(kt)], axis=1))
        a_ab = jnp.where(strict4, gm[:, :CHUNK, :GW], 0.0)
        a_ak = jnp.where(strict4, gm[:, :CHUNK, GW:], 0.0)
        a_rb = jnp.where(incl4, gm[:, CHUNK:, :GW], 0.0)
        a_rk = jnp.where(incl4, gm[:, CHUNK:, GW:], 0.0)
        akv = bmm(jnp.concatenate([a_ak, a_rk], axis=1), block_diag(vv))
        xk = eye4 + a_ab
        pk = bmm(a_ab, block_diag(a_ab))
        for _ in range(4):
            both = bmm(jnp.concatenate([xk, pk], axis=1), block_diag(pk))
            xk = xk + both[:, :CHUNK]
            pk = both[:, CHUNK:]
        xk = xk + bmm(xk, block_diag(pk))
        pq = bmm(xk, jnp.concatenate([block_diag(at), block_diag(akv[:, :CHUNK])], axis=2))
        p_bd, q_bd = block_diag(pq[:, :, :GW]), block_diag(pq[:, :, GW:])
        ry = bmm(a_rb, jnp.concatenate([p_bd, q_bd], axis=2)) + jnp.concatenate(
            [rt.astype(F32), akv[:, CHUNK:]], axis=2)
        m_sbs = head_blocks(bmm_tn(bh, pq[:, :, :GW])) + eye4 * wend_ref[sl]
        n_sbs = head_blocks(bmm_tn(jnp.concatenate([bh, kh], axis=1),
                                   jnp.concatenate([pq[:, :, GW:].astype(BF16), vv], axis=1)))
        rm_ref[sl] = jnp.concatenate([ry[:, :, :GW], m_sbs], axis=1).astype(BF16)
        yn_ref[sl] = jnp.concatenate([ry[:, :, GW:], n_sbs], axis=1)
        return carry

    lax.fori_loop(0, n_slab, slab_body, 0)

    def chunk_body(ci, carry):
        for sl in range(n_slab):
            lanes = slice(sl * GW, (sl + 1) * GW)
            st = s_ref[:, lanes]
            st_bd = jnp.concatenate([jnp.where(m[0], st, 0.0) for m in head_masks], axis=0).astype(BF16)
            res = jnp.dot(rm_ref[sl, ci], st_bd, preferred_element_type=F32) + yn_ref[sl, ci]
            y_ref[ci, :, lanes] = res[:CHUNK]
            s_ref[:, lanes] = res[CHUNK:]
        return carry

    lax.fori_loop(0, ng, chunk_body, 0)

    y = y_ref[...].reshape(rows_n, W)
    mean = _mm_split2_rhs_ones(y, seg) * (1.0 / RWKV_HEAD)
    yc = y - mean
    var = _mm_split2_rhs_ones(yc * yc, seg) * (1.0 / RWKV_HEAD)
    yn = yc * lax.rsqrt(var + RWKV_LN_EPS)
    o_ref[0] = (yn * lng_ref[...] + lnb_ref[...] + bonus_ref[...]) * gate_ref[...]


def _rwkv(pr, mu, w0, w2, a0, a2, g2, k_k, k_a, r_k, ln_g, ln_b, seg):
    bn, lp, _ = pr.shape
    nc = lp // CHUNK
    ng = max(d for d in range(1, RWKV_GROUP + 1) if nc % d == 0)
    rows = ng * CHUNK
    vec = lambda n: pl.BlockSpec((1, n), lambda b, c: (0, 0))
    mat = lambda m, n: pl.BlockSpec((m, n), lambda b, c: (0, 0))
    hd = RWKV_HEAD
    n_slab = RWKV_WIDTH // RWKV_SLAB
    return pl.pallas_call(
        _rwkv_kernel,
        name="rwkv_mixer",
        grid=(bn, nc // ng),
        in_specs=[
            pl.BlockSpec((1, rows, RWKV_COLS), lambda b, c: (b, c, 0)),
            pl.BlockSpec((1, 8, RWKV_COLS), lambda b, c: (b, jnp.maximum(c * (rows // 8) - 1, 0), 0)),
            vec(RWKV_COLS), vec(RWKV_WIDTH), mat(RWKV_W_RANK, RWKV_WIDTH), vec(RWKV_WIDTH),
            mat(RWKV_A_RANK, RWKV_WIDTH), mat(RWKV_G_RANK, RWKV_WIDTH), vec(RWKV_WIDTH), vec(RWKV_WIDTH),
            vec(RWKV_WIDTH), vec(RWKV_WIDTH), vec(RWKV_WIDTH), mat(RWKV_WIDTH, RWKV_WIDTH),
        ],
        out_specs=pl.BlockSpec((1, rows, RWKV_WIDTH), lambda b, c: (b, c, 0)),
        out_shape=jax.ShapeDtypeStruct((bn, lp, RWKV_WIDTH), F32),
        scratch_shapes=[
            pltpu.VMEM((hd, RWKV_WIDTH), F32),
            pltpu.VMEM((7, n_slab, ng, CHUNK, RWKV_SLAB), BF16),
            pltpu.VMEM((n_slab, ng, 1, RWKV_SLAB), F32),
            pltpu.VMEM((n_slab, ng, 2 * CHUNK, RWKV_SLAB), BF16),
            pltpu.VMEM((n_slab, ng, 2 * CHUNK, RWKV_SLAB), F32),
            pltpu.VMEM((ng, CHUNK, RWKV_WIDTH), F32),
            pltpu.VMEM((rows, RWKV_WIDTH), F32),
            pltpu.VMEM((rows, RWKV_WIDTH), F32),
        ],
        compiler_params=pltpu.CompilerParams(dimension_semantics=("arbitrary", "arbitrary"),
                                             vmem_limit_bytes=VMEM_LIMIT),
    )(pr, pr, mu, w0, w2, a0, a2, g2, k_k, k_a, r_k, ln_g, ln_b, seg.astype(BF16))


def _outproj_router_kernel(yg_ref, yr_ref, h_ref, wo_ref, g_ref, b_ref, rwt_ref, rb_ref, tri_ref,
                           h1_ref, idx_ref, gate_ref, rank_ref, cnt_ref, base_ref):
    i = pl.program_id(0)

    @pl.when(i == 0)
    def _():
        base_ref[...] = jnp.zeros_like(base_ref)

    wo = wo_ref[...]
    mix = _mm(yg_ref[...], wo[:GLA_WIDTH]) + _mm(yr_ref[...], wo[GLA_WIDTH:])
    h1 = _layer_norm(DEEPNORM_ALPHA * h_ref[...] + mix, g_ref[...], b_ref[...])
    _store_row_tiles(h1_ref, h1)
    h_hi = h1.astype(BF16)
    h_lo = (h1 - h_hi.astype(F32)).astype(BF16)
    nt = (((1,), (1,)), ((), ()))
    part = lax.dot_general(rwt_ref[...], h_hi, nt, preferred_element_type=F32)
    work = (part[:N_EXPERTS] + part[N_EXPERTS:]
            + lax.dot_general(rwt_ref[:N_EXPERTS, :], h_lo, nt, preferred_element_type=F32)
            + rb_ref[...][:, 0:1])
    tm = work.shape[1]
    e_iota = lax.broadcasted_iota(I32, (N_EXPERTS, tm), 0)
    base = base_ref[...][:, 0:1]
    vals, onehots = [], []
    for kk in range(TOP_K):
        m = jnp.max(work, axis=0, keepdims=True)
        sel = jnp.min(jnp.where(work == m, e_iota, N_EXPERTS), axis=0, keepdims=True)
        onehot = e_iota == sel
        work = jnp.where(onehot, -jnp.inf, work)
        vals.append(m)
        onehots.append(onehot.astype(F32))
        idx_ref[kk:kk + 1, :] = sel
    cnt_all = jnp.dot(jnp.concatenate(onehots, axis=0).astype(BF16), tri_ref[...], preferred_element_type=F32)
    prior = jnp.zeros((N_EXPERTS, 1), F32)
    for kk in range(TOP_K):
        cnt = cnt_all[kk * N_EXPERTS:(kk + 1) * N_EXPERTS]
        rank = jnp.sum(onehots[kk] * (base + prior + cnt - 1.0), axis=0, keepdims=True)
        prior = prior + cnt[:, tm - 1:tm]
        rank_ref[kk:kk + 1, :] = rank.astype(I32)
    es = [jnp.exp(vv - vals[0]) for vv in vals]
    den = es[0] + es[1] + es[2] + es[3]
    for kk in range(TOP_K):
        gate_ref[kk:kk + 1, :] = es[kk] / den
    new_base = base + prior
    base_ref[...] = jnp.broadcast_to(new_base, base_ref.shape)
    cnt_ref[...] = jnp.broadcast_to(new_base, cnt_ref.shape)


def _outproj_router(yg, yr, h, wo, g, b, rwt, rb, tri):
    tp = h.shape[0]
    tm = ROUTER_TILE
    const = lambda m, n: pl.BlockSpec((m, n), lambda i: (0, 0))
    return pl.pallas_call(
        _outproj_router_kernel,
        name="outproj_router",
        grid=(tp // tm,),
        in_specs=[
            pl.BlockSpec((tm, GLA_WIDTH), lambda i: (i, 0)),
            pl.BlockSpec((tm, RWKV_WIDTH), lambda i: (i, 0)),
            pl.BlockSpec((tm, D_MODEL), lambda i: (i, 0)),
            const(D_MODEL, D_MODEL), const(1, D_MODEL), const(1, D_MODEL),
            const(2 * N_EXPERTS, D_MODEL), const(N_EXPERTS, 128), const(tm, tm),
        ],
        out_specs=[
            pl.BlockSpec(_tiled_rows(tm), lambda i: (i, 0)),
            pl.BlockSpec((TOP_K, tm), lambda i: (0, i)),
            pl.BlockSpec((TOP_K, tm), lambda i: (0, i)),
            pl.BlockSpec((TOP_K, tm), lambda i: (0, i)),
            pl.BlockSpec((N_EXPERTS, 128), lambda i: (0, 0)),
        ],
        out_shape=[
            jax.ShapeDtypeStruct(_tiled_rows(tp), F32),
            jax.ShapeDtypeStruct((TOP_K, tp), I32),
            jax.ShapeDtypeStruct((TOP_K, tp), F32),
            jax.ShapeDtypeStruct((TOP_K, tp), I32),
            jax.ShapeDtypeStruct((N_EXPERTS, 128), F32),
        ],
        scratch_shapes=[pltpu.VMEM((N_EXPERTS, 128), F32)],
        compiler_params=pltpu.CompilerParams(dimension_semantics=("arbitrary",),
                                             vmem_limit_bytes=VMEM_LIMIT),
    )(yg, yr, h, wo, g, b, rwt, rb, tri)


def _zero_blocks_kernel(last_ref, o_ref):
    o_ref[...] = jnp.zeros_like(o_ref)


def _zero_blocks(last_block, n_slots):
    return pl.pallas_call(
        _zero_blocks_kernel,
        name="moe_zero_blocks",
        grid_spec=pltpu.PrefetchScalarGridSpec(
            num_scalar_prefetch=1,
            grid=(N_EXPERTS,),
            in_specs=[],
            out_specs=pl.BlockSpec(_tiled_rows(MOE_BLOCK), lambda e, last: (last[e], 0)),
        ),
        out_shape=jax.ShapeDtypeStruct(_tiled_rows(n_slots), F32),
        compiler_params=pltpu.CompilerParams(dimension_semantics=("arbitrary",)),
    )(last_block)


def _dispatch_kernel(pos_ref, x_ref, xs_in_ref, xs_ref, sem):
    del xs_in_ref
    tm = pos_ref.shape[0] // TOP_K

    def start(r, carry):
        for kk in range(TOP_K):
            pltpu.make_async_copy(_row_tile(x_ref, r), _row_tile(xs_ref, pos_ref[r * TOP_K + kk]),
                                  sem).start(priority=kk % 2)
        return carry

    lax.fori_loop(0, tm, start, 0, unroll=DMA_UNROLL)
    for kk in range(TOP_K):
        pltpu.make_async_copy(x_ref, xs_ref.at[pl.ds(0, x_ref.shape[0])], sem).wait()


def _dispatch(pos, x, xs):
    tp = pos.shape[0] // TOP_K
    tm = ROUTER_TILE
    return pl.pallas_call(
        _dispatch_kernel,
        name="moe_dispatch",
        grid=(tp // tm,),
        in_specs=[
            pl.BlockSpec((tm * TOP_K,), lambda i: (i,), memory_space=pltpu.SMEM),
            pl.BlockSpec(_tiled_rows(tm), lambda i: (i, 0)),
            pl.BlockSpec(memory_space=pl.ANY),
        ],
        out_specs=pl.BlockSpec(memory_space=pl.ANY),
        out_shape=jax.ShapeDtypeStruct(xs.shape, xs.dtype),
        scratch_shapes=[pltpu.SemaphoreType.DMA(())],
        input_output_aliases={2: 0},
        compiler_params=pltpu.CompilerParams(dimension_semantics=("arbitrary",)),
    )(pos, x, xs)


SPLIT_TILE = 256


def _moe_kernel(be_ref, nu_ref, first_ref, slot_ref, nxt_ref, xs_ref, wu_hbm, wd_hbm, perm_ref, bg_ref,
                bl_ref, bd_ref, ys_ref, wu_buf, wd_buf, wsem, wg_s, wl_s, wd_s):
    i = pl.program_id(0)

    def weight_copies(e, sl):
        return (pltpu.make_async_copy(wu_hbm.at[e], wu_buf.at[sl], wsem.at[0, sl]),
                pltpu.make_async_copy(wd_hbm.at[e], wd_buf.at[sl], wsem.at[1, sl]))

    @pl.when(i < nu_ref[0])
    def _():
        @pl.when(first_ref[i] == 1)
        def _():
            sl = slot_ref[i]

            @pl.when(i == 0)
            def _():
                for c in weight_copies(be_ref[0], sl):
                    c.start()

            for c in weight_copies(be_ref[i], sl):
                c.wait()

            @pl.when(nxt_ref[i] >= 0)
            def _():
                for c in weight_copies(nxt_ref[i], 1 - sl):
                    c.start()

            half = SPLIT_TILE // 2
            for t in range(wu_buf.shape[2] // SPLIT_TILE):
                d = jnp.dot(wu_buf[sl, :, t * SPLIT_TILE:(t + 1) * SPLIT_TILE].astype(BF16), perm_ref[...],
                            preferred_element_type=F32).astype(BF16)
                wg_s[:, t * half:(t + 1) * half] = d[:, :half]
                wl_s[:, t * half:(t + 1) * half] = d[:, half:]
            wd_s[...] = wd_buf[sl].astype(BF16)

        x = _load_row_tiles(xs_ref).astype(BF16)
        x_glu = jnp.dot(x, wg_s[...], preferred_element_type=F32) + bg_ref[0]
        x_lin = jnp.dot(x, wl_s[...], preferred_element_type=F32) + bl_ref[0]
        x_glu = jnp.minimum(x_glu, SWIGLU_LIMIT)
        x_lin = jnp.clip(x_lin, -SWIGLU_LIMIT, SWIGLU_LIMIT)
        act = x_glu * _sigmoid(SWIGLU_ALPHA * x_glu) * (x_lin + 1.0)
        _store_row_tiles(ys_ref, jnp.dot(act.astype(BF16), wd_s[...], preferred_element_type=F32) + bd_ref[0])


def _moe(block_e, n_used, xs, w_up, w_down, bg, bl, bd):
    n_slots = xs.shape[0] // ROW_TILE[0]
    nb = n_slots // MOE_BLOCK
    src = jnp.arange(SPLIT_TILE, dtype=I32)[:, None]
    dst = jnp.arange(SPLIT_TILE, dtype=I32)[None, :]
    half = SPLIT_TILE // 2
    perm = (src == jnp.where(dst < half, 2 * dst, 2 * (dst - half) + 1)).astype(BF16)
    ids = jnp.arange(nb, dtype=I32)
    first = jnp.logical_and(ids < n_used[0], jnp.logical_or(ids == 0, block_e != jnp.roll(block_e, 1)))
    slot = (jnp.cumsum(first.astype(I32)) - 1) % 2
    first_at = jnp.where(first, ids, nb)
    next_first = jnp.concatenate([lax.cummin(first_at[::-1])[::-1][1:], jnp.full((1,), nb, I32)])
    nxt_e = jnp.where(next_first < nb, block_e[jnp.minimum(next_first, nb - 1)], -1).astype(I32)
    blk = lambda i, be, nu, *_: (jnp.maximum(jnp.minimum(i, nu[0] - 1), 0), 0)
    bspec = pl.BlockSpec((1, 1, D_FF), lambda i, be, *_: (be[i], 0, 0))
    return pl.pallas_call(
        _moe_kernel,
        name="moe_experts",
        grid_spec=pltpu.PrefetchScalarGridSpec(
            num_scalar_prefetch=5,
            grid=(nb,),
            in_specs=[
                pl.BlockSpec(_tiled_rows(MOE_BLOCK), blk),
                pl.BlockSpec(memory_space=pl.ANY),
                pl.BlockSpec(memory_space=pl.ANY),
                pl.BlockSpec((SPLIT_TILE, SPLIT_TILE), lambda i, *_: (0, 0)),
                bspec, bspec, bspec,
            ],
            out_specs=pl.BlockSpec(_tiled_rows(MOE_BLOCK), blk),
            scratch_shapes=[
                pltpu.VMEM((2, D_MODEL, 2 * D_FF), F32),
                pltpu.VMEM((2, D_FF, D_MODEL), F32),
                pltpu.SemaphoreType.DMA((2, 2)),
                pltpu.VMEM((D_MODEL, D_FF), BF16),
                pltpu.VMEM((D_MODEL, D_FF), BF16),
                pltpu.VMEM((D_FF, D_MODEL), BF16),
            ],
        ),
        out_shape=jax.ShapeDtypeStruct(_tiled_rows(n_slots), F32),
        compiler_params=pltpu.CompilerParams(dimension_semantics=("arbitrary",),
                                             vmem_limit_bytes=VMEM_LIMIT),
    )(block_e, n_used, first.astype(I32), slot.astype(I32), nxt_e, xs, w_up, w_down, perm, bg, bl, bd)


COMBINE_SLOTS = 3
COMBINE_CHUNKS = 4


def _combine_kernel(*refs, cps):
    first_pos = refs[0:2]
    ahead_pos = refs[2:2 + cps]
    gts = refs[2 + cps:2 + 2 * cps]
    h1s = refs[2 + 2 * cps:2 + 3 * cps]
    g_ref, b_ref, ys_ref, o_ref, buf, sem = refs[2 + 3 * cps:]
    s = pl.program_id(0)
    rows = TOP_K * CHUNK

    def issue_row(p_ref, to_slot, r):
        for kk in range(TOP_K):
            pltpu.make_async_copy(_row_tile(ys_ref, p_ref[r * TOP_K + kk]),
                                  _row_tile(buf, to_slot * rows + kk * CHUNK + r),
                                  sem.at[to_slot]).start(priority=kk % 2)

    def wait_slot(which):
        span = rows * ROW_TILE[0]
        pltpu.make_async_copy(ys_ref.at[pl.ds(0, span)], buf.at[pl.ds(which * span, span)], sem.at[which]).wait()

    @pl.when(s == 0)
    def _():
        def body(r, carry):
            issue_row(first_pos[0], 0, r)
            issue_row(first_pos[1], 1, r)
            return carry

        lax.fori_loop(0, CHUNK, body, 0, unroll=DMA_UNROLL)

    def reduce_chunk(j, cur):
        ahead = (cur + 2) % COMBINE_SLOTS
        wait_slot(cur)
        gt = gts[j][...]
        quarter = CHUNK // TOP_K
        ffn = None
        for kk in range(TOP_K):
            part = _load_row_tiles(buf, cur * rows + kk * CHUNK, CHUNK) * gt[:, kk:kk + 1]
            ffn = part if ffn is None else ffn + part
            for r in range(kk * quarter, (kk + 1) * quarter):
                issue_row(ahead_pos[j], ahead, r)
        o_ref[0, j * CHUNK:(j + 1) * CHUNK, :] = _layer_norm(
            DEEPNORM_ALPHA * _load_row_tiles(h1s[j]) + ffn, g_ref[...], b_ref[...])

    def step(first_slot):
        for j in range(cps):
            reduce_chunk(j, (first_slot + j) % COMBINE_SLOTS)

        @pl.when(s == pl.num_programs(0) - 1)
        def _():
            last = (first_slot + cps - 1) % COMBINE_SLOTS
            wait_slot((last + 1) % COMBINE_SLOTS)
            wait_slot((last + 2) % COMBINE_SLOTS)

    for first_slot in range(COMBINE_SLOTS):
        pl.when((s * cps) % COMBINE_SLOTS == first_slot)(functools.partial(step, first_slot))


def _combine(pos_flat, gates_t, h1, g, b, ys, bn, lp):
    seq = lp - CHUNK
    nc = lp // CHUNK
    ncs = seq // CHUNK
    cps = COMBINE_CHUNKS if ncs % COMBINE_CHUNKS == 0 else 1
    n_chunks = bn * ncs
    chunk_of = lambda m: (m // ncs) * nc + m % ncs + 1
    clamp = lambda m: jnp.minimum(m, n_chunks - 1)
    pos_spec = lambda f: pl.BlockSpec((TOP_K * CHUNK,), lambda s: (chunk_of(f(s)),), memory_space=pltpu.SMEM)
    per_chunk = lambda shape: [pl.BlockSpec(shape, lambda s, j=j: (chunk_of(s * cps + j), 0)) for j in range(cps)]
    return pl.pallas_call(
        functools.partial(_combine_kernel, cps=cps),
        name="moe_combine",
        grid=(n_chunks // cps,),
        in_specs=[pos_spec(lambda s: 0 * s), pos_spec(lambda s: 0 * s + 1)]
        + [pos_spec(lambda s, j=j: clamp(s * cps + j + 2)) for j in range(cps)]
        + per_chunk((CHUNK, TOP_K)) + per_chunk(_tiled_rows(CHUNK))
        + [
            pl.BlockSpec((1, D_MODEL), lambda s: (0, 0)),
            pl.BlockSpec((1, D_MODEL), lambda s: (0, 0)),
            pl.BlockSpec(memory_space=pl.ANY),
        ],
        out_specs=pl.BlockSpec((1, cps * CHUNK, D_MODEL), lambda s: (s // (ncs // cps), s % (ncs // cps), 0)),
        out_shape=jax.ShapeDtypeStruct((bn, seq, D_MODEL), F32),
        scratch_shapes=[pltpu.VMEM(_tiled_rows(COMBINE_SLOTS * TOP_K * CHUNK), F32),
                        pltpu.SemaphoreType.DMA((COMBINE_SLOTS,))],
        compiler_params=pltpu.CompilerParams(dimension_semantics=("arbitrary",)),
    )(*([pos_flat] * (2 + cps) + [gates_t] * cps + [h1] * cps + [g, b, ys]))


def kernel(x, meta, ln_in_g, ln_in_b, w_in, gla_gk_w2, gla_gk_b, gla_norm_g, rwkv_mu, rwkv_w0, rwkv_w2, rwkv_a0, rwkv_a2, rwkv_g2, rwkv_k_k, rwkv_k_a, rwkv_r_k, rwkv_ln_g, rwkv_ln_b, w_out, ln1_g, ln1_b, router_w, router_b, exp_w_up, exp_b_up, exp_w_down, exp_b_down, ln2_g, ln2_b):
    bn, seq, _ = x.shape
    assert seq % CHUNK == 0
    lp = seq + CHUNK
    tp = bn * lp
    assert tp % ROUTER_TILE == 0
    row = lambda t: t.reshape(1, -1).astype(F32)

    hcat = jnp.concatenate([jnp.zeros((bn, N_FRONT, D_MODEL), F32),
                            jnp.broadcast_to(meta[None].astype(F32), (bn, N_META, D_MODEL)), x], axis=1)
    hcat = hcat.reshape(tp, D_MODEL)
    gla_in = 2 * GLA_KEY + 2 * GLA_WIDTH + GLA_GATE_RANK
    w = w_in[0]
    w_gk = w[:, gla_in - GLA_GATE_RANK:gla_in]
    w_cols = jnp.concatenate([w[:, :gla_in], w_gk, w_gk, jnp.zeros((D_MODEL, 128 - 3 * GLA_GATE_RANK), F32),
                              w[:, gla_in:]], axis=1).astype(BF16)
    h, pg, pr = _ln_inproj(hcat, row(ln_in_g), row(ln_in_b), w_cols, lp)

    w2_hi = gla_gk_w2[0].astype(BF16)
    w2_lo = (gla_gk_w2[0] - w2_hi.astype(F32)).astype(BF16)
    w2p = jnp.concatenate([w2_hi, w2_lo, w2_hi, jnp.zeros((128 - 3 * GLA_GATE_RANK, GLA_KEY), BF16)], axis=0)
    y_gla = _gla(pg.reshape(bn, lp, GLA_COLS), w2p, row(gla_gk_b[0]), row(gla_norm_g[0]))

    head_id = jnp.arange(RWKV_WIDTH, dtype=I32) // RWKV_HEAD
    seg = (head_id[:, None] == head_id[None, :]).astype(F32)
    y_rwkv = _rwkv(pr.reshape(bn, lp, RWKV_COLS), row(rwkv_mu[0]), row(rwkv_w0[0]), rwkv_w2[0],
                   row(rwkv_a0[0]), rwkv_a2[0], rwkv_g2[0], row(rwkv_k_k[0]), row(rwkv_k_a[0]),
                   row(rwkv_r_k[0]), row(rwkv_ln_g[0]), row(rwkv_ln_b[0]), seg)

    tri = jnp.triu(jnp.ones((ROUTER_TILE, ROUTER_TILE), F32)).astype(BF16)
    rb = jnp.broadcast_to(router_b[0].reshape(N_EXPERTS, 1), (N_EXPERTS, 128))
    rw_t = router_w[0].T
    rw_hi = rw_t.astype(BF16)
    rwt = jnp.concatenate([rw_hi, (rw_t - rw_hi.astype(F32)).astype(BF16)], axis=0)
    h1, idx, gates, rank, cnt = _outproj_router(
        y_gla.reshape(tp, GLA_WIDTH), y_rwkv.reshape(tp, RWKV_WIDTH), h, w_out[0].astype(BF16),
        row(ln1_g[0]), row(ln1_b[0]), rwt, rb, tri)

    counts = cnt[:, 0].astype(I32)
    padded = (counts + MOE_BLOCK - 1) // MOE_BLOCK * MOE_BLOCK
    ends_p = jnp.cumsum(padded)
    starts_p = ends_p - padded
    e_ids = jnp.arange(N_EXPERTS, dtype=I32)
    start_of = jnp.sum(jnp.where(idx[None] == e_ids[:, None, None], starts_p[:, None, None], 0), axis=0)
    pos = start_of + rank
    nb = tp * TOP_K // MOE_BLOCK + N_EXPERTS
    n_slots = nb * MOE_BLOCK
    block_start = jnp.arange(nb, dtype=I32) * MOE_BLOCK
    block_e = jnp.minimum(jnp.sum((block_start[:, None] >= ends_p[None, :]).astype(I32), axis=1), N_EXPERTS - 1)
    n_used = (ends_p[-1:] // MOE_BLOCK).astype(I32)
    last_block = jnp.maximum(ends_p // MOE_BLOCK - 1, 0).astype(I32)

    pos_flat = pos.T.reshape(tp * TOP_K)
    xs = _dispatch(pos_flat, h1, _zero_blocks(last_block, n_slots))
    bg = exp_b_up[0][:, None, 0::2]
    bl = exp_b_up[0][:, None, 1::2]
    ys = _moe(block_e, n_used, xs, exp_w_up[0], exp_w_down[0], bg, bl, exp_b_down[0][:, None, :])

    return _combine(pos_flat, gates.T, h1, row(ln2_g[0]), row(ln2_b[0]), ys, bn, lp)
```

```python
import functools
import math

import jax
import jax.numpy as jnp
from jax import lax
from jax.experimental import pallas as pl
from jax.experimental.pallas import tpu as pltpu

F32 = jnp.float32
BF16 = jnp.bfloat16
I32 = jnp.int32

D_MODEL = 1024
N_META = 16
CHUNK = 64
N_FRONT = (-N_META) % CHUNK
GLA_HEADS = 4
GLA_DK = 64
GLA_DV = 128
GLA_KEY = GLA_HEADS * GLA_DK
GLA_WIDTH = GLA_HEADS * GLA_DV
GLA_GATE_RANK = 16
GLA_TAU = 16.0
GLA_COLS = 2 * GLA_KEY + 2 * GLA_WIDTH + 128
RWKV_WIDTH = 512
RWKV_HEAD = 64
RWKV_HEADS = RWKV_WIDTH // RWKV_HEAD
RWKV_W_RANK = 64
RWKV_A_RANK = 64
RWKV_G_RANK = 128
RWKV_COLS = 3 * RWKV_WIDTH + RWKV_W_RANK + RWKV_A_RANK + RWKV_G_RANK
N_EXPERTS = 32
TOP_K = 4
D_FF = D_MODEL
SWIGLU_ALPHA = 1.702
SWIGLU_LIMIT = 7.0
MOE_BLOCK = 512
DEPTH = 1
DEEPNORM_ALPHA = (2.0 * DEPTH) ** 0.25
LN_EPS = 1e-5
RWKV_LN_EPS = 64e-5
RMS_EPS = 1e-6

ROUTER_TILE = 768
DMA_UNROLL = 4
LN_INPROJ_ROWS = 704
VMEM_LIMIT = 56 * 1024 * 1024


def _mm(a, b):
    return jnp.dot(a.astype(BF16), b.astype(BF16), preferred_element_type=F32)


def _layer_norm(x, g, b):
    mu = jnp.mean(x, axis=-1, keepdims=True)
    xc = x - mu
    var = jnp.mean(xc * xc, axis=-1, keepdims=True)
    return xc * lax.rsqrt(var + LN_EPS) * g + b


def _sigmoid(x):
    return 1.0 / (1.0 + jnp.exp(-x))


def _log_sigmoid(x):
    return jnp.minimum(x, 0.0) - jnp.log(1.0 + jnp.exp(-jnp.abs(x)))


ROW_TILE = (8, 128)


def _tiled_rows(n):
    return (n * ROW_TILE[0], ROW_TILE[1])


def _row_tile(ref, i):
    return ref.at[pl.ds(pl.multiple_of(i * ROW_TILE[0], ROW_TILE[0]), ROW_TILE[0])]


def _store_row_tiles(ref, x, row0=0):
    n = x.shape[0]
    for j in range(ROW_TILE[0]):
        ref[pl.ds(row0 * ROW_TILE[0] + j, n, stride=ROW_TILE[0]), :] = x[:, j * ROW_TILE[1]:(j + 1) * ROW_TILE[1]]


def _load_row_tiles(ref, row0=0, n=None):
    n = ref.shape[0] // ROW_TILE[0] if n is None else n
    return jnp.concatenate([ref[pl.ds(row0 * ROW_TILE[0] + j, n, stride=ROW_TILE[0]), :]
                            for j in range(ROW_TILE[0])], axis=1)


def _tri_masks(n):
    r = lax.broadcasted_iota(I32, (n, n), 0)
    c = lax.broadcasted_iota(I32, (n, n), 1)
    return r >= c, r > c, r == c


def _ln_inproj_kernel(x_ref, g_ref, b_ref, w_ref, h_ref, pg_ref, pr_ref, *, tiles_per_seq):
    i = pl.program_id(0)
    y = _layer_norm(x_ref[...], g_ref[...], b_ref[...])
    row = lax.broadcasted_iota(I32, (y.shape[0], 1), 0)
    is_front = jnp.logical_and(i % tiles_per_seq == 0, row < N_FRONT)
    y = jnp.where(is_front, 0.0, y)
    h_ref[...] = y
    p = _mm(y, w_ref[...])
    pg_ref[...] = p[:, :GLA_COLS]
    pr_ref[...] = p[:, GLA_COLS:]


def _ln_inproj(hcat, g, b, w, lp):
    tp = hcat.shape[0]
    tiles_per_seq = 1
    for cand in range(1, lp // 8 + 1):
        if lp % cand == 0 and (lp // cand) % 8 == 0 and lp // cand >= N_FRONT and lp // cand <= LN_INPROJ_ROWS:
            tiles_per_seq = cand
            break
    tm = lp // tiles_per_seq
    ncols = GLA_COLS + RWKV_COLS
    return pl.pallas_call(
        functools.partial(_ln_inproj_kernel, tiles_per_seq=tiles_per_seq),
        name="ln_inproj",
        grid=(tp // tm,),
        in_specs=[
            pl.BlockSpec((tm, D_MODEL), lambda i: (i, 0)),
            pl.BlockSpec((1, D_MODEL), lambda i: (0, 0)),
            pl.BlockSpec((1, D_MODEL), lambda i: (0, 0)),
            pl.BlockSpec((D_MODEL, ncols), lambda i: (0, 0)),
        ],
        out_specs=[
            pl.BlockSpec((tm, D_MODEL), lambda i: (i, 0)),
            pl.BlockSpec((tm, GLA_COLS), lambda i: (i, 0)),
            pl.BlockSpec((tm, RWKV_COLS), lambda i: (i, 0)),
        ],
        out_shape=[
            jax.ShapeDtypeStruct((tp, D_MODEL), F32),
            jax.ShapeDtypeStruct((tp, GLA_COLS), F32),
            jax.ShapeDtypeStruct((tp, RWKV_COLS), F32),
        ],
        compiler_params=pltpu.CompilerParams(dimension_semantics=("arbitrary",),
                                             vmem_limit_bytes=VMEM_LIMIT),
    )(hcat, g, b, w)


GLA_GROUP = 11
GLA_SUB = 16
GLA_EXP_CAP = 80.0


def _gla_kernel(pg_ref, w2_ref, gkb_ref, ng_ref, o_ref, st_ref, qe_ref, oi_ref, kvt_ref, el_ref):
    c = pl.program_id(1)

    @pl.when(c == 0)
    def _():
        st_ref[...] = jnp.zeros_like(st_ref)

    rows_n = pg_ref.shape[1]
    ng = rows_n // CHUNK
    g_off = 2 * GLA_KEY + GLA_WIDTH
    p = pg_ref[0]
    gl3 = p[:, g_off + GLA_WIDTH:]
    gl_hi = gl3.astype(BF16)
    gl_lo = (gl3 - gl_hi.astype(F32)).astype(BF16)
    lane = lax.broadcasted_iota(I32, (1, gl3.shape[1]), 1)
    third = jnp.logical_and(lane >= 2 * GLA_GATE_RANK, lane < 3 * GLA_GATE_RANK)
    gate_pre = jnp.dot(jnp.where(third, gl_lo, gl_hi), w2_ref[...], preferred_element_type=F32)
    lg = _log_sigmoid(gate_pre + gkb_ref[...]) * (1.0 / GLA_TAU)
    row = lax.broadcasted_iota(I32, (rows_n, 1), 0)
    lg = jnp.where(jnp.logical_and(c == 0, row < N_FRONT), 0.0, lg)
    incl, _, _ = _tri_masks(CHUNK)
    tril = incl.astype(BF16)
    bc = jnp.concatenate([_mm_split3(tril, lg[i * CHUNK:(i + 1) * CHUNK]) for i in range(ng)], axis=0)
    g3 = lambda t: t.reshape(ng, CHUNK, t.shape[-1])
    bc = g3(bc)
    b_last = bc[:, CHUNK - 1:CHUNK, :]
    k = g3(p[:, GLA_KEY:2 * GLA_KEY])
    q = g3(p[:, 0:GLA_KEY]) * (GLA_DK ** -0.5)
    qe = q * jnp.exp(bc)
    n_sub = CHUNK // GLA_SUB
    sub_refs = [jnp.zeros_like(b_last)] + [bc[:, i * GLA_SUB - 1:i * GLA_SUB, :] for i in range(1, n_sub)]
    q_ref = jnp.concatenate([jnp.broadcast_to(r, (ng, GLA_SUB, GLA_KEY)) for r in sub_refs], axis=1)
    qs = (q * jnp.exp(bc - q_ref)).astype(BF16)
    ks_sub = []
    for i, r in enumerate(sub_refs):
        n_rows = (i + 1) * GLA_SUB
        scaled = (k[:, :n_rows] * jnp.exp(jnp.minimum(r - bc[:, :n_rows], GLA_EXP_CAP))).astype(BF16)
        if n_rows < CHUNK:
            scaled = jnp.concatenate([scaled, jnp.zeros((ng, CHUNK - n_rows, GLA_KEY), BF16)], axis=1)
        ks_sub.append(scaled)
    kl = k * jnp.exp(b_last - bc)
    e_last = jnp.exp(b_last)
    v = g3(p[:, 2 * GLA_KEY:g_off]).astype(BF16)
    for h in range(GLA_HEADS):
        ks = slice(h * GLA_DK, (h + 1) * GLA_DK)
        vs = slice(h * GLA_DV, (h + 1) * GLA_DV)
        qh = qe[:, :, ks].astype(BF16)
        a = jnp.concatenate(
            [jnp.einsum('gtd,gsd->gts', qs[:, i * GLA_SUB:(i + 1) * GLA_SUB, ks], ks_sub[i][:, :, ks],
                        preferred_element_type=F32) for i in range(n_sub)], axis=1)
        a = jnp.where(incl, a, 0.0).astype(BF16)
        oi_ref[:, h] = jnp.einsum('gts,gsv->gtv', a, v[:, :, vs], preferred_element_type=F32)
        kvt_ref[:, h] = jnp.einsum('gtv,gtd->gvd', v[:, :, vs], kl[:, :, ks].astype(BF16),
                                   preferred_element_type=F32)
        qe_ref[:, h] = qh
        el_ref[:, h] = e_last[:, :, ks]

    def chunk_body(ci, carry):
        st = st_ref[...]
        oi_ref[ci] = oi_ref[ci] + jnp.einsum('htd,hvd->htv', qe_ref[ci], st.astype(BF16),
                                             preferred_element_type=F32)
        st_ref[...] = st * el_ref[ci] + kvt_ref[ci]
        return carry

    lax.fori_loop(0, ng, chunk_body, 0)

    for ci in range(ng):
        o = oi_ref[ci]
        o = o * lax.rsqrt(jnp.mean(o * o, axis=-1, keepdims=True) + RMS_EPS) * ng_ref[...]
        o = jnp.concatenate([o[h] for h in range(GLA_HEADS)], axis=1)
        rows = slice(ci * CHUNK, (ci + 1) * CHUNK)
        gate = pg_ref[0, rows, g_off:g_off + GLA_WIDTH]
        o_ref[0, rows, :] = o * (gate * _sigmoid(gate))


def _gla(pg, w2p, gkb, ng_w):
    bn, lp, _ = pg.shape
    nc = lp // CHUNK
    ng = max(d for d in range(1, GLA_GROUP + 1) if nc % d == 0)
    rows = ng * CHUNK
    hh = GLA_HEADS
    return pl.pallas_call(
        _gla_kernel,
        name="gla_mixer",
        grid=(bn, nc // ng),
        in_specs=[
            pl.BlockSpec((1, rows, GLA_COLS), lambda b, c: (b, c, 0)),
            pl.BlockSpec((128, GLA_KEY), lambda b, c: (0, 0)),
            pl.BlockSpec((1, GLA_KEY), lambda b, c: (0, 0)),
            pl.BlockSpec((1, GLA_DV), lambda b, c: (0, 0)),
        ],
        out_specs=pl.BlockSpec((1, rows, GLA_WIDTH), lambda b, c: (b, c, 0)),
        out_shape=jax.ShapeDtypeStruct((bn, lp, GLA_WIDTH), F32),
        scratch_shapes=[
            pltpu.VMEM((hh, GLA_DV, GLA_DK), F32),
            pltpu.VMEM((ng, hh, CHUNK, GLA_DK), BF16),
            pltpu.VMEM((ng, hh, CHUNK, GLA_DV), F32),
            pltpu.VMEM((ng, hh, GLA_DV, GLA_DK), F32),
            pltpu.VMEM((ng, hh, 1, GLA_DK), F32),
        ],
        compiler_params=pltpu.CompilerParams(dimension_semantics=("arbitrary", "arbitrary"),
                                             vmem_limit_bytes=VMEM_LIMIT),
    )(pg, w2p, gkb, ng_w)


RWKV_GROUP = 11
RWKV_SLAB = 256
def _mm_split3(ones_bf16, x):
    hi = x.astype(BF16)
    r1 = x - hi.astype(F32)
    mid = r1.astype(BF16)
    lo = (r1 - mid.astype(F32)).astype(BF16)
    return jnp.dot(jnp.concatenate([ones_bf16] * 3, axis=1), jnp.concatenate([hi, mid, lo], axis=0),
                   preferred_element_type=F32)


def _mm_split2_rhs_ones(x, ones_bf16):
    hi = x.astype(BF16)
    mid = (x - hi.astype(F32)).astype(BF16)
    dot = lambda t: jnp.dot(t, ones_bf16, preferred_element_type=F32)
    return dot(hi) + dot(mid)


def _rwkv_kernel(pr_ref, pv_ref, mu_ref, w0_ref, w2_ref, a0_ref, a2_ref, g2_ref, kk_ref, ka_ref,
                 rk_ref, lng_ref, lnb_ref, seg_ref, o_ref, s_ref, ops_ref, wend_ref, rm_ref, yn_ref,
                 y_ref, bonus_ref, gate_ref):
    c = pl.program_id(1)
    rows_n = pr_ref.shape[1]
    ng = rows_n // CHUNK

    @pl.when(c == 0)
    def _():
        s_ref[...] = jnp.zeros_like(s_ref)

    p = pr_ref[0]
    prev_row = jnp.where(c > 0, pv_ref[0][7:8, :], 0.0)
    rolled = pltpu.roll(p, 1, 0)
    row8 = lax.broadcasted_iota(I32, (8, 1), 0)
    prev = jnp.concatenate([jnp.where(row8 == 0, prev_row, rolled[:8]), rolled[8:]], axis=0)
    p = p + (prev - p) * mu_ref[...]
    W = RWKV_WIDTH
    r = p[:, 0:W]
    k = p[:, W:2 * W]
    v = p[:, 2 * W:3 * W]
    w_low = p[:, 3 * W:3 * W + RWKV_W_RANK]
    a_low = p[:, 3 * W + RWKV_W_RANK:3 * W + RWKV_W_RANK + RWKV_A_RANK]
    g_low = p[:, 3 * W + RWKV_W_RANK + RWKV_A_RANK:]
    wx = w0_ref[...] + _mm(jnp.tanh(w_low), w2_ref[...])
    logd = _sigmoid(wx) * (-math.exp(-0.5))
    a = _sigmoid(a0_ref[...] + _mm(a_low, a2_ref[...]))
    g = _mm(_sigmoid(g_low), g2_ref[...])
    seg = seg_ref[...]
    kk = k * kk_ref[...]
    kk = kk * lax.rsqrt(jnp.maximum(_mm_split2_rhs_ones(kk * kk, seg), 1e-24))
    k = k * (1.0 + (a - 1.0) * ka_ref[...])
    bonus = _mm_split2_rhs_ones(r * k * rk_ref[...], seg) * v

    bonus_ref[...] = bonus
    gate_ref[...] = g

    incl, strict, diag = _tri_masks(CHUNK)
    tril = incl.astype(BF16)
    c_in = jnp.concatenate([_mm_split3(tril, logd[i * CHUNK:(i + 1) * CHUNK]) for i in range(ng)], axis=0)
    g3 = lambda t: t.reshape(ng, CHUNK, W)
    logd, c_in, r, k, v, kk, a = g3(logd), g3(c_in), g3(r), g3(k), g3(v), g3(kk), g3(a)
    c_last = c_in[:, CHUNK - 1:CHUNK, :]
    e_neg = jnp.exp(-c_in)
    e_end = jnp.exp(c_last - c_in)
    kka = kk * a
    per_head = (-kk * jnp.exp(c_in - logd), r * jnp.exp(c_in), kka * e_neg, k * e_neg, kka * e_end,
                k * e_end, v)
    GW = RWKV_SLAB
    n_slab = W // GW
    for i, t in enumerate(per_head):
        t = t.astype(BF16)
        for sl in range(n_slab):
            ops_ref[i, sl] = t[:, :, sl * GW:(sl + 1) * GW]
    w_end = jnp.exp(c_last)
    for sl in range(n_slab):
        wend_ref[sl] = w_end[:, :, sl * GW:(sl + 1) * GW]

    hpl = GW // RWKV_HEAD
    lane_head = lax.broadcasted_iota(I32, (1, 1, GW), 2) // RWKV_HEAD
    head_masks = [lane_head == h for h in range(hpl)]
    rr = lax.broadcasted_iota(I32, (CHUNK, GW), 0)
    cc = lax.broadcasted_iota(I32, (CHUNK, GW), 1) % RWKV_HEAD
    incl4, strict4, eye4 = rr >= cc, rr > cc, (rr == cc).astype(F32)

    def block_diag(x):
        x = x.astype(BF16)
        return jnp.concatenate([jnp.where(m, x, jnp.zeros_like(x)) for m in head_masks], axis=1)

    def head_blocks(full):
        out = jnp.where(head_masks[0], full[:, :RWKV_HEAD], 0.0)
        for h in range(1, hpl):
            out = out + jnp.where(head_masks[h], full[:, h * RWKV_HEAD:(h + 1) * RWKV_HEAD], 0.0)
        return out

    bmm = lambda x, y: jnp.einsum('gts,gsd->gtd', x.astype(BF16), y.astype(BF16), preferred_element_type=F32)
    bmm_nt = lambda x, y: jnp.einsum('gtd,gsd->gts', x.astype(BF16), y.astype(BF16), preferred_element_type=F32)
    bmm_tn = lambda x, y: jnp.einsum('gtk,gtd->gkd', x.astype(BF16), y.astype(BF16), preferred_element_type=F32)

    def slab_body(sl, carry):
        at, rt, bt, kt, bh, kh, vv = [ops_ref[i, sl] for i in range(7)]
        gm = bmm_nt(jnp.concatenate([at, rt], axis=1),
                    jnp.concatenate([block_diag(bt), block_diag(kt)], axis=1))
        a_ab = jnp.where(strict4, gm[:, :CHUNK, :GW], 0.0)
        a_ak = jnp.where(strict4, gm[:, :CHUNK, GW:], 0.0)
        a_rb = jnp.where(incl4, gm[:, CHUNK:, :GW], 0.0)
        a_rk = jnp.where(incl4, gm[:, CHUNK:, GW:], 0.0)
        akv = bmm(jnp.concatenate([a_ak, a_rk], axis=1), block_diag(vv))
        xk = eye4 + a_ab
        pk = bmm(a_ab, block_diag(a_ab))
        for _ in range(4):
            both = bmm(jnp.concatenate([xk, pk], axis=1), block_diag(pk))
            xk = xk + both[:, :CHUNK]
            pk = both[:, CHUNK:]
        xk = xk + bmm(xk, block_diag(pk))
        pq = bmm(xk, jnp.concatenate([block_diag(at), block_diag(akv[:, :CHUNK])], axis=2))
        p_bd, q_bd = block_diag(pq[:, :, :GW]), block_diag(pq[:, :, GW:])
        ry = bmm(a_rb, jnp.concatenate([p_bd, q_bd], axis=2)) + jnp.concatenate(
            [rt.astype(F32), akv[:, CHUNK:]], axis=2)
        m_sbs = head_blocks(bmm_tn(bh, pq[:, :, :GW])) + eye4 * wend_ref[sl]
        n_sbs = head_blocks(bmm_tn(jnp.concatenate([bh, kh], axis=1),
                                   jnp.concatenate([pq[:, :, GW:].astype(BF16), vv], axis=1)))
        rm_ref[sl] = jnp.concatenate([ry[:, :, :GW], m_sbs], axis=1).astype(BF16)
        yn_ref[sl] = jnp.concatenate([ry[:, :, GW:], n_sbs], axis=1)
        return carry

    lax.fori_loop(0, n_slab, slab_body, 0)

    def chunk_body(ci, carry):
        for sl in range(n_slab):
            lanes = slice(sl * GW, (sl + 1) * GW)
            st = s_ref[:, lanes]
            st_bd = jnp.concatenate([jnp.where(m[0], st, 0.0) for m in head_masks], axis=0).astype(BF16)
            res = jnp.dot(rm_ref[sl, ci], st_bd, preferred_element_type=F32) + yn_ref[sl, ci]
            y_ref[ci, :, lanes] = res[:CHUNK]
            s_ref[:, lanes] = res[CHUNK:]
        return carry

    lax.fori_loop(0, ng, chunk_body, 0)

    y = y_ref[...].reshape(rows_n, W)
    mean = _mm_split2_rhs_ones(y, seg) * (1.0 / RWKV_HEAD)
    yc = y - mean
    var = _mm_split2_rhs_ones(yc * yc, seg) * (1.0 / RWKV_HEAD)
    yn = yc * lax.rsqrt(var + RWKV_LN_EPS)
    o_ref[0] = (yn * lng_ref[...] + lnb_ref[...] + bonus_ref[...]) * gate_ref[...]


def _rwkv(pr, mu, w0, w2, a0, a2, g2, k_k, k_a, r_k, ln_g, ln_b, seg):
    bn, lp, _ = pr.shape
    nc = lp // CHUNK
    ng = max(d for d in range(1, RWKV_GROUP + 1) if nc % d == 0)
    rows = ng * CHUNK
    vec = lambda n: pl.BlockSpec((1, n), lambda b, c: (0, 0))
    mat = lambda m, n: pl.BlockSpec((m, n), lambda b, c: (0, 0))
    hd = RWKV_HEAD
    n_slab = RWKV_WIDTH // RWKV_SLAB
    return pl.pallas_call(
        _rwkv_kernel,
        name="rwkv_mixer",
        grid=(bn, nc // ng),
        in_specs=[
            pl.BlockSpec((1, rows, RWKV_COLS), lambda b, c: (b, c, 0)),
            pl.BlockSpec((1, 8, RWKV_COLS), lambda b, c: (b, jnp.maximum(c * (rows // 8) - 1, 0), 0)),
            vec(RWKV_COLS), vec(RWKV_WIDTH), mat(RWKV_W_RANK, RWKV_WIDTH), vec(RWKV_WIDTH),
            mat(RWKV_A_RANK, RWKV_WIDTH), mat(RWKV_G_RANK, RWKV_WIDTH), vec(RWKV_WIDTH), vec(RWKV_WIDTH),
            vec(RWKV_WIDTH), vec(RWKV_WIDTH), vec(RWKV_WIDTH), mat(RWKV_WIDTH, RWKV_WIDTH),
        ],
        out_specs=pl.BlockSpec((1, rows, RWKV_WIDTH), lambda b, c: (b, c, 0)),
        out_shape=jax.ShapeDtypeStruct((bn, lp, RWKV_WIDTH), F32),
        scratch_shapes=[
            pltpu.VMEM((hd, RWKV_WIDTH), F32),
            pltpu.VMEM((7, n_slab, ng, CHUNK, RWKV_SLAB), BF16),
            pltpu.VMEM((n_slab, ng, 1, RWKV_SLAB), F32),
            pltpu.VMEM((n_slab, ng, 2 * CHUNK, RWKV_SLAB), BF16),
            pltpu.VMEM((n_slab, ng, 2 * CHUNK, RWKV_SLAB), F32),
            pltpu.VMEM((ng, CHUNK, RWKV_WIDTH), F32),
            pltpu.VMEM((rows, RWKV_WIDTH), F32),
            pltpu.VMEM((rows, RWKV_WIDTH), F32),
        ],
        compiler_params=pltpu.CompilerParams(dimension_semantics=("arbitrary", "arbitrary"),
                                             vmem_limit_bytes=VMEM_LIMIT),
    )(pr, pr, mu, w0, w2, a0, a2, g2, k_k, k_a, r_k, ln_g, ln_b, seg.astype(BF16))


def _outproj_router_kernel(yg_ref, yr_ref, h_ref, wo_ref, g_ref, b_ref, rwt_ref, rb_ref, tri_ref,
                           h1_ref, idx_ref, gate_ref, rank_ref, cnt_ref, base_ref):
    i = pl.program_id(0)

    @pl.when(i == 0)
    def _():
        base_ref[...] = jnp.zeros_like(base_ref)

    wo = wo_ref[...]
    mix = _mm(yg_ref[...], wo[:GLA_WIDTH]) + _mm(yr_ref[...], wo[GLA_WIDTH:])
    h1 = _layer_norm(DEEPNORM_ALPHA * h_ref[...] + mix, g_ref[...], b_ref[...])
    _store_row_tiles(h1_ref, h1)
    h_hi = h1.astype(BF16)
    h_lo = (h1 - h_hi.astype(F32)).astype(BF16)
    nt = (((1,), (1,)), ((), ()))
    part = lax.dot_general(rwt_ref[...], h_hi, nt, preferred_element_type=F32)
    work = (part[:N_EXPERTS] + part[N_EXPERTS:]
            + lax.dot_general(rwt_ref[:N_EXPERTS, :], h_lo, nt, preferred_element_type=F32)
            + rb_ref[...][:, 0:1])
    tm = work.shape[1]
    e_iota = lax.broadcasted_iota(I32, (N_EXPERTS, tm), 0)
    base = base_ref[...][:, 0:1]
    vals, onehots = [], []
    for kk in range(TOP_K):
        m = jnp.max(work, axis=0, keepdims=True)
        sel = jnp.min(jnp.where(work == m, e_iota, N_EXPERTS), axis=0, keepdims=True)
        onehot = e_iota == sel
        work = jnp.where(onehot, -jnp.inf, work)
        vals.append(m)
        onehots.append(onehot.astype(F32))
        idx_ref[kk:kk + 1, :] = sel
    cnt_all = jnp.dot(jnp.concatenate(onehots, axis=0).astype(BF16), tri_ref[...], preferred_element_type=F32)
    prior = jnp.zeros((N_EXPERTS, 1), F32)
    for kk in range(TOP_K):
        cnt = cnt_all[kk * N_EXPERTS:(kk + 1) * N_EXPERTS]
        rank = jnp.sum(onehots[kk] * (base + prior + cnt - 1.0), axis=0, keepdims=True)
        prior = prior + cnt[:, tm - 1:tm]
        rank_ref[kk:kk + 1, :] = rank.astype(I32)
    es = [jnp.exp(vv - vals[0]) for vv in vals]
    den = es[0] + es[1] + es[2] + es[3]
    for kk in range(TOP_K):
        gate_ref[kk:kk + 1, :] = es[kk] / den
    new_base = base + prior
    base_ref[...] = jnp.broadcast_to(new_base, base_ref.shape)
    cnt_ref[...] = jnp.broadcast_to(new_base, cnt_ref.shape)


def _outproj_router(yg, yr, h, wo, g, b, rwt, rb, tri):
    tp = h.shape[0]
    tm = ROUTER_TILE
    const = lambda m, n: pl.BlockSpec((m, n), lambda i: (0, 0))
    return pl.pallas_call(
        _outproj_router_kernel,
        name="outproj_router",
        grid=(tp // tm,),
        in_specs=[
            pl.BlockSpec((tm, GLA_WIDTH), lambda i: (i, 0)),
            pl.BlockSpec((tm, RWKV_WIDTH), lambda i: (i, 0)),
            pl.BlockSpec((tm, D_MODEL), lambda i: (i, 0)),
            const(D_MODEL, D_MODEL), const(1, D_MODEL), const(1, D_MODEL),
            const(2 * N_EXPERTS, D_MODEL), const(N_EXPERTS, 128), const(tm, tm),
        ],
        out_specs=[
            pl.BlockSpec(_tiled_rows(tm), lambda i: (i, 0)),
            pl.BlockSpec((TOP_K, tm), lambda i: (0, i)),
            pl.BlockSpec((TOP_K, tm), lambda i: (0, i)),
            pl.BlockSpec((TOP_K, tm), lambda i: (0, i)),
            pl.BlockSpec((N_EXPERTS, 128), lambda i: (0, 0)),
        ],
        out_shape=[
            jax.ShapeDtypeStruct(_tiled_rows(tp), F32),
            jax.ShapeDtypeStruct((TOP_K, tp), I32),
            jax.ShapeDtypeStruct((TOP_K, tp), F32),
            jax.ShapeDtypeStruct((TOP_K, tp), I32),
            jax.ShapeDtypeStruct((N_EXPERTS, 128), F32),
        ],
        scratch_shapes=[pltpu.VMEM((N_EXPERTS, 128), F32)],
        compiler_params=pltpu.CompilerParams(dimension_semantics=("arbitrary",),
                                             vmem_limit_bytes=VMEM_LIMIT),
    )(yg, yr, h, wo, g, b, rwt, rb, tri)


def _zero_blocks_kernel(last_ref, o_ref):
    o_ref[...] = jnp.zeros_like(o_ref)


def _zero_blocks(last_block, n_slots):
    return pl.pallas_call(
        _zero_blocks_kernel,
        name="moe_zero_blocks",
        grid_spec=pltpu.PrefetchScalarGridSpec(
            num_scalar_prefetch=1,
            grid=(N_EXPERTS,),
            in_specs=[],
            out_specs=pl.BlockSpec(_tiled_rows(MOE_BLOCK), lambda e, last: (last[e], 0)),
        ),
        out_shape=jax.ShapeDtypeStruct(_tiled_rows(n_slots), F32),
        compiler_params=pltpu.CompilerParams(dimension_semantics=("arbitrary",)),
    )(last_block)


def _dispatch_kernel(pos_ref, x_ref, xs_in_ref, xs_ref, sem):
    del xs_in_ref
    tm = pos_ref.shape[0] // TOP_K

    def start(r, carry):
        for kk in range(TOP_K):
            pltpu.make_async_copy(_row_tile(x_ref, r), _row_tile(xs_ref, pos_ref[r * TOP_K + kk]),
                                  sem).start(priority=kk % 2)
        return carry

    lax.fori_loop(0, tm, start, 0, unroll=DMA_UNROLL)
    for kk in range(TOP_K):
        pltpu.make_async_copy(x_ref, xs_ref.at[pl.ds(0, x_ref.shape[0])], sem).wait()


def _dispatch(pos, x, xs):
    tp = pos.shape[0] // TOP_K
    tm = ROUTER_TILE
    return pl.pallas_call(
        _dispatch_kernel,
        name="moe_dispatch",
        grid=(tp // tm,),
        in_specs=[
            pl.BlockSpec((tm * TOP_K,), lambda i: (i,), memory_space=pltpu.SMEM),
            pl.BlockSpec(_tiled_rows(tm), lambda i: (i, 0)),
            pl.BlockSpec(memory_space=pl.ANY),
        ],
        out_specs=pl.BlockSpec(memory_space=pl.ANY),
        out_shape=jax.ShapeDtypeStruct(xs.shape, xs.dtype),
        scratch_shapes=[pltpu.SemaphoreType.DMA(())],
        input_output_aliases={2: 0},
        compiler_params=pltpu.CompilerParams(dimension_semantics=("arbitrary",)),
    )(pos, x, xs)


SPLIT_TILE = 256


def _moe_kernel(be_ref, nu_ref, first_ref, slot_ref, nxt_ref, xs_ref, wu_hbm, wd_hbm, perm_ref, bg_ref,
                bl_ref, bd_ref, ys_ref, wu_buf, wd_buf, wsem, wg_s, wl_s, wd_s):
    i = pl.program_id(0)

    def weight_copies(e, sl):
        return (pltpu.make_async_copy(wu_hbm.at[e], wu_buf.at[sl], wsem.at[0, sl]),
                pltpu.make_async_copy(wd_hbm.at[e], wd_buf.at[sl], wsem.at[1, sl]))

    @pl.when(i < nu_ref[0])
    def _():
        @pl.when(first_ref[i] == 1)
        def _():
            sl = slot_ref[i]

            @pl.when(i == 0)
            def _():
                for c in weight_copies(be_ref[0], sl):
                    c.start()

            for c in weight_copies(be_ref[i], sl):
                c.wait()

            @pl.when(nxt_ref[i] >= 0)
            def _():
                for c in weight_copies(nxt_ref[i], 1 - sl):
                    c.start()

            half = SPLIT_TILE // 2
            for t in range(wu_buf.shape[2] // SPLIT_TILE):
                d = jnp.dot(wu_buf[sl, :, t * SPLIT_TILE:(t + 1) * SPLIT_TILE].astype(BF16), perm_ref[...],
                            preferred_element_type=F32).astype(BF16)
                wg_s[:, t * half:(t + 1) * half] = d[:, :half]
                wl_s[:, t * half:(t + 1) * half] = d[:, half:]
            wd_s[...] = wd_buf[sl].astype(BF16)

        x = _load_row_tiles(xs_ref).astype(BF16)
        x_glu = jnp.dot(x, wg_s[...], preferred_element_type=F32) + bg_ref[0]
        x_lin = jnp.dot(x, wl_s[...], preferred_element_type=F32) + bl_ref[0]
        x_glu = jnp.minimum(x_glu, SWIGLU_LIMIT)
        x_lin = jnp.clip(x_lin, -SWIGLU_LIMIT, SWIGLU_LIMIT)
        act = x_glu * _sigmoid(SWIGLU_ALPHA * x_glu) * (x_lin + 1.0)
        _store_row_tiles(ys_ref, jnp.dot(act.astype(BF16), wd_s[...], preferred_element_type=F32) + bd_ref[0])


def _moe(block_e, n_used, xs, w_up, w_down, bg, bl, bd):
    n_slots = xs.shape[0] // ROW_TILE[0]
    nb = n_slots // MOE_BLOCK
    src = jnp.arange(SPLIT_TILE, dtype=I32)[:, None]
    dst = jnp.arange(SPLIT_TILE, dtype=I32)[None, :]
    half = SPLIT_TILE // 2
    perm = (src == jnp.where(dst < half, 2 * dst, 2 * (dst - half) + 1)).astype(BF16)
    ids = jnp.arange(nb, dtype=I32)
    first = jnp.logical_and(ids < n_used[0], jnp.logical_or(ids == 0, block_e != jnp.roll(block_e, 1)))
    slot = (jnp.cumsum(first.astype(I32)) - 1) % 2
    first_at = jnp.where(first, ids, nb)
    next_first = jnp.concatenate([lax.cummin(first_at[::-1])[::-1][1:], jnp.full((1,), nb, I32)])
    nxt_e = jnp.where(next_first < nb, block_e[jnp.minimum(next_first, nb - 1)], -1).astype(I32)
    blk = lambda i, be, nu, *_: (jnp.maximum(jnp.minimum(i, nu[0] - 1), 0), 0)
    bspec = pl.BlockSpec((1, 1, D_FF), lambda i, be, *_: (be[i], 0, 0))
    return pl.pallas_call(
        _moe_kernel,
        name="moe_experts",
        grid_spec=pltpu.PrefetchScalarGridSpec(
            num_scalar_prefetch=5,
            grid=(nb,),
            in_specs=[
                pl.BlockSpec(_tiled_rows(MOE_BLOCK), blk),
                pl.BlockSpec(memory_space=pl.ANY),
                pl.BlockSpec(memory_space=pl.ANY),
                pl.BlockSpec((SPLIT_TILE, SPLIT_TILE), lambda i, *_: (0, 0)),
                bspec, bspec, bspec,
            ],
            out_specs=pl.BlockSpec(_tiled_rows(MOE_BLOCK), blk),
            scratch_shapes=[
                pltpu.VMEM((2, D_MODEL, 2 * D_FF), F32),
                pltpu.VMEM((2, D_FF, D_MODEL), F32),
                pltpu.SemaphoreType.DMA((2, 2)),
                pltpu.VMEM((D_MODEL, D_FF), BF16),
                pltpu.VMEM((D_MODEL, D_FF), BF16),
                pltpu.VMEM((D_FF, D_MODEL), BF16),
            ],
        ),
        out_shape=jax.ShapeDtypeStruct(_tiled_rows(n_slots), F32),
        compiler_params=pltpu.CompilerParams(dimension_semantics=("arbitrary",),
                                             vmem_limit_bytes=VMEM_LIMIT),
    )(block_e, n_used, first.astype(I32), slot.astype(I32), nxt_e, xs, w_up, w_down, perm, bg, bl, bd)


COMBINE_SLOTS = 3
COMBINE_CHUNKS = 2


def _combine_kernel(*refs, cps):
    first_pos = refs[0:2]
    ahead_pos = refs[2:2 + cps]
    gts = refs[2 + cps:2 + 2 * cps]
    h1s = refs[2 + 2 * cps:2 + 3 * cps]
    g_ref, b_ref, ys_ref, o_ref, buf, sem = refs[2 + 3 * cps:]
    s = pl.program_id(0)
    rows = TOP_K * CHUNK

    def issue_row(p_ref, to_slot, r):
        for kk in range(TOP_K):
            pltpu.make_async_copy(_row_tile(ys_ref, p_ref[r * TOP_K + kk]),
                                  _row_tile(buf, to_slot * rows + kk * CHUNK + r),
                                  sem.at[to_slot]).start(priority=kk % 2)

    def wait_slot(which):
        span = rows * ROW_TILE[0]
        pltpu.make_async_copy(ys_ref.at[pl.ds(0, span)], buf.at[pl.ds(which * span, span)], sem.at[which]).wait()

    @pl.when(s == 0)
    def _():
        def body(r, carry):
            issue_row(first_pos[0], 0, r)
            issue_row(first_pos[1], 1, r)
            return carry

        lax.fori_loop(0, CHUNK, body, 0, unroll=DMA_UNROLL)

    def reduce_chunk(j, cur):
        ahead = (cur + 2) % COMBINE_SLOTS
        wait_slot(cur)
        gt = gts[j][...]
        quarter = CHUNK // TOP_K
        ffn = None
        for kk in range(TOP_K):
            part = _load_row_tiles(buf, cur * rows + kk * CHUNK, CHUNK) * gt[:, kk:kk + 1]
            ffn = part if ffn is None else ffn + part
            for r in range(kk * quarter, (kk + 1) * quarter):
                issue_row(ahead_pos[j], ahead, r)
        o_ref[0, j * CHUNK:(j + 1) * CHUNK, :] = _layer_norm(
            DEEPNORM_ALPHA * _load_row_tiles(h1s[j]) + ffn, g_ref[...], b_ref[...])

    def step(first_slot):
        for j in range(cps):
            reduce_chunk(j, (first_slot + j) % COMBINE_SLOTS)

        @pl.when(s == pl.num_programs(0) - 1)
        def _():
            last = (first_slot + cps - 1) % COMBINE_SLOTS
            wait_slot((last + 1) % COMBINE_SLOTS)
            wait_slot((last + 2) % COMBINE_SLOTS)

    for first_slot in range(COMBINE_SLOTS):
        pl.when((s * cps) % COMBINE_SLOTS == first_slot)(functools.partial(step, first_slot))


def _combine(pos_flat, gates_t, h1, g, b, ys, bn, lp):
    seq = lp - CHUNK
    nc = lp // CHUNK
    ncs = seq // CHUNK
    cps = COMBINE_CHUNKS if ncs % COMBINE_CHUNKS == 0 else 1
    n_chunks = bn * ncs
    chunk_of = lambda m: (m // ncs) * nc + m % ncs + 1
    clamp = lambda m: jnp.minimum(m, n_chunks - 1)
    pos_spec = lambda f: pl.BlockSpec((TOP_K * CHUNK,), lambda s: (chunk_of(f(s)),), memory_space=pltpu.SMEM)
    per_chunk = lambda shape: [pl.BlockSpec(shape, lambda s, j=j: (chunk_of(s * cps + j), 0)) for j in range(cps)]
    return pl.pallas_call(
        functools.partial(_combine_kernel, cps=cps),
        name="moe_combine",
        grid=(n_chunks // cps,),
        in_specs=[pos_spec(lambda s: 0 * s), pos_spec(lambda s: 0 * s + 1)]
        + [pos_spec(lambda s, j=j: clamp(s * cps + j + 2)) for j in range(cps)]
        + per_chunk((CHUNK, TOP_K)) + per_chunk(_tiled_rows(CHUNK))
        + [
            pl.BlockSpec((1, D_MODEL), lambda s: (0, 0)),
            pl.BlockSpec((1, D_MODEL), lambda s: (0, 0)),
            pl.BlockSpec(memory_space=pl.ANY),
        ],
        out_specs=pl.BlockSpec((1, cps * CHUNK, D_MODEL), lambda s: (s // (ncs // cps), s % (ncs // cps), 0)),
        out_shape=jax.ShapeDtypeStruct((bn, seq, D_MODEL), F32),
        scratch_shapes=[pltpu.VMEM(_tiled_rows(COMBINE_SLOTS * TOP_K * CHUNK), F32),
                        pltpu.SemaphoreType.DMA((COMBINE_SLOTS,))],
        compiler_params=pltpu.CompilerParams(dimension_semantics=("arbitrary",)),
    )(*([pos_flat] * (2 + cps) + [gates_t] * cps + [h1] * cps + [g, b, ys]))


def kernel(x, meta, ln_in_g, ln_in_b, w_in, gla_gk_w2, gla_gk_b, gla_norm_g, rwkv_mu, rwkv_w0, rwkv_w2, rwkv_a0, rwkv_a2, rwkv_g2, rwkv_k_k, rwkv_k_a, rwkv_r_k, rwkv_ln_g, rwkv_ln_b, w_out, ln1_g, ln1_b, router_w, router_b, exp_w_up, exp_b_up, exp_w_down, exp_b_down, ln2_g, ln2_b):
    bn, seq, _ = x.shape
    assert seq % CHUNK == 0
    lp = seq + CHUNK
    tp = bn * lp
    assert tp % ROUTER_TILE == 0
    row = lambda t: t.reshape(1, -1).astype(F32)

    hcat = jnp.concatenate([jnp.zeros((bn, N_FRONT, D_MODEL), F32),
                            jnp.broadcast_to(meta[None].astype(F32), (bn, N_META, D_MODEL)), x], axis=1)
    hcat = hcat.reshape(tp, D_MODEL)
    gla_in = 2 * GLA_KEY + 2 * GLA_WIDTH + GLA_GATE_RANK
    w = w_in[0]
    w_gk = w[:, gla_in - GLA_GATE_RANK:gla_in]
    w_cols = jnp.concatenate([w[:, :gla_in], w_gk, w_gk, jnp.zeros((D_MODEL, 128 - 3 * GLA_GATE_RANK), F32),
                              w[:, gla_in:]], axis=1).astype(BF16)
    h, pg, pr = _ln_inproj(hcat, row(ln_in_g), row(ln_in_b), w_cols, lp)

    w2_hi = gla_gk_w2[0].astype(BF16)
    w2_lo = (gla_gk_w2[0] - w2_hi.astype(F32)).astype(BF16)
    w2p = jnp.concatenate([w2_hi, w2_lo, w2_hi, jnp.zeros((128 - 3 * GLA_GATE_RANK, GLA_KEY), BF16)], axis=0)
    y_gla = _gla(pg.reshape(bn, lp, GLA_COLS), w2p, row(gla_gk_b[0]), row(gla_norm_g[0]))

    head_id = jnp.arange(RWKV_WIDTH, dtype=I32) // RWKV_HEAD
    seg = (head_id[:, None] == head_id[None, :]).astype(F32)
    y_rwkv = _rwkv(pr.reshape(bn, lp, RWKV_COLS), row(rwkv_mu[0]), row(rwkv_w0[0]), rwkv_w2[0],
                   row(rwkv_a0[0]), rwkv_a2[0], rwkv_g2[0], row(rwkv_k_k[0]), row(rwkv_k_a[0]),
                   row(rwkv_r_k[0]), row(rwkv_ln_g[0]), row(rwkv_ln_b[0]), seg)

    tri = jnp.triu(jnp.ones((ROUTER_TILE, ROUTER_TILE), F32)).astype(BF16)
    rb = jnp.broadcast_to(router_b[0].reshape(N_EXPERTS, 1), (N_EXPERTS, 128))
    rw_t = router_w[0].T
    rw_hi = rw_t.astype(BF16)
    rwt = jnp.concatenate([rw_hi, (rw_t - rw_hi.astype(F32)).astype(BF16)], axis=0)
    h1, idx, gates, rank, cnt = _outproj_router(
        y_gla.reshape(tp, GLA_WIDTH), y_rwkv.reshape(tp, RWKV_WIDTH), h, w_out[0].astype(BF16),
        row(ln1_g[0]), row(ln1_b[0]), rwt, rb, tri)

    counts = cnt[:, 0].astype(I32)
    padded = (counts + MOE_BLOCK - 1) // MOE_BLOCK * MOE_BLOCK
    ends_p = jnp.cumsum(padded)
    starts_p = ends_p - padded
    e_ids = jnp.arange(N_EXPERTS, dtype=I32)
    start_of = jnp.sum(jnp.where(idx[None] == e_ids[:, None, None], starts_p[:, None, None], 0), axis=0)
    pos = start_of + rank
    nb = tp * TOP_K // MOE_BLOCK + N_EXPERTS
    n_slots = nb * MOE_BLOCK
    block_start = jnp.arange(nb, dtype=I32) * MOE_BLOCK
    block_e = jnp.minimum(jnp.sum((block_start[:, None] >= ends_p[None, :]).astype(I32), axis=1), N_EXPERTS - 1)
    n_used = (ends_p[-1:] // MOE_BLOCK).astype(I32)
    last_block = jnp.maximum(ends_p // MOE_BLOCK - 1, 0).astype(I32)

    pos_flat = pos.T.reshape(tp * TOP_K)
    xs = _dispatch(pos_flat, h1, _zero_blocks(last_block, n_slots))
    bg = exp_b_up[0][:, None, 0::2]
    bl = exp_b_up[0][:, None, 1::2]
    ys = _moe(block_e, n_used, xs, exp_w_up[0], exp_w_down[0], bg, bl, exp_b_down[0][:, None, :])

    return _combine(pos_flat, gates.T, h1, row(ln2_g[0]), row(ln2_b[0]), ys, bn, lp)
```

```python
import functools
import math

import jax
import jax.numpy as jnp
from jax import lax
from jax.experimental import pallas as pl
from jax.experimental.pallas import tpu as pltpu

F32 = jnp.float32
BF16 = jnp.bfloat16
I32 = jnp.int32

D_MODEL = 1024
N_META = 16
CHUNK = 64
N_FRONT = (-N_META) % CHUNK
GLA_HEADS = 4
GLA_DK = 64
GLA_DV = 128
GLA_KEY = GLA_HEADS * GLA_DK
GLA_WIDTH = GLA_HEADS * GLA_DV
GLA_GATE_RANK = 16
GLA_TAU = 16.0
GLA_COLS = 2 * GLA_KEY + 2 * GLA_WIDTH + 128
RWKV_WIDTH = 512
RWKV_HEAD = 64
RWKV_HEADS = RWKV_WIDTH // RWKV_HEAD
RWKV_W_RANK = 64
RWKV_A_RANK = 64
RWKV_G_RANK = 128
RWKV_COLS = 3 * RWKV_WIDTH + RWKV_W_RANK + RWKV_A_RANK + RWKV_G_RANK
N_EXPERTS = 32
TOP_K = 4
D_FF = D_MODEL
SWIGLU_ALPHA = 1.702
SWIGLU_LIMIT = 7.0
MOE_BLOCK = 512
DEPTH = 1
DEEPNORM_ALPHA = (2.0 * DEPTH) ** 0.25
LN_EPS = 1e-5
RWKV_LN_EPS = 64e-5
RMS_EPS = 1e-6

ROUTER_TILE = 768
DISPATCH_TILE = 1536
DMA_UNROLL = 4
LN_INPROJ_ROWS = 704
VMEM_LIMIT = 56 * 1024 * 1024


def _mm(a, b):
    return jnp.dot(a.astype(BF16), b.astype(BF16), preferred_element_type=F32)


def _layer_norm(x, g, b):
    mu = jnp.mean(x, axis=-1, keepdims=True)
    xc = x - mu
    var = jnp.mean(xc * xc, axis=-1, keepdims=True)
    return xc * lax.rsqrt(var + LN_EPS) * g + b


def _sigmoid(x):
    return 1.0 / (1.0 + jnp.exp(-x))


def _log_sigmoid(x):
    return jnp.minimum(x, 0.0) - jnp.log(1.0 + jnp.exp(-jnp.abs(x)))


ROW_TILE = (8, 128)


def _tiled_rows(n):
    return (n * ROW_TILE[0], ROW_TILE[1])


def _row_tile(ref, i):
    return ref.at[pl.ds(pl.multiple_of(i * ROW_TILE[0], ROW_TILE[0]), ROW_TILE[0])]


def _store_row_tiles(ref, x, row0=0):
    n = x.shape[0]
    for j in range(ROW_TILE[0]):
        ref[pl.ds(row0 * ROW_TILE[0] + j, n, stride=ROW_TILE[0]), :] = x[:, j * ROW_TILE[1]:(j + 1) * ROW_TILE[1]]


def _load_row_tiles(ref, row0=0, n=None):
    n = ref.shape[0] // ROW_TILE[0] if n is None else n
    return jnp.concatenate([ref[pl.ds(row0 * ROW_TILE[0] + j, n, stride=ROW_TILE[0]), :]
                            for j in range(ROW_TILE[0])], axis=1)


def _tri_masks(n):
    r = lax.broadcasted_iota(I32, (n, n), 0)
    c = lax.broadcasted_iota(I32, (n, n), 1)
    return r >= c, r > c, r == c


def _ln_inproj_kernel(x_ref, g_ref, b_ref, w_ref, h_ref, pg_ref, pr_ref, *, tiles_per_seq):
    i = pl.program_id(0)
    y = _layer_norm(x_ref[...], g_ref[...], b_ref[...])
    row = lax.broadcasted_iota(I32, (y.shape[0], 1), 0)
    is_front = jnp.logical_and(i % tiles_per_seq == 0, row < N_FRONT)
    y = jnp.where(is_front, 0.0, y)
    h_ref[...] = y
    p = _mm(y, w_ref[...])
    pg_ref[...] = p[:, :GLA_COLS]
    pr_ref[...] = p[:, GLA_COLS:]


def _ln_inproj(hcat, g, b, w, lp):
    tp = hcat.shape[0]
    tiles_per_seq = 1
    for cand in range(1, lp // 8 + 1):
        if lp % cand == 0 and (lp // cand) % 8 == 0 and lp // cand >= N_FRONT and lp // cand <= LN_INPROJ_ROWS:
            tiles_per_seq = cand
            break
    tm = lp // tiles_per_seq
    ncols = GLA_COLS + RWKV_COLS
    return pl.pallas_call(
        functools.partial(_ln_inproj_kernel, tiles_per_seq=tiles_per_seq),
        name="ln_inproj",
        grid=(tp // tm,),
        in_specs=[
            pl.BlockSpec((tm, D_MODEL), lambda i: (i, 0)),
            pl.BlockSpec((1, D_MODEL), lambda i: (0, 0)),
            pl.BlockSpec((1, D_MODEL), lambda i: (0, 0)),
            pl.BlockSpec((D_MODEL, ncols), lambda i: (0, 0)),
        ],
        out_specs=[
            pl.BlockSpec((tm, D_MODEL), lambda i: (i, 0)),
            pl.BlockSpec((tm, GLA_COLS), lambda i: (i, 0)),
            pl.BlockSpec((tm, RWKV_COLS), lambda i: (i, 0)),
        ],
        out_shape=[
            jax.ShapeDtypeStruct((tp, D_MODEL), F32),
            jax.ShapeDtypeStruct((tp, GLA_COLS), F32),
            jax.ShapeDtypeStruct((tp, RWKV_COLS), F32),
        ],
        compiler_params=pltpu.CompilerParams(dimension_semantics=("arbitrary",),
                                             vmem_limit_bytes=VMEM_LIMIT),
    )(hcat, g, b, w)


GLA_GROUP = 11
GLA_SUB = 16
GLA_EXP_CAP = 80.0


def _gla_kernel(pg_ref, w2_ref, gkb_ref, ng_ref, o_ref, st_ref, qe_ref, oi_ref, kvt_ref, el_ref):
    c = pl.program_id(1)

    @pl.when(c == 0)
    def _():
        st_ref[...] = jnp.zeros_like(st_ref)

    rows_n = pg_ref.shape[1]
    ng = rows_n // CHUNK
    g_off = 2 * GLA_KEY + GLA_WIDTH
    p = pg_ref[0]
    gl3 = p[:, g_off + GLA_WIDTH:]
    gl_hi = gl3.astype(BF16)
    gl_lo = (gl3 - gl_hi.astype(F32)).astype(BF16)
    lane = lax.broadcasted_iota(I32, (1, gl3.shape[1]), 1)
    third = jnp.logical_and(lane >= 2 * GLA_GATE_RANK, lane < 3 * GLA_GATE_RANK)
    gate_pre = jnp.dot(jnp.where(third, gl_lo, gl_hi), w2_ref[...], preferred_element_type=F32)
    lg = _log_sigmoid(gate_pre + gkb_ref[...]) * (1.0 / GLA_TAU)
    row = lax.broadcasted_iota(I32, (rows_n, 1), 0)
    lg = jnp.where(jnp.logical_and(c == 0, row < N_FRONT), 0.0, lg)
    incl, _, _ = _tri_masks(CHUNK)
    tril = incl.astype(BF16)
    bc = jnp.concatenate([_mm_split3(tril, lg[i * CHUNK:(i + 1) * CHUNK]) for i in range(ng)], axis=0)
    g3 = lambda t: t.reshape(ng, CHUNK, t.shape[-1])
    bc = g3(bc)
    b_last = bc[:, CHUNK - 1:CHUNK, :]
    k = g3(p[:, GLA_KEY:2 * GLA_KEY])
    q = g3(p[:, 0:GLA_KEY]) * (GLA_DK ** -0.5)
    qe = q * jnp.exp(bc)
    n_sub = CHUNK // GLA_SUB
    sub_refs = [jnp.zeros_like(b_last)] + [bc[:, i * GLA_SUB - 1:i * GLA_SUB, :] for i in range(1, n_sub)]
    q_ref = jnp.concatenate([jnp.broadcast_to(r, (ng, GLA_SUB, GLA_KEY)) for r in sub_refs], axis=1)
    qs = (q * jnp.exp(bc - q_ref)).astype(BF16)
    ks_sub = []
    for i, r in enumerate(sub_refs):
        n_rows = (i + 1) * GLA_SUB
        scaled = (k[:, :n_rows] * jnp.exp(jnp.minimum(r - bc[:, :n_rows], GLA_EXP_CAP))).astype(BF16)
        if n_rows < CHUNK:
            scaled = jnp.concatenate([scaled, jnp.zeros((ng, CHUNK - n_rows, GLA_KEY), BF16)], axis=1)
        ks_sub.append(scaled)
    kl = k * jnp.exp(b_last - bc)
    e_last = jnp.exp(b_last)
    v = g3(p[:, 2 * GLA_KEY:g_off]).astype(BF16)
    for h in range(GLA_HEADS):
        ks = slice(h * GLA_DK, (h + 1) * GLA_DK)
        vs = slice(h * GLA_DV, (h + 1) * GLA_DV)
        qh = qe[:, :, ks].astype(BF16)
        a = jnp.concatenate(
            [jnp.einsum('gtd,gsd->gts', qs[:, i * GLA_SUB:(i + 1) * GLA_SUB, ks], ks_sub[i][:, :, ks],
                        preferred_element_type=F32) for i in range(n_sub)], axis=1)
        a = jnp.where(incl, a, 0.0).astype(BF16)
        oi_ref[:, h] = jnp.einsum('gts,gsv->gtv', a, v[:, :, vs], preferred_element_type=F32)
        kvt_ref[:, h] = jnp.einsum('gtv,gtd->gvd', v[:, :, vs], kl[:, :, ks].astype(BF16),
                                   preferred_element_type=F32)
        qe_ref[:, h] = qh
        el_ref[:, h] = e_last[:, :, ks]

    def chunk_body(ci, carry):
        st = st_ref[...]
        oi_ref[ci] = oi_ref[ci] + jnp.einsum('htd,hvd->htv', qe_ref[ci], st.astype(BF16),
                                             preferred_element_type=F32)
        st_ref[...] = st * el_ref[ci] + kvt_ref[ci]
        return carry

    lax.fori_loop(0, ng, chunk_body, 0)

    for ci in range(ng):
        o = oi_ref[ci]
        o = o * lax.rsqrt(jnp.mean(o * o, axis=-1, keepdims=True) + RMS_EPS) * ng_ref[...]
        o = jnp.concatenate([o[h] for h in range(GLA_HEADS)], axis=1)
        rows = slice(ci * CHUNK, (ci + 1) * CHUNK)
        gate = pg_ref[0, rows, g_off:g_off + GLA_WIDTH]
        o_ref[0, rows, :] = o * (gate * _sigmoid(gate))


def _gla(pg, w2p, gkb, ng_w):
    bn, lp, _ = pg.shape
    nc = lp // CHUNK
    ng = max(d for d in range(1, GLA_GROUP + 1) if nc % d == 0)
    rows = ng * CHUNK
    hh = GLA_HEADS
    return pl.pallas_call(
        _gla_kernel,
        name="gla_mixer",
        grid=(bn, nc // ng),
        in_specs=[
            pl.BlockSpec((1, rows, GLA_COLS), lambda b, c: (b, c, 0)),
            pl.BlockSpec((128, GLA_KEY), lambda b, c: (0, 0)),
            pl.BlockSpec((1, GLA_KEY), lambda b, c: (0, 0)),
            pl.BlockSpec((1, GLA_DV), lambda b, c: (0, 0)),
        ],
        out_specs=pl.BlockSpec((1, rows, GLA_WIDTH), lambda b, c: (b, c, 0)),
        out_shape=jax.ShapeDtypeStruct((bn, lp, GLA_WIDTH), F32),
        scratch_shapes=[
            pltpu.VMEM((hh, GLA_DV, GLA_DK), F32),
            pltpu.VMEM((ng, hh, CHUNK, GLA_DK), BF16),
            pltpu.VMEM((ng, hh, CHUNK, GLA_DV), F32),
            pltpu.VMEM((ng, hh, GLA_DV, GLA_DK), F32),
            pltpu.VMEM((ng, hh, 1, GLA_DK), F32),
        ],
        compiler_params=pltpu.CompilerParams(dimension_semantics=("arbitrary", "arbitrary"),
                                             vmem_limit_bytes=VMEM_LIMIT),
    )(pg, w2p, gkb, ng_w)


RWKV_GROUP = 11
RWKV_SLAB = 256
def _mm_split3(ones_bf16, x):
    hi = x.astype(BF16)
    r1 = x - hi.astype(F32)
    mid = r1.astype(BF16)
    lo = (r1 - mid.astype(F32)).astype(BF16)
    return jnp.dot(jnp.concatenate([ones_bf16] * 3, axis=1), jnp.concatenate([hi, mid, lo], axis=0),
                   preferred_element_type=F32)


def _mm_split2_rhs_ones(x, ones_bf16):
    hi = x.astype(BF16)
    mid = (x - hi.astype(F32)).astype(BF16)
    dot = lambda t: jnp.dot(t, ones_bf16, preferred_element_type=F32)
    return dot(hi) + dot(mid)


def _rwkv_kernel(pr_ref, pv_ref, mu_ref, w0_ref, w2_ref, a0_ref, a2_ref, g2_ref, kk_ref, ka_ref,
                 rk_ref, lng_ref, lnb_ref, seg_ref, o_ref, s_ref, ops_ref, wend_ref, rm_ref, yn_ref,
                 y_ref, bonus_ref, gate_ref):
    c = pl.program_id(1)
    rows_n = pr_ref.shape[1]
    ng = rows_n // CHUNK

    @pl.when(c == 0)
    def _():
        s_ref[...] = jnp.zeros_like(s_ref)

    p = pr_ref[0]
    prev_row = jnp.where(c > 0, pv_ref[0][7:8, :], 0.0)
    rolled = pltpu.roll(p, 1, 0)
    row8 = lax.broadcasted_iota(I32, (8, 1), 0)
    prev = jnp.concatenate([jnp.where(row8 == 0, prev_row, rolled[:8]), rolled[8:]], axis=0)
    p = p + (prev - p) * mu_ref[...]
    W = RWKV_WIDTH
    r = p[:, 0:W]
    k = p[:, W:2 * W]
    v = p[:, 2 * W:3 * W]
    w_low = p[:, 3 * W:3 * W + RWKV_W_RANK]
    a_low = p[:, 3 * W + RWKV_W_RANK:3 * W + RWKV_W_RANK + RWKV_A_RANK]
    g_low = p[:, 3 * W + RWKV_W_RANK + RWKV_A_RANK:]
    wx = w0_ref[...] + _mm(jnp.tanh(w_low), w2_ref[...])
    logd = _sigmoid(wx) * (-math.exp(-0.5))
    a = _sigmoid(a0_ref[...] + _mm(a_low, a2_ref[...]))
    g = _mm(_sigmoid(g_low), g2_ref[...])
    seg = seg_ref[...]
    kk = k * kk_ref[...]
    kk = kk * lax.rsqrt(jnp.maximum(_mm_split2_rhs_ones(kk * kk, seg), 1e-24))
    k = k * (1.0 + (a - 1.0) * ka_ref[...])
    bonus = _mm_split2_rhs_ones(r * k * rk_ref[...], seg) * v

    bonus_ref[...] = bonus
    gate_ref[...] = g

    incl, strict, diag = _tri_masks(CHUNK)
    tril = incl.astype(BF16)
    c_in = jnp.concatenate([_mm_split3(tril, logd[i * CHUNK:(i + 1) * CHUNK]) for i in range(ng)], axis=0)
    g3 = lambda t: t.reshape(ng, CHUNK, W)
    logd, c_in, r, k, v, kk, a = g3(logd), g3(c_in), g3(r), g3(k), g3(v), g3(kk), g3(a)
    c_last = c_in[:, CHUNK - 1:CHUNK, :]
    e_neg = jnp.exp(-c_in)
    e_end = jnp.exp(c_last - c_in)
    kka = kk * a
    per_head = (-kk * jnp.exp(c_in - logd), r * jnp.exp(c_in), kka * e_neg, k * e_neg, kka * e_end,
                k * e_end, v)
    GW = RWKV_SLAB
    n_slab = W // GW
    for i, t in enumerate(per_head):
        t = t.astype(BF16)
        for sl in range(n_slab):
            ops_ref[i, sl] = t[:, :, sl * GW:(sl + 1) * GW]
    w_end = jnp.exp(c_last)
    for sl in range(n_slab):
        wend_ref[sl] = w_end[:, :, sl * GW:(sl + 1) * GW]

    hpl = GW // RWKV_HEAD
    lane_head = lax.broadcasted_iota(I32, (1, 1, GW), 2) // RWKV_HEAD
    head_masks = [lane_head == h for h in range(hpl)]
    rr = lax.broadcasted_iota(I32, (CHUNK, GW), 0)
    cc = lax.broadcasted_iota(I32, (CHUNK, GW), 1) % RWKV_HEAD
    incl4, strict4, eye4 = rr >= cc, rr > cc, (rr == cc).astype(F32)

    def block_diag(x):
        x = x.astype(BF16)
        return jnp.concatenate([jnp.where(m, x, jnp.zeros_like(x)) for m in head_masks], axis=1)

    def head_blocks(full):
        out = jnp.where(head_masks[0], full[:, :RWKV_HEAD], 0.0)
        for h in range(1, hpl):
            out = out + jnp.where(head_masks[h], full[:, h * RWKV_HEAD:(h + 1) * RWKV_HEAD], 0.0)
        return out

    bmm = lambda x, y: jnp.einsum('gts,gsd->gtd', x.astype(BF16), y.astype(BF16), preferred_element_type=F32)
    bmm_nt = lambda x, y: jnp.einsum('gtd,gsd->gts', x.astype(BF16), y.astype(BF16), preferred_element_type=F32)
    bmm_tn = lambda x, y: jnp.einsum('gtk,gtd->gkd', x.astype(BF16), y.astype(BF16), preferred_element_type=F32)

    def slab_body(sl, carry):
        at, rt, bt, kt, bh, kh, vv = [ops_ref[i, sl] for i in range(7)]
        gm = bmm_nt(jnp.concatenate([at, rt], axis=1),
                    jnp.concatenate([block_diag(bt), block_diag(kt)], axis=1))
        a_ab = jnp.where(strict4, gm[:, :CHUNK, :GW], 0.0)
        a_ak = jnp.where(strict4, gm[:, :CHUNK, GW:], 0.0)
        a_rb = jnp.where(incl4, gm[:, CHUNK:, :GW], 0.0)
        a_rk = jnp.where(incl4, gm[:, CHUNK:, GW:], 0.0)
        akv = bmm(jnp.concatenate([a_ak, a_rk], axis=1), block_diag(vv))
        xk = eye4 + a_ab
        pk = bmm(a_ab, block_diag(a_ab))
        for _ in range(4):
            both = bmm(jnp.concatenate([xk, pk], axis=1), block_diag(pk))
            xk = xk + both[:, :CHUNK]
            pk = both[:, CHUNK:]
        xk = xk + bmm(xk, block_diag(pk))
        pq = bmm(xk, jnp.concatenate([block_diag(at), block_diag(akv[:, :CHUNK])], axis=2))
        p_bd, q_bd = block_diag(pq[:, :, :GW]), block_diag(pq[:, :, GW:])
        ry = bmm(a_rb, jnp.concatenate([p_bd, q_bd], axis=2)) + jnp.concatenate(
            [rt.astype(F32), akv[:, CHUNK:]], axis=2)
        m_sbs = head_blocks(bmm_tn(bh, pq[:, :, :GW])) + eye4 * wend_ref[sl]
        n_sbs = head_blocks(bmm_tn(jnp.concatenate([bh, kh], axis=1),
                                   jnp.concatenate([pq[:, :, GW:].astype(BF16), vv], axis=1)))
        rm_ref[sl] = jnp.concatenate([ry[:, :, :GW], m_sbs], axis=1).astype(BF16)
        yn_ref[sl] = jnp.concatenate([ry[:, :, GW:], n_sbs], axis=1)
        return carry

    lax.fori_loop(0, n_slab, slab_body, 0)

    def chunk_body(ci, carry):
        for sl in range(n_slab):
            lanes = slice(sl * GW, (sl + 1) * GW)
            st = s_ref[:, lanes]
            st_bd = jnp.concatenate([jnp.where(m[0], st, 0.0) for m in head_masks], axis=0).astype(BF16)
            res = jnp.dot(rm_ref[sl, ci], st_bd, preferred_element_type=F32) + yn_ref[sl, ci]
            y_ref[ci, :, lanes] = res[:CHUNK]
            s_ref[:, lanes] = res[CHUNK:]
        return carry

    lax.fori_loop(0, ng, chunk_body, 0)

    y = y_ref[...].reshape(rows_n, W)
    mean = _mm_split2_rhs_ones(y, seg) * (1.0 / RWKV_HEAD)
    yc = y - mean
    var = _mm_split2_rhs_ones(yc * yc, seg) * (1.0 / RWKV_HEAD)
    yn = yc * lax.rsqrt(var + RWKV_LN_EPS)
    o_ref[0] = (yn * lng_ref[...] + lnb_ref[...] + bonus_ref[...]) * gate_ref[...]


def _rwkv(pr, mu, w0, w2, a0, a2, g2, k_k, k_a, r_k, ln_g, ln_b, seg):
    bn, lp, _ = pr.shape
    nc = lp // CHUNK
    ng = max(d for d in range(1, RWKV_GROUP + 1) if nc % d == 0)
    rows = ng * CHUNK
    vec = lambda n: pl.BlockSpec((1, n), lambda b, c: (0, 0))
    mat = lambda m, n: pl.BlockSpec((m, n), lambda b, c: (0, 0))
    hd = RWKV_HEAD
    n_slab = RWKV_WIDTH // RWKV_SLAB
    return pl.pallas_call(
        _rwkv_kernel,
        name="rwkv_mixer",
        grid=(bn, nc // ng),
        in_specs=[
            pl.BlockSpec((1, rows, RWKV_COLS), lambda b, c: (b, c, 0)),
            pl.BlockSpec((1, 8, RWKV_COLS), lambda b, c: (b, jnp.maximum(c * (rows // 8) - 1, 0), 0)),
            vec(RWKV_COLS), vec(RWKV_WIDTH), mat(RWKV_W_RANK, RWKV_WIDTH), vec(RWKV_WIDTH),
            mat(RWKV_A_RANK, RWKV_WIDTH), mat(RWKV_G_RANK, RWKV_WIDTH), vec(RWKV_WIDTH), vec(RWKV_WIDTH),
            vec(RWKV_WIDTH), vec(RWKV_WIDTH), vec(RWKV_WIDTH), mat(RWKV_WIDTH, RWKV_WIDTH),
        ],
        out_specs=pl.BlockSpec((1, rows, RWKV_WIDTH), lambda b, c: (b, c, 0)),
        out_shape=jax.ShapeDtypeStruct((bn, lp, RWKV_WIDTH), F32),
        scratch_shapes=[
            pltpu.VMEM((hd, RWKV_WIDTH), F32),
            pltpu.VMEM((7, n_slab, ng, CHUNK, RWKV_SLAB), BF16),
            pltpu.VMEM((n_slab, ng, 1, RWKV_SLAB), F32),
            pltpu.VMEM((n_slab, ng, 2 * CHUNK, RWKV_SLAB), BF16),
            pltpu.VMEM((n_slab, ng, 2 * CHUNK, RWKV_SLAB), F32),
            pltpu.VMEM((ng, CHUNK, RWKV_WIDTH), F32),
            pltpu.VMEM((rows, RWKV_WIDTH), F32),
            pltpu.VMEM((rows, RWKV_WIDTH), F32),
        ],
        compiler_params=pltpu.CompilerParams(dimension_semantics=("arbitrary", "arbitrary"),
                                             vmem_limit_bytes=VMEM_LIMIT),
    )(pr, pr, mu, w0, w2, a0, a2, g2, k_k, k_a, r_k, ln_g, ln_b, seg.astype(BF16))


def _outproj_router_kernel(yg_ref, yr_ref, h_ref, wo_ref, g_ref, b_ref, rwt_ref, rb_ref, tri_ref,
                           h1_ref, idx_ref, gate_ref, rank_ref, cnt_ref, base_ref):
    i = pl.program_id(0)

    @pl.when(i == 0)
    def _():
        base_ref[...] = jnp.zeros_like(base_ref)

    wo = wo_ref[...]
    mix = _mm(yg_ref[...], wo[:GLA_WIDTH]) + _mm(yr_ref[...], wo[GLA_WIDTH:])
    h1 = _layer_norm(DEEPNORM_ALPHA * h_ref[...] + mix, g_ref[...], b_ref[...])
    _store_row_tiles(h1_ref, h1)
    h_hi = h1.astype(BF16)
    h_lo = (h1 - h_hi.astype(F32)).astype(BF16)
    nt = (((1,), (1,)), ((), ()))
    part = lax.dot_general(rwt_ref[...], h_hi, nt, preferred_element_type=F32)
    work = (part[:N_EXPERTS] + part[N_EXPERTS:]
            + lax.dot_general(rwt_ref[:N_EXPERTS, :], h_lo, nt, preferred_element_type=F32)
            + rb_ref[...][:, 0:1])
    tm = work.shape[1]
    e_iota = lax.broadcasted_iota(I32, (N_EXPERTS, tm), 0)
    base = base_ref[...][:, 0:1]
    vals, onehots = [], []
    for kk in range(TOP_K):
        m = jnp.max(work, axis=0, keepdims=True)
        sel = jnp.min(jnp.where(work == m, e_iota, N_EXPERTS), axis=0, keepdims=True)
        onehot = e_iota == sel
        work = jnp.where(onehot, -jnp.inf, work)
        vals.append(m)
        onehots.append(onehot.astype(F32))
        idx_ref[kk:kk + 1, :] = sel
    cnt_all = jnp.dot(jnp.concatenate(onehots, axis=0).astype(BF16), tri_ref[...], preferred_element_type=F32)
    prior = jnp.zeros((N_EXPERTS, 1), F32)
    for kk in range(TOP_K):
        cnt = cnt_all[kk * N_EXPERTS:(kk + 1) * N_EXPERTS]
        rank = jnp.sum(onehots[kk] * (base + prior + cnt - 1.0), axis=0, keepdims=True)
        prior = prior + cnt[:, tm - 1:tm]
        rank_ref[kk:kk + 1, :] = rank.astype(I32)
    es = [jnp.exp(vv - vals[0]) for vv in vals]
    den = es[0] + es[1] + es[2] + es[3]
    for kk in range(TOP_K):
        gate_ref[kk:kk + 1, :] = es[kk] / den
    new_base = base + prior
    base_ref[...] = jnp.broadcast_to(new_base, base_ref.shape)
    cnt_ref[...] = jnp.broadcast_to(new_base, cnt_ref.shape)


def _outproj_router(yg, yr, h, wo, g, b, rwt, rb, tri):
    tp = h.shape[0]
    tm = ROUTER_TILE
    const = lambda m, n: pl.BlockSpec((m, n), lambda i: (0, 0))
    return pl.pallas_call(
        _outproj_router_kernel,
        name="outproj_router",
        grid=(tp // tm,),
        in_specs=[
            pl.BlockSpec((tm, GLA_WIDTH), lambda i: (i, 0)),
            pl.BlockSpec((tm, RWKV_WIDTH), lambda i: (i, 0)),
            pl.BlockSpec((tm, D_MODEL), lambda i: (i, 0)),
            const(D_MODEL, D_MODEL), const(1, D_MODEL), const(1, D_MODEL),
            const(2 * N_EXPERTS, D_MODEL), const(N_EXPERTS, 128), const(tm, tm),
        ],
        out_specs=[
            pl.BlockSpec(_tiled_rows(tm), lambda i: (i, 0)),
            pl.BlockSpec((TOP_K, tm), lambda i: (0, i)),
            pl.BlockSpec((TOP_K, tm), lambda i: (0, i)),
            pl.BlockSpec((TOP_K, tm), lambda i: (0, i)),
            pl.BlockSpec((N_EXPERTS, 128), lambda i: (0, 0)),
        ],
        out_shape=[
            jax.ShapeDtypeStruct(_tiled_rows(tp), F32),
            jax.ShapeDtypeStruct((TOP_K, tp), I32),
            jax.ShapeDtypeStruct((TOP_K, tp), F32),
            jax.ShapeDtypeStruct((TOP_K, tp), I32),
            jax.ShapeDtypeStruct((N_EXPERTS, 128), F32),
        ],
        scratch_shapes=[pltpu.VMEM((N_EXPERTS, 128), F32)],
        compiler_params=pltpu.CompilerParams(dimension_semantics=("arbitrary",),
                                             vmem_limit_bytes=VMEM_LIMIT),
    )(yg, yr, h, wo, g, b, rwt, rb, tri)


def _zero_blocks_kernel(last_ref, o_ref):
    o_ref[...] = jnp.zeros_like(o_ref)


def _zero_blocks(last_block, n_slots):
    return pl.pallas_call(
        _zero_blocks_kernel,
        name="moe_zero_blocks",
        grid_spec=pltpu.PrefetchScalarGridSpec(
            num_scalar_prefetch=1,
            grid=(N_EXPERTS,),
            in_specs=[],
            out_specs=pl.BlockSpec(_tiled_rows(MOE_BLOCK), lambda e, last: (last[e], 0)),
        ),
        out_shape=jax.ShapeDtypeStruct(_tiled_rows(n_slots), F32),
        compiler_params=pltpu.CompilerParams(dimension_semantics=("arbitrary",)),
    )(last_block)


def _dispatch_kernel(pos_ref, x_ref, xs_in_ref, xs_ref, sem):
    del xs_in_ref
    tm = pos_ref.shape[0] // TOP_K

    def start(r, carry):
        for kk in range(TOP_K):
            pltpu.make_async_copy(_row_tile(x_ref, r), _row_tile(xs_ref, pos_ref[r * TOP_K + kk]),
                                  sem).start(priority=kk % 2)
        return carry

    lax.fori_loop(0, tm, start, 0, unroll=DMA_UNROLL)
    for kk in range(TOP_K):
        pltpu.make_async_copy(x_ref, xs_ref.at[pl.ds(0, x_ref.shape[0])], sem).wait()


def _dispatch(pos, x, xs):
    tp = pos.shape[0] // TOP_K
    tm = DISPATCH_TILE if tp % DISPATCH_TILE == 0 else ROUTER_TILE
    return pl.pallas_call(
        _dispatch_kernel,
        name="moe_dispatch",
        grid=(tp // tm,),
        in_specs=[
            pl.BlockSpec((tm * TOP_K,), lambda i: (i,), memory_space=pltpu.SMEM),
            pl.BlockSpec(_tiled_rows(tm), lambda i: (i, 0)),
            pl.BlockSpec(memory_space=pl.ANY),
        ],
        out_specs=pl.BlockSpec(memory_space=pl.ANY),
        out_shape=jax.ShapeDtypeStruct(xs.shape, xs.dtype),
        scratch_shapes=[pltpu.SemaphoreType.DMA(())],
        input_output_aliases={2: 0},
        compiler_params=pltpu.CompilerParams(dimension_semantics=("arbitrary",)),
    )(pos, x, xs)


SPLIT_TILE = 256
def _moe_kernel(be_ref, nu_ref, first_ref, slot_ref, nxt_ref, xs_ref, wu_hbm, wd_hbm, perm_ref, bg_ref,
                bl_ref, bd_ref, ys_ref, wu_buf, wd_buf, wsem, wg_s, wl_s, wd_s):
    i = pl.program_id(0)

    def weight_copies(e, sl):
        return (pltpu.make_async_copy(wu_hbm.at[e], wu_buf.at[sl], wsem.at[0, sl]),
                pltpu.make_async_copy(wd_hbm.at[e], wd_buf.at[sl], wsem.at[1, sl]))

    @pl.when(i < nu_ref[0])
    def _():
        @pl.when(first_ref[i] == 1)
        def _():
            sl = slot_ref[i]

            @pl.when(i == 0)
            def _():
                for c in weight_copies(be_ref[0], sl):
                    c.start()

            for c in weight_copies(be_ref[i], sl):
                c.wait()

            @pl.when(nxt_ref[i] >= 0)
            def _():
                for c in weight_copies(nxt_ref[i], 1 - sl):
                    c.start()

            half = SPLIT_TILE // 2
            for t in range(wu_buf.shape[2] // SPLIT_TILE):
                d = jnp.dot(wu_buf[sl, :, t * SPLIT_TILE:(t + 1) * SPLIT_TILE].astype(BF16), perm_ref[...],
                            preferred_element_type=F32).astype(BF16)
                wg_s[:, t * half:(t + 1) * half] = d[:, :half]
                wl_s[:, t * half:(t + 1) * half] = d[:, half:]
            wd_s[...] = wd_buf[sl].astype(BF16)

        x = _load_row_tiles(xs_ref).astype(BF16)
        x_glu = jnp.dot(x, wg_s[...], preferred_element_type=F32) + bg_ref[0]
        x_lin = jnp.dot(x, wl_s[...], preferred_element_type=F32) + bl_ref[0]
        x_glu = jnp.minimum(x_glu, SWIGLU_LIMIT)
        x_lin = jnp.clip(x_lin, -SWIGLU_LIMIT, SWIGLU_LIMIT)
        act = x_glu * _sigmoid(SWIGLU_ALPHA * x_glu) * (x_lin + 1.0)
        _store_row_tiles(ys_ref, jnp.dot(act.astype(BF16), wd_s[...], preferred_element_type=F32) + bd_ref[0])


def _moe(block_e, n_used, xs, w_up, w_down, bg, bl, bd):
    n_slots = xs.shape[0] // ROW_TILE[0]
    nb = n_slots // MOE_BLOCK
    src = jnp.arange(SPLIT_TILE, dtype=I32)[:, None]
    dst = jnp.arange(SPLIT_TILE, dtype=I32)[None, :]
    half = SPLIT_TILE // 2
    perm = (src == jnp.where(dst < half, 2 * dst, 2 * (dst - half) + 1)).astype(BF16)
    ids = jnp.arange(nb, dtype=I32)
    first = jnp.logical_and(ids < n_used[0], jnp.logical_or(ids == 0, block_e != jnp.roll(block_e, 1)))
    slot = (jnp.cumsum(first.astype(I32)) - 1) % 2
    first_at = jnp.where(first, ids, nb)
    next_first = jnp.concatenate([lax.cummin(first_at[::-1])[::-1][1:], jnp.full((1,), nb, I32)])
    nxt_e = jnp.where(next_first < nb, block_e[jnp.minimum(next_first, nb - 1)], -1).astype(I32)
    blk = lambda i, be, nu, *_: (jnp.maximum(jnp.minimum(i, nu[0] - 1), 0), 0)
    bspec = pl.BlockSpec((1, 1, D_FF), lambda i, be, *_: (be[i], 0, 0))
    return pl.pallas_call(
        _moe_kernel,
        name="moe_experts",
        grid_spec=pltpu.PrefetchScalarGridSpec(
            num_scalar_prefetch=5,
            grid=(nb,),
            in_specs=[
                pl.BlockSpec(_tiled_rows(MOE_BLOCK), blk),
                pl.BlockSpec(memory_space=pl.ANY),
                pl.BlockSpec(memory_space=pl.ANY),
                pl.BlockSpec((SPLIT_TILE, SPLIT_TILE), lambda i, *_: (0, 0)),
                bspec, bspec, bspec,
            ],
            out_specs=pl.BlockSpec(_tiled_rows(MOE_BLOCK), blk),
            scratch_shapes=[
                pltpu.VMEM((2, D_MODEL, 2 * D_FF), F32),
                pltpu.VMEM((2, D_FF, D_MODEL), F32),
                pltpu.SemaphoreType.DMA((2, 2)),
                pltpu.VMEM((D_MODEL, D_FF), BF16),
                pltpu.VMEM((D_MODEL, D_FF), BF16),
                pltpu.VMEM((D_FF, D_MODEL), BF16),
            ],
        ),
        out_shape=jax.ShapeDtypeStruct(_tiled_rows(n_slots), F32),
        compiler_params=pltpu.CompilerParams(dimension_semantics=("arbitrary",),
                                             vmem_limit_bytes=VMEM_LIMIT),
    )(block_e, n_used, first.astype(I32), slot.astype(I32), nxt_e, xs, w_up, w_down, perm, bg, bl, bd)


COMBINE_SLOTS = 3
COMBINE_CHUNKS = 2


def _combine_kernel(*refs, cps):
    first_pos = refs[0:2]
    ahead_pos = refs[2:2 + cps]
    gts = refs[2 + cps:2 + 2 * cps]
    h1s = refs[2 + 2 * cps:2 + 3 * cps]
    g_ref, b_ref, ys_ref, o_ref, buf, sem = refs[2 + 3 * cps:]
    s = pl.program_id(0)
    rows = TOP_K * CHUNK

    def issue_row(p_ref, to_slot, r):
        for kk in range(TOP_K):
            pltpu.make_async_copy(_row_tile(ys_ref, p_ref[r * TOP_K + kk]),
                                  _row_tile(buf, to_slot * rows + kk * CHUNK + r),
                                  sem.at[to_slot]).start(priority=kk % 2)

    def wait_slot(which):
        span = rows * ROW_TILE[0]
        pltpu.make_async_copy(ys_ref.at[pl.ds(0, span)], buf.at[pl.ds(which * span, span)], sem.at[which]).wait()

    @pl.when(s == 0)
    def _():
        def body(r, carry):
            issue_row(first_pos[0], 0, r)
            issue_row(first_pos[1], 1, r)
            return carry

        lax.fori_loop(0, CHUNK, body, 0, unroll=DMA_UNROLL)

    def reduce_chunk(j, cur):
        ahead = (cur + 2) % COMBINE_SLOTS
        wait_slot(cur)
        gt = gts[j][...]
        quarter = CHUNK // TOP_K
        ffn = None
        for kk in range(TOP_K):
            part = _load_row_tiles(buf, cur * rows + kk * CHUNK, CHUNK) * gt[:, kk:kk + 1]
            ffn = part if ffn is None else ffn + part
            for r in range(kk * quarter, (kk + 1) * quarter):
                issue_row(ahead_pos[j], ahead, r)
        o_ref[0, j * CHUNK:(j + 1) * CHUNK, :] = _layer_norm(
            DEEPNORM_ALPHA * _load_row_tiles(h1s[j]) + ffn, g_ref[...], b_ref[...])

    def step(first_slot):
        for j in range(cps):
            reduce_chunk(j, (first_slot + j) % COMBINE_SLOTS)

        @pl.when(s == pl.num_programs(0) - 1)
        def _():
            last = (first_slot + cps - 1) % COMBINE_SLOTS
            wait_slot((last + 1) % COMBINE_SLOTS)
            wait_slot((last + 2) % COMBINE_SLOTS)

    for first_slot in range(COMBINE_SLOTS):
        pl.when((s * cps) % COMBINE_SLOTS == first_slot)(functools.partial(step, first_slot))


def _combine(pos_flat, gates_t, h1, g, b, ys, bn, lp):
    seq = lp - CHUNK
    nc = lp // CHUNK
    ncs = seq // CHUNK
    cps = COMBINE_CHUNKS if ncs % COMBINE_CHUNKS == 0 else 1
    n_chunks = bn * ncs
    chunk_of = lambda m: (m // ncs) * nc + m % ncs + 1
    clamp = lambda m: jnp.minimum(m, n_chunks - 1)
    pos_spec = lambda f: pl.BlockSpec((TOP_K * CHUNK,), lambda s: (chunk_of(f(s)),), memory_space=pltpu.SMEM)
    per_chunk = lambda shape: [pl.BlockSpec(shape, lambda s, j=j: (chunk_of(s * cps + j), 0)) for j in range(cps)]
    return pl.pallas_call(
        functools.partial(_combine_kernel, cps=cps),
        name="moe_combine",
        grid=(n_chunks // cps,),
        in_specs=[pos_spec(lambda s: 0 * s), pos_spec(lambda s: 0 * s + 1)]
        + [pos_spec(lambda s, j=j: clamp(s * cps + j + 2)) for j in range(cps)]
        + per_chunk((CHUNK, TOP_K)) + per_chunk(_tiled_rows(CHUNK))
        + [
            pl.BlockSpec((1, D_MODEL), lambda s: (0, 0)),
            pl.BlockSpec((1, D_MODEL), lambda s: (0, 0)),
            pl.BlockSpec(memory_space=pl.ANY),
        ],
        out_specs=pl.BlockSpec((1, cps * CHUNK, D_MODEL), lambda s: (s // (ncs // cps), s % (ncs // cps), 0)),
        out_shape=jax.ShapeDtypeStruct((bn, seq, D_MODEL), F32),
        scratch_shapes=[pltpu.VMEM(_tiled_rows(COMBINE_SLOTS * TOP_K * CHUNK), F32),
                        pltpu.SemaphoreType.DMA((COMBINE_SLOTS,))],
        compiler_params=pltpu.CompilerParams(dimension_semantics=("arbitrary",)),
    )(*([pos_flat] * (2 + cps) + [gates_t] * cps + [h1] * cps + [g, b, ys]))


def kernel(x, meta, ln_in_g, ln_in_b, w_in, gla_gk_w2, gla_gk_b, gla_norm_g, rwkv_mu, rwkv_w0, rwkv_w2, rwkv_a0, rwkv_a2, rwkv_g2, rwkv_k_k, rwkv_k_a, rwkv_r_k, rwkv_ln_g, rwkv_ln_b, w_out, ln1_g, ln1_b, router_w, router_b, exp_w_up, exp_b_up, exp_w_down, exp_b_down, ln2_g, ln2_b):
    bn, seq, _ = x.shape
    assert seq % CHUNK == 0
    lp = seq + CHUNK
    tp = bn * lp
    assert tp % ROUTER_TILE == 0
    row = lambda t: t.reshape(1, -1).astype(F32)

    hcat = jnp.concatenate([jnp.zeros((bn, N_FRONT, D_MODEL), F32),
                            jnp.broadcast_to(meta[None].astype(F32), (bn, N_META, D_MODEL)), x], axis=1)
    hcat = hcat.reshape(tp, D_MODEL)
    gla_in = 2 * GLA_KEY + 2 * GLA_WIDTH + GLA_GATE_RANK
    w = w_in[0]
    w_gk = w[:, gla_in - GLA_GATE_RANK:gla_in]
    w_cols = jnp.concatenate([w[:, :gla_in], w_gk, w_gk, jnp.zeros((D_MODEL, 128 - 3 * GLA_GATE_RANK), F32),
                              w[:, gla_in:]], axis=1).astype(BF16)
    h, pg, pr = _ln_inproj(hcat, row(ln_in_g), row(ln_in_b), w_cols, lp)

    w2_hi = gla_gk_w2[0].astype(BF16)
    w2_lo = (gla_gk_w2[0] - w2_hi.astype(F32)).astype(BF16)
    w2p = jnp.concatenate([w2_hi, w2_lo, w2_hi, jnp.zeros((128 - 3 * GLA_GATE_RANK, GLA_KEY), BF16)], axis=0)
    y_gla = _gla(pg.reshape(bn, lp, GLA_COLS), w2p, row(gla_gk_b[0]), row(gla_norm_g[0]))

    head_id = jnp.arange(RWKV_WIDTH, dtype=I32) // RWKV_HEAD
    seg = (head_id[:, None] == head_id[None, :]).astype(F32)
    y_rwkv = _rwkv(pr.reshape(bn, lp, RWKV_COLS), row(rwkv_mu[0]), row(rwkv_w0[0]), rwkv_w2[0],
                   row(rwkv_a0[0]), rwkv_a2[0], rwkv_g2[0], row(rwkv_k_k[0]), row(rwkv_k_a[0]),
                   row(rwkv_r_k[0]), row(rwkv_ln_g[0]), row(rwkv_ln_b[0]), seg)

    tri = jnp.triu(jnp.ones((ROUTER_TILE, ROUTER_TILE), F32)).astype(BF16)
    rb = jnp.broadcast_to(router_b[0].reshape(N_EXPERTS, 1), (N_EXPERTS, 128))
    rw_t = router_w[0].T
    rw_hi = rw_t.astype(BF16)
    rwt = jnp.concatenate([rw_hi, (rw_t - rw_hi.astype(F32)).astype(BF16)], axis=0)
    h1, idx, gates, rank, cnt = _outproj_router(
        y_gla.reshape(tp, GLA_WIDTH), y_rwkv.reshape(tp, RWKV_WIDTH), h, w_out[0].astype(BF16),
        row(ln1_g[0]), row(ln1_b[0]), rwt, rb, tri)

    counts = cnt[:, 0].astype(I32)
    padded = (counts + MOE_BLOCK - 1) // MOE_BLOCK * MOE_BLOCK
    ends_p = jnp.cumsum(padded)
    starts_p = ends_p - padded
    e_ids = jnp.arange(N_EXPERTS, dtype=I32)
    start_of = jnp.sum(jnp.where(idx[None] == e_ids[:, None, None], starts_p[:, None, None], 0), axis=0)
    pos = start_of + rank
    nb = tp * TOP_K // MOE_BLOCK + N_EXPERTS
    n_slots = nb * MOE_BLOCK
    block_start = jnp.arange(nb, dtype=I32) * MOE_BLOCK
    block_e = jnp.minimum(jnp.sum((block_start[:, None] >= ends_p[None, :]).astype(I32), axis=1), N_EXPERTS - 1)
    n_used = (ends_p[-1:] // MOE_BLOCK).astype(I32)
    last_block = jnp.maximum(ends_p // MOE_BLOCK - 1, 0).astype(I32)

    pos_flat = pos.T.reshape(tp * TOP_K)
    xs = _dispatch(pos_flat, h1, _zero_blocks(last_block, n_slots))
    bg = exp_b_up[0][:, None, 0::2]
    bl = exp_b_up[0][:, None, 1::2]
    ys = _moe(block_e, n_used, xs, exp_w_up[0], exp_w_down[0], bg, bl, exp_b_down[0][:, None, :])

    return _combine(pos_flat, gates.T, h1, row(ln2_g[0]), row(ln2_b[0]), ys, bn, lp)
```

```python
import functools
import math

import jax
import jax.numpy as jnp
from jax import lax
from jax.experimental import pallas as pl
from jax.experimental.pallas import tpu as pltpu

F32 = jnp.float32
BF16 = jnp.bfloat16
I32 = jnp.int32

D_MODEL = 1024
N_META = 16
CHUNK = 64
N_FRONT = (-N_META) % CHUNK
GLA_HEADS = 4
GLA_DK = 64
GLA_DV = 128
GLA_KEY = GLA_HEADS * GLA_DK
GLA_WIDTH = GLA_HEADS * GLA_DV
GLA_GATE_RANK = 16
GLA_TAU = 16.0
GLA_COLS = 2 * GLA_KEY + 2 * GLA_WIDTH + 128
RWKV_WIDTH = 512
RWKV_HEAD = 64
RWKV_HEADS = RWKV_WIDTH // RWKV_HEAD
RWKV_W_RANK = 64
RWKV_A_RANK = 64
RWKV_G_RANK = 128
RWKV_COLS = 3 * RWKV_WIDTH + RWKV_W_RANK + RWKV_A_RANK + RWKV_G_RANK
N_EXPERTS = 32
TOP_K = 4
D_FF = D_MODEL
SWIGLU_ALPHA = 1.702
SWIGLU_LIMIT = 7.0
MOE_BLOCK = 512
DEPTH = 1
DEEPNORM_ALPHA = (2.0 * DEPTH) ** 0.25
LN_EPS = 1e-5
RWKV_LN_EPS = 64e-5
RMS_EPS = 1e-6

ROUTER_TILE = 768
DISPATCH_TILE = 1536
DMA_UNROLL = 4
LN_INPROJ_ROWS = 704
VMEM_LIMIT = 56 * 1024 * 1024


def _mm(a, b):
    return jnp.dot(a.astype(BF16), b.astype(BF16), preferred_element_type=F32)


def _layer_norm(x, g, b):
    mu = jnp.mean(x, axis=-1, keepdims=True)
    xc = x - mu
    var = jnp.mean(xc * xc, axis=-1, keepdims=True)
    return xc * lax.rsqrt(var + LN_EPS) * g + b


def _sigmoid(x):
    return 1.0 / (1.0 + jnp.exp(-x))


def _log_sigmoid(x):
    return jnp.minimum(x, 0.0) - jnp.log(1.0 + jnp.exp(-jnp.abs(x)))


ROW_TILE = (8, 128)


def _tiled_rows(n):
    return (n * ROW_TILE[0], ROW_TILE[1])


def _row_tile(ref, i):
    return ref.at[pl.ds(pl.multiple_of(i * ROW_TILE[0], ROW_TILE[0]), ROW_TILE[0])]


def _store_row_tiles(ref, x, row0=0):
    n = x.shape[0]
    for j in range(ROW_TILE[0]):
        ref[pl.ds(row0 * ROW_TILE[0] + j, n, stride=ROW_TILE[0]), :] = x[:, j * ROW_TILE[1]:(j + 1) * ROW_TILE[1]]


def _load_row_tiles(ref, row0=0, n=None):
    n = ref.shape[0] // ROW_TILE[0] if n is None else n
    return jnp.concatenate([ref[pl.ds(row0 * ROW_TILE[0] + j, n, stride=ROW_TILE[0]), :]
                            for j in range(ROW_TILE[0])], axis=1)


def _tri_masks(n):
    r = lax.broadcasted_iota(I32, (n, n), 0)
    c = lax.broadcasted_iota(I32, (n, n), 1)
    return r >= c, r > c, r == c


def _ln_inproj_kernel(x_ref, front_ref, g_ref, b_ref, w_ref, h_ref, pg_ref, pr_ref, *, tiles_per_seq):
    i = pl.program_id(0)
    first = i % tiles_per_seq == 0
    x = x_ref[0]
    x = jnp.where(first, jnp.concatenate([front_ref[...], x[:x.shape[0] - CHUNK]], axis=0), x)
    y = _layer_norm(x, g_ref[...], b_ref[...])
    row = lax.broadcasted_iota(I32, (y.shape[0], 1), 0)
    is_front = jnp.logical_and(first, row < N_FRONT)
    y = jnp.where(is_front, 0.0, y)
    h_ref[...] = y
    p = _mm(y, w_ref[...])
    pg_ref[...] = p[:, :GLA_COLS]
    pr_ref[...] = p[:, GLA_COLS:]


def _ln_inproj(x, front, g, b, w, lp):
    tp = x.shape[0] * lp
    tiles_per_seq = 1
    for cand in range(1, lp // 8 + 1):
        if lp % cand == 0 and (lp // cand) % 8 == 0 and lp // cand >= N_FRONT and lp // cand <= LN_INPROJ_ROWS:
            tiles_per_seq = cand
            break
    tm = lp // tiles_per_seq
    ncols = GLA_COLS + RWKV_COLS
    return pl.pallas_call(
        functools.partial(_ln_inproj_kernel, tiles_per_seq=tiles_per_seq),
        name="ln_inproj",
        grid=(tp // tm,),
        in_specs=[
            pl.BlockSpec((pl.Element(1), pl.Element(tm), pl.Element(D_MODEL)),
                         lambda i: (i // tiles_per_seq,
                                    pl.multiple_of(jnp.maximum((i % tiles_per_seq) * tm - CHUNK, 0), CHUNK), 0)),
            pl.BlockSpec((CHUNK, D_MODEL), lambda i: (0, 0)),
            pl.BlockSpec((1, D_MODEL), lambda i: (0, 0)),
            pl.BlockSpec((1, D_MODEL), lambda i: (0, 0)),
            pl.BlockSpec((D_MODEL, ncols), lambda i: (0, 0)),
        ],
        out_specs=[
            pl.BlockSpec((tm, D_MODEL), lambda i: (i, 0)),
            pl.BlockSpec((tm, GLA_COLS), lambda i: (i, 0)),
            pl.BlockSpec((tm, RWKV_COLS), lambda i: (i, 0)),
        ],
        out_shape=[
            jax.ShapeDtypeStruct((tp, D_MODEL), F32),
            jax.ShapeDtypeStruct((tp, GLA_COLS), F32),
            jax.ShapeDtypeStruct((tp, RWKV_COLS), F32),
        ],
        compiler_params=pltpu.CompilerParams(dimension_semantics=("arbitrary",),
                                             vmem_limit_bytes=VMEM_LIMIT),
    )(x, front, g, b, w)


GLA_GROUP = 11
GLA_SUB = 16
GLA_EXP_CAP = 80.0


def _gla_kernel(pg_ref, w2_ref, gkb_ref, ng_ref, o_ref, st_ref, qe_ref, oi_ref, kvt_ref, el_ref):
    c = pl.program_id(1)

    @pl.when(c == 0)
    def _():
        st_ref[...] = jnp.zeros_like(st_ref)

    rows_n = pg_ref.shape[1]
    ng = rows_n // CHUNK
    g_off = 2 * GLA_KEY + GLA_WIDTH
    p = pg_ref[0]
    gl3 = p[:, g_off + GLA_WIDTH:]
    gl_hi = gl3.astype(BF16)
    gl_lo = (gl3 - gl_hi.astype(F32)).astype(BF16)
    lane = lax.broadcasted_iota(I32, (1, gl3.shape[1]), 1)
    third = jnp.logical_and(lane >= 2 * GLA_GATE_RANK, lane < 3 * GLA_GATE_RANK)
    gate_pre = jnp.dot(jnp.where(third, gl_lo, gl_hi), w2_ref[...], preferred_element_type=F32)
    lg = _log_sigmoid(gate_pre + gkb_ref[...]) * (1.0 / GLA_TAU)
    row = lax.broadcasted_iota(I32, (rows_n, 1), 0)
    lg = jnp.where(jnp.logical_and(c == 0, row < N_FRONT), 0.0, lg)
    incl, _, _ = _tri_masks(CHUNK)
    tril = incl.astype(BF16)
    bc = jnp.concatenate([_mm_split3(tril, lg[i * CHUNK:(i + 1) * CHUNK]) for i in range(ng)], axis=0)
    g3 = lambda t: t.reshape(ng, CHUNK, t.shape[-1])
    bc = g3(bc)
    b_last = bc[:, CHUNK - 1:CHUNK, :]
    k = g3(p[:, GLA_KEY:2 * GLA_KEY])
    q = g3(p[:, 0:GLA_KEY]) * (GLA_DK ** -0.5)
    qe = q * jnp.exp(bc)
    n_sub = CHUNK // GLA_SUB
    sub_refs = [jnp.zeros_like(b_last)] + [bc[:, i * GLA_SUB - 1:i * GLA_SUB, :] for i in range(1, n_sub)]
    q_ref = jnp.concatenate([jnp.broadcast_to(r, (ng, GLA_SUB, GLA_KEY)) for r in sub_refs], axis=1)
    qs = (q * jnp.exp(bc - q_ref)).astype(BF16)
    ks_sub = []
    for i, r in enumerate(sub_refs):
        n_rows = (i + 1) * GLA_SUB
        scaled = (k[:, :n_rows] * jnp.exp(jnp.minimum(r - bc[:, :n_rows], GLA_EXP_CAP))).astype(BF16)
        if n_rows < CHUNK:
            scaled = jnp.concatenate([scaled, jnp.zeros((ng, CHUNK - n_rows, GLA_KEY), BF16)], axis=1)
        ks_sub.append(scaled)
    kl = k * jnp.exp(b_last - bc)
    e_last = jnp.exp(b_last)
    v = g3(p[:, 2 * GLA_KEY:g_off]).astype(BF16)
    for h in range(GLA_HEADS):
        ks = slice(h * GLA_DK, (h + 1) * GLA_DK)
        vs = slice(h * GLA_DV, (h + 1) * GLA_DV)
        qh = qe[:, :, ks].astype(BF16)
        a = jnp.concatenate(
            [jnp.einsum('gtd,gsd->gts', qs[:, i * GLA_SUB:(i + 1) * GLA_SUB, ks], ks_sub[i][:, :, ks],
                        preferred_element_type=F32) for i in range(n_sub)], axis=1)
        a = jnp.where(incl, a, 0.0).astype(BF16)
        oi_ref[:, h] = jnp.einsum('gts,gsv->gtv', a, v[:, :, vs], preferred_element_type=F32)
        kvt_ref[:, h] = jnp.einsum('gtv,gtd->gvd', v[:, :, vs], kl[:, :, ks].astype(BF16),
                                   preferred_element_type=F32)
        qe_ref[:, h] = qh
        el_ref[:, h] = e_last[:, :, ks]

    def chunk_body(ci, carry):
        st = st_ref[...]
        oi_ref[ci] = oi_ref[ci] + jnp.einsum('htd,hvd->htv', qe_ref[ci], st.astype(BF16),
                                             preferred_element_type=F32)
        st_ref[...] = st * el_ref[ci] + kvt_ref[ci]
        return carry

    lax.fori_loop(0, ng, chunk_body, 0)

    for ci in range(ng):
        o = oi_ref[ci]
        o = o * lax.rsqrt(jnp.mean(o * o, axis=-1, keepdims=True) + RMS_EPS) * ng_ref[...]
        o = jnp.concatenate([o[h] for h in range(GLA_HEADS)], axis=1)
        rows = slice(ci * CHUNK, (ci + 1) * CHUNK)
        gate = pg_ref[0, rows, g_off:g_off + GLA_WIDTH]
        o_ref[0, rows, :] = o * (gate * _sigmoid(gate))


def _gla(pg, w2p, gkb, ng_w):
    bn, lp, _ = pg.shape
    nc = lp // CHUNK
    ng = max(d for d in range(1, GLA_GROUP + 1) if nc % d == 0)
    rows = ng * CHUNK
    hh = GLA_HEADS
    return pl.pallas_call(
        _gla_kernel,
        name="gla_mixer",
        grid=(bn, nc // ng),
        in_specs=[
            pl.BlockSpec((1, rows, GLA_COLS), lambda b, c: (b, c, 0)),
            pl.BlockSpec((128, GLA_KEY), lambda b, c: (0, 0)),
            pl.BlockSpec((1, GLA_KEY), lambda b, c: (0, 0)),
            pl.BlockSpec((1, GLA_DV), lambda b, c: (0, 0)),
        ],
        out_specs=pl.BlockSpec((1, rows, GLA_WIDTH), lambda b, c: (b, c, 0)),
        out_shape=jax.ShapeDtypeStruct((bn, lp, GLA_WIDTH), F32),
        scratch_shapes=[
            pltpu.VMEM((hh, GLA_DV, GLA_DK), F32),
            pltpu.VMEM((ng, hh, CHUNK, GLA_DK), BF16),
            pltpu.VMEM((ng, hh, CHUNK, GLA_DV), F32),
            pltpu.VMEM((ng, hh, GLA_DV, GLA_DK), F32),
            pltpu.VMEM((ng, hh, 1, GLA_DK), F32),
        ],
        compiler_params=pltpu.CompilerParams(dimension_semantics=("arbitrary", "arbitrary"),
                                             vmem_limit_bytes=VMEM_LIMIT),
    )(pg, w2p, gkb, ng_w)


RWKV_GROUP = 11
RWKV_SLAB = 256
def _mm_split3(ones_bf16, x):
    hi = x.astype(BF16)
    r1 = x - hi.astype(F32)
    mid = r1.astype(BF16)
    lo = (r1 - mid.astype(F32)).astype(BF16)
    return jnp.dot(jnp.concatenate([ones_bf16] * 3, axis=1), jnp.concatenate([hi, mid, lo], axis=0),
                   preferred_element_type=F32)


def _mm_split2_rhs_ones(x, ones_bf16):
    hi = x.astype(BF16)
    mid = (x - hi.astype(F32)).astype(BF16)
    dot = lambda t: jnp.dot(t, ones_bf16, preferred_element_type=F32)
    return dot(hi) + dot(mid)


def _rwkv_kernel(pr_ref, pv_ref, mu_ref, w0_ref, w2_ref, a0_ref, a2_ref, g2_ref, kk_ref, ka_ref,
                 rk_ref, lng_ref, lnb_ref, seg_ref, o_ref, s_ref, ops_ref, wend_ref, rm_ref, yn_ref,
                 y_ref, bonus_ref, gate_ref):
    c = pl.program_id(1)
    rows_n = pr_ref.shape[1]
    ng = rows_n // CHUNK

    @pl.when(c == 0)
    def _():
        s_ref[...] = jnp.zeros_like(s_ref)

    p = pr_ref[0]
    prev_row = jnp.where(c > 0, pv_ref[0][7:8, :], 0.0)
    rolled = pltpu.roll(p, 1, 0)
    row8 = lax.broadcasted_iota(I32, (8, 1), 0)
    prev = jnp.concatenate([jnp.where(row8 == 0, prev_row, rolled[:8]), rolled[8:]], axis=0)
    p = p + (prev - p) * mu_ref[...]
    W = RWKV_WIDTH
    r = p[:, 0:W]
    k = p[:, W:2 * W]
    v = p[:, 2 * W:3 * W]
    w_low = p[:, 3 * W:3 * W + RWKV_W_RANK]
    a_low = p[:, 3 * W + RWKV_W_RANK:3 * W + RWKV_W_RANK + RWKV_A_RANK]
    g_low = p[:, 3 * W + RWKV_W_RANK + RWKV_A_RANK:]
    wx = w0_ref[...] + _mm(jnp.tanh(w_low), w2_ref[...])
    logd = _sigmoid(wx) * (-math.exp(-0.5))
    a = _sigmoid(a0_ref[...] + _mm(a_low, a2_ref[...]))
    g = _mm(_sigmoid(g_low), g2_ref[...])
    seg = seg_ref[...]
    kk = k * kk_ref[...]
    kk = kk * lax.rsqrt(jnp.maximum(_mm_split2_rhs_ones(kk * kk, seg), 1e-24))
    k = k * (1.0 + (a - 1.0) * ka_ref[...])
    bonus = _mm_split2_rhs_ones(r * k * rk_ref[...], seg) * v

    bonus_ref[...] = bonus
    gate_ref[...] = g

    incl, strict, diag = _tri_masks(CHUNK)
    tril = incl.astype(BF16)
    c_in = jnp.concatenate([_mm_split3(tril, logd[i * CHUNK:(i + 1) * CHUNK]) for i in range(ng)], axis=0)
    g3 = lambda t: t.reshape(ng, CHUNK, W)
    logd, c_in, r, k, v, kk, a = g3(logd), g3(c_in), g3(r), g3(k), g3(v), g3(kk), g3(a)
    c_last = c_in[:, CHUNK - 1:CHUNK, :]
    e_neg = jnp.exp(-c_in)
    e_end = jnp.exp(c_last - c_in)
    kka = kk * a
    per_head = (-kk * jnp.exp(c_in - logd), r * jnp.exp(c_in), kka * e_neg, k * e_neg, kka * e_end,
                k * e_end, v)
    GW = RWKV_SLAB
    n_slab = W // GW
    for i, t in enumerate(per_head):
        t = t.astype(BF16)
        for sl in range(n_slab):
            ops_ref[i, sl] = t[:, :, sl * GW:(sl + 1) * GW]
    w_end = jnp.exp(c_last)
    for sl in range(n_slab):
        wend_ref[sl] = w_end[:, :, sl * GW:(sl + 1) * GW]

    hpl = GW // RWKV_HEAD
    lane_head = lax.broadcasted_iota(I32, (1, 1, GW), 2) // RWKV_HEAD
    head_masks = [lane_head == h for h in range(hpl)]
    rr = lax.broadcasted_iota(I32, (CHUNK, GW), 0)
    cc = lax.broadcasted_iota(I32, (CHUNK, GW), 1) % RWKV_HEAD
    incl4, strict4, eye4 = rr >= cc, rr > cc, (rr == cc).astype(F32)

    def block_diag(x):
        x = x.astype(BF16)
        return jnp.concatenate([jnp.where(m, x, jnp.zeros_like(x)) for m in head_masks], axis=1)

    def head_blocks(full):
        out = jnp.where(head_masks[0], full[:, :RWKV_HEAD], 0.0)
        for h in range(1, hpl):
            out = out + jnp.where(head_masks[h], full[:, h * RWKV_HEAD:(h + 1) * RWKV_HEAD], 0.0)
        return out

    bmm = lambda x, y: jnp.einsum('gts,gsd->gtd', x.astype(BF16), y.astype(BF16), preferred_element_type=F32)
    bmm_nt = lambda x, y: jnp.einsum('gtd,gsd->gts', x.astype(BF16), y.astype(BF16), preferred_element_type=F32)
    bmm_tn = lambda x, y: jnp.einsum('gtk,gtd->gkd', x.astype(BF16), y.astype(BF16), preferred_element_type=F32)

    def slab_body(sl, carry):
        at, rt, bt, kt, bh, kh, vv = [ops_ref[i, sl] for i in range(7)]
        gm = bmm_nt(jnp.concatenate([at, rt], axis=1),
                    jnp.concatenate([block_diag(bt), block_diag(kt)], axis=1))
        a_ab = jnp.where(strict4, gm[:, :CHUNK, :GW], 0.0)
        a_ak = jnp.where(strict4, gm[:, :CHUNK, GW:], 0.0)
        a_rb = jnp.where(incl4, gm[:, CHUNK:, :GW], 0.0)
        a_rk = jnp.where(incl4, gm[:, CHUNK:, GW:], 0.0)
        akv = bmm(jnp.concatenate([a_ak, a_rk], axis=1), block_diag(vv))
        xk = eye4 + a_ab
        pk = bmm(a_ab, block_diag(a_ab))
        for _ in range(4):
            both = bmm(jnp.concatenate([xk, pk], axis=1), block_diag(pk))
            xk = xk + both[:, :CHUNK]
            pk = both[:, CHUNK:]
        xk = xk + bmm(xk, block_diag(pk))
        pq = bmm(xk, jnp.concatenate([block_diag(at), block_diag(akv[:, :CHUNK])], axis=2))
        p_bd, q_bd = block_diag(pq[:, :, :GW]), block_diag(pq[:, :, GW:])
        ry = bmm(a_rb, jnp.concatenate([p_bd, q_bd], axis=2)) + jnp.concatenate(
            [rt.astype(F32), akv[:, CHUNK:]], axis=2)
        m_sbs = head_blocks(bmm_tn(bh, pq[:, :, :GW])) + eye4 * wend_ref[sl]
        n_sbs = head_blocks(bmm_tn(jnp.concatenate([bh, kh], axis=1),
                                   jnp.concatenate([pq[:, :, GW:].astype(BF16), vv], axis=1)))
        rm_ref[sl] = jnp.concatenate([ry[:, :, :GW], m_sbs], axis=1).astype(BF16)
        yn_ref[sl] = jnp.concatenate([ry[:, :, GW:], n_sbs], axis=1)
        return carry

    lax.fori_loop(0, n_slab, slab_body, 0)

    def chunk_body(ci, carry):
        for sl in range(n_slab):
            lanes = slice(sl * GW, (sl + 1) * GW)
            st = s_ref[:, lanes]
            st_bd = jnp.concatenate([jnp.where(m[0], st, 0.0) for m in head_masks], axis=0).astype(BF16)
            res = jnp.dot(rm_ref[sl, ci], st_bd, preferred_element_type=F32) + yn_ref[sl, ci]
            y_ref[ci, :, lanes] = res[:CHUNK]
            s_ref[:, lanes] = res[CHUNK:]
        return carry

    lax.fori_loop(0, ng, chunk_body, 0)

    y = y_ref[...].reshape(rows_n, W)
    mean = _mm_split2_rhs_ones(y, seg) * (1.0 / RWKV_HEAD)
    yc = y - mean
    var = _mm_split2_rhs_ones(yc * yc, seg) * (1.0 / RWKV_HEAD)
    yn = yc * lax.rsqrt(var + RWKV_LN_EPS)
    o_ref[0] = (yn * lng_ref[...] + lnb_ref[...] + bonus_ref[...]) * gate_ref[...]


def _rwkv(pr, mu, w0, w2, a0, a2, g2, k_k, k_a, r_k, ln_g, ln_b, seg):
    bn, lp, _ = pr.shape
    nc = lp // CHUNK
    ng = max(d for d in range(1, RWKV_GROUP + 1) if nc % d == 0)
    rows = ng * CHUNK
    vec = lambda n: pl.BlockSpec((1, n), lambda b, c: (0, 0))
    mat = lambda m, n: pl.BlockSpec((m, n), lambda b, c: (0, 0))
    hd = RWKV_HEAD
    n_slab = RWKV_WIDTH // RWKV_SLAB
    return pl.pallas_call(
        _rwkv_kernel,
        name="rwkv_mixer",
        grid=(bn, nc // ng),
        in_specs=[
            pl.BlockSpec((1, rows, RWKV_COLS), lambda b, c: (b, c, 0)),
            pl.BlockSpec((1, 8, RWKV_COLS), lambda b, c: (b, jnp.maximum(c * (rows // 8) - 1, 0), 0)),
            vec(RWKV_COLS), vec(RWKV_WIDTH), mat(RWKV_W_RANK, RWKV_WIDTH), vec(RWKV_WIDTH),
            mat(RWKV_A_RANK, RWKV_WIDTH), mat(RWKV_G_RANK, RWKV_WIDTH), vec(RWKV_WIDTH), vec(RWKV_WIDTH),
            vec(RWKV_WIDTH), vec(RWKV_WIDTH), vec(RWKV_WIDTH), mat(RWKV_WIDTH, RWKV_WIDTH),
        ],
        out_specs=pl.BlockSpec((1, rows, RWKV_WIDTH), lambda b, c: (b, c, 0)),
        out_shape=jax.ShapeDtypeStruct((bn, lp, RWKV_WIDTH), F32),
        scratch_shapes=[
            pltpu.VMEM((hd, RWKV_WIDTH), F32),
            pltpu.VMEM((7, n_slab, ng, CHUNK, RWKV_SLAB), BF16),
            pltpu.VMEM((n_slab, ng, 1, RWKV_SLAB), F32),
            pltpu.VMEM((n_slab, ng, 2 * CHUNK, RWKV_SLAB), BF16),
            pltpu.VMEM((n_slab, ng, 2 * CHUNK, RWKV_SLAB), F32),
            pltpu.VMEM((ng, CHUNK, RWKV_WIDTH), F32),
            pltpu.VMEM((rows, RWKV_WIDTH), F32),
            pltpu.VMEM((rows, RWKV_WIDTH), F32),
        ],
        compiler_params=pltpu.CompilerParams(dimension_semantics=("arbitrary", "arbitrary"),
                                             vmem_limit_bytes=VMEM_LIMIT),
    )(pr, pr, mu, w0, w2, a0, a2, g2, k_k, k_a, r_k, ln_g, ln_b, seg.astype(BF16))


def _outproj_router_kernel(yg_ref, yr_ref, h_ref, wo_ref, g_ref, b_ref, rwt_ref, rb_ref, tri_ref,
                           h1_ref, idx_ref, gate_ref, rank_ref, cnt_ref, base_ref):
    i = pl.program_id(0)

    @pl.when(i == 0)
    def _():
        base_ref[...] = jnp.zeros_like(base_ref)

    wo = wo_ref[...]
    mix = _mm(yg_ref[...], wo[:GLA_WIDTH]) + _mm(yr_ref[...], wo[GLA_WIDTH:])
    h1 = _layer_norm(DEEPNORM_ALPHA * h_ref[...] + mix, g_ref[...], b_ref[...])
    _store_row_tiles(h1_ref, h1)
    h_hi = h1.astype(BF16)
    h_lo = (h1 - h_hi.astype(F32)).astype(BF16)
    nt = (((1,), (1,)), ((), ()))
    part = lax.dot_general(rwt_ref[...], h_hi, nt, preferred_element_type=F32)
    work = (part[:N_EXPERTS] + part[N_EXPERTS:]
            + lax.dot_general(rwt_ref[:N_EXPERTS, :], h_lo, nt, preferred_element_type=F32)
            + rb_ref[...][:, 0:1])
    tm = work.shape[1]
    e_iota = lax.broadcasted_iota(I32, (N_EXPERTS, tm), 0)
    base = base_ref[...][:, 0:1]
    vals, onehots = [], []
    for kk in range(TOP_K):
        m = jnp.max(work, axis=0, keepdims=True)
        sel = jnp.min(jnp.where(work == m, e_iota, N_EXPERTS), axis=0, keepdims=True)
        onehot = e_iota == sel
        work = jnp.where(onehot, -jnp.inf, work)
        vals.append(m)
        onehots.append(onehot.astype(F32))
        idx_ref[kk:kk + 1, :] = sel
    cnt_all = jnp.dot(jnp.concatenate(onehots, axis=0).astype(BF16), tri_ref[...], preferred_element_type=F32)
    prior = jnp.zeros((N_EXPERTS, 1), F32)
    for kk in range(TOP_K):
        cnt = cnt_all[kk * N_EXPERTS:(kk + 1) * N_EXPERTS]
        rank = jnp.sum(onehots[kk] * (base + prior + cnt - 1.0), axis=0, keepdims=True)
        prior = prior + cnt[:, tm - 1:tm]
        rank_ref[kk:kk + 1, :] = rank.astype(I32)
    es = [jnp.exp(vv - vals[0]) for vv in vals]
    den = es[0] + es[1] + es[2] + es[3]
    for kk in range(TOP_K):
        gate_ref[kk:kk + 1, :] = es[kk] / den
    new_base = base + prior
    base_ref[...] = jnp.broadcast_to(new_base, base_ref.shape)
    cnt_ref[...] = jnp.broadcast_to(new_base, cnt_ref.shape)


def _outproj_router(yg, yr, h, wo, g, b, rwt, rb, tri):
    tp = h.shape[0]
    tm = ROUTER_TILE
    const = lambda m, n: pl.BlockSpec((m, n), lambda i: (0, 0))
    return pl.pallas_call(
        _outproj_router_kernel,
        name="outproj_router",
        grid=(tp // tm,),
        in_specs=[
            pl.BlockSpec((tm, GLA_WIDTH), lambda i: (i, 0)),
            pl.BlockSpec((tm, RWKV_WIDTH), lambda i: (i, 0)),
            pl.BlockSpec((tm, D_MODEL), lambda i: (i, 0)),
            const(D_MODEL, D_MODEL), const(1, D_MODEL), const(1, D_MODEL),
            const(2 * N_EXPERTS, D_MODEL), const(N_EXPERTS, 128), const(tm, tm),
        ],
        out_specs=[
            pl.BlockSpec(_tiled_rows(tm), lambda i: (i, 0)),
            pl.BlockSpec((TOP_K, tm), lambda i: (0, i)),
            pl.BlockSpec((TOP_K, tm), lambda i: (0, i)),
            pl.BlockSpec((TOP_K, tm), lambda i: (0, i)),
            pl.BlockSpec((N_EXPERTS, 128), lambda i: (0, 0)),
        ],
        out_shape=[
            jax.ShapeDtypeStruct(_tiled_rows(tp), F32),
            jax.ShapeDtypeStruct((TOP_K, tp), I32),
            jax.ShapeDtypeStruct((TOP_K, tp), F32),
            jax.ShapeDtypeStruct((TOP_K, tp), I32),
            jax.ShapeDtypeStruct((N_EXPERTS, 128), F32),
        ],
        scratch_shapes=[pltpu.VMEM((N_EXPERTS, 128), F32)],
        compiler_params=pltpu.CompilerParams(dimension_semantics=("arbitrary",),
                                             vmem_limit_bytes=VMEM_LIMIT),
    )(yg, yr, h, wo, g, b, rwt, rb, tri)


def _zero_blocks_kernel(last_ref, o_ref):
    o_ref[...] = jnp.zeros_like(o_ref)


def _zero_blocks(last_block, n_slots):
    return pl.pallas_call(
        _zero_blocks_kernel,
        name="moe_zero_blocks",
        grid_spec=pltpu.PrefetchScalarGridSpec(
            num_scalar_prefetch=1,
            grid=(N_EXPERTS,),
            in_specs=[],
            out_specs=pl.BlockSpec(_tiled_rows(MOE_BLOCK), lambda e, last: (last[e], 0)),
        ),
        out_shape=jax.ShapeDtypeStruct(_tiled_rows(n_slots), F32),
        compiler_params=pltpu.CompilerParams(dimension_semantics=("arbitrary",)),
    )(last_block)


def _dispatch_kernel(pos_ref, x_ref, xs_in_ref, xs_ref, sem):
    del xs_in_ref
    tm = pos_ref.shape[0] // TOP_K

    def start(r, carry):
        for kk in range(TOP_K):
            pltpu.make_async_copy(_row_tile(x_ref, r), _row_tile(xs_ref, pos_ref[r * TOP_K + kk]),
                                  sem).start(priority=kk % 2)
        return carry

    lax.fori_loop(0, tm, start, 0, unroll=DMA_UNROLL)
    for kk in range(TOP_K):
        pltpu.make_async_copy(x_ref, xs_ref.at[pl.ds(0, x_ref.shape[0])], sem).wait()


def _dispatch(pos, x, xs):
    tp = pos.shape[0] // TOP_K
    tm = DISPATCH_TILE if tp % DISPATCH_TILE == 0 else ROUTER_TILE
    return pl.pallas_call(
        _dispatch_kernel,
        name="moe_dispatch",
        grid=(tp // tm,),
        in_specs=[
            pl.BlockSpec((tm * TOP_K,), lambda i: (i,), memory_space=pltpu.SMEM),
            pl.BlockSpec(_tiled_rows(tm), lambda i: (i, 0)),
            pl.BlockSpec(memory_space=pl.ANY),
        ],
        out_specs=pl.BlockSpec(memory_space=pl.ANY),
        out_shape=jax.ShapeDtypeStruct(xs.shape, xs.dtype),
        scratch_shapes=[pltpu.SemaphoreType.DMA(())],
        input_output_aliases={2: 0},
        compiler_params=pltpu.CompilerParams(dimension_semantics=("arbitrary",)),
    )(pos, x, xs)


SPLIT_TILE = 256
def _moe_kernel(be_ref, nu_ref, first_ref, slot_ref, nxt_ref, xs_ref, wu_hbm, wd_hbm, perm_ref, bg_ref,
                bl_ref, bd_ref, ys_ref, wu_buf, wd_buf, wsem, wg_s, wl_s, wd_s):
    i = pl.program_id(0)

    def weight_copies(e, sl):
        return (pltpu.make_async_copy(wu_hbm.at[e], wu_buf.at[sl], wsem.at[0, sl]),
                pltpu.make_async_copy(wd_hbm.at[e], wd_buf.at[sl], wsem.at[1, sl]))

    @pl.when(i < nu_ref[0])
    def _():
        @pl.when(first_ref[i] == 1)
        def _():
            sl = slot_ref[i]

            @pl.when(i == 0)
            def _():
                for c in weight_copies(be_ref[0], sl):
                    c.start()

            for c in weight_copies(be_ref[i], sl):
                c.wait()

            @pl.when(nxt_ref[i] >= 0)
            def _():
                for c in weight_copies(nxt_ref[i], 1 - sl):
                    c.start()

            half = SPLIT_TILE // 2
            for t in range(wu_buf.shape[2] // SPLIT_TILE):
                d = jnp.dot(wu_buf[sl, :, t * SPLIT_TILE:(t + 1) * SPLIT_TILE].astype(BF16), perm_ref[...],
                            preferred_element_type=F32).astype(BF16)
                wg_s[:, t * half:(t + 1) * half] = d[:, :half]
                wl_s[:, t * half:(t + 1) * half] = d[:, half:]
            wd_s[...] = wd_buf[sl].astype(BF16)

        x = _load_row_tiles(xs_ref).astype(BF16)
        x_glu = jnp.dot(x, wg_s[...], preferred_element_type=F32) + bg_ref[0]
        x_lin = jnp.dot(x, wl_s[...], preferred_element_type=F32) + bl_ref[0]
        x_glu = jnp.minimum(x_glu, SWIGLU_LIMIT)
        x_lin = jnp.clip(x_lin, -SWIGLU_LIMIT, SWIGLU_LIMIT)
        act = x_glu * _sigmoid(SWIGLU_ALPHA * x_glu) * (x_lin + 1.0)
        _store_row_tiles(ys_ref, jnp.dot(act.astype(BF16), wd_s[...], preferred_element_type=F32) + bd_ref[0])


def _moe(block_e, n_used, xs, w_up, w_down, bg, bl, bd):
    n_slots = xs.shape[0] // ROW_TILE[0]
    nb = n_slots // MOE_BLOCK
    src = jnp.arange(SPLIT_TILE, dtype=I32)[:, None]
    dst = jnp.arange(SPLIT_TILE, dtype=I32)[None, :]
    half = SPLIT_TILE // 2
    perm = (src == jnp.where(dst < half, 2 * dst, 2 * (dst - half) + 1)).astype(BF16)
    ids = jnp.arange(nb, dtype=I32)
    first = jnp.logical_and(ids < n_used[0], jnp.logical_or(ids == 0, block_e != jnp.roll(block_e, 1)))
    slot = (jnp.cumsum(first.astype(I32)) - 1) % 2
    first_at = jnp.where(first, ids, nb)
    next_first = jnp.concatenate([lax.cummin(first_at[::-1])[::-1][1:], jnp.full((1,), nb, I32)])
    nxt_e = jnp.where(next_first < nb, block_e[jnp.minimum(next_first, nb - 1)], -1).astype(I32)
    blk = lambda i, be, nu, *_: (jnp.maximum(jnp.minimum(i, nu[0] - 1), 0), 0)
    bspec = pl.BlockSpec((1, 1, D_FF), lambda i, be, *_: (be[i], 0, 0))
    return pl.pallas_call(
        _moe_kernel,
        name="moe_experts",
        grid_spec=pltpu.PrefetchScalarGridSpec(
            num_scalar_prefetch=5,
            grid=(nb,),
            in_specs=[
                pl.BlockSpec(_tiled_rows(MOE_BLOCK), blk),
                pl.BlockSpec(memory_space=pl.ANY),
                pl.BlockSpec(memory_space=pl.ANY),
                pl.BlockSpec((SPLIT_TILE, SPLIT_TILE), lambda i, *_: (0, 0)),
                bspec, bspec, bspec,
            ],
            out_specs=pl.BlockSpec(_tiled_rows(MOE_BLOCK), blk),
            scratch_shapes=[
                pltpu.VMEM((2, D_MODEL, 2 * D_FF), F32),
                pltpu.VMEM((2, D_FF, D_MODEL), F32),
                pltpu.SemaphoreType.DMA((2, 2)),
                pltpu.VMEM((D_MODEL, D_FF), BF16),
                pltpu.VMEM((D_MODEL, D_FF), BF16),
                pltpu.VMEM((D_FF, D_MODEL), BF16),
            ],
        ),
        out_shape=jax.ShapeDtypeStruct(_tiled_rows(n_slots), F32),
        compiler_params=pltpu.CompilerParams(dimension_semantics=("arbitrary",),
                                             vmem_limit_bytes=VMEM_LIMIT),
    )(block_e, n_used, first.astype(I32), slot.astype(I32), nxt_e, xs, w_up, w_down, perm, bg, bl, bd)


COMBINE_SLOTS = 3
COMBINE_CHUNKS = 2


def _combine_kernel(*refs, cps):
    first_pos = refs[0:2]
    ahead_pos = refs[2:2 + cps]
    gts = refs[2 + cps:2 + 2 * cps]
    h1s = refs[2 + 2 * cps:2 + 3 * cps]
    g_ref, b_ref, ys_ref, o_ref, buf, sem = refs[2 + 3 * cps:]
    s = pl.program_id(0)
    rows = TOP_K * CHUNK

    def issue_row(p_ref, to_slot, r):
        for kk in range(TOP_K):
            pltpu.make_async_copy(_row_tile(ys_ref, p_ref[r * TOP_K + kk]),
                                  _row_tile(buf, to_slot * rows + kk * CHUNK + r),
                                  sem.at[to_slot]).start(priority=kk % 2)

    def wait_slot(which):
        span = rows * ROW_TILE[0]
        pltpu.make_async_copy(ys_ref.at[pl.ds(0, span)], buf.at[pl.ds(which * span, span)], sem.at[which]).wait()

    @pl.when(s == 0)
    def _():
        def body(r, carry):
            issue_row(first_pos[0], 0, r)
            issue_row(first_pos[1], 1, r)
            return carry

        lax.fori_loop(0, CHUNK, body, 0, unroll=DMA_UNROLL)

    def reduce_chunk(j, cur):
        ahead = (cur + 2) % COMBINE_SLOTS
        wait_slot(cur)
        gt = gts[j][...]
        quarter = CHUNK // TOP_K
        ffn = None
        for kk in range(TOP_K):
            part = _load_row_tiles(buf, cur * rows + kk * CHUNK, CHUNK) * gt[:, kk:kk + 1]
            ffn = part if ffn is None else ffn + part
            for r in range(kk * quarter, (kk + 1) * quarter):
                issue_row(ahead_pos[j], ahead, r)
        o_ref[0, j * CHUNK:(j + 1) * CHUNK, :] = _layer_norm(
            DEEPNORM_ALPHA * _load_row_tiles(h1s[j]) + ffn, g_ref[...], b_ref[...])

    def step(first_slot):
        for j in range(cps):
            reduce_chunk(j, (first_slot + j) % COMBINE_SLOTS)

        @pl.when(s == pl.num_programs(0) - 1)
        def _():
            last = (first_slot + cps - 1) % COMBINE_SLOTS
            wait_slot((last + 1) % COMBINE_SLOTS)
            wait_slot((last + 2) % COMBINE_SLOTS)

    for first_slot in range(COMBINE_SLOTS):
        pl.when((s * cps) % COMBINE_SLOTS == first_slot)(functools.partial(step, first_slot))


def _combine(pos_flat, gates_t, h1, g, b, ys, bn, lp):
    seq = lp - CHUNK
    nc = lp // CHUNK
    ncs = seq // CHUNK
    cps = COMBINE_CHUNKS if ncs % COMBINE_CHUNKS == 0 else 1
    n_chunks = bn * ncs
    chunk_of = lambda m: (m // ncs) * nc + m % ncs + 1
    clamp = lambda m: jnp.minimum(m, n_chunks - 1)
    pos_spec = lambda f: pl.BlockSpec((TOP_K * CHUNK,), lambda s: (chunk_of(f(s)),), memory_space=pltpu.SMEM)
    per_chunk = lambda shape: [pl.BlockSpec(shape, lambda s, j=j: (chunk_of(s * cps + j), 0)) for j in range(cps)]
    return pl.pallas_call(
        functools.partial(_combine_kernel, cps=cps),
        name="moe_combine",
        grid=(n_chunks // cps,),
        in_specs=[pos_spec(lambda s: 0 * s), pos_spec(lambda s: 0 * s + 1)]
        + [pos_spec(lambda s, j=j: clamp(s * cps + j + 2)) for j in range(cps)]
        + per_chunk((CHUNK, TOP_K)) + per_chunk(_tiled_rows(CHUNK))
        + [
            pl.BlockSpec((1, D_MODEL), lambda s: (0, 0)),
            pl.BlockSpec((1, D_MODEL), lambda s: (0, 0)),
            pl.BlockSpec(memory_space=pl.ANY),
        ],
        out_specs=pl.BlockSpec((1, cps * CHUNK, D_MODEL), lambda s: (s // (ncs // cps), s % (ncs // cps), 0)),
        out_shape=jax.ShapeDtypeStruct((bn, seq, D_MODEL), F32),
        scratch_shapes=[pltpu.VMEM(_tiled_rows(COMBINE_SLOTS * TOP_K * CHUNK), F32),
                        pltpu.SemaphoreType.DMA((COMBINE_SLOTS,))],
        compiler_params=pltpu.CompilerParams(dimension_semantics=("arbitrary",)),
    )(*([pos_flat] * (2 + cps) + [gates_t] * cps + [h1] * cps + [g, b, ys]))


def kernel(x, meta, ln_in_g, ln_in_b, w_in, gla_gk_w2, gla_gk_b, gla_norm_g, rwkv_mu, rwkv_w0, rwkv_w2, rwkv_a0, rwkv_a2, rwkv_g2, rwkv_k_k, rwkv_k_a, rwkv_r_k, rwkv_ln_g, rwkv_ln_b, w_out, ln1_g, ln1_b, router_w, router_b, exp_w_up, exp_b_up, exp_w_down, exp_b_down, ln2_g, ln2_b):
    bn, seq, _ = x.shape
    assert seq % CHUNK == 0
    lp = seq + CHUNK
    tp = bn * lp
    assert tp % ROUTER_TILE == 0
    row = lambda t: t.reshape(1, -1).astype(F32)

    front = jnp.concatenate([jnp.zeros((N_FRONT, D_MODEL), F32), meta.astype(F32)], axis=0)
    gla_in = 2 * GLA_KEY + 2 * GLA_WIDTH + GLA_GATE_RANK
    w = w_in[0]
    w_gk = w[:, gla_in - GLA_GATE_RANK:gla_in]
    w_cols = jnp.concatenate([w[:, :gla_in], w_gk, w_gk, jnp.zeros((D_MODEL, 128 - 3 * GLA_GATE_RANK), F32),
                              w[:, gla_in:]], axis=1).astype(BF16)
    h, pg, pr = _ln_inproj(x, front, row(ln_in_g), row(ln_in_b), w_cols, lp)

    w2_hi = gla_gk_w2[0].astype(BF16)
    w2_lo = (gla_gk_w2[0] - w2_hi.astype(F32)).astype(BF16)
    w2p = jnp.concatenate([w2_hi, w2_lo, w2_hi, jnp.zeros((128 - 3 * GLA_GATE_RANK, GLA_KEY), BF16)], axis=0)
    y_gla = _gla(pg.reshape(bn, lp, GLA_COLS), w2p, row(gla_gk_b[0]), row(gla_norm_g[0]))

    head_id = jnp.arange(RWKV_WIDTH, dtype=I32) // RWKV_HEAD
    seg = (head_id[:, None] == head_id[None, :]).astype(F32)
    y_rwkv = _rwkv(pr.reshape(bn, lp, RWKV_COLS), row(rwkv_mu[0]), row(rwkv_w0[0]), rwkv_w2[0],
                   row(rwkv_a0[0]), rwkv_a2[0], rwkv_g2[0], row(rwkv_k_k[0]), row(rwkv_k_a[0]),
                   row(rwkv_r_k[0]), row(rwkv_ln_g[0]), row(rwkv_ln_b[0]), seg)

    tri = jnp.triu(jnp.ones((ROUTER_TILE, ROUTER_TILE), F32)).astype(BF16)
    rb = jnp.broadcast_to(router_b[0].reshape(N_EXPERTS, 1), (N_EXPERTS, 128))
    rw_t = router_w[0].T
    rw_hi = rw_t.astype(BF16)
    rwt = jnp.concatenate([rw_hi, (rw_t - rw_hi.astype(F32)).astype(BF16)], axis=0)
    h1, idx, gates, rank, cnt = _outproj_router(
        y_gla.reshape(tp, GLA_WIDTH), y_rwkv.reshape(tp, RWKV_WIDTH), h, w_out[0].astype(BF16),
        row(ln1_g[0]), row(ln1_b[0]), rwt, rb, tri)

    counts = cnt[:, 0].astype(I32)
    padded = (counts + MOE_BLOCK - 1) // MOE_BLOCK * MOE_BLOCK
    ends_p = jnp.cumsum(padded)
    starts_p = ends_p - padded
    e_ids = jnp.arange(N_EXPERTS, dtype=I32)
    start_of = jnp.sum(jnp.where(idx[None] == e_ids[:, None, None], starts_p[:, None, None], 0), axis=0)
    pos = start_of + rank
    nb = tp * TOP_K // MOE_BLOCK + N_EXPERTS
    n_slots = nb * MOE_BLOCK
    block_start = jnp.arange(nb, dtype=I32) * MOE_BLOCK
    block_e = jnp.minimum(jnp.sum((block_start[:, None] >= ends_p[None, :]).astype(I32), axis=1), N_EXPERTS - 1)
    n_used = (ends_p[-1:] // MOE_BLOCK).astype(I32)
    last_block = jnp.maximum(ends_p // MOE_BLOCK - 1, 0).astype(I32)

    pos_flat = pos.T.reshape(tp * TOP_K)
    xs = _dispatch(pos_flat, h1, _zero_blocks(last_block, n_slots))
    bg = exp_b_up[0][:, None, 0::2]
    bl = exp_b_up[0][:, None, 1::2]
    ys = _moe(block_e, n_used, xs, exp_w_up[0], exp_w_down[0], bg, bl, exp_b_down[0][:, None, :])

    return _combine(pos_flat, gates.T, h1, row(ln2_g[0]), row(ln2_b[0]), ys, bn, lp)
```

```python
import functools
import math

import jax
import jax.numpy as jnp
from jax import lax
from jax.experimental import pallas as pl
from jax.experimental.pallas import tpu as pltpu

F32 = jnp.float32
BF16 = jnp.bfloat16
I32 = jnp.int32

D_MODEL = 1024
N_META = 16
CHUNK = 64
N_FRONT = (-N_META) % CHUNK
GLA_HEADS = 4
GLA_DK = 64
GLA_DV = 128
GLA_KEY = GLA_HEADS * GLA_DK
GLA_WIDTH = GLA_HEADS * GLA_DV
GLA_GATE_RANK = 16
GLA_TAU = 16.0
GLA_COLS = 2 * GLA_KEY + 2 * GLA_WIDTH + 128
RWKV_WIDTH = 512
RWKV_HEAD = 64
RWKV_HEADS = RWKV_WIDTH // RWKV_HEAD
RWKV_W_RANK = 64
RWKV_A_RANK = 64
RWKV_G_RANK = 128
RWKV_COLS = 3 * RWKV_WIDTH + RWKV_W_RANK + RWKV_A_RANK + RWKV_G_RANK
N_EXPERTS = 32
TOP_K = 4
D_FF = D_MODEL
SWIGLU_ALPHA = 1.702
SWIGLU_LIMIT = 7.0
MOE_BLOCK = 640
DEPTH = 1
DEEPNORM_ALPHA = (2.0 * DEPTH) ** 0.25
LN_EPS = 1e-5
RWKV_LN_EPS = 64e-5
RMS_EPS = 1e-6

ROUTER_TILE = 768
DISPATCH_TILE = 1536
DMA_UNROLL = 4
LN_INPROJ_ROWS = 704
VMEM_LIMIT = 56 * 1024 * 1024


def _mm(a, b):
    return jnp.dot(a.astype(BF16), b.astype(BF16), preferred_element_type=F32)


def _layer_norm(x, g, b):
    mu = jnp.mean(x, axis=-1, keepdims=True)
    xc = x - mu
    var = jnp.mean(xc * xc, axis=-1, keepdims=True)
    return xc * lax.rsqrt(var + LN_EPS) * g + b


def _sigmoid(x):
    return 1.0 / (1.0 + jnp.exp(-x))


def _log_sigmoid(x):
    return jnp.minimum(x, 0.0) - jnp.log(1.0 + jnp.exp(-jnp.abs(x)))


ROW_TILE = (8, 128)


def _tiled_rows(n):
    return (n * ROW_TILE[0], ROW_TILE[1])


def _row_tile(ref, i):
    return ref.at[pl.ds(pl.multiple_of(i * ROW_TILE[0], ROW_TILE[0]), ROW_TILE[0])]


def _store_row_tiles(ref, x, row0=0):
    n = x.shape[0]
    for j in range(ROW_TILE[0]):
        ref[pl.ds(row0 * ROW_TILE[0] + j, n, stride=ROW_TILE[0]), :] = x[:, j * ROW_TILE[1]:(j + 1) * ROW_TILE[1]]


def _load_row_tiles(ref, row0=0, n=None):
    n = ref.shape[0] // ROW_TILE[0] if n is None else n
    return jnp.concatenate([ref[pl.ds(row0 * ROW_TILE[0] + j, n, stride=ROW_TILE[0]), :]
                            for j in range(ROW_TILE[0])], axis=1)


def _tri_masks(n):
    r = lax.broadcasted_iota(I32, (n, n), 0)
    c = lax.broadcasted_iota(I32, (n, n), 1)
    return r >= c, r > c, r == c


def _ln_inproj_kernel(x_ref, front_ref, g_ref, b_ref, w_ref, h_ref, pg_ref, pr_ref, *, tiles_per_seq):
    i = pl.program_id(0)
    first = i % tiles_per_seq == 0
    x = x_ref[0]
    x = jnp.where(first, jnp.concatenate([front_ref[...], x[:x.shape[0] - CHUNK]], axis=0), x)
    y = _layer_norm(x, g_ref[...], b_ref[...])
    row = lax.broadcasted_iota(I32, (y.shape[0], 1), 0)
    is_front = jnp.logical_and(first, row < N_FRONT)
    y = jnp.where(is_front, 0.0, y)
    h_ref[...] = y
    p = _mm(y, w_ref[...])
    pg_ref[...] = p[:, :GLA_COLS]
    pr_ref[...] = p[:, GLA_COLS:]


def _ln_inproj(x, front, g, b, w, lp):
    tp = x.shape[0] * lp
    tiles_per_seq = 1
    for cand in range(1, lp // 8 + 1):
        if lp % cand == 0 and (lp // cand) % 8 == 0 and lp // cand >= N_FRONT and lp // cand <= LN_INPROJ_ROWS:
            tiles_per_seq = cand
            break
    tm = lp // tiles_per_seq
    ncols = GLA_COLS + RWKV_COLS
    return pl.pallas_call(
        functools.partial(_ln_inproj_kernel, tiles_per_seq=tiles_per_seq),
        name="ln_inproj",
        grid=(tp // tm,),
        in_specs=[
            pl.BlockSpec((pl.Element(1), pl.Element(tm), pl.Element(D_MODEL)),
                         lambda i: (i // tiles_per_seq,
                                    pl.multiple_of(jnp.maximum((i % tiles_per_seq) * tm - CHUNK, 0), CHUNK), 0)),
            pl.BlockSpec((CHUNK, D_MODEL), lambda i: (0, 0)),
            pl.BlockSpec((1, D_MODEL), lambda i: (0, 0)),
            pl.BlockSpec((1, D_MODEL), lambda i: (0, 0)),
            pl.BlockSpec((D_MODEL, ncols), lambda i: (0, 0)),
        ],
        out_specs=[
            pl.BlockSpec((tm, D_MODEL), lambda i: (i, 0)),
            pl.BlockSpec((tm, GLA_COLS), lambda i: (i, 0)),
            pl.BlockSpec((tm, RWKV_COLS), lambda i: (i, 0)),
        ],
        out_shape=[
            jax.ShapeDtypeStruct((tp, D_MODEL), F32),
            jax.ShapeDtypeStruct((tp, GLA_COLS), F32),
            jax.ShapeDtypeStruct((tp, RWKV_COLS), F32),
        ],
        compiler_params=pltpu.CompilerParams(dimension_semantics=("arbitrary",),
                                             vmem_limit_bytes=VMEM_LIMIT),
    )(x, front, g, b, w)


GLA_GROUP = 11
GLA_SUB = 16
GLA_EXP_CAP = 80.0


def _gla_kernel(pg_ref, w2_ref, gkb_ref, ng_ref, o_ref, st_ref, qe_ref, oi_ref, kvt_ref, el_ref):
    c = pl.program_id(1)

    @pl.when(c == 0)
    def _():
        st_ref[...] = jnp.zeros_like(st_ref)

    rows_n = pg_ref.shape[1]
    ng = rows_n // CHUNK
    g_off = 2 * GLA_KEY + GLA_WIDTH
    p = pg_ref[0]
    gl3 = p[:, g_off + GLA_WIDTH:]
    gl_hi = gl3.astype(BF16)
    gl_lo = (gl3 - gl_hi.astype(F32)).astype(BF16)
    lane = lax.broadcasted_iota(I32, (1, gl3.shape[1]), 1)
    third = jnp.logical_and(lane >= 2 * GLA_GATE_RANK, lane < 3 * GLA_GATE_RANK)
    gate_pre = jnp.dot(jnp.where(third, gl_lo, gl_hi), w2_ref[...], preferred_element_type=F32)
    lg = _log_sigmoid(gate_pre + gkb_ref[...]) * (1.0 / GLA_TAU)
    row = lax.broadcasted_iota(I32, (rows_n, 1), 0)
    lg = jnp.where(jnp.logical_and(c == 0, row < N_FRONT), 0.0, lg)
    incl, _, _ = _tri_masks(CHUNK)
    tril = incl.astype(BF16)
    bc = jnp.concatenate([_mm_split3(tril, lg[i * CHUNK:(i + 1) * CHUNK]) for i in range(ng)], axis=0)
    g3 = lambda t: t.reshape(ng, CHUNK, t.shape[-1])
    bc = g3(bc)
    b_last = bc[:, CHUNK - 1:CHUNK, :]
    k = g3(p[:, GLA_KEY:2 * GLA_KEY])
    q = g3(p[:, 0:GLA_KEY]) * (GLA_DK ** -0.5)
    qe = q * jnp.exp(bc)
    n_sub = CHUNK // GLA_SUB
    sub_refs = [jnp.zeros_like(b_last)] + [bc[:, i * GLA_SUB - 1:i * GLA_SUB, :] for i in range(1, n_sub)]
    q_ref = jnp.concatenate([jnp.broadcast_to(r, (ng, GLA_SUB, GLA_KEY)) for r in sub_refs], axis=1)
    qs = (q * jnp.exp(bc - q_ref)).astype(BF16)
    ks_sub = []
    for i, r in enumerate(sub_refs):
        n_rows = (i + 1) * GLA_SUB
        scaled = (k[:, :n_rows] * jnp.exp(jnp.minimum(r - bc[:, :n_rows], GLA_EXP_CAP))).astype(BF16)
        if n_rows < CHUNK:
            scaled = jnp.concatenate([scaled, jnp.zeros((ng, CHUNK - n_rows, GLA_KEY), BF16)], axis=1)
        ks_sub.append(scaled)
    kl = k * jnp.exp(b_last - bc)
    e_last = jnp.exp(b_last)
    v = g3(p[:, 2 * GLA_KEY:g_off]).astype(BF16)
    for h in range(GLA_HEADS):
        ks = slice(h * GLA_DK, (h + 1) * GLA_DK)
        vs = slice(h * GLA_DV, (h + 1) * GLA_DV)
        qh = qe[:, :, ks].astype(BF16)
        a = jnp.concatenate(
            [jnp.einsum('gtd,gsd->gts', qs[:, i * GLA_SUB:(i + 1) * GLA_SUB, ks], ks_sub[i][:, :, ks],
                        preferred_element_type=F32) for i in range(n_sub)], axis=1)
        a = jnp.where(incl, a, 0.0).astype(BF16)
        oi_ref[:, h] = jnp.einsum('gts,gsv->gtv', a, v[:, :, vs], preferred_element_type=F32)
        kvt_ref[:, h] = jnp.einsum('gtv,gtd->gvd', v[:, :, vs], kl[:, :, ks].astype(BF16),
                                   preferred_element_type=F32)
        qe_ref[:, h] = qh
        el_ref[:, h] = e_last[:, :, ks]

    def chunk_body(ci, carry):
        st = st_ref[...]
        oi_ref[ci] = oi_ref[ci] + jnp.einsum('htd,hvd->htv', qe_ref[ci], st.astype(BF16),
                                             preferred_element_type=F32)
        st_ref[...] = st * el_ref[ci] + kvt_ref[ci]
        return carry

    lax.fori_loop(0, ng, chunk_body, 0)

    for ci in range(ng):
        o = oi_ref[ci]
        o = o * lax.rsqrt(jnp.mean(o * o, axis=-1, keepdims=True) + RMS_EPS) * ng_ref[...]
        o = jnp.concatenate([o[h] for h in range(GLA_HEADS)], axis=1)
        rows = slice(ci * CHUNK, (ci + 1) * CHUNK)
        gate = pg_ref[0, rows, g_off:g_off + GLA_WIDTH]
        o_ref[0, rows, :] = o * (gate * _sigmoid(gate))


def _gla(pg, w2p, gkb, ng_w):
    bn, lp, _ = pg.shape
    nc = lp // CHUNK
    ng = max(d for d in range(1, GLA_GROUP + 1) if nc % d == 0)
    rows = ng * CHUNK
    hh = GLA_HEADS
    return pl.pallas_call(
        _gla_kernel,
        name="gla_mixer",
        grid=(bn, nc // ng),
        in_specs=[
            pl.BlockSpec((1, rows, GLA_COLS), lambda b, c: (b, c, 0)),
            pl.BlockSpec((128, GLA_KEY), lambda b, c: (0, 0)),
            pl.BlockSpec((1, GLA_KEY), lambda b, c: (0, 0)),
            pl.BlockSpec((1, GLA_DV), lambda b, c: (0, 0)),
        ],
        out_specs=pl.BlockSpec((1, rows, GLA_WIDTH), lambda b, c: (b, c, 0)),
        out_shape=jax.ShapeDtypeStruct((bn, lp, GLA_WIDTH), F32),
        scratch_shapes=[
            pltpu.VMEM((hh, GLA_DV, GLA_DK), F32),
            pltpu.VMEM((ng, hh, CHUNK, GLA_DK), BF16),
            pltpu.VMEM((ng, hh, CHUNK, GLA_DV), F32),
            pltpu.VMEM((ng, hh, GLA_DV, GLA_DK), F32),
            pltpu.VMEM((ng, hh, 1, GLA_DK), F32),
        ],
        compiler_params=pltpu.CompilerParams(dimension_semantics=("arbitrary", "arbitrary"),
                                             vmem_limit_bytes=VMEM_LIMIT),
    )(pg, w2p, gkb, ng_w)


RWKV_GROUP = 11
RWKV_SLAB = 256
def _mm_split3(ones_bf16, x):
    hi = x.astype(BF16)
    r1 = x - hi.astype(F32)
    mid = r1.astype(BF16)
    lo = (r1 - mid.astype(F32)).astype(BF16)
    return jnp.dot(jnp.concatenate([ones_bf16] * 3, axis=1), jnp.concatenate([hi, mid, lo], axis=0),
                   preferred_element_type=F32)


def _mm_split2_rhs_ones(x, ones_bf16):
    hi = x.astype(BF16)
    mid = (x - hi.astype(F32)).astype(BF16)
    dot = lambda t: jnp.dot(t, ones_bf16, preferred_element_type=F32)
    return dot(hi) + dot(mid)


def _rwkv_kernel(pr_ref, pv_ref, mu_ref, w0_ref, w2_ref, a0_ref, a2_ref, g2_ref, kk_ref, ka_ref,
                 rk_ref, lng_ref, lnb_ref, seg_ref, o_ref, s_ref, ops_ref, wend_ref, rm_ref, yn_ref,
                 y_ref, bonus_ref, gate_ref):
    c = pl.program_id(1)
    rows_n = pr_ref.shape[1]
    ng = rows_n // CHUNK

    @pl.when(c == 0)
    def _():
        s_ref[...] = jnp.zeros_like(s_ref)

    p = pr_ref[0]
    prev_row = jnp.where(c > 0, pv_ref[0][7:8, :], 0.0)
    rolled = pltpu.roll(p, 1, 0)
    row8 = lax.broadcasted_iota(I32, (8, 1), 0)
    prev = jnp.concatenate([jnp.where(row8 == 0, prev_row, rolled[:8]), rolled[8:]], axis=0)
    p = p + (prev - p) * mu_ref[...]
    W = RWKV_WIDTH
    r = p[:, 0:W]
    k = p[:, W:2 * W]
    v = p[:, 2 * W:3 * W]
    w_low = p[:, 3 * W:3 * W + RWKV_W_RANK]
    a_low = p[:, 3 * W + RWKV_W_RANK:3 * W + RWKV_W_RANK + RWKV_A_RANK]
    g_low = p[:, 3 * W + RWKV_W_RANK + RWKV_A_RANK:]
    wx = w0_ref[...] + _mm(jnp.tanh(w_low), w2_ref[...])
    logd = _sigmoid(wx) * (-math.exp(-0.5))
    a = _sigmoid(a0_ref[...] + _mm(a_low, a2_ref[...]))
    g = _mm(_sigmoid(g_low), g2_ref[...])
    seg = seg_ref[...]
    kk = k * kk_ref[...]
    kk = kk * lax.rsqrt(jnp.maximum(_mm_split2_rhs_ones(kk * kk, seg), 1e-24))
    k = k * (1.0 + (a - 1.0) * ka_ref[...])
    bonus = _mm_split2_rhs_ones(r * k * rk_ref[...], seg) * v

    bonus_ref[...] = bonus
    gate_ref[...] = g

    incl, strict, diag = _tri_masks(CHUNK)
    tril = incl.astype(BF16)
    c_in = jnp.concatenate([_mm_split3(tril, logd[i * CHUNK:(i + 1) * CHUNK]) for i in range(ng)], axis=0)
    g3 = lambda t: t.reshape(ng, CHUNK, W)
    logd, c_in, r, k, v, kk, a = g3(logd), g3(c_in), g3(r), g3(k), g3(v), g3(kk), g3(a)
    c_last = c_in[:, CHUNK - 1:CHUNK, :]
    e_neg = jnp.exp(-c_in)
    e_end = jnp.exp(c_last - c_in)
    kka = kk * a
    per_head = (-kk * jnp.exp(c_in - logd), r * jnp.exp(c_in), kka * e_neg, k * e_neg, kka * e_end,
                k * e_end, v)
    GW = RWKV_SLAB
    n_slab = W // GW
    for i, t in enumerate(per_head):
        t = t.astype(BF16)
        for sl in range(n_slab):
            ops_ref[i, sl] = t[:, :, sl * GW:(sl + 1) * GW]
    w_end = jnp.exp(c_last)
    for sl in range(n_slab):
        wend_ref[sl] = w_end[:, :, sl * GW:(sl + 1) * GW]

    hpl = GW // RWKV_HEAD
    lane_head = lax.broadcasted_iota(I32, (1, 1, GW), 2) // RWKV_HEAD
    head_masks = [lane_head == h for h in range(hpl)]
    rr = lax.broadcasted_iota(I32, (CHUNK, GW), 0)
    cc = lax.broadcasted_iota(I32, (CHUNK, GW), 1) % RWKV_HEAD
    incl4, strict4, eye4 = rr >= cc, rr > cc, (rr == cc).astype(F32)

    def block_diag(x):
        x = x.astype(BF16)
        return jnp.concatenate([jnp.where(m, x, jnp.zeros_like(x)) for m in head_masks], axis=1)

    def head_blocks(full):
        out = jnp.where(head_masks[0], full[:, :RWKV_HEAD], 0.0)
        for h in range(1, hpl):
            out = out + jnp.where(head_masks[h], full[:, h * RWKV_HEAD:(h + 1) * RWKV_HEAD], 0.0)
        return out

    bmm = lambda x, y: jnp.einsum('gts,gsd->gtd', x.astype(BF16), y.astype(BF16), preferred_element_type=F32)
    bmm_nt = lambda x, y: jnp.einsum('gtd,gsd->gts', x.astype(BF16), y.astype(BF16), preferred_element_type=F32)
    bmm_tn = lambda x, y: jnp.einsum('gtk,gtd->gkd', x.astype(BF16), y.astype(BF16), preferred_element_type=F32)

    def slab_body(sl, carry):
        at, rt, bt, kt, bh, kh, vv = [ops_ref[i, sl] for i in range(7)]
        gm = bmm_nt(jnp.concatenate([at, rt], axis=1),
                    jnp.concatenate([block_diag(bt), block_diag(kt)], axis=1))
        a_ab = jnp.where(strict4, gm[:, :CHUNK, :GW], 0.0)
        a_ak = jnp.where(strict4, gm[:, :CHUNK, GW:], 0.0)
        a_rb = jnp.where(incl4, gm[:, CHUNK:, :GW], 0.0)
        a_rk = jnp.where(incl4, gm[:, CHUNK:, GW:], 0.0)
        akv = bmm(jnp.concatenate([a_ak, a_rk], axis=1), block_diag(vv))
        xk = eye4 + a_ab
        pk = bmm(a_ab, block_diag(a_ab))
        for _ in range(4):
            both = bmm(jnp.concatenate([xk, pk], axis=1), block_diag(pk))
            xk = xk + both[:, :CHUNK]
            pk = both[:, CHUNK:]
        xk = xk + bmm(xk, block_diag(pk))
        pq = bmm(xk, jnp.concatenate([block_diag(at), block_diag(akv[:, :CHUNK])], axis=2))
        p_bd, q_bd = block_diag(pq[:, :, :GW]), block_diag(pq[:, :, GW:])
        ry = bmm(a_rb, jnp.concatenate([p_bd, q_bd], axis=2)) + jnp.concatenate(
            [rt.astype(F32), akv[:, CHUNK:]], axis=2)
        m_sbs = head_blocks(bmm_tn(bh, pq[:, :, :GW])) + eye4 * wend_ref[sl]
        n_sbs = head_blocks(bmm_tn(jnp.concatenate([bh, kh], axis=1),
                                   jnp.concatenate([pq[:, :, GW:].astype(BF16), vv], axis=1)))
        rm_ref[sl] = jnp.concatenate([ry[:, :, :GW], m_sbs], axis=1).astype(BF16)
        yn_ref[sl] = jnp.concatenate([ry[:, :, GW:], n_sbs], axis=1)
        return carry

    lax.fori_loop(0, n_slab, slab_body, 0)

    def chunk_body(ci, carry):
        for sl in range(n_slab):
            lanes = slice(sl * GW, (sl + 1) * GW)
            st = s_ref[:, lanes]
            st_bd = jnp.concatenate([jnp.where(m[0], st, 0.0) for m in head_masks], axis=0).astype(BF16)
            res = jnp.dot(rm_ref[sl, ci], st_bd, preferred_element_type=F32) + yn_ref[sl, ci]
            y_ref[ci, :, lanes] = res[:CHUNK]
            s_ref[:, lanes] = res[CHUNK:]
        return carry

    lax.fori_loop(0, ng, chunk_body, 0)

    y = y_ref[...].reshape(rows_n, W)
    mean = _mm_split2_rhs_ones(y, seg) * (1.0 / RWKV_HEAD)
    yc = y - mean
    var = _mm_split2_rhs_ones(yc * yc, seg) * (1.0 / RWKV_HEAD)
    yn = yc * lax.rsqrt(var + RWKV_LN_EPS)
    o_ref[0] = (yn * lng_ref[...] + lnb_ref[...] + bonus_ref[...]) * gate_ref[...]


def _rwkv(pr, mu, w0, w2, a0, a2, g2, k_k, k_a, r_k, ln_g, ln_b, seg):
    bn, lp, _ = pr.shape
    nc = lp // CHUNK
    ng = max(d for d in range(1, RWKV_GROUP + 1) if nc % d == 0)
    rows = ng * CHUNK
    vec = lambda n: pl.BlockSpec((1, n), lambda b, c: (0, 0))
    mat = lambda m, n: pl.BlockSpec((m, n), lambda b, c: (0, 0))
    hd = RWKV_HEAD
    n_slab = RWKV_WIDTH // RWKV_SLAB
    return pl.pallas_call(
        _rwkv_kernel,
        name="rwkv_mixer",
        grid=(bn, nc // ng),
        in_specs=[
            pl.BlockSpec((1, rows, RWKV_COLS), lambda b, c: (b, c, 0)),
            pl.BlockSpec((1, 8, RWKV_COLS), lambda b, c: (b, jnp.maximum(c * (rows // 8) - 1, 0), 0)),
            vec(RWKV_COLS), vec(RWKV_WIDTH), mat(RWKV_W_RANK, RWKV_WIDTH), vec(RWKV_WIDTH),
            mat(RWKV_A_RANK, RWKV_WIDTH), mat(RWKV_G_RANK, RWKV_WIDTH), vec(RWKV_WIDTH), vec(RWKV_WIDTH),
            vec(RWKV_WIDTH), vec(RWKV_WIDTH), vec(RWKV_WIDTH), mat(RWKV_WIDTH, RWKV_WIDTH),
        ],
        out_specs=pl.BlockSpec((1, rows, RWKV_WIDTH), lambda b, c: (b, c, 0)),
        out_shape=jax.ShapeDtypeStruct((bn, lp, RWKV_WIDTH), F32),
        scratch_shapes=[
            pltpu.VMEM((hd, RWKV_WIDTH), F32),
            pltpu.VMEM((7, n_slab, ng, CHUNK, RWKV_SLAB), BF16),
            pltpu.VMEM((n_slab, ng, 1, RWKV_SLAB), F32),
            pltpu.VMEM((n_slab, ng, 2 * CHUNK, RWKV_SLAB), BF16),
            pltpu.VMEM((n_slab, ng, 2 * CHUNK, RWKV_SLAB), F32),
            pltpu.VMEM((ng, CHUNK, RWKV_WIDTH), F32),
            pltpu.VMEM((rows, RWKV_WIDTH), F32),
            pltpu.VMEM((rows, RWKV_WIDTH), F32),
        ],
        compiler_params=pltpu.CompilerParams(dimension_semantics=("arbitrary", "arbitrary"),
                                             vmem_limit_bytes=VMEM_LIMIT),
    )(pr, pr, mu, w0, w2, a0, a2, g2, k_k, k_a, r_k, ln_g, ln_b, seg.astype(BF16))


def _outproj_router_kernel(yg_ref, yr_ref, h_ref, wo_ref, g_ref, b_ref, rwt_ref, rb_ref, tri_ref,
                           h1_ref, idx_ref, gate_ref, rank_ref, cnt_ref, base_ref):
    i = pl.program_id(0)

    @pl.when(i == 0)
    def _():
        base_ref[...] = jnp.zeros_like(base_ref)

    wo = wo_ref[...]
    mix = _mm(yg_ref[...], wo[:GLA_WIDTH]) + _mm(yr_ref[...], wo[GLA_WIDTH:])
    h1 = _layer_norm(DEEPNORM_ALPHA * h_ref[...] + mix, g_ref[...], b_ref[...])
    _store_row_tiles(h1_ref, h1)
    h_hi = h1.astype(BF16)
    h_lo = (h1 - h_hi.astype(F32)).astype(BF16)
    nt = (((1,), (1,)), ((), ()))
    part = lax.dot_general(rwt_ref[...], h_hi, nt, preferred_element_type=F32)
    work = (part[:N_EXPERTS] + part[N_EXPERTS:]
            + lax.dot_general(rwt_ref[:N_EXPERTS, :], h_lo, nt, preferred_element_type=F32)
            + rb_ref[...][:, 0:1])
    tm = work.shape[1]
    e_iota = lax.broadcasted_iota(I32, (N_EXPERTS, tm), 0)
    base = base_ref[...][:, 0:1]
    vals, onehots = [], []
    for kk in range(TOP_K):
        m = jnp.max(work, axis=0, keepdims=True)
        sel = jnp.min(jnp.where(work == m, e_iota, N_EXPERTS), axis=0, keepdims=True)
        onehot = e_iota == sel
        work = jnp.where(onehot, -jnp.inf, work)
        vals.append(m)
        onehots.append(onehot.astype(F32))
        idx_ref[kk:kk + 1, :] = sel
    cnt_all = jnp.dot(jnp.concatenate(onehots, axis=0).astype(BF16), tri_ref[...], preferred_element_type=F32)
    prior = jnp.zeros((N_EXPERTS, 1), F32)
    for kk in range(TOP_K):
        cnt = cnt_all[kk * N_EXPERTS:(kk + 1) * N_EXPERTS]
        rank = jnp.sum(onehots[kk] * (base + prior + cnt - 1.0), axis=0, keepdims=True)
        prior = prior + cnt[:, tm - 1:tm]
        rank_ref[kk:kk + 1, :] = rank.astype(I32)
    es = [jnp.exp(vv - vals[0]) for vv in vals]
    den = es[0] + es[1] + es[2] + es[3]
    for kk in range(TOP_K):
        gate_ref[kk:kk + 1, :] = es[kk] / den
    new_base = base + prior
    base_ref[...] = jnp.broadcast_to(new_base, base_ref.shape)
    cnt_ref[...] = jnp.broadcast_to(new_base, cnt_ref.shape)


def _outproj_router(yg, yr, h, wo, g, b, rwt, rb, tri):
    tp = h.shape[0]
    tm = ROUTER_TILE
    const = lambda m, n: pl.BlockSpec((m, n), lambda i: (0, 0))
    return pl.pallas_call(
        _outproj_router_kernel,
        name="outproj_router",
        grid=(tp // tm,),
        in_specs=[
            pl.BlockSpec((tm, GLA_WIDTH), lambda i: (i, 0)),
            pl.BlockSpec((tm, RWKV_WIDTH), lambda i: (i, 0)),
            pl.BlockSpec((tm, D_MODEL), lambda i: (i, 0)),
            const(D_MODEL, D_MODEL), const(1, D_MODEL), const(1, D_MODEL),
            const(2 * N_EXPERTS, D_MODEL), const(N_EXPERTS, 128), const(tm, tm),
        ],
        out_specs=[
            pl.BlockSpec(_tiled_rows(tm), lambda i: (i, 0)),
            pl.BlockSpec((TOP_K, tm), lambda i: (0, i)),
            pl.BlockSpec((TOP_K, tm), lambda i: (0, i)),
            pl.BlockSpec((TOP_K, tm), lambda i: (0, i)),
            pl.BlockSpec((N_EXPERTS, 128), lambda i: (0, 0)),
        ],
        out_shape=[
            jax.ShapeDtypeStruct(_tiled_rows(tp), F32),
            jax.ShapeDtypeStruct((TOP_K, tp), I32),
            jax.ShapeDtypeStruct((TOP_K, tp), F32),
            jax.ShapeDtypeStruct((TOP_K, tp), I32),
            jax.ShapeDtypeStruct((N_EXPERTS, 128), F32),
        ],
        scratch_shapes=[pltpu.VMEM((N_EXPERTS, 128), F32)],
        compiler_params=pltpu.CompilerParams(dimension_semantics=("arbitrary",),
                                             vmem_limit_bytes=VMEM_LIMIT),
    )(yg, yr, h, wo, g, b, rwt, rb, tri)


def _zero_blocks_kernel(last_ref, o_ref):
    o_ref[...] = jnp.zeros_like(o_ref)


def _zero_blocks(last_block, n_slots):
    return pl.pallas_call(
        _zero_blocks_kernel,
        name="moe_zero_blocks",
        grid_spec=pltpu.PrefetchScalarGridSpec(
            num_scalar_prefetch=1,
            grid=(N_EXPERTS,),
            in_specs=[],
            out_specs=pl.BlockSpec(_tiled_rows(MOE_BLOCK), lambda e, last: (last[e], 0)),
        ),
        out_shape=jax.ShapeDtypeStruct(_tiled_rows(n_slots), F32),
        compiler_params=pltpu.CompilerParams(dimension_semantics=("arbitrary",)),
    )(last_block)


def _dispatch_kernel(pos_ref, x_ref, xs_in_ref, xs_ref, sem):
    del xs_in_ref
    tm = pos_ref.shape[0] // TOP_K

    def start(r, carry):
        for kk in range(TOP_K):
            pltpu.make_async_copy(_row_tile(x_ref, r), _row_tile(xs_ref, pos_ref[r * TOP_K + kk]),
                                  sem).start(priority=kk % 2)
        return carry

    lax.fori_loop(0, tm, start, 0, unroll=DMA_UNROLL)
    for kk in range(TOP_K):
        pltpu.make_async_copy(x_ref, xs_ref.at[pl.ds(0, x_ref.shape[0])], sem).wait()


def _dispatch(pos, x, xs):
    tp = pos.shape[0] // TOP_K
    tm = DISPATCH_TILE if tp % DISPATCH_TILE == 0 else ROUTER_TILE
    return pl.pallas_call(
        _dispatch_kernel,
        name="moe_dispatch",
        grid=(tp // tm,),
        in_specs=[
            pl.BlockSpec((tm * TOP_K,), lambda i: (i,), memory_space=pltpu.SMEM),
            pl.BlockSpec(_tiled_rows(tm), lambda i: (i, 0)),
            pl.BlockSpec(memory_space=pl.ANY),
        ],
        out_specs=pl.BlockSpec(memory_space=pl.ANY),
        out_shape=jax.ShapeDtypeStruct(xs.shape, xs.dtype),
        scratch_shapes=[pltpu.SemaphoreType.DMA(())],
        input_output_aliases={2: 0},
        compiler_params=pltpu.CompilerParams(dimension_semantics=("arbitrary",)),
    )(pos, x, xs)


SPLIT_TILE = 256
def _moe_kernel(be_ref, nu_ref, first_ref, slot_ref, nxt_ref, xs_ref, wu_hbm, wd_hbm, perm_ref, bg_ref,
                bl_ref, bd_ref, ys_ref, wu_buf, wd_buf, wsem, wg_s, wl_s, wd_s):
    i = pl.program_id(0)

    def weight_copies(e, sl):
        return (pltpu.make_async_copy(wu_hbm.at[e], wu_buf.at[sl], wsem.at[0, sl]),
                pltpu.make_async_copy(wd_hbm.at[e], wd_buf.at[sl], wsem.at[1, sl]))

    @pl.when(i < nu_ref[0])
    def _():
        @pl.when(first_ref[i] == 1)
        def _():
            sl = slot_ref[i]

            @pl.when(i == 0)
            def _():
                for c in weight_copies(be_ref[0], sl):
                    c.start()

            for c in weight_copies(be_ref[i], sl):
                c.wait()

            @pl.when(nxt_ref[i] >= 0)
            def _():
                for c in weight_copies(nxt_ref[i], 1 - sl):
                    c.start()

            half = SPLIT_TILE // 2
            for t in range(wu_buf.shape[2] // SPLIT_TILE):
                d = jnp.dot(wu_buf[sl, :, t * SPLIT_TILE:(t + 1) * SPLIT_TILE].astype(BF16), perm_ref[...],
                            preferred_element_type=F32).astype(BF16)
                wg_s[:, t * half:(t + 1) * half] = d[:, :half]
                wl_s[:, t * half:(t + 1) * half] = d[:, half:]
            wd_s[...] = wd_buf[sl].astype(BF16)

        x = _load_row_tiles(xs_ref).astype(BF16)
        x_glu = jnp.dot(x, wg_s[...], preferred_element_type=F32) + bg_ref[0]
        x_lin = jnp.dot(x, wl_s[...], preferred_element_type=F32) + bl_ref[0]
        x_glu = jnp.minimum(x_glu, SWIGLU_LIMIT)
        x_lin = jnp.clip(x_lin, -SWIGLU_LIMIT, SWIGLU_LIMIT)
        act = x_glu * _sigmoid(SWIGLU_ALPHA * x_glu) * (x_lin + 1.0)
        _store_row_tiles(ys_ref, jnp.dot(act.astype(BF16), wd_s[...], preferred_element_type=F32) + bd_ref[0])


def _moe(block_e, n_used, xs, w_up, w_down, bg, bl, bd):
    n_slots = xs.shape[0] // ROW_TILE[0]
    nb = n_slots // MOE_BLOCK
    src = jnp.arange(SPLIT_TILE, dtype=I32)[:, None]
    dst = jnp.arange(SPLIT_TILE, dtype=I32)[None, :]
    half = SPLIT_TILE // 2
    perm = (src == jnp.where(dst < half, 2 * dst, 2 * (dst - half) + 1)).astype(BF16)
    ids = jnp.arange(nb, dtype=I32)
    first = jnp.logical_and(ids < n_used[0], jnp.logical_or(ids == 0, block_e != jnp.roll(block_e, 1)))
    slot = (jnp.cumsum(first.astype(I32)) - 1) % 2
    first_at = jnp.where(first, ids, nb)
    next_first = jnp.concatenate([lax.cummin(first_at[::-1])[::-1][1:], jnp.full((1,), nb, I32)])
    nxt_e = jnp.where(next_first < nb, block_e[jnp.minimum(next_first, nb - 1)], -1).astype(I32)
    blk = lambda i, be, nu, *_: (jnp.maximum(jnp.minimum(i, nu[0] - 1), 0), 0)
    bspec = pl.BlockSpec((1, 1, D_FF), lambda i, be, *_: (be[i], 0, 0))
    return pl.pallas_call(
        _moe_kernel,
        name="moe_experts",
        grid_spec=pltpu.PrefetchScalarGridSpec(
            num_scalar_prefetch=5,
            grid=(nb,),
            in_specs=[
                pl.BlockSpec(_tiled_rows(MOE_BLOCK), blk),
                pl.BlockSpec(memory_space=pl.ANY),
                pl.BlockSpec(memory_space=pl.ANY),
                pl.BlockSpec((SPLIT_TILE, SPLIT_TILE), lambda i, *_: (0, 0)),
                bspec, bspec, bspec,
            ],
            out_specs=pl.BlockSpec(_tiled_rows(MOE_BLOCK), blk),
            scratch_shapes=[
                pltpu.VMEM((2, D_MODEL, 2 * D_FF), F32),
                pltpu.VMEM((2, D_FF, D_MODEL), F32),
                pltpu.SemaphoreType.DMA((2, 2)),
                pltpu.VMEM((D_MODEL, D_FF), BF16),
                pltpu.VMEM((D_MODEL, D_FF), BF16),
                pltpu.VMEM((D_FF, D_MODEL), BF16),
            ],
        ),
        out_shape=jax.ShapeDtypeStruct(_tiled_rows(n_slots), F32),
        compiler_params=pltpu.CompilerParams(dimension_semantics=("arbitrary",),
                                             vmem_limit_bytes=VMEM_LIMIT),
    )(block_e, n_used, first.astype(I32), slot.astype(I32), nxt_e, xs, w_up, w_down, perm, bg, bl, bd)


COMBINE_SLOTS = 3
COMBINE_CHUNKS = 2


def _combine_kernel(*refs, cps):
    first_pos = refs[0:2]
    ahead_pos = refs[2:2 + cps]
    gts = refs[2 + cps:2 + 2 * cps]
    h1s = refs[2 + 2 * cps:2 + 3 * cps]
    g_ref, b_ref, ys_ref, o_ref, buf, sem = refs[2 + 3 * cps:]
    s = pl.program_id(0)
    rows = TOP_K * CHUNK

    def issue_row(p_ref, to_slot, r):
        for kk in range(TOP_K):
            pltpu.make_async_copy(_row_tile(ys_ref, p_ref[r * TOP_K + kk]),
                                  _row_tile(buf, to_slot * rows + kk * CHUNK + r),
                                  sem.at[to_slot]).start(priority=kk % 2)

    def wait_slot(which):
        span = rows * ROW_TILE[0]
        pltpu.make_async_copy(ys_ref.at[pl.ds(0, span)], buf.at[pl.ds(which * span, span)], sem.at[which]).wait()

    @pl.when(s == 0)
    def _():
        def body(r, carry):
            issue_row(first_pos[0], 0, r)
            issue_row(first_pos[1], 1, r)
            return carry

        lax.fori_loop(0, CHUNK, body, 0, unroll=DMA_UNROLL)

    def reduce_chunk(j, cur):
        ahead = (cur + 2) % COMBINE_SLOTS
        wait_slot(cur)
        gt = gts[j][...]
        quarter = CHUNK // TOP_K
        ffn = None
        for kk in range(TOP_K):
            part = _load_row_tiles(buf, cur * rows + kk * CHUNK, CHUNK) * gt[:, kk:kk + 1]
            ffn = part if ffn is None else ffn + part
            for r in range(kk * quarter, (kk + 1) * quarter):
                issue_row(ahead_pos[j], ahead, r)
        o_ref[0, j * CHUNK:(j + 1) * CHUNK, :] = _layer_norm(
            DEEPNORM_ALPHA * _load_row_tiles(h1s[j]) + ffn, g_ref[...], b_ref[...])

    def step(first_slot):
        for j in range(cps):
            reduce_chunk(j, (first_slot + j) % COMBINE_SLOTS)

        @pl.when(s == pl.num_programs(0) - 1)
        def _():
            last = (first_slot + cps - 1) % COMBINE_SLOTS
            wait_slot((last + 1) % COMBINE_SLOTS)
            wait_slot((last + 2) % COMBINE_SLOTS)

    for first_slot in range(COMBINE_SLOTS):
        pl.when((s * cps) % COMBINE_SLOTS == first_slot)(functools.partial(step, first_slot))


def _combine(pos_flat, gates_t, h1, g, b, ys, bn, lp):
    seq = lp - CHUNK
    nc = lp // CHUNK
    ncs = seq // CHUNK
    cps = COMBINE_CHUNKS if ncs % COMBINE_CHUNKS == 0 else 1
    n_chunks = bn * ncs
    chunk_of = lambda m: (m // ncs) * nc + m % ncs + 1
    clamp = lambda m: jnp.minimum(m, n_chunks - 1)
    pos_spec = lambda f: pl.BlockSpec((TOP_K * CHUNK,), lambda s: (chunk_of(f(s)),), memory_space=pltpu.SMEM)
    per_chunk = lambda shape: [pl.BlockSpec(shape, lambda s, j=j: (chunk_of(s * cps + j), 0)) for j in range(cps)]
    return pl.pallas_call(
        functools.partial(_combine_kernel, cps=cps),
        name="moe_combine",
        grid=(n_chunks // cps,),
        in_specs=[pos_spec(lambda s: 0 * s), pos_spec(lambda s: 0 * s + 1)]
        + [pos_spec(lambda s, j=j: clamp(s * cps + j + 2)) for j in range(cps)]
        + per_chunk((CHUNK, TOP_K)) + per_chunk(_tiled_rows(CHUNK))
        + [
            pl.BlockSpec((1, D_MODEL), lambda s: (0, 0)),
            pl.BlockSpec((1, D_MODEL), lambda s: (0, 0)),
            pl.BlockSpec(memory_space=pl.ANY),
        ],
        out_specs=pl.BlockSpec((1, cps * CHUNK, D_MODEL), lambda s: (s // (ncs // cps), s % (ncs // cps), 0)),
        out_shape=jax.ShapeDtypeStruct((bn, seq, D_MODEL), F32),
        scratch_shapes=[pltpu.VMEM(_tiled_rows(COMBINE_SLOTS * TOP_K * CHUNK), F32),
                        pltpu.SemaphoreType.DMA((COMBINE_SLOTS,))],
        compiler_params=pltpu.CompilerParams(dimension_semantics=("arbitrary",)),
    )(*([pos_flat] * (2 + cps) + [gates_t] * cps + [h1] * cps + [g, b, ys]))


def kernel(x, meta, ln_in_g, ln_in_b, w_in, gla_gk_w2, gla_gk_b, gla_norm_g, rwkv_mu, rwkv_w0, rwkv_w2, rwkv_a0, rwkv_a2, rwkv_g2, rwkv_k_k, rwkv_k_a, rwkv_r_k, rwkv_ln_g, rwkv_ln_b, w_out, ln1_g, ln1_b, router_w, router_b, exp_w_up, exp_b_up, exp_w_down, exp_b_down, ln2_g, ln2_b):
    bn, seq, _ = x.shape
    assert seq % CHUNK == 0
    lp = seq + CHUNK
    tp = bn * lp
    assert tp % ROUTER_TILE == 0
    row = lambda t: t.reshape(1, -1).astype(F32)

    front = jnp.concatenate([jnp.zeros((N_FRONT, D_MODEL), F32), meta.astype(F32)], axis=0)
    gla_in = 2 * GLA_KEY + 2 * GLA_WIDTH + GLA_GATE_RANK
    w = w_in[0]
    w_gk = w[:, gla_in - GLA_GATE_RANK:gla_in]
    w_cols = jnp.concatenate([w[:, :gla_in], w_gk, w_gk, jnp.zeros((D_MODEL, 128 - 3 * GLA_GATE_RANK), F32),
                              w[:, gla_in:]], axis=1).astype(BF16)
    h, pg, pr = _ln_inproj(x, front, row(ln_in_g), row(ln_in_b), w_cols, lp)

    w2_hi = gla_gk_w2[0].astype(BF16)
    w2_lo = (gla_gk_w2[0] - w2_hi.astype(F32)).astype(BF16)
    w2p = jnp.concatenate([w2_hi, w2_lo, w2_hi, jnp.zeros((128 - 3 * GLA_GATE_RANK, GLA_KEY), BF16)], axis=0)
    y_gla = _gla(pg.reshape(bn, lp, GLA_COLS), w2p, row(gla_gk_b[0]), row(gla_norm_g[0]))

    head_id = jnp.arange(RWKV_WIDTH, dtype=I32) // RWKV_HEAD
    seg = (head_id[:, None] == head_id[None, :]).astype(F32)
    y_rwkv = _rwkv(pr.reshape(bn, lp, RWKV_COLS), row(rwkv_mu[0]), row(rwkv_w0[0]), rwkv_w2[0],
                   row(rwkv_a0[0]), rwkv_a2[0], rwkv_g2[0], row(rwkv_k_k[0]), row(rwkv_k_a[0]),
                   row(rwkv_r_k[0]), row(rwkv_ln_g[0]), row(rwkv_ln_b[0]), seg)

    tri = jnp.triu(jnp.ones((ROUTER_TILE, ROUTER_TILE), F32)).astype(BF16)
    rb = jnp.broadcast_to(router_b[0].reshape(N_EXPERTS, 1), (N_EXPERTS, 128))
    rw_t = router_w[0].T
    rw_hi = rw_t.astype(BF16)
    rwt = jnp.concatenate([rw_hi, (rw_t - rw_hi.astype(F32)).astype(BF16)], axis=0)
    h1, idx, gates, rank, cnt = _outproj_router(
        y_gla.reshape(tp, GLA_WIDTH), y_rwkv.reshape(tp, RWKV_WIDTH), h, w_out[0].astype(BF16),
        row(ln1_g[0]), row(ln1_b[0]), rwt, rb, tri)

    counts = cnt[:, 0].astype(I32)
    padded = (counts + MOE_BLOCK - 1) // MOE_BLOCK * MOE_BLOCK
    ends_p = jnp.cumsum(padded)
    starts_p = ends_p - padded
    e_ids = jnp.arange(N_EXPERTS, dtype=I32)
    start_of = jnp.sum(jnp.where(idx[None] == e_ids[:, None, None], starts_p[:, None, None], 0), axis=0)
    pos = start_of + rank
    nb = tp * TOP_K // MOE_BLOCK + N_EXPERTS
    n_slots = nb * MOE_BLOCK
    block_start = jnp.arange(nb, dtype=I32) * MOE_BLOCK
    block_e = jnp.minimum(jnp.sum((block_start[:, None] >= ends_p[None, :]).astype(I32), axis=1), N_EXPERTS - 1)
    n_used = (ends_p[-1:] // MOE_BLOCK).astype(I32)
    last_block = jnp.maximum(ends_p // MOE_BLOCK - 1, 0).astype(I32)

    pos_flat = pos.T.reshape(tp * TOP_K)
    xs = _dispatch(pos_flat, h1, _zero_blocks(last_block, n_slots))
    bg = exp_b_up[0][:, None, 0::2]
    bl = exp_b_up[0][:, None, 1::2]
    ys = _moe(block_e, n_used, xs, exp_w_up[0], exp_w_down[0], bg, bl, exp_b_down[0][:, None, :])

    return _combine(pos_flat, gates.T, h1, row(ln2_g[0]), row(ln2_b[0]), ys, bn, lp)
```

```python
import functools
import math

import jax
import jax.numpy as jnp
from jax import lax
from jax.experimental import pallas as pl
from jax.experimental.pallas import tpu as pltpu

F32 = jnp.float32
BF16 = jnp.bfloat16
I32 = jnp.int32

D_MODEL = 1024
N_META = 16
CHUNK = 64
N_FRONT = (-N_META) % CHUNK
GLA_HEADS = 4
GLA_DK = 64
GLA_DV = 128
GLA_KEY = GLA_HEADS * GLA_DK
GLA_WIDTH = GLA_HEADS * GLA_DV
GLA_GATE_RANK = 16
GLA_TAU = 16.0
GLA_COLS = 2 * GLA_KEY + 2 * GLA_WIDTH + 128
RWKV_WIDTH = 512
RWKV_HEAD = 64
RWKV_HEADS = RWKV_WIDTH // RWKV_HEAD
RWKV_W_RANK = 64
RWKV_A_RANK = 64
RWKV_G_RANK = 128
RWKV_COLS = 3 * RWKV_WIDTH + RWKV_W_RANK + RWKV_A_RANK + RWKV_G_RANK
N_EXPERTS = 32
TOP_K = 4
D_FF = D_MODEL
SWIGLU_ALPHA = 1.702
SWIGLU_LIMIT = 7.0
MOE_BLOCK = 512
DEPTH = 1
DEEPNORM_ALPHA = (2.0 * DEPTH) ** 0.25
LN_EPS = 1e-5
RWKV_LN_EPS = 64e-5
RMS_EPS = 1e-6

ROUTER_TILE = 768
DISPATCH_TILE = 1536
DMA_UNROLL = 4
LN_INPROJ_ROWS = 704
VMEM_LIMIT = 56 * 1024 * 1024


def _mm(a, b):
    return jnp.dot(a.astype(BF16), b.astype(BF16), preferred_element_type=F32)


def _layer_norm(x, g, b):
    mu = jnp.mean(x, axis=-1, keepdims=True)
    xc = x - mu
    var = jnp.mean(xc * xc, axis=-1, keepdims=True)
    return xc * lax.rsqrt(var + LN_EPS) * g + b


def _sigmoid(x):
    return 1.0 / (1.0 + jnp.exp(-x))


def _log_sigmoid(x):
    return jnp.minimum(x, 0.0) - jnp.log(1.0 + jnp.exp(-jnp.abs(x)))


ROW_TILE = (8, 128)


def _tiled_rows(n):
    return (n * ROW_TILE[0], ROW_TILE[1])


def _row_tile(ref, i):
    return ref.at[pl.ds(pl.multiple_of(i * ROW_TILE[0], ROW_TILE[0]), ROW_TILE[0])]


def _store_row_tiles(ref, x, row0=0):
    n = x.shape[0]
    for j in range(ROW_TILE[0]):
        ref[pl.ds(row0 * ROW_TILE[0] + j, n, stride=ROW_TILE[0]), :] = x[:, j * ROW_TILE[1]:(j + 1) * ROW_TILE[1]]


def _load_row_tiles(ref, row0=0, n=None):
    n = ref.shape[0] // ROW_TILE[0] if n is None else n
    return jnp.concatenate([ref[pl.ds(row0 * ROW_TILE[0] + j, n, stride=ROW_TILE[0]), :]
                            for j in range(ROW_TILE[0])], axis=1)


def _tri_masks(n):
    r = lax.broadcasted_iota(I32, (n, n), 0)
    c = lax.broadcasted_iota(I32, (n, n), 1)
    return r >= c, r > c, r == c


def _ln_inproj_kernel(x_ref, front_ref, g_ref, b_ref, w_ref, h_ref, pg_ref, pr_ref, *, tiles_per_seq):
    i = pl.program_id(0)
    first = i % tiles_per_seq == 0
    x = x_ref[0]
    x = jnp.where(first, jnp.concatenate([front_ref[...], x[:x.shape[0] - CHUNK]], axis=0), x)
    y = _layer_norm(x, g_ref[...], b_ref[...])
    row = lax.broadcasted_iota(I32, (y.shape[0], 1), 0)
    is_front = jnp.logical_and(first, row < N_FRONT)
    y = jnp.where(is_front, 0.0, y)
    h_ref[...] = y
    p = _mm(y, w_ref[...])
    pg_ref[...] = p[:, :GLA_COLS]
    pr_ref[...] = p[:, GLA_COLS:]


def _ln_inproj(x, front, g, b, w, lp):
    tp = x.shape[0] * lp
    tiles_per_seq = 1
    for cand in range(1, lp // 8 + 1):
        if lp % cand == 0 and (lp // cand) % 8 == 0 and lp // cand >= N_FRONT and lp // cand <= LN_INPROJ_ROWS:
            tiles_per_seq = cand
            break
    tm = lp // tiles_per_seq
    ncols = GLA_COLS + RWKV_COLS
    return pl.pallas_call(
        functools.partial(_ln_inproj_kernel, tiles_per_seq=tiles_per_seq),
        name="ln_inproj",
        grid=(tp // tm,),
        in_specs=[
            pl.BlockSpec((pl.Element(1), pl.Element(tm), pl.Element(D_MODEL)),
                         lambda i: (i // tiles_per_seq,
                                    pl.multiple_of(jnp.maximum((i % tiles_per_seq) * tm - CHUNK, 0), CHUNK), 0)),
            pl.BlockSpec((CHUNK, D_MODEL), lambda i: (0, 0)),
            pl.BlockSpec((1, D_MODEL), lambda i: (0, 0)),
            pl.BlockSpec((1, D_MODEL), lambda i: (0, 0)),
            pl.BlockSpec((D_MODEL, ncols), lambda i: (0, 0)),
        ],
        out_specs=[
            pl.BlockSpec((tm, D_MODEL), lambda i: (i, 0)),
            pl.BlockSpec((tm, GLA_COLS), lambda i: (i, 0)),
            pl.BlockSpec((tm, RWKV_COLS), lambda i: (i, 0)),
        ],
        out_shape=[
            jax.ShapeDtypeStruct((tp, D_MODEL), F32),
            jax.ShapeDtypeStruct((tp, GLA_COLS), F32),
            jax.ShapeDtypeStruct((tp, RWKV_COLS), F32),
        ],
        compiler_params=pltpu.CompilerParams(dimension_semantics=("arbitrary",),
                                             vmem_limit_bytes=VMEM_LIMIT),
    )(x, front, g, b, w)


GLA_GROUP = 11
GLA_SUB = 16
GLA_EXP_CAP = 80.0


def _gla_kernel(pg_ref, w2_ref, gkb_ref, ng_ref, o_ref, st_ref, qe_ref, oi_ref, kvt_ref, el_ref):
    c = pl.program_id(1)

    @pl.when(c == 0)
    def _():
        st_ref[...] = jnp.zeros_like(st_ref)

    rows_n = pg_ref.shape[1]
    ng = rows_n // CHUNK
    g_off = 2 * GLA_KEY + GLA_WIDTH
    p = pg_ref[0]
    gl3 = p[:, g_off + GLA_WIDTH:]
    gl_hi = gl3.astype(BF16)
    gl_lo = (gl3 - gl_hi.astype(F32)).astype(BF16)
    lane = lax.broadcasted_iota(I32, (1, gl3.shape[1]), 1)
    third = jnp.logical_and(lane >= 2 * GLA_GATE_RANK, lane < 3 * GLA_GATE_RANK)
    gate_pre = jnp.dot(jnp.where(third, gl_lo, gl_hi), w2_ref[...], preferred_element_type=F32)
    lg = _log_sigmoid(gate_pre + gkb_ref[...]) * (1.0 / GLA_TAU)
    row = lax.broadcasted_iota(I32, (rows_n, 1), 0)
    lg = jnp.where(jnp.logical_and(c == 0, row < N_FRONT), 0.0, lg)
    incl, _, _ = _tri_masks(CHUNK)
    tril = incl.astype(BF16)
    bc = jnp.concatenate([_mm_split3(tril, lg[i * CHUNK:(i + 1) * CHUNK]) for i in range(ng)], axis=0)
    g3 = lambda t: t.reshape(ng, CHUNK, t.shape[-1])
    bc = g3(bc)
    b_last = bc[:, CHUNK - 1:CHUNK, :]
    k = g3(p[:, GLA_KEY:2 * GLA_KEY])
    q = g3(p[:, 0:GLA_KEY]) * (GLA_DK ** -0.5)
    qe = q * jnp.exp(bc)
    n_sub = CHUNK // GLA_SUB
    sub_refs = [jnp.zeros_like(b_last)] + [bc[:, i * GLA_SUB - 1:i * GLA_SUB, :] for i in range(1, n_sub)]
    q_ref = jnp.concatenate([jnp.broadcast_to(r, (ng, GLA_SUB, GLA_KEY)) for r in sub_refs], axis=1)
    qs = (q * jnp.exp(bc - q_ref)).astype(BF16)
    ks_sub = []
    for i, r in enumerate(sub_refs):
        n_rows = (i + 1) * GLA_SUB
        scaled = (k[:, :n_rows] * jnp.exp(jnp.minimum(r - bc[:, :n_rows], GLA_EXP_CAP))).astype(BF16)
        if n_rows < CHUNK:
            scaled = jnp.concatenate([scaled, jnp.zeros((ng, CHUNK - n_rows, GLA_KEY), BF16)], axis=1)
        ks_sub.append(scaled)
    kl = k * jnp.exp(b_last - bc)
    e_last = jnp.exp(b_last)
    v = g3(p[:, 2 * GLA_KEY:g_off]).astype(BF16)
    for h in range(GLA_HEADS):
        ks = slice(h * GLA_DK, (h + 1) * GLA_DK)
        vs = slice(h * GLA_DV, (h + 1) * GLA_DV)
        qh = qe[:, :, ks].astype(BF16)
        a = jnp.concatenate(
            [jnp.einsum('gtd,gsd->gts', qs[:, i * GLA_SUB:(i + 1) * GLA_SUB, ks], ks_sub[i][:, :, ks],
                        preferred_element_type=F32) for i in range(n_sub)], axis=1)
        a = jnp.where(incl, a, 0.0).astype(BF16)
        oi_ref[:, h] = jnp.einsum('gts,gsv->gtv', a, v[:, :, vs], preferred_element_type=F32)
        kvt_ref[:, h] = jnp.einsum('gtv,gtd->gvd', v[:, :, vs], kl[:, :, ks].astype(BF16),
                                   preferred_element_type=F32)
        qe_ref[:, h] = qh
        el_ref[:, h] = e_last[:, :, ks]

    def chunk_body(ci, carry):
        st = st_ref[...]
        oi_ref[ci] = oi_ref[ci] + jnp.einsum('htd,hvd->htv', qe_ref[ci], st.astype(BF16),
                                             preferred_element_type=F32)
        st_ref[...] = st * el_ref[ci] + kvt_ref[ci]
        return carry

    lax.fori_loop(0, ng, chunk_body, 0)

    for ci in range(ng):
        o = oi_ref[ci]
        o = o * lax.rsqrt(jnp.mean(o * o, axis=-1, keepdims=True) + RMS_EPS) * ng_ref[...]
        o = jnp.concatenate([o[h] for h in range(GLA_HEADS)], axis=1)
        rows = slice(ci * CHUNK, (ci + 1) * CHUNK)
        gate = pg_ref[0, rows, g_off:g_off + GLA_WIDTH]
        o_ref[0, rows, :] = o * (gate * _sigmoid(gate))


def _gla(pg, w2p, gkb, ng_w):
    bn, lp, _ = pg.shape
    nc = lp // CHUNK
    ng = max(d for d in range(1, GLA_GROUP + 1) if nc % d == 0)
    rows = ng * CHUNK
    hh = GLA_HEADS
    return pl.pallas_call(
        _gla_kernel,
        name="gla_mixer",
        grid=(bn, nc // ng),
        in_specs=[
            pl.BlockSpec((1, rows, GLA_COLS), lambda b, c: (b, c, 0)),
            pl.BlockSpec((128, GLA_KEY), lambda b, c: (0, 0)),
            pl.BlockSpec((1, GLA_KEY), lambda b, c: (0, 0)),
            pl.BlockSpec((1, GLA_DV), lambda b, c: (0, 0)),
        ],
        out_specs=pl.BlockSpec((1, rows, GLA_WIDTH), lambda b, c: (b, c, 0)),
        out_shape=jax.ShapeDtypeStruct((bn, lp, GLA_WIDTH), F32),
        scratch_shapes=[
            pltpu.VMEM((hh, GLA_DV, GLA_DK), F32),
            pltpu.VMEM((ng, hh, CHUNK, GLA_DK), BF16),
            pltpu.VMEM((ng, hh, CHUNK, GLA_DV), F32),
            pltpu.VMEM((ng, hh, GLA_DV, GLA_DK), F32),
            pltpu.VMEM((ng, hh, 1, GLA_DK), F32),
        ],
        compiler_params=pltpu.CompilerParams(dimension_semantics=("arbitrary", "arbitrary"),
                                             vmem_limit_bytes=VMEM_LIMIT),
    )(pg, w2p, gkb, ng_w)


RWKV_GROUP = 11
RWKV_SLAB = 256
def _mm_split3(ones_bf16, x):
    hi = x.astype(BF16)
    r1 = x - hi.astype(F32)
    mid = r1.astype(BF16)
    lo = (r1 - mid.astype(F32)).astype(BF16)
    return jnp.dot(jnp.concatenate([ones_bf16] * 3, axis=1), jnp.concatenate([hi, mid, lo], axis=0),
                   preferred_element_type=F32)


def _mm_split2_rhs_ones(x, ones_bf16):
    hi = x.astype(BF16)
    mid = (x - hi.astype(F32)).astype(BF16)
    dot = lambda t: jnp.dot(t, ones_bf16, preferred_element_type=F32)
    return dot(hi) + dot(mid)


def _rwkv_kernel(pr_ref, pv_ref, mu_ref, w0_ref, w2_ref, a0_ref, a2_ref, g2_ref, kk_ref, ka_ref,
                 rk_ref, lng_ref, lnb_ref, seg_ref, o_ref, s_ref, ops_ref, wend_ref, rm_ref, yn_ref,
                 y_ref, bonus_ref, gate_ref):
    c = pl.program_id(1)
    rows_n = pr_ref.shape[1]
    ng = rows_n // CHUNK

    @pl.when(c == 0)
    def _():
        s_ref[...] = jnp.zeros_like(s_ref)

    p = pr_ref[0]
    prev_row = jnp.where(c > 0, pv_ref[0][7:8, :], 0.0)
    rolled = pltpu.roll(p, 1, 0)
    row8 = lax.broadcasted_iota(I32, (8, 1), 0)
    prev = jnp.concatenate([jnp.where(row8 == 0, prev_row, rolled[:8]), rolled[8:]], axis=0)
    p = p + (prev - p) * mu_ref[...]
    W = RWKV_WIDTH
    r = p[:, 0:W]
    k = p[:, W:2 * W]
    v = p[:, 2 * W:3 * W]
    w_low = p[:, 3 * W:3 * W + RWKV_W_RANK]
    a_low = p[:, 3 * W + RWKV_W_RANK:3 * W + RWKV_W_RANK + RWKV_A_RANK]
    g_low = p[:, 3 * W + RWKV_W_RANK + RWKV_A_RANK:]
    wx = w0_ref[...] + _mm(jnp.tanh(w_low), w2_ref[...])
    logd = _sigmoid(wx) * (-math.exp(-0.5))
    a = _sigmoid(a0_ref[...] + _mm(a_low, a2_ref[...]))
    g = _mm(_sigmoid(g_low), g2_ref[...])
    seg = seg_ref[...]
    kk = k * kk_ref[...]
    kk = kk * lax.rsqrt(jnp.maximum(_mm_split2_rhs_ones(kk * kk, seg), 1e-24))
    k = k * (1.0 + (a - 1.0) * ka_ref[...])
    bonus = _mm_split2_rhs_ones(r * k * rk_ref[...], seg) * v

    bonus_ref[...] = bonus
    gate_ref[...] = g

    incl, strict, diag = _tri_masks(CHUNK)
    tril = incl.astype(BF16)
    c_in = jnp.concatenate([_mm_split3(tril, logd[i * CHUNK:(i + 1) * CHUNK]) for i in range(ng)], axis=0)
    g3 = lambda t: t.reshape(ng, CHUNK, W)
    logd, c_in, r, k, v, kk, a = g3(logd), g3(c_in), g3(r), g3(k), g3(v), g3(kk), g3(a)
    c_last = c_in[:, CHUNK - 1:CHUNK, :]
    e_neg = jnp.exp(-c_in)
    e_end = jnp.exp(c_last - c_in)
    kka = kk * a
    per_head = (-kk * jnp.exp(c_in - logd), r * jnp.exp(c_in), kka * e_neg, k * e_neg, kka * e_end,
                k * e_end, v)
    GW = RWKV_SLAB
    n_slab = W // GW
    for i, t in enumerate(per_head):
        t = t.astype(BF16)
        for sl in range(n_slab):
            ops_ref[i, sl] = t[:, :, sl * GW:(sl + 1) * GW]
    w_end = jnp.exp(c_last)
    for sl in range(n_slab):
        wend_ref[sl] = w_end[:, :, sl * GW:(sl + 1) * GW]

    hpl = GW // RWKV_HEAD
    lane_head = lax.broadcasted_iota(I32, (1, 1, GW), 2) // RWKV_HEAD
    head_masks = [lane_head == h for h in range(hpl)]
    rr = lax.broadcasted_iota(I32, (CHUNK, GW), 0)
    cc = lax.broadcasted_iota(I32, (CHUNK, GW), 1) % RWKV_HEAD
    incl4, strict4, eye4 = rr >= cc, rr > cc, (rr == cc).astype(F32)

    def block_diag(x):
        x = x.astype(BF16)
        return jnp.concatenate([jnp.where(m, x, jnp.zeros_like(x)) for m in head_masks], axis=1)

    def head_blocks(full):
        out = jnp.where(head_masks[0], full[:, :RWKV_HEAD], 0.0)
        for h in range(1, hpl):
            out = out + jnp.where(head_masks[h], full[:, h * RWKV_HEAD:(h + 1) * RWKV_HEAD], 0.0)
        return out

    bmm = lambda x, y: jnp.einsum('gts,gsd->gtd', x.astype(BF16), y.astype(BF16), preferred_element_type=F32)
    bmm_nt = lambda x, y: jnp.einsum('gtd,gsd->gts', x.astype(BF16), y.astype(BF16), preferred_element_type=F32)
    bmm_tn = lambda x, y: jnp.einsum('gtk,gtd->gkd', x.astype(BF16), y.astype(BF16), preferred_element_type=F32)

    def slab_body(sl, carry):
        at, rt, bt, kt, bh, kh, vv = [ops_ref[i, sl] for i in range(7)]
        gm = bmm_nt(jnp.concatenate([at, rt], axis=1),
                    jnp.concatenate([block_diag(bt), block_diag(kt)], axis=1))
        a_ab = jnp.where(strict4, gm[:, :CHUNK, :GW], 0.0)
        a_ak = jnp.where(strict4, gm[:, :CHUNK, GW:], 0.0)
        a_rb = jnp.where(incl4, gm[:, CHUNK:, :GW], 0.0)
        a_rk = jnp.where(incl4, gm[:, CHUNK:, GW:], 0.0)
        akv = bmm(jnp.concatenate([a_ak, a_rk], axis=1), block_diag(vv))
        xk = eye4 + a_ab
        pk = bmm(a_ab, block_diag(a_ab))
        for _ in range(4):
            both = bmm(jnp.concatenate([xk, pk], axis=1), block_diag(pk))
            xk = xk + both[:, :CHUNK]
            pk = both[:, CHUNK:]
        xk = xk + bmm(xk, block_diag(pk))
        pq = bmm(xk, jnp.concatenate([block_diag(at), block_diag(akv[:, :CHUNK])], axis=2))
        p_bd, q_bd = block_diag(pq[:, :, :GW]), block_diag(pq[:, :, GW:])
        ry = bmm(a_rb, jnp.concatenate([p_bd, q_bd], axis=2)) + jnp.concatenate(
            [rt.astype(F32), akv[:, CHUNK:]], axis=2)
        m_sbs = head_blocks(bmm_tn(bh, pq[:, :, :GW])) + eye4 * wend_ref[sl]
        n_sbs = head_blocks(bmm_tn(jnp.concatenate([bh, kh], axis=1),
                                   jnp.concatenate([pq[:, :, GW:].astype(BF16), vv], axis=1)))
        rm_ref[sl] = jnp.concatenate([ry[:, :, :GW], m_sbs], axis=1).astype(BF16)
        yn_ref[sl] = jnp.concatenate([ry[:, :, GW:], n_sbs], axis=1)
        return carry

    lax.fori_loop(0, n_slab, slab_body, 0)

    def chunk_body(ci, carry):
        for sl in range(n_slab):
            lanes = slice(sl * GW, (sl + 1) * GW)
            st = s_ref[:, lanes]
            st_bd = jnp.concatenate([jnp.where(m[0], st, 0.0) for m in head_masks], axis=0).astype(BF16)
            res = jnp.dot(rm_ref[sl, ci], st_bd, preferred_element_type=F32) + yn_ref[sl, ci]
            y_ref[ci, :, lanes] = res[:CHUNK]
            s_ref[:, lanes] = res[CHUNK:]
        return carry

    lax.fori_loop(0, ng, chunk_body, 0)

    y = y_ref[...].reshape(rows_n, W)
    mean = _mm_split2_rhs_ones(y, seg) * (1.0 / RWKV_HEAD)
    yc = y - mean
    var = _mm_split2_rhs_ones(yc * yc, seg) * (1.0 / RWKV_HEAD)
    yn = yc * lax.rsqrt(var + RWKV_LN_EPS)
    o_ref[0] = (yn * lng_ref[...] + lnb_ref[...] + bonus_ref[...]) * gate_ref[...]


def _rwkv(pr, mu, w0, w2, a0, a2, g2, k_k, k_a, r_k, ln_g, ln_b, seg):
    bn, lp, _ = pr.shape
    nc = lp // CHUNK
    ng = max(d for d in range(1, RWKV_GROUP + 1) if nc % d == 0)
    rows = ng * CHUNK
    vec = lambda n: pl.BlockSpec((1, n), lambda b, c: (0, 0))
    mat = lambda m, n: pl.BlockSpec((m, n), lambda b, c: (0, 0))
    hd = RWKV_HEAD
    n_slab = RWKV_WIDTH // RWKV_SLAB
    return pl.pallas_call(
        _rwkv_kernel,
        name="rwkv_mixer",
        grid=(bn, nc // ng),
        in_specs=[
            pl.BlockSpec((1, rows, RWKV_COLS), lambda b, c: (b, c, 0)),
            pl.BlockSpec((1, 8, RWKV_COLS), lambda b, c: (b, jnp.maximum(c * (rows // 8) - 1, 0), 0)),
            vec(RWKV_COLS), vec(RWKV_WIDTH), mat(RWKV_W_RANK, RWKV_WIDTH), vec(RWKV_WIDTH),
            mat(RWKV_A_RANK, RWKV_WIDTH), mat(RWKV_G_RANK, RWKV_WIDTH), vec(RWKV_WIDTH), vec(RWKV_WIDTH),
            vec(RWKV_WIDTH), vec(RWKV_WIDTH), vec(RWKV_WIDTH), mat(RWKV_WIDTH, RWKV_WIDTH),
        ],
        out_specs=pl.BlockSpec((1, rows, RWKV_WIDTH), lambda b, c: (b, c, 0)),
        out_shape=jax.ShapeDtypeStruct((bn, lp, RWKV_WIDTH), F32),
        scratch_shapes=[
            pltpu.VMEM((hd, RWKV_WIDTH), F32),
            pltpu.VMEM((7, n_slab, ng, CHUNK, RWKV_SLAB), BF16),
            pltpu.VMEM((n_slab, ng, 1, RWKV_SLAB), F32),
            pltpu.VMEM((n_slab, ng, 2 * CHUNK, RWKV_SLAB), BF16),
            pltpu.VMEM((n_slab, ng, 2 * CHUNK, RWKV_SLAB), F32),
            pltpu.VMEM((ng, CHUNK, RWKV_WIDTH), F32),
            pltpu.VMEM((rows, RWKV_WIDTH), F32),
            pltpu.VMEM((rows, RWKV_WIDTH), F32),
        ],
        compiler_params=pltpu.CompilerParams(dimension_semantics=("arbitrary", "arbitrary"),
                                             vmem_limit_bytes=VMEM_LIMIT),
    )(pr, pr, mu, w0, w2, a0, a2, g2, k_k, k_a, r_k, ln_g, ln_b, seg.astype(BF16))


def _outproj_router_kernel(yg_ref, yr_ref, h_ref, real_ref, wo_ref, g_ref, b_ref, rwt_ref, rb_ref, tri_ref,
                           h1_ref, idx_ref, gate_ref, rank_ref, cnt_ref, base_ref):
    i = pl.program_id(0)

    @pl.when(i == 0)
    def _():
        base_ref[...] = jnp.zeros_like(base_ref)

    wo = wo_ref[...]
    mix = _mm(yg_ref[...], wo[:GLA_WIDTH]) + _mm(yr_ref[...], wo[GLA_WIDTH:])
    h1 = _layer_norm(DEEPNORM_ALPHA * h_ref[...] + mix, g_ref[...], b_ref[...])
    _store_row_tiles(h1_ref, h1)
    h_hi = h1.astype(BF16)
    h_lo = (h1 - h_hi.astype(F32)).astype(BF16)
    nt = (((1,), (1,)), ((), ()))
    part = lax.dot_general(rwt_ref[...], h_hi, nt, preferred_element_type=F32)
    work = (part[:N_EXPERTS] + part[N_EXPERTS:]
            + lax.dot_general(rwt_ref[:N_EXPERTS, :], h_lo, nt, preferred_element_type=F32)
            + rb_ref[...][:, 0:1])
    tm = work.shape[1]
    e_iota = lax.broadcasted_iota(I32, (N_EXPERTS, tm), 0)
    base = base_ref[...][:, 0:1]
    vals, onehots = [], []
    for kk in range(TOP_K):
        m = jnp.max(work, axis=0, keepdims=True)
        sel = jnp.min(jnp.where(work == m, e_iota, N_EXPERTS), axis=0, keepdims=True)
        onehot = e_iota == sel
        work = jnp.where(onehot, -jnp.inf, work)
        vals.append(m)
        onehots.append(onehot.astype(F32) * real_ref[...])
        idx_ref[kk:kk + 1, :] = sel
    cnt_all = jnp.dot(jnp.concatenate(onehots, axis=0).astype(BF16), tri_ref[...], preferred_element_type=F32)
    prior = jnp.zeros((N_EXPERTS, 1), F32)
    for kk in range(TOP_K):
        cnt = cnt_all[kk * N_EXPERTS:(kk + 1) * N_EXPERTS]
        rank = jnp.sum(onehots[kk] * (base + prior + cnt - 1.0), axis=0, keepdims=True)
        prior = prior + cnt[:, tm - 1:tm]
        rank_ref[kk:kk + 1, :] = rank.astype(I32)
    es = [jnp.exp(vv - vals[0]) for vv in vals]
    den = es[0] + es[1] + es[2] + es[3]
    for kk in range(TOP_K):
        gate_ref[kk:kk + 1, :] = es[kk] / den
    new_base = base + prior
    base_ref[...] = jnp.broadcast_to(new_base, base_ref.shape)
    cnt_ref[...] = jnp.broadcast_to(new_base, cnt_ref.shape)


def _outproj_router(yg, yr, h, real, wo, g, b, rwt, rb, tri):
    tp = h.shape[0]
    tm = ROUTER_TILE
    const = lambda m, n: pl.BlockSpec((m, n), lambda i: (0, 0))
    return pl.pallas_call(
        _outproj_router_kernel,
        name="outproj_router",
        grid=(tp // tm,),
        in_specs=[
            pl.BlockSpec((tm, GLA_WIDTH), lambda i: (i, 0)),
            pl.BlockSpec((tm, RWKV_WIDTH), lambda i: (i, 0)),
            pl.BlockSpec((tm, D_MODEL), lambda i: (i, 0)),
            pl.BlockSpec((1, tm), lambda i: (0, i)),
            const(D_MODEL, D_MODEL), const(1, D_MODEL), const(1, D_MODEL),
            const(2 * N_EXPERTS, D_MODEL), const(N_EXPERTS, 128), const(tm, tm),
        ],
        out_specs=[
            pl.BlockSpec(_tiled_rows(tm), lambda i: (i, 0)),
            pl.BlockSpec((TOP_K, tm), lambda i: (0, i)),
            pl.BlockSpec((TOP_K, tm), lambda i: (0, i)),
            pl.BlockSpec((TOP_K, tm), lambda i: (0, i)),
            pl.BlockSpec((N_EXPERTS, 128), lambda i: (0, 0)),
        ],
        out_shape=[
            jax.ShapeDtypeStruct(_tiled_rows(tp), F32),
            jax.ShapeDtypeStruct((TOP_K, tp), I32),
            jax.ShapeDtypeStruct((TOP_K, tp), F32),
            jax.ShapeDtypeStruct((TOP_K, tp), I32),
            jax.ShapeDtypeStruct((N_EXPERTS, 128), F32),
        ],
        scratch_shapes=[pltpu.VMEM((N_EXPERTS, 128), F32)],
        compiler_params=pltpu.CompilerParams(dimension_semantics=("arbitrary",),
                                             vmem_limit_bytes=VMEM_LIMIT),
    )(yg, yr, h, real, wo, g, b, rwt, rb, tri)


def _zero_blocks_kernel(last_ref, o_ref):
    o_ref[...] = jnp.zeros_like(o_ref)


def _zero_blocks(last_block, n_slots):
    return pl.pallas_call(
        _zero_blocks_kernel,
        name="moe_zero_blocks",
        grid_spec=pltpu.PrefetchScalarGridSpec(
            num_scalar_prefetch=1,
            grid=(N_EXPERTS,),
            in_specs=[],
            out_specs=pl.BlockSpec(_tiled_rows(MOE_BLOCK), lambda e, last: (last[e], 0)),
        ),
        out_shape=jax.ShapeDtypeStruct(_tiled_rows(n_slots), F32),
        compiler_params=pltpu.CompilerParams(dimension_semantics=("arbitrary",)),
    )(last_block)


def _dispatch_kernel(pos_ref, x_ref, xs_in_ref, xs_ref, sem):
    del xs_in_ref
    tm = pos_ref.shape[0] // TOP_K

    def start(r, carry):
        for kk in range(TOP_K):
            pltpu.make_async_copy(_row_tile(x_ref, r), _row_tile(xs_ref, pos_ref[r * TOP_K + kk]),
                                  sem).start(priority=kk % 2)
        return carry

    lax.fori_loop(0, tm, start, 0, unroll=DMA_UNROLL)
    for kk in range(TOP_K):
        pltpu.make_async_copy(x_ref, xs_ref.at[pl.ds(0, x_ref.shape[0])], sem).wait()


def _dispatch(pos, x, xs):
    tp = pos.shape[0] // TOP_K
    tm = DISPATCH_TILE if tp % DISPATCH_TILE == 0 else ROUTER_TILE
    return pl.pallas_call(
        _dispatch_kernel,
        name="moe_dispatch",
        grid=(tp // tm,),
        in_specs=[
            pl.BlockSpec((tm * TOP_K,), lambda i: (i,), memory_space=pltpu.SMEM),
            pl.BlockSpec(_tiled_rows(tm), lambda i: (i, 0)),
            pl.BlockSpec(memory_space=pl.ANY),
        ],
        out_specs=pl.BlockSpec(memory_space=pl.ANY),
        out_shape=jax.ShapeDtypeStruct(xs.shape, xs.dtype),
        scratch_shapes=[pltpu.SemaphoreType.DMA(())],
        input_output_aliases={2: 0},
        compiler_params=pltpu.CompilerParams(dimension_semantics=("arbitrary",)),
    )(pos, x, xs)


SPLIT_TILE = 256
def _moe_kernel(be_ref, nu_ref, first_ref, slot_ref, nxt_ref, xs_ref, wu_hbm, wd_hbm, perm_ref, bg_ref,
                bl_ref, bd_ref, ys_ref, wu_buf, wd_buf, wsem, wg_s, wl_s, wd_s):
    i = pl.program_id(0)

    def weight_copies(e, sl):
        return (pltpu.make_async_copy(wu_hbm.at[e], wu_buf.at[sl], wsem.at[0, sl]),
                pltpu.make_async_copy(wd_hbm.at[e], wd_buf.at[sl], wsem.at[1, sl]))

    @pl.when(i < nu_ref[0])
    def _():
        @pl.when(first_ref[i] == 1)
        def _():
            sl = slot_ref[i]

            @pl.when(i == 0)
            def _():
                for c in weight_copies(be_ref[0], sl):
                    c.start()

            for c in weight_copies(be_ref[i], sl):
                c.wait()

            @pl.when(nxt_ref[i] >= 0)
            def _():
                for c in weight_copies(nxt_ref[i], 1 - sl):
                    c.start()

            half = SPLIT_TILE // 2
            for t in range(wu_buf.shape[2] // SPLIT_TILE):
                d = jnp.dot(wu_buf[sl, :, t * SPLIT_TILE:(t + 1) * SPLIT_TILE].astype(BF16), perm_ref[...],
                            preferred_element_type=F32).astype(BF16)
                wg_s[:, t * half:(t + 1) * half] = d[:, :half]
                wl_s[:, t * half:(t + 1) * half] = d[:, half:]
            wd_s[...] = wd_buf[sl].astype(BF16)

        x = _load_row_tiles(xs_ref).astype(BF16)
        x_glu = jnp.dot(x, wg_s[...], preferred_element_type=F32) + bg_ref[0]
        x_lin = jnp.dot(x, wl_s[...], preferred_element_type=F32) + bl_ref[0]
        x_glu = jnp.minimum(x_glu, SWIGLU_LIMIT)
        x_lin = jnp.clip(x_lin, -SWIGLU_LIMIT, SWIGLU_LIMIT)
        act = x_glu * _sigmoid(SWIGLU_ALPHA * x_glu) * (x_lin + 1.0)
        _store_row_tiles(ys_ref, jnp.dot(act.astype(BF16), wd_s[...], preferred_element_type=F32) + bd_ref[0])


def _moe(block_e, n_used, xs, w_up, w_down, bg, bl, bd):
    n_slots = xs.shape[0] // ROW_TILE[0]
    nb = n_slots // MOE_BLOCK
    src = jnp.arange(SPLIT_TILE, dtype=I32)[:, None]
    dst = jnp.arange(SPLIT_TILE, dtype=I32)[None, :]
    half = SPLIT_TILE // 2
    perm = (src == jnp.where(dst < half, 2 * dst, 2 * (dst - half) + 1)).astype(BF16)
    ids = jnp.arange(nb, dtype=I32)
    first = jnp.logical_and(ids < n_used[0], jnp.logical_or(ids == 0, block_e != jnp.roll(block_e, 1)))
    slot = (jnp.cumsum(first.astype(I32)) - 1) % 2
    first_at = jnp.where(first, ids, nb)
    next_first = jnp.concatenate([lax.cummin(first_at[::-1])[::-1][1:], jnp.full((1,), nb, I32)])
    nxt_e = jnp.where(next_first < nb, block_e[jnp.minimum(next_first, nb - 1)], -1).astype(I32)
    blk = lambda i, be, nu, *_: (jnp.maximum(jnp.minimum(i, nu[0] - 1), 0), 0)
    bspec = pl.BlockSpec((1, 1, D_FF), lambda i, be, *_: (be[i], 0, 0))
    return pl.pallas_call(
        _moe_kernel,
        name="moe_experts",
        grid_spec=pltpu.PrefetchScalarGridSpec(
            num_scalar_prefetch=5,
            grid=(nb,),
            in_specs=[
                pl.BlockSpec(_tiled_rows(MOE_BLOCK), blk),
                pl.BlockSpec(memory_space=pl.ANY),
                pl.BlockSpec(memory_space=pl.ANY),
                pl.BlockSpec((SPLIT_TILE, SPLIT_TILE), lambda i, *_: (0, 0)),
                bspec, bspec, bspec,
            ],
            out_specs=pl.BlockSpec(_tiled_rows(MOE_BLOCK), blk),
            scratch_shapes=[
                pltpu.VMEM((2, D_MODEL, 2 * D_FF), F32),
                pltpu.VMEM((2, D_FF, D_MODEL), F32),
                pltpu.SemaphoreType.DMA((2, 2)),
                pltpu.VMEM((D_MODEL, D_FF), BF16),
                pltpu.VMEM((D_MODEL, D_FF), BF16),
                pltpu.VMEM((D_FF, D_MODEL), BF16),
            ],
        ),
        out_shape=jax.ShapeDtypeStruct(_tiled_rows(n_slots), F32),
        compiler_params=pltpu.CompilerParams(dimension_semantics=("arbitrary",),
                                             vmem_limit_bytes=VMEM_LIMIT),
    )(block_e, n_used, first.astype(I32), slot.astype(I32), nxt_e, xs, w_up, w_down, perm, bg, bl, bd)


COMBINE_SLOTS = 3
COMBINE_CHUNKS = 2


def _combine_kernel(*refs, cps):
    first_pos = refs[0:2]
    ahead_pos = refs[2:2 + cps]
    gts = refs[2 + cps:2 + 2 * cps]
    h1s = refs[2 + 2 * cps:2 + 3 * cps]
    g_ref, b_ref, ys_ref, o_ref, buf, sem = refs[2 + 3 * cps:]
    s = pl.program_id(0)
    rows = TOP_K * CHUNK

    def issue_row(p_ref, to_slot, r):
        for kk in range(TOP_K):
            pltpu.make_async_copy(_row_tile(ys_ref, p_ref[r * TOP_K + kk]),
                                  _row_tile(buf, to_slot * rows + kk * CHUNK + r),
                                  sem.at[to_slot]).start(priority=kk % 2)

    def wait_slot(which):
        span = rows * ROW_TILE[0]
        pltpu.make_async_copy(ys_ref.at[pl.ds(0, span)], buf.at[pl.ds(which * span, span)], sem.at[which]).wait()

    @pl.when(s == 0)
    def _():
        def body(r, carry):
            issue_row(first_pos[0], 0, r)
            issue_row(first_pos[1], 1, r)
            return carry

        lax.fori_loop(0, CHUNK, body, 0, unroll=DMA_UNROLL)

    def reduce_chunk(j, cur):
        ahead = (cur + 2) % COMBINE_SLOTS
        wait_slot(cur)
        gt = gts[j][...]
        quarter = CHUNK // TOP_K
        ffn = None
        for kk in range(TOP_K):
            part = _load_row_tiles(buf, cur * rows + kk * CHUNK, CHUNK) * gt[:, kk:kk + 1]
            ffn = part if ffn is None else ffn + part
            for r in range(kk * quarter, (kk + 1) * quarter):
                issue_row(ahead_pos[j], ahead, r)
        o_ref[0, j * CHUNK:(j + 1) * CHUNK, :] = _layer_norm(
            DEEPNORM_ALPHA * _load_row_tiles(h1s[j]) + ffn, g_ref[...], b_ref[...])

    def step(first_slot):
        for j in range(cps):
            reduce_chunk(j, (first_slot + j) % COMBINE_SLOTS)

        @pl.when(s == pl.num_programs(0) - 1)
        def _():
            last = (first_slot + cps - 1) % COMBINE_SLOTS
            wait_slot((last + 1) % COMBINE_SLOTS)
            wait_slot((last + 2) % COMBINE_SLOTS)

    for first_slot in range(COMBINE_SLOTS):
        pl.when((s * cps) % COMBINE_SLOTS == first_slot)(functools.partial(step, first_slot))


def _combine(pos_flat, gates_t, h1, g, b, ys, bn, lp):
    seq = lp - CHUNK
    nc = lp // CHUNK
    ncs = seq // CHUNK
    cps = COMBINE_CHUNKS if ncs % COMBINE_CHUNKS == 0 else 1
    n_chunks = bn * ncs
    chunk_of = lambda m: (m // ncs) * nc + m % ncs + 1
    clamp = lambda m: jnp.minimum(m, n_chunks - 1)
    pos_spec = lambda f: pl.BlockSpec((TOP_K * CHUNK,), lambda s: (chunk_of(f(s)),), memory_space=pltpu.SMEM)
    per_chunk = lambda shape: [pl.BlockSpec(shape, lambda s, j=j: (chunk_of(s * cps + j), 0)) for j in range(cps)]
    return pl.pallas_call(
        functools.partial(_combine_kernel, cps=cps),
        name="moe_combine",
        grid=(n_chunks // cps,),
        in_specs=[pos_spec(lambda s: 0 * s), pos_spec(lambda s: 0 * s + 1)]
        + [pos_spec(lambda s, j=j: clamp(s * cps + j + 2)) for j in range(cps)]
        + per_chunk((CHUNK, TOP_K)) + per_chunk(_tiled_rows(CHUNK))
        + [
            pl.BlockSpec((1, D_MODEL), lambda s: (0, 0)),
            pl.BlockSpec((1, D_MODEL), lambda s: (0, 0)),
            pl.BlockSpec(memory_space=pl.ANY),
        ],
        out_specs=pl.BlockSpec((1, cps * CHUNK, D_MODEL), lambda s: (s // (ncs // cps), s % (ncs // cps), 0)),
        out_shape=jax.ShapeDtypeStruct((bn, seq, D_MODEL), F32),
        scratch_shapes=[pltpu.VMEM(_tiled_rows(COMBINE_SLOTS * TOP_K * CHUNK), F32),
                        pltpu.SemaphoreType.DMA((COMBINE_SLOTS,))],
        compiler_params=pltpu.CompilerParams(dimension_semantics=("arbitrary",)),
    )(*([pos_flat] * (2 + cps) + [gates_t] * cps + [h1] * cps + [g, b, ys]))


def kernel(x, meta, ln_in_g, ln_in_b, w_in, gla_gk_w2, gla_gk_b, gla_norm_g, rwkv_mu, rwkv_w0, rwkv_w2, rwkv_a0, rwkv_a2, rwkv_g2, rwkv_k_k, rwkv_k_a, rwkv_r_k, rwkv_ln_g, rwkv_ln_b, w_out, ln1_g, ln1_b, router_w, router_b, exp_w_up, exp_b_up, exp_w_down, exp_b_down, ln2_g, ln2_b):
    bn, seq, _ = x.shape
    assert seq % CHUNK == 0
    lp = seq + CHUNK
    tp = bn * lp
    assert tp % ROUTER_TILE == 0
    row = lambda t: t.reshape(1, -1).astype(F32)

    front = jnp.concatenate([jnp.zeros((N_FRONT, D_MODEL), F32), meta.astype(F32)], axis=0)
    gla_in = 2 * GLA_KEY + 2 * GLA_WIDTH + GLA_GATE_RANK
    w = w_in[0]
    w_gk = w[:, gla_in - GLA_GATE_RANK:gla_in]
    w_cols = jnp.concatenate([w[:, :gla_in], w_gk, w_gk, jnp.zeros((D_MODEL, 128 - 3 * GLA_GATE_RANK), F32),
                              w[:, gla_in:]], axis=1).astype(BF16)
    h, pg, pr = _ln_inproj(x, front, row(ln_in_g), row(ln_in_b), w_cols, lp)

    w2_hi = gla_gk_w2[0].astype(BF16)
    w2_lo = (gla_gk_w2[0] - w2_hi.astype(F32)).astype(BF16)
    w2p = jnp.concatenate([w2_hi, w2_lo, w2_hi, jnp.zeros((128 - 3 * GLA_GATE_RANK, GLA_KEY), BF16)], axis=0)
    y_gla = _gla(pg.reshape(bn, lp, GLA_COLS), w2p, row(gla_gk_b[0]), row(gla_norm_g[0]))

    head_id = jnp.arange(RWKV_WIDTH, dtype=I32) // RWKV_HEAD
    seg = (head_id[:, None] == head_id[None, :]).astype(F32)
    y_rwkv = _rwkv(pr.reshape(bn, lp, RWKV_COLS), row(rwkv_mu[0]), row(rwkv_w0[0]), rwkv_w2[0],
                   row(rwkv_a0[0]), rwkv_a2[0], rwkv_g2[0], row(rwkv_k_k[0]), row(rwkv_k_a[0]),
                   row(rwkv_r_k[0]), row(rwkv_ln_g[0]), row(rwkv_ln_b[0]), seg)

    tri = jnp.triu(jnp.ones((ROUTER_TILE, ROUTER_TILE), F32)).astype(BF16)
    rb = jnp.broadcast_to(router_b[0].reshape(N_EXPERTS, 1), (N_EXPERTS, 128))
    rw_t = router_w[0].T
    rw_hi = rw_t.astype(BF16)
    rwt = jnp.concatenate([rw_hi, (rw_t - rw_hi.astype(F32)).astype(BF16)], axis=0)
    tok = jnp.arange(tp, dtype=I32)
    real = (tok % lp >= N_FRONT).reshape(1, tp)
    h1, idx, gates, rank, cnt = _outproj_router(
        y_gla.reshape(tp, GLA_WIDTH), y_rwkv.reshape(tp, RWKV_WIDTH), h, real.astype(F32), w_out[0].astype(BF16),
        row(ln1_g[0]), row(ln1_b[0]), rwt, rb, tri)

    counts = cnt[:, 0].astype(I32)
    padded = (counts + MOE_BLOCK - 1) // MOE_BLOCK * MOE_BLOCK
    ends_p = jnp.cumsum(padded)
    starts_p = ends_p - padded
    e_ids = jnp.arange(N_EXPERTS, dtype=I32)
    start_of = jnp.sum(jnp.where(idx[None] == e_ids[:, None, None], starts_p[:, None, None], 0), axis=0)
    nb_experts = tp * TOP_K // MOE_BLOCK + N_EXPERTS
    nb = nb_experts + pl.cdiv(bn * N_FRONT * TOP_K, MOE_BLOCK)
    n_slots = nb * MOE_BLOCK
    spare = nb_experts * MOE_BLOCK + ((tok // lp) * N_FRONT + tok % lp) * TOP_K
    pos = jnp.where(real, start_of + rank, spare[None, :] + jnp.arange(TOP_K, dtype=I32)[:, None])
    block_start = jnp.arange(nb, dtype=I32) * MOE_BLOCK
    block_e = jnp.minimum(jnp.sum((block_start[:, None] >= ends_p[None, :]).astype(I32), axis=1), N_EXPERTS - 1)
    n_used = (ends_p[-1:] // MOE_BLOCK).astype(I32)
    last_block = jnp.maximum(ends_p // MOE_BLOCK - 1, 0).astype(I32)

    pos_flat = pos.T.reshape(tp * TOP_K)
    xs = _dispatch(pos_flat, h1, _zero_blocks(last_block, n_slots))
    bg = exp_b_up[0][:, None, 0::2]
    bl = exp_b_up[0][:, None, 1::2]
    ys = _moe(block_e, n_used, xs, exp_w_up[0], exp_w_down[0], bg, bl, exp_b_down[0][:, None, :])

    return _combine(pos_flat, gates.T, h1, row(ln2_g[0]), row(ln2_b[0]), ys, bn, lp)
```
